```python
import jax, jax.numpy as jnp
from jax import lax
import numpy as np

D_MODEL = 1024
BATCH = 8
SEQ = 2048
DEPTH = 2
DEC_BATCH = 128
DEC_SEQ = 1
PAST_LEN = 16384
PAGE_SIZE = 128

ML_HEADS = 4
ML_DH = 128
ML_W = ML_HEADS * ML_DH
GLA_HEADS = 4
GLA_DK = 64
GLA_DV = 128
GLA_KW = GLA_HEADS * GLA_DK
GLA_VW = GLA_HEADS * GLA_DV
GLA_RANK = 16
GLA_TAU = 16.0
POOL_GROUPS = 4
POOL_GDIM = 128
POOL_W = POOL_GROUPS * POOL_GDIM
POOL_WINDOWS = (2, 4, 8, 16)
POOL_BUF = 15
N_BRANCH = 3
CHUNK = 64
D_FF = 2816
N_EXPERTS = 8
TOP_K = 2
D_FF_EXPERT = 1408
N_DENSE = (DEPTH + 1) // 2
N_MOE = DEPTH // 2
EPS = 1e-6
NEG = -1e30
IN_SPLITS = (ML_W, ML_W, ML_W, ML_HEADS, ML_HEADS, ML_W,
             GLA_KW, GLA_KW, GLA_VW, GLA_RANK, GLA_VW,
             POOL_W, N_BRANCH * D_MODEL)
D_IN = sum(IN_SPLITS)

kernel_name = 'hybrid_mlstm_gla_pool_decoder_step'


def rmsnorm(x, g):
    xf = x.astype(jnp.float32)
    y = xf * lax.rsqrt(jnp.mean(xf * xf, axis=-1, keepdims=True) + EPS)
    return (y * g.astype(jnp.float32)).astype(x.dtype)


def head_rmsnorm(h, g):
    y = h * lax.rsqrt(jnp.mean(h * h, axis=-1, keepdims=True) + EPS)
    return y.reshape(h.shape[:2] + (-1,)) * g.astype(jnp.float32)


def split_cols(z):
    parts, off = [], 0
    for w in IN_SPLITS:
        parts.append(z[..., off:off + w])
        off += w
    return parts


def to_chunks(a, L, pad_value):
    B, T = a.shape[:2]
    n = -(-T // L)
    pad = n * L - T
    a = jnp.pad(a, [(0, 0), (0, pad)] + [(0, 0)] * (a.ndim - 2), constant_values=pad_value)
    a = a.reshape((B, n, L) + a.shape[2:])
    return jnp.moveaxis(a, 1, 0)


def from_chunks(h, T):
    h = jnp.moveaxis(h, 0, 1)
    h = h.reshape((h.shape[0], h.shape[1] * h.shape[2]) + h.shape[3:])
    return h[:, :T]


def mlstm_chunked(q, k, v, i_pre, logf, C0, n0, m0):
    T = q.shape[1]
    L = min(CHUNK, T)
    qc, kc, vc = to_chunks(q, L, 0.0), to_chunks(k, L, 0.0), to_chunks(v, L, 0.0)
    ic = to_chunks(i_pre, L, NEG)
    fc = to_chunks(logf, L, 0.0)
    causal = jnp.tril(jnp.ones((L, L), dtype=bool))[None, :, :, None]

    def step(carry, inp):
        C, n, m = carry
        qb, kb, vb, ib, fb = inp
        b = jnp.cumsum(fb, axis=1)
        Dm = b[:, :, None, :] - b[:, None, :, :] + ib[:, None, :, :]
        Dm = jnp.where(causal, Dm, NEG)
        inter = b + m[:, None, :]
        m_t = jnp.maximum(inter, jnp.max(Dm, axis=2))
        w_intra = jnp.exp(Dm - m_t[:, :, None, :])
        w_inter = jnp.exp(inter - m_t)
        s = jnp.einsum('bthd,bshd->btsh', qb, kb) * w_intra
        num = (jnp.einsum('btsh,bshe->bthe', s, vb)
               + w_inter[..., None] * jnp.einsum('bhed,bthd->bthe', C, qb))
        den = jnp.sum(s, axis=2) + w_inter * jnp.einsum('bhd,bthd->bth', n, qb)
        h = num / jnp.maximum(jnp.abs(den), jnp.exp(-m_t))[..., None]
        bL = b[:, -1, :]
        g = bL[:, None, :] - b + ib
        m_new = jnp.maximum(bL + m, jnp.max(g, axis=1))
        w_s = jnp.exp(g - m_new[:, None, :])
        w_c = jnp.exp(bL + m - m_new)
        C_new = w_c[..., None, None] * C + jnp.einsum('bsh,bshe,bshd->bhed', w_s, vb, kb)
        n_new = w_c[..., None] * n + jnp.einsum('bsh,bshd->bhd', w_s, kb)
        return (C_new, n_new, m_new), h

    (C1, n1, m1), hs = lax.scan(step, (C0, n0, m0), (qc, kc, vc, ic, fc))
    return from_chunks(hs, T), C1, n1, m1


def gla_chunked(q, k, v, log_a, S0):
    T = q.shape[1]
    L = min(CHUNK, T)
    qc, kc, vc = to_chunks(q, L, 0.0), to_chunks(k, L, 0.0), to_chunks(v, L, 0.0)
    ac = to_chunks(log_a, L, 0.0)
    causal = jnp.tril(jnp.ones((L, L), dtype=bool))[None, :, :, None, None]

    def step(S, inp):
        qb, kb, vb, ab = inp
        b = jnp.cumsum(ab, axis=1)
        diff = b[:, :, None] - b[:, None, :]
        decay = jnp.where(causal, jnp.exp(jnp.where(causal, diff, 0.0)), 0.0)
        A = jnp.einsum('bthd,bshd,btshd->btsh', qb, kb, decay)
        o = (jnp.einsum('btsh,bshe->bthe', A, vb)
             + jnp.einsum('bthd,bhde->bthe', qb * jnp.exp(b), S))
        bL = b[:, -1]
        S_new = (jnp.exp(bL)[..., None] * S
                 + jnp.einsum('bshd,bshe->bhde', kb * jnp.exp(bL[:, None] - b), vb))
        return S_new, o

    S1, os_ = lax.scan(step, S0, (qc, kc, vc, ac))
    return from_chunks(os_, T), S1


def pool_mixer(u, buf, pos0, pool_w, pool_scale):
    B, T, P = u.shape
    ext = jnp.concatenate([buf, u], axis=1)
    cs = jnp.concatenate([jnp.zeros((B, 1, P), jnp.float32), jnp.cumsum(ext, axis=1)], axis=1)
    end = cs[:, POOL_BUF + 1:]
    pos = pos0 + jnp.arange(T)
    outs = []
    for gi, w in enumerate(POOL_WINDOWS):
        lo, hi = gi * POOL_GDIM, (gi + 1) * POOL_GDIM
        start = cs[:, POOL_BUF + 1 - w:POOL_BUF + 1 - w + T, lo:hi]
        cnt = jnp.minimum(pos + 1, w).astype(jnp.float32)[None, :, None]
        outs.append((end[..., lo:hi] - start) / cnt - u[..., lo:hi])
    d = jnp.stack(outs, axis=2)
    y = jnp.einsum('btgc,gcd->btgd', d, pool_w.astype(jnp.float32)).reshape(B, T, P)
    return y * pool_scale.astype(jnp.float32), ext[:, -POOL_BUF:]


def token_mix(x, C0, n0, m0, S0, buf0, pos0, norm1_g, w_in, if_bias, ml_g,
              gla_gw, gla_gb, gla_g, pool_w, pool_scale, w_branch, w_out):
    f32 = jnp.float32
    B, T, _ = x.shape
    h = rmsnorm(x, norm1_g)
    z = h @ w_in
    mq, mk, mv, mi, mf, mo, gq, gk, gv, glr, go, pu, gates = split_cols(z)
    q = mq.astype(f32).reshape(B, T, ML_HEADS, ML_DH)
    k = mk.astype(f32).reshape(B, T, ML_HEADS, ML_DH) * (ML_DH ** -0.5)
    v = mv.astype(f32).reshape(B, T, ML_HEADS, ML_DH)
    i_pre = mi.astype(f32) + if_bias[0].astype(f32)
    logf = jax.nn.log_sigmoid(mf.astype(f32) + if_bias[1].astype(f32))
    hm, C1, n1, m1 = mlstm_chunked(q, k, v, i_pre, logf, C0.astype(f32), n0.astype(f32), m0.astype(f32))
    y_ml = head_rmsnorm(hm, ml_g) * jax.nn.sigmoid(mo.astype(f32))
    q2 = gq.astype(f32).reshape(B, T, GLA_HEADS, GLA_DK) * (GLA_DK ** -0.5)
    k2 = gk.astype(f32).reshape(B, T, GLA_HEADS, GLA_DK)
    v2 = gv.astype(f32).reshape(B, T, GLA_HEADS, GLA_DV)
    log_a = jax.nn.log_sigmoid(glr.astype(f32) @ gla_gw.astype(f32) + gla_gb.astype(f32))
    log_a = log_a.reshape(B, T, GLA_HEADS, GLA_DK) / GLA_TAU
    og, S1 = gla_chunked(q2, k2, v2, log_a, S0.astype(f32))
    y_gla = head_rmsnorm(og, gla_g) * jax.nn.silu(go.astype(f32))
    y_pool, buf1 = pool_mixer(pu.astype(f32), buf0.astype(f32), pos0, pool_w, pool_scale)
    br = jnp.stack([y_ml, y_gla, y_pool], axis=2).astype(x.dtype)
    proj = jnp.einsum('btnc,ncd->btnd', br, w_branch)
    gate = jax.nn.sigmoid(gates.reshape(B, T, N_BRANCH, D_MODEL))
    mixed = jnp.sum(gate * proj, axis=2) @ w_out
    return x + mixed.astype(x.dtype), C1, n1, m1, S1, buf1


def swiglu(h, wg, wu, wd):
    return (jax.nn.silu(h @ wg) * (h @ wu)) @ wd


def moe_swiglu(h, router_w, wg, wu, wd):
    logits = (h @ router_w).astype(jnp.float32)
    probs = jax.nn.softmax(logits, axis=-1)
    top_p, top_i = lax.top_k(probs, TOP_K)
    top_p = top_p / jnp.sum(top_p, axis=-1, keepdims=True)
    comb = jnp.sum(jax.nn.one_hot(top_i, N_EXPERTS, dtype=jnp.float32) * top_p[..., None], axis=-2)
    out = jnp.zeros(h.shape[:-1] + (D_MODEL,), jnp.float32)
    for e in range(N_EXPERTS):
        out = out + comb[..., e:e + 1] * swiglu(h, wg[e], wu[e], wd[e]).astype(jnp.float32)
    return out.astype(h.dtype)


def channel_mix(l, x, norm2_g, ffn_wg, ffn_wu, ffn_wd, router_w, moe_wg, moe_wu, moe_wd):
    h = rmsnorm(x, norm2_g[l])
    j = l // 2
    if l % 2 == 0:
        return x + swiglu(h, ffn_wg[j], ffn_wu[j], ffn_wd[j])
    return x + moe_swiglu(h, router_w[j], moe_wg[j], moe_wu[j], moe_wd[j])


def setup_inputs(seed: int = 0) -> dict:
    key = jax.random.key(seed)
    ks = jax.random.split(key, 32)
    f32 = jnp.float32

    def nrm(k, shape, s):
        return jax.random.normal(k, shape, f32) * s

    forget_base = jnp.linspace(3.0, 6.0, ML_HEADS, dtype=f32)
    return {
        'x_prompt': nrm(ks[0], (BATCH, SEQ, D_MODEL), 1.0),
        'x_sample': nrm(ks[1], (DEC_BATCH, DEC_SEQ, D_MODEL), 1.0),
        'state_mlstm_C': nrm(ks[2], (DEPTH, DEC_BATCH, ML_HEADS, ML_DH, ML_DH), 0.1),
        'state_mlstm_n': nrm(ks[3], (DEPTH, DEC_BATCH, ML_HEADS, ML_DH), 0.1),
        'state_mlstm_m': nrm(ks[4], (DEPTH, DEC_BATCH, ML_HEADS), 0.5),
        'state_gla_S': nrm(ks[5], (DEPTH, DEC_BATCH, GLA_HEADS, GLA_DK, GLA_DV), 1.0),
        'state_pool_buf': nrm(ks[6], (DEPTH, DEC_BATCH, POOL_BUF, POOL_W), 1.0),
        'norm1_g': 1.0 + nrm(ks[7], (DEPTH, D_MODEL), 0.02),
        'w_in': nrm(ks[8], (DEPTH, D_MODEL, D_IN), D_MODEL ** -0.5),
        'mlstm_if_bias': jnp.stack([nrm(ks[9], (DEPTH, ML_HEADS), 0.1),
                                    forget_base + nrm(ks[10], (DEPTH, ML_HEADS), 0.1)], axis=1),
        'mlstm_norm_g': 1.0 + nrm(ks[11], (DEPTH, ML_W), 0.02),
        'gla_gate_w': nrm(ks[12], (DEPTH, GLA_RANK, GLA_KW), GLA_RANK ** -0.5),
        'gla_gate_b': nrm(ks[13], (DEPTH, GLA_KW), 0.1),
        'gla_norm_g': 1.0 + nrm(ks[14], (DEPTH, GLA_VW), 0.02),
        'pool_w': nrm(ks[15], (DEPTH, POOL_GROUPS, POOL_GDIM, POOL_GDIM), POOL_GDIM ** -0.5),
        'pool_scale': 1.0 + nrm(ks[16], (DEPTH, POOL_W), 0.02),
        'w_branch': nrm(ks[17], (DEPTH, N_BRANCH, ML_W, D_MODEL), ML_W ** -0.5),
        'w_out': nrm(ks[18], (DEPTH, D_MODEL, D_MODEL), D_MODEL ** -0.5),
        'norm2_g': 1.0 + nrm(ks[19], (DEPTH, D_MODEL), 0.02),
        'ffn_wg': nrm(ks[20], (N_DENSE, D_MODEL, D_FF), D_MODEL ** -0.5),
        'ffn_wu': nrm(ks[21], (N_DENSE, D_MODEL, D_FF), D_MODEL ** -0.5),
        'ffn_wd': nrm(ks[22], (N_DENSE, D_FF, D_MODEL), D_FF ** -0.5),
        'router_w': nrm(ks[23], (N_MOE, D_MODEL, N_EXPERTS), D_MODEL ** -0.5),
        'moe_wg': nrm(ks[24], (N_MOE, N_EXPERTS, D_MODEL, D_FF_EXPERT), D_MODEL ** -0.5),
        'moe_wu': nrm(ks[25], (N_MOE, N_EXPERTS, D_MODEL, D_FF_EXPERT), D_MODEL ** -0.5),
        'moe_wd': nrm(ks[26], (N_MOE, N_EXPERTS, D_FF_EXPERT, D_MODEL), D_FF_EXPERT ** -0.5),
        'final_norm_g': 1.0 + nrm(ks[27], (D_MODEL,), 0.02),
    }


def reference(x_prompt, x_sample, state_mlstm_C, state_mlstm_n, state_mlstm_m, state_gla_S,
              state_pool_buf, norm1_g, w_in, mlstm_if_bias, mlstm_norm_g, gla_gate_w, gla_gate_b,
              gla_norm_g, pool_w, pool_scale, w_branch, w_out, norm2_g, ffn_wg, ffn_wu, ffn_wd,
              router_w, moe_wg, moe_wu, moe_wd, final_norm_g):
    f32 = jnp.float32
    xp, xs = x_prompt, x_sample
    Cp_l, Cs_l, np_l, ns_l, mp_l, ms_l, Sp_l, Ss_l, bp_l, bs_l = [], [], [], [], [], [], [], [], [], []
    for l in range(DEPTH):
        lw = (norm1_g[l], w_in[l], mlstm_if_bias[l], mlstm_norm_g[l], gla_gate_w[l], gla_gate_b[l],
              gla_norm_g[l], pool_w[l], pool_scale[l], w_branch[l], w_out[l])
        xp, Cp, np_, mp, Sp, bp = token_mix(
            xp, jnp.zeros((BATCH, ML_HEADS, ML_DH, ML_DH), f32), jnp.zeros((BATCH, ML_HEADS, ML_DH), f32),
            jnp.zeros((BATCH, ML_HEADS), f32), jnp.zeros((BATCH, GLA_HEADS, GLA_DK, GLA_DV), f32),
            jnp.zeros((BATCH, POOL_BUF, POOL_W), f32), 0, *lw)
        xs, Cs, ns, ms, Ss, bs = token_mix(
            xs, state_mlstm_C[l], state_mlstm_n[l], state_mlstm_m[l], state_gla_S[l],
            state_pool_buf[l], PAST_LEN, *lw)
        xp = channel_mix(l, xp, norm2_g, ffn_wg, ffn_wu, ffn_wd, router_w, moe_wg, moe_wu, moe_wd)
        xs = channel_mix(l, xs, norm2_g, ffn_wg, ffn_wu, ffn_wd, router_w, moe_wg, moe_wu, moe_wd)
        Cp_l.append(Cp); Cs_l.append(Cs); np_l.append(np_); ns_l.append(ns)
        mp_l.append(mp); ms_l.append(ms); Sp_l.append(Sp); Ss_l.append(Ss)
        bp_l.append(bp); bs_l.append(bs)
    y_prompt = rmsnorm(xp, final_norm_g)
    y_sample = rmsnorm(xs, final_norm_g)
    mlstm_C_prompt = jnp.stack(Cp_l).astype(state_mlstm_C.dtype)
    mlstm_C_sample = jnp.stack(Cs_l).astype(state_mlstm_C.dtype)
    mlstm_n_prompt = jnp.stack(np_l).astype(state_mlstm_n.dtype)
    mlstm_n_sample = jnp.stack(ns_l).astype(state_mlstm_n.dtype)
    mlstm_m_prompt = jnp.stack(mp_l).astype(state_mlstm_m.dtype)
    mlstm_m_sample = jnp.stack(ms_l).astype(state_mlstm_m.dtype)
    gla_S_prompt = jnp.stack(Sp_l).astype(state_gla_S.dtype)
    gla_S_sample = jnp.stack(Ss_l).astype(state_gla_S.dtype)
    pool_buf_prompt = jnp.stack(bp_l).astype(state_pool_buf.dtype)
    pool_buf_sample = jnp.stack(bs_l).astype(state_pool_buf.dtype)
    return (y_prompt, y_sample, mlstm_C_prompt, mlstm_C_sample, mlstm_n_prompt, mlstm_n_sample,
            mlstm_m_prompt, mlstm_m_sample, gla_S_prompt, gla_S_sample, pool_buf_prompt, pool_buf_sample)
```

```python
import functools

import jax
import jax.numpy as jnp
from jax import lax
from jax.experimental import pallas as pl
from jax.experimental.pallas import tpu as pltpu

f32 = jnp.float32
bf16 = jnp.bfloat16

D_MODEL = 1024
ML_HEADS, ML_DH = 4, 128
ML_W = ML_HEADS * ML_DH
GLA_HEADS, GLA_DK, GLA_DV = 4, 64, 128
GLA_KW, GLA_VW = GLA_HEADS * GLA_DK, GLA_HEADS * GLA_DV
GLA_RANK = 16
GLA_TAU = 16.0
POOL_GDIM = 128
POOL_WINDOWS = (2, 4, 8, 16)
POOL_W = POOL_GDIM * len(POOL_WINDOWS)
POOL_BUF = 15
N_BRANCH = 3
N_EXPERTS = 8
EPS = 1e-6
NEG = -1e30
LANES = 128

C_MQ, C_MK, C_MV, C_MO = 0, 512, 1024, 1536
C_GQ, C_GK, C_GV, C_GO, C_PU, C_GATES = 2048, 2304, 2560, 3072, 3584, 4096
Z_MAIN = C_GATES + N_BRANCH * D_MODEL
S_MI, S_MF, S_GLR = 0, 4, 8

VMEM_LIMIT = 56 * 1024 * 1024

_NT = (((1,), (1,)), ((), ()))


def _cparams(sem):
    return pltpu.CompilerParams(dimension_semantics=sem, vmem_limit_bytes=VMEM_LIMIT)


def _log_sigmoid(x):
    return jnp.minimum(x, 0.0) - jnp.log1p(jnp.exp(-jnp.abs(x)))


def _sigmoid(x):
    return 1.0 / (1.0 + jnp.exp(-x))


def _rms(x, g):
    ms = jnp.mean(x * x, axis=-1, keepdims=True)
    return x * lax.rsqrt(ms + EPS) * g


def _lower_tri(n):
    r = lax.broadcasted_iota(jnp.int32, (n, n), 0)
    c = lax.broadcasted_iota(jnp.int32, (n, n), 1)
    return c <= r


def _cumsum_rows(tri_bf16, a):
    a1 = a.astype(bf16)
    r = a - a1.astype(f32)
    a2 = r.astype(bf16)
    a3 = (r - a2.astype(f32)).astype(bf16)
    d = lambda y: jnp.dot(tri_bf16, y, preferred_element_type=f32)
    return d(a1) + d(a2) + d(a3)


def _norm_matmul_kernel(x_ref, g_ref, w_ref, ws_ref, z_ref, zs_ref, h_ref):
    @pl.when(pl.program_id(1) == 0)
    def _():
        h = _rms(x_ref[...], g_ref[...]).astype(bf16)
        h_ref[...] = h
        zs_ref[...] = jnp.dot(h, ws_ref[...], preferred_element_type=f32)

    z_ref[...] = jnp.dot(h_ref[...], w_ref[...], preferred_element_type=f32)


def _norm_matmul(x, g, w_main, w_small, tm, tn):
    mp = x.shape[0]
    n = w_main.shape[1]
    return pl.pallas_call(
        _norm_matmul_kernel,
        grid=(mp // tm, n // tn),
        in_specs=[
            pl.BlockSpec((tm, D_MODEL), lambda i, j: (i, 0)),
            pl.BlockSpec((1, D_MODEL), lambda i, j: (0, 0)),
            pl.BlockSpec((D_MODEL, tn), lambda i, j: (0, j)),
            pl.BlockSpec((D_MODEL, LANES), lambda i, j: (0, 0)),
        ],
        out_specs=[
            pl.BlockSpec((tm, tn), lambda i, j: (i, j)),
            pl.BlockSpec((tm, LANES), lambda i, j: (i, 0)),
        ],
        out_shape=[jax.ShapeDtypeStruct((mp, n), f32), jax.ShapeDtypeStruct((mp, LANES), f32)],
        scratch_shapes=[pltpu.VMEM((tm, D_MODEL), bf16)],
        compiler_params=_cparams(("parallel", "arbitrary")),
        name="norm_matmul",
    )(x, g, w_main, w_small)


def _head_norm(h, g):
    outs = []
    for j in range(h.shape[1] // LANES):
        hj = h[:, j * LANES:(j + 1) * LANES]
        outs.append(hj * lax.rsqrt(jnp.mean(hj * hj, axis=-1, keepdims=True) + EPS))
    return jnp.concatenate(outs, axis=1) * g


def _gla_log_decay(sm, gw_ref, gb_ref):
    xg = jnp.dot(sm.astype(bf16), gw_ref[...], preferred_element_type=f32) + gb_ref[...]
    return _log_sigmoid(xg) * (1.0 / GLA_TAU)


def _prompt_mix_kernel(q_ref, k_ref, v_ref, mo_ref, gqk_ref, gv_ref, go_ref, pu_ref, sm_ref,
                       ifb_ref, mlg_ref, gw_ref, gb_ref, glag_ref, pw_ref, ps_ref,
                       br_ref, c_out, n_out, m_out, s_out, buf_out,
                       c_s, n_s, m_s, s_s, ext_s, *, tt, chunk, sub, n_t):
    t_idx = pl.program_id(1)
    L = chunk

    @pl.when(t_idx == 0)
    def _():
        c_s[...] = jnp.zeros_like(c_s)
        n_s[...] = jnp.zeros_like(n_s)
        m_s[...] = jnp.zeros_like(m_s)
        s_s[...] = jnp.zeros_like(s_s)
        ext_s[0:16, :] = jnp.zeros((16, POOL_W), f32)

    tri = _lower_tri(L)
    tri_b = tri.astype(bf16)
    ifb = ifb_ref[...]
    k_scale = ML_DH ** -0.5
    q_scale = GLA_DK ** -0.5

    def chunk_body(c, carry):
        r0 = pl.multiple_of(c * L, L)
        rows = pl.ds(r0, L)
        sm = sm_ref[rows, :]

        y0 = sm + ifb
        bc = _cumsum_rows(tri_b, _log_sigmoid(y0))
        y0t = y0.T
        bct = bc.T
        hm = []
        for h in range(ML_HEADS):
            hs = slice(h * ML_DH, (h + 1) * ML_DH)
            q = q_ref[rows, hs]
            k = k_ref[rows, hs] * k_scale
            v = v_ref[rows, hs]
            qb, kb, vb = q.astype(bf16), k.astype(bf16), v.astype(bf16)
            bcol = bc[:, S_MF + h:S_MF + h + 1]
            icol = y0[:, S_MI + h:S_MI + h + 1]
            brow = bct[S_MF + h:S_MF + h + 1, :]
            irow = y0t[S_MI + h:S_MI + h + 1, :]
            m = m_s[h:h + 1, 0:1]
            cmat = c_s[h]
            nrow = n_s[h:h + 1, :]
            dm = jnp.where(tri, bcol - brow + irow, NEG)
            inter = bcol + m
            m_t = jnp.maximum(inter, jnp.max(dm, axis=1, keepdims=True))
            w_intra = jnp.exp(dm - m_t)
            w_inter = jnp.exp(inter - m_t)
            s = lax.dot_general(qb, kb, _NT, preferred_element_type=f32) * w_intra
            num = (jnp.dot(s.astype(bf16), vb, preferred_element_type=f32)
                   + w_inter * lax.dot_general(qb, cmat.astype(bf16), _NT, preferred_element_type=f32))
            den = (jnp.sum(s, axis=1, keepdims=True)
                   + w_inter * jnp.sum(q * nrow, axis=1, keepdims=True))
            hm.append(num / jnp.maximum(jnp.abs(den), jnp.exp(-m_t)))
            b_last = bcol[L - 1:L, :]
            g = b_last - bcol + icol
            m_new = jnp.maximum(b_last + m, jnp.max(g, axis=0, keepdims=True))
            w_s = jnp.exp(g - m_new)
            w_c = jnp.exp(b_last + m - m_new)
            vwt = (v * w_s).T.astype(bf16)
            c_s[h] = w_c * cmat + jnp.dot(vwt, kb, preferred_element_type=f32)
            n_s[h:h + 1, :] = w_c * nrow + jnp.sum(w_s * k, axis=0, keepdims=True)
            m_s[h:h + 1, :] = jnp.broadcast_to(m_new, (1, LANES))
        y_ml = (_head_norm(jnp.concatenate(hm, axis=1), mlg_ref[...])
                * _sigmoid(mo_ref[rows, :]))
        br_ref[rows, 0:ML_W] = y_ml.astype(br_ref.dtype)

        log_a = _gla_log_decay(sm, gw_ref, gb_ref)
        b = _cumsum_rows(tri_b, log_a)
        q2 = gqk_ref[rows, 0:GLA_KW] * q_scale
        k2 = gqk_ref[rows, GLA_KW:2 * GLA_KW]
        gv = gv_ref[rows, :].astype(bf16)
        qe_chunk = (q2 * jnp.exp(b)).astype(bf16)
        b_last = b[L - 1:L, :]
        kdt = (k2 * jnp.exp(b_last - b)).T.astype(bf16)
        decay_col = jnp.exp(b.T[:, L - 1:L])
        o_blocks = []
        for blk in range(L // sub):
            s0, s1 = blk * sub, (blk + 1) * sub
            b_ref = b[s0 - 1:s0, :] if blk > 0 else jnp.zeros((1, GLA_KW), f32)
            qe = (q2[s0:s1] * jnp.exp(b[s0:s1] - b_ref)).astype(bf16)
            ke = (k2[0:s1] * jnp.exp(b_ref - b[0:s1])).astype(bf16)
            mask = (lax.broadcasted_iota(jnp.int32, (sub, s1), 1)
                    <= lax.broadcasted_iota(jnp.int32, (sub, s1), 0) + s0)
            o_heads = []
            for h in range(GLA_HEADS):
                ks = slice(h * GLA_DK, (h + 1) * GLA_DK)
                vs = slice(h * GLA_DV, (h + 1) * GLA_DV)
                a = lax.dot_general(qe[:, ks], ke[:, ks], _NT, preferred_element_type=f32)
                a = jnp.where(mask, a, 0.0)
                o = (jnp.dot(a.astype(bf16), gv[0:s1, vs], preferred_element_type=f32)
                     + jnp.dot(qe_chunk[s0:s1, ks], s_s[h].astype(bf16), preferred_element_type=f32))
                o_heads.append(o)
            o_blocks.append(jnp.concatenate(o_heads, axis=1))
        og = jnp.concatenate(o_blocks, axis=0)
        for h in range(GLA_HEADS):
            ks = slice(h * GLA_DK, (h + 1) * GLA_DK)
            vs = slice(h * GLA_DV, (h + 1) * GLA_DV)
            s_s[h] = (decay_col[ks, :] * s_s[h]
                      + jnp.dot(kdt[ks, :], gv[:, vs], preferred_element_type=f32))
        go = go_ref[rows, :]
        y_gla = _head_norm(og, glag_ref[...]) * (go * _sigmoid(go))
        br_ref[rows, ML_W:ML_W + GLA_VW] = y_gla.astype(br_ref.dtype)
        return carry

    lax.fori_loop(0, tt // L, chunk_body, 0)

    ext_s[16:16 + tt, :] = pu_ref[...]
    pos = t_idx * tt + lax.broadcasted_iota(jnp.int32, (tt, 1), 0)
    for g, w in enumerate(POOL_WINDOWS):
        gs = slice(g * POOL_GDIM, (g + 1) * POOL_GDIM)
        u = ext_s[16:16 + tt, gs]
        acc = u
        for j in range(1, w):
            acc = acc + ext_s[16 - j:16 - j + tt, gs]
        cnt = jnp.minimum(pos + 1, w).astype(f32)
        d = acc / cnt - u
        y = jnp.dot(d.astype(bf16), pw_ref[g], preferred_element_type=f32) * ps_ref[:, gs]
        br_ref[:, ML_W + GLA_VW + g * POOL_GDIM:ML_W + GLA_VW + (g + 1) * POOL_GDIM] = y.astype(br_ref.dtype)
    ext_s[0:16, :] = ext_s[tt:tt + 16, :]

    @pl.when(t_idx == n_t - 1)
    def _():
        c_out[0] = c_s[...]
        n_out[0] = n_s[0:ML_HEADS, :]
        m_out[0] = m_s[...]
        s_out[0] = s_s[...]
        buf_out[0] = ext_s[1:16, :]


def _prompt_mix(z, zs, lw, batch, seq, mp, tt, chunk, sub):
    n_t = seq // tt
    row = lambda b, t: b * n_t + t

    def zspec(col, width):
        blk = col // width
        return pl.BlockSpec((tt, width), lambda b, t: (row(b, t), blk))

    def full(shape):
        nd = len(shape)
        return pl.BlockSpec(shape, lambda b, t: (0,) * nd)

    kern = functools.partial(_prompt_mix_kernel, tt=tt, chunk=chunk, sub=sub, n_t=n_t)
    return pl.pallas_call(
        kern,
        grid=(batch, n_t),
        in_specs=[
            zspec(C_MQ, 512), zspec(C_MK, 512), zspec(C_MV, 512), zspec(C_MO, 512),
            zspec(C_GQ, 512), zspec(C_GV, 512), zspec(C_GO, 512), zspec(C_PU, 512),
            pl.BlockSpec((tt, LANES), lambda b, t: (row(b, t), 0)),
            full((1, LANES)), full((1, ML_W)), full((LANES, GLA_KW)), full((1, GLA_KW)),
            full((1, GLA_VW)), full((len(POOL_WINDOWS), POOL_GDIM, POOL_GDIM)), full((1, POOL_W)),
        ],
        out_specs=[
            pl.BlockSpec((tt, N_BRANCH * ML_W), lambda b, t: (row(b, t), 0)),
            pl.BlockSpec((1, ML_HEADS, ML_DH, ML_DH), lambda b, t: (b, 0, 0, 0)),
            pl.BlockSpec((1, ML_HEADS, ML_DH), lambda b, t: (b, 0, 0)),
            pl.BlockSpec((1, 8, LANES), lambda b, t: (b, 0, 0)),
            pl.BlockSpec((1, GLA_HEADS, GLA_DK, GLA_DV), lambda b, t: (b, 0, 0, 0)),
            pl.BlockSpec((1, POOL_BUF, POOL_W), lambda b, t: (b, 0, 0)),
        ],
        out_shape=[
            jax.ShapeDtypeStruct((mp, N_BRANCH * ML_W), bf16),
            jax.ShapeDtypeStruct((batch, ML_HEADS, ML_DH, ML_DH), f32),
            jax.ShapeDtypeStruct((batch, ML_HEADS, ML_DH), f32),
            jax.ShapeDtypeStruct((batch, 8, LANES), f32),
            jax.ShapeDtypeStruct((batch, GLA_HEADS, GLA_DK, GLA_DV), f32),
            jax.ShapeDtypeStruct((batch, POOL_BUF, POOL_W), f32),
        ],
        scratch_shapes=[
            pltpu.VMEM((ML_HEADS, ML_DH, ML_DH), f32),
            pltpu.VMEM((8, LANES), f32),
            pltpu.VMEM((8, LANES), f32),
            pltpu.VMEM((GLA_HEADS, GLA_DK, GLA_DV), f32),
            pltpu.VMEM((tt + 16, POOL_W), f32),
        ],
        compiler_params=_cparams(("parallel", "arbitrary")),
        name="prompt_mix",
    )(z, z, z, z, z, z, z, z, zs,
      lw["ifb"], lw["ml_g"], lw["gla_gw"], lw["gla_gb"], lw["gla_g"], lw["pool_w"], lw["pool_scale"])


def _sample_mix_kernel(br_in, q_ref, k_ref, v_ref, mo_ref, gqk_ref, gv_ref, go_ref, pu_ref, sm_ref,
                       c_in, n_in, m_in, s_in, buf_in,
                       ifb_ref, mlg_ref, gw_ref, gb_ref, glag_ref, pw_ref, ps_ref,
                       br_ref, c_out, n_out, m_out, s_out, buf_out, *, sb):
    del br_in
    sm = sm_ref[...]
    eye = (lax.broadcasted_iota(jnp.int32, (LANES, LANES), 0)
           == lax.broadcasted_iota(jnp.int32, (LANES, LANES), 1)).astype(f32)

    def to_cols(x):
        parts = [lax.dot_general(eye, x[:, j * LANES:(j + 1) * LANES], _NT,
                                 preferred_element_type=f32, precision=lax.Precision.HIGHEST)
                 for j in range(x.shape[1] // LANES)]
        return jnp.concatenate(parts, axis=0)

    y0 = sm + ifb_ref[...]
    logf_all = _log_sigmoid(y0)
    k_scale = ML_DH ** -0.5
    m_all = m_in[...]
    hm, m_new_cols = [], []
    for h in range(ML_HEADS):
        hs = slice(h * ML_DH, (h + 1) * ML_DH)
        q = q_ref[:, hs]
        k = k_ref[:, hs] * k_scale
        v = v_ref[:, hs]
        i_pre = y0[:, S_MI + h:S_MI + h + 1]
        logf = logf_all[:, S_MF + h:S_MF + h + 1]
        m = m_all[:, h:h + 1]
        inter = logf + m
        m_t = jnp.maximum(inter, i_pre)
        w_intra = jnp.exp(i_pre - m_t)
        w_inter = jnp.exp(inter - m_t)
        s = jnp.sum(q * k, axis=1, keepdims=True) * w_intra
        qb = q.astype(bf16)
        n_h = n_in[:, h, :]
        cq = jnp.concatenate(
            [lax.dot_general(qb, c_in[j, h].astype(bf16), _NT, preferred_element_type=f32)[j:j + 1, :]
             for j in range(sb)], axis=0)
        num = s * v + w_inter * cq
        den = s + w_inter * jnp.sum(n_h * q, axis=1, keepdims=True)
        hm.append(num / jnp.maximum(jnp.abs(den), jnp.exp(-m_t)))
        m_new = m_t
        w_s = w_intra
        w_c = w_inter
        n_out[:, h, :] = w_c * n_h + w_s * k
        m_new_cols.append(m_new)
        vw_cols = to_cols(v * w_s)
        for j in range(sb):
            c_out[j, h] = w_c[j:j + 1, :] * c_in[j, h] + vw_cols[:, j:j + 1] * k[j:j + 1, :]
    lane = lax.broadcasted_iota(jnp.int32, (sb, LANES), 1)
    m_pack = jnp.zeros((sb, LANES), f32)
    for h in range(ML_HEADS):
        m_pack = jnp.where(lane == h, m_new_cols[h], m_pack)
    m_out[...] = m_pack
    y_ml = _head_norm(jnp.concatenate(hm, axis=1), mlg_ref[...]) * _sigmoid(mo_ref[...])
    br_ref[:, 0:ML_W] = y_ml.astype(br_ref.dtype)

    log_a = _gla_log_decay(sm, gw_ref, gb_ref)
    decay = jnp.exp(log_a)
    q2 = gqk_ref[:, 0:GLA_KW] * (GLA_DK ** -0.5)
    k2 = gqk_ref[:, GLA_KW:2 * GLA_KW]
    gv = gv_ref[...]
    qe = (q2 * decay).astype(bf16)
    qk = q2 * k2
    k_cols = to_cols(k2)
    decay_cols = to_cols(decay)
    og = []
    for h in range(GLA_HEADS):
        ks = slice(h * GLA_DK, (h + 1) * GLA_DK)
        vs = slice(h * GLA_DV, (h + 1) * GLA_DV)
        a = jnp.sum(qk[:, ks], axis=1, keepdims=True)
        inter = jnp.concatenate(
            [jnp.dot(qe[:, ks], s_in[j, h].astype(bf16), preferred_element_type=f32)[j:j + 1, :]
             for j in range(sb)], axis=0)
        og.append(a * gv[:, vs] + inter)
        for j in range(sb):
            s_out[j, h] = (decay_cols[ks, j:j + 1] * s_in[j, h]
                           + k_cols[ks, j:j + 1] * gv[j:j + 1, vs])
    go = go_ref[...]
    y_gla = _head_norm(jnp.concatenate(og, axis=1), glag_ref[...]) * (go * _sigmoid(go))
    br_ref[:, ML_W:ML_W + GLA_VW] = y_gla.astype(br_ref.dtype)

    u = pu_ref[...]
    rowi = lax.broadcasted_iota(jnp.int32, (POOL_BUF + 1, POOL_GDIM), 0)
    d_rows = []
    for j in range(sb):
        ext = jnp.concatenate([buf_in[j], u[j:j + 1, :]], axis=0)
        buf_out[j] = ext[1:POOL_BUF + 1, :]
        parts = []
        for g, w in enumerate(POOL_WINDOWS):
            gs = slice(g * POOL_GDIM, (g + 1) * POOL_GDIM)
            win = jnp.sum(jnp.where(rowi >= POOL_BUF + 1 - w, ext[:, gs], 0.0), axis=0, keepdims=True)
            parts.append(win / float(w) - u[j:j + 1, gs])
        d_rows.append(jnp.concatenate(parts, axis=1))
    d = jnp.concatenate(d_rows, axis=0)
    for g in range(len(POOL_WINDOWS)):
        gs = slice(g * POOL_GDIM, (g + 1) * POOL_GDIM)
        y = jnp.dot(d[:, gs].astype(bf16), pw_ref[g], preferred_element_type=f32) * ps_ref[:, gs]
        br_ref[:, ML_W + GLA_VW + g * POOL_GDIM:ML_W + GLA_VW + (g + 1) * POOL_GDIM] = y.astype(br_ref.dtype)


def _sample_mix(br, z, zs, st, lw, row0, dec, sb):
    base = row0 // sb

    def zspec(col, width):
        blk = col // width
        return pl.BlockSpec((sb, width), lambda i: (base + i, blk))

    def full(shape):
        nd = len(shape)
        return pl.BlockSpec(shape, lambda i: (0,) * nd)

    def st_spec(shape):
        nd = len(shape)
        return pl.BlockSpec((sb,) + shape, lambda i: (i,) + (0,) * nd)

    c0, n0, m0, s0, buf0 = st
    st_specs = [st_spec((ML_HEADS, ML_DH, ML_DH)), st_spec((ML_HEADS, ML_DH)), st_spec((LANES,)),
                st_spec((GLA_HEADS, GLA_DK, GLA_DV)), st_spec((POOL_BUF, POOL_W))]
    br_spec = pl.BlockSpec((sb, N_BRANCH * ML_W), lambda i: (base + i, 0))
    return pl.pallas_call(
        functools.partial(_sample_mix_kernel, sb=sb),
        grid=(dec // sb,),
        in_specs=[pl.BlockSpec(memory_space=pl.ANY),
                  zspec(C_MQ, 512), zspec(C_MK, 512), zspec(C_MV, 512), zspec(C_MO, 512),
                  zspec(C_GQ, 512), zspec(C_GV, 512), zspec(C_GO, 512), zspec(C_PU, 512),
                  pl.BlockSpec((sb, LANES), lambda i: (base + i, 0))]
                 + st_specs
                 + [full((1, LANES)), full((1, ML_W)), full((LANES, GLA_KW)), full((1, GLA_KW)),
                    full((1, GLA_VW)), full((len(POOL_WINDOWS), POOL_GDIM, POOL_GDIM)), full((1, POOL_W))],
        out_specs=[br_spec] + st_specs,
        out_shape=[jax.ShapeDtypeStruct(br.shape, br.dtype),
                   jax.ShapeDtypeStruct(c0.shape, f32), jax.ShapeDtypeStruct(n0.shape, f32),
                   jax.ShapeDtypeStruct(m0.shape, f32), jax.ShapeDtypeStruct(s0.shape, f32),
                   jax.ShapeDtypeStruct(buf0.shape, f32)],
        input_output_aliases={0: 0},
        compiler_params=_cparams(("parallel",)),
        name="sample_mix",
    )(br, z, z, z, z, z, z, z, z, zs, c0, n0, m0, s0, buf0,
      lw["ifb"], lw["ml_g"], lw["gla_gw"], lw["gla_gb"], lw["gla_g"], lw["pool_w"], lw["pool_scale"])


def _merge_kernel(x_ref, br_ref, g0_ref, g1_ref, g2_ref, wb_ref, wo_ref, o_ref):
    mixed = None
    for n, g_ref in enumerate((g0_ref, g1_ref, g2_ref)):
        proj = jnp.dot(br_ref[:, n * ML_W:(n + 1) * ML_W], wb_ref[n], preferred_element_type=f32)
        term = _sigmoid(g_ref[...]) * proj
        mixed = term if mixed is None else mixed + term
    o_ref[...] = x_ref[...] + jnp.dot(mixed.astype(bf16), wo_ref[...], preferred_element_type=f32)


def _merge(x, br, z, w_branch, w_out, tm):
    mp = x.shape[0]
    gate_specs = [pl.BlockSpec((tm, D_MODEL), functools.partial(lambda i, n: (i, C_GATES // D_MODEL + n), n=n))
                  for n in range(N_BRANCH)]
    return pl.pallas_call(
        _merge_kernel,
        grid=(mp // tm,),
        in_specs=[
            pl.BlockSpec((tm, D_MODEL), lambda i: (i, 0)),
            pl.BlockSpec((tm, N_BRANCH * ML_W), lambda i: (i, 0)),
            *gate_specs,
            pl.BlockSpec((N_BRANCH, ML_W, D_MODEL), lambda i: (0, 0, 0)),
            pl.BlockSpec((D_MODEL, D_MODEL), lambda i: (0, 0)),
        ],
        out_specs=pl.BlockSpec((tm, D_MODEL), lambda i: (i, 0)),
        out_shape=jax.ShapeDtypeStruct((mp, D_MODEL), f32),
        compiler_params=_cparams(("parallel",)),
        name="merge",
    )(x, br, z, z, z, w_branch, w_out)


def _router_kernel(x_ref, g_ref, wr_ref, comb_ref):
    h = _rms(x_ref[...], g_ref[...]).astype(bf16)
    logits = jnp.dot(h, wr_ref[...], preferred_element_type=f32)
    lane = lax.broadcasted_iota(jnp.int32, logits.shape, 1)
    valid = lane < N_EXPERTS
    logits = jnp.where(valid, logits, NEG)
    mx = jnp.max(logits, axis=1, keepdims=True)
    e = jnp.where(valid, jnp.exp(logits - mx), 0.0)
    probs = e / jnp.sum(e, axis=1, keepdims=True)
    p1 = jnp.max(probs, axis=1, keepdims=True)
    i1 = jnp.min(jnp.where(probs == p1, lane, LANES), axis=1, keepdims=True)
    rest = jnp.where((lane == i1) | ~valid, -1.0, probs)
    p2 = jnp.max(rest, axis=1, keepdims=True)
    i2 = jnp.min(jnp.where(rest == p2, lane, LANES), axis=1, keepdims=True)
    tot = p1 + p2
    comb_ref[...] = jnp.where(lane == i1, p1 / tot, 0.0) + jnp.where(lane == i2, p2 / tot, 0.0)


def _router(x, g, wr, tm):
    mp = x.shape[0]
    return pl.pallas_call(
        _router_kernel,
        grid=(mp // tm,),
        in_specs=[pl.BlockSpec((tm, D_MODEL), lambda i: (i, 0)),
                  pl.BlockSpec((1, D_MODEL), lambda i: (0, 0)),
                  pl.BlockSpec((D_MODEL, LANES), lambda i: (0, 0))],
        out_specs=pl.BlockSpec((tm, LANES), lambda i: (i, 0)),
        out_shape=jax.ShapeDtypeStruct((mp, LANES), f32),
        compiler_params=_cparams(("parallel",)),
        name="router",
    )(x, g, wr)


def _ffn_kernel(x_ref, g_ref, comb_ref, wg_ref, wu_ref, wd_ref, o_ref, h_ref, acc_ref, *, n_e, n_f):
    e, f = pl.program_id(1), pl.program_id(2)

    @pl.when((e == 0) & (f == 0))
    def _():
        h_ref[...] = _rms(x_ref[...], g_ref[...]).astype(bf16)
        acc_ref[...] = x_ref[...]

    h = h_ref[...]
    a = jnp.dot(h, wg_ref[0], preferred_element_type=f32)
    a = (a * _sigmoid(a)) * jnp.dot(h, wu_ref[0], preferred_element_type=f32)
    y = jnp.dot(a.astype(bf16), wd_ref[0], preferred_element_type=f32)
    if comb_ref is not None:
        lane = lax.broadcasted_iota(jnp.int32, comb_ref.shape, 1)
        y = y * jnp.sum(jnp.where(lane == e, comb_ref[...], 0.0), axis=1, keepdims=True)
    acc_ref[...] += y

    @pl.when((e == n_e - 1) & (f == n_f - 1))
    def _():
        o_ref[...] = acc_ref[...]


def _ffn(x, g, comb, wg, wu, wd, tm, tf):
    mp = x.shape[0]
    n_e, _, d_ff = wg.shape
    n_f = d_ff // tf
    kern = functools.partial(_ffn_kernel, n_e=n_e, n_f=n_f)
    in_specs = [pl.BlockSpec((tm, D_MODEL), lambda i, e, f: (i, 0)),
                pl.BlockSpec((1, D_MODEL), lambda i, e, f: (0, 0))]
    args = [x, g]
    if comb is None:
        kern_fn = lambda x_ref, g_ref, *rest: kern(x_ref, g_ref, None, *rest)
    else:
        kern_fn = kern
        in_specs.append(pl.BlockSpec((tm, LANES), lambda i, e, f: (i, 0)))
        args.append(comb)
    in_specs += [pl.BlockSpec((1, D_MODEL, tf), lambda i, e, f: (e, 0, f)),
                 pl.BlockSpec((1, D_MODEL, tf), lambda i, e, f: (e, 0, f)),
                 pl.BlockSpec((1, tf, D_MODEL), lambda i, e, f: (e, f, 0))]
    return pl.pallas_call(
        kern_fn,
        grid=(mp // tm, n_e, n_f),
        in_specs=in_specs,
        out_specs=pl.BlockSpec((tm, D_MODEL), lambda i, e, f: (i, 0)),
        out_shape=jax.ShapeDtypeStruct((mp, D_MODEL), f32),
        scratch_shapes=[pltpu.VMEM((tm, D_MODEL), bf16), pltpu.VMEM((tm, D_MODEL), f32)],
        compiler_params=_cparams(("parallel", "arbitrary", "arbitrary")),
        name="ffn" if comb is None else "moe",
    )(*args, wg, wu, wd)


def _final_norm_kernel(x_ref, g_ref, o_ref):
    o_ref[...] = _rms(x_ref[...], g_ref[...])


def _final_norm(x, g, tm):
    mp = x.shape[0]
    return pl.pallas_call(
        _final_norm_kernel,
        grid=(mp // tm,),
        in_specs=[pl.BlockSpec((tm, D_MODEL), lambda i: (i, 0)),
                  pl.BlockSpec((1, D_MODEL), lambda i: (0, 0))],
        out_specs=pl.BlockSpec((tm, D_MODEL), lambda i: (i, 0)),
        out_shape=jax.ShapeDtypeStruct((mp, D_MODEL), f32),
        compiler_params=_cparams(("parallel",)),
        name="final_norm",
    )(x, g)


def _regroup_w_in(w):
    o_mi = 3 * ML_W
    o_mo = o_mi + 2 * ML_HEADS
    o_gq = o_mo + ML_W
    o_glr = o_gq + 2 * GLA_KW + GLA_VW
    o_go = o_glr + GLA_RANK
    main = jnp.concatenate([w[:, :o_mi], w[:, o_mo:o_glr], w[:, o_go:]], axis=1)
    small = jnp.concatenate([w[:, o_mi:o_mo], w[:, o_glr:o_go],
                             jnp.zeros((w.shape[0], LANES - 2 * ML_HEADS - GLA_RANK), w.dtype)], axis=1)
    return main.astype(bf16), small.astype(bf16)


def _layer_weights(l, norm1_g, w_in, if_bias, ml_g, gla_gw, gla_gb, gla_g, pool_w, pool_scale,
                   w_branch, w_out):
    w_main, w_small = _regroup_w_in(w_in[l])
    ifb = jnp.zeros((1, LANES), f32)
    ifb = ifb.at[0, S_MI:S_MI + ML_HEADS].set(if_bias[l, 0]).at[0, S_MF:S_MF + ML_HEADS].set(if_bias[l, 1])
    gw = jnp.zeros((LANES, GLA_KW), f32).at[S_GLR:S_GLR + GLA_RANK].set(gla_gw[l]).astype(bf16)
    return dict(
        norm1_g=norm1_g[l][None], w_main=w_main, w_small=w_small, ifb=ifb, ml_g=ml_g[l][None],
        gla_gw=gw, gla_gb=gla_gb[l][None], gla_g=gla_g[l][None], pool_w=pool_w[l].astype(bf16),
        pool_scale=pool_scale[l][None], w_branch=w_branch[l].astype(bf16), w_out=w_out[l].astype(bf16))


def _forward(x_prompt, x_sample, state_mlstm_C, state_mlstm_n, state_mlstm_m, state_gla_S,
             state_pool_buf, norm1_g, w_in, mlstm_if_bias, mlstm_norm_g, gla_gate_w, gla_gate_b,
             gla_norm_g, pool_w, pool_scale, w_branch, w_out, norm2_g, ffn_wg, ffn_wu, ffn_wd,
             router_w, moe_wg, moe_wu, moe_wd, final_norm_g, *, tm, tn, tt, chunk, sub, sb, tf_dense, tf_moe):
    batch, seq, _ = x_prompt.shape
    dec = x_sample.shape[0]
    depth = w_in.shape[0]
    m_prompt = batch * seq
    m_all = m_prompt + dec
    mp = -(-m_all // tm) * tm
    x = jnp.concatenate([x_prompt.reshape(m_prompt, D_MODEL), x_sample.reshape(dec, D_MODEL),
                         jnp.zeros((mp - m_all, D_MODEL), f32)], axis=0)
    outs = [[] for _ in range(10)]
    for l in range(depth):
        lw = _layer_weights(l, norm1_g, w_in, mlstm_if_bias, mlstm_norm_g, gla_gate_w, gla_gate_b,
                            gla_norm_g, pool_w, pool_scale, w_branch, w_out)
        z, zs = _norm_matmul(x, lw["norm1_g"], lw["w_main"], lw["w_small"], tm, tn)
        br, c_p, n_p, m_p, s_p, buf_p = _prompt_mix(z, zs, lw, batch, seq, mp, tt, chunk, sub)
        m_state = jnp.pad(state_mlstm_m[l], ((0, 0), (0, LANES - ML_HEADS)))
        st = (state_mlstm_C[l], state_mlstm_n[l], m_state, state_gla_S[l], state_pool_buf[l])
        br, c_s, n_s, m_s, s_s, buf_s = _sample_mix(br, z, zs, st, lw, m_prompt, dec, sb)
        x = _merge(x, br, z, lw["w_branch"], lw["w_out"], tm)
        j = l // 2
        if l % 2 == 0:
            x = _ffn(x, norm2_g[l][None], None, ffn_wg[j][None].astype(bf16), ffn_wu[j][None].astype(bf16),
                     ffn_wd[j][None].astype(bf16), tm, tf_dense)
        else:
            wr = jnp.pad(router_w[j], ((0, 0), (0, LANES - N_EXPERTS))).astype(bf16)
            comb = _router(x, norm2_g[l][None], wr, tm)
            x = _ffn(x, norm2_g[l][None], comb, moe_wg[j].astype(bf16), moe_wu[j].astype(bf16),
                     moe_wd[j].astype(bf16), tm, tf_moe)
        for lst, val in zip(outs, (c_p, c_s, n_p, n_s, m_p[:, :ML_HEADS, 0], m_s[:, :ML_HEADS],
                                   s_p, s_s, buf_p, buf_s)):
            lst.append(val)
    y = _final_norm(x, final_norm_g[None], tm)
    y_prompt = y[:m_prompt].reshape(batch, seq, D_MODEL)
    y_sample = y[m_prompt:m_all].reshape(dec, 1, D_MODEL)
    return (y_prompt, y_sample) + tuple(jnp.stack(o) for o in outs)


def kernel(x_prompt, x_sample, state_mlstm_C, state_mlstm_n, state_mlstm_m, state_gla_S, state_pool_buf, norm1_g, w_in, mlstm_if_bias, mlstm_norm_g, gla_gate_w, gla_gate_b, gla_norm_g, pool_w, pool_scale, w_branch, w_out, norm2_g, ffn_wg, ffn_wu, ffn_wd, router_w, moe_wg, moe_wu, moe_wd, final_norm_g):
    return _forward(x_prompt, x_sample, state_mlstm_C, state_mlstm_n, state_mlstm_m, state_gla_S,
                    state_pool_buf, norm1_g, w_in, mlstm_if_bias, mlstm_norm_g, gla_gate_w, gla_gate_b,
                    gla_norm_g, pool_w, pool_scale, w_branch, w_out, norm2_g, ffn_wg, ffn_wu, ffn_wd,
                    router_w, moe_wg, moe_wu, moe_wd, final_norm_g,
                    tm=512, tn=3584, tt=512, chunk=64, sub=32, sb=8, tf_dense=1408, tf_moe=1408)
```

```python
import functools

import jax
import jax.numpy as jnp
from jax import lax
from jax.experimental import pallas as pl
from jax.experimental.pallas import tpu as pltpu

f32 = jnp.float32
bf16 = jnp.bfloat16

D_MODEL = 1024
ML_HEADS, ML_DH = 4, 128
ML_W = ML_HEADS * ML_DH
GLA_HEADS, GLA_DK, GLA_DV = 4, 64, 128
GLA_KW, GLA_VW = GLA_HEADS * GLA_DK, GLA_HEADS * GLA_DV
GLA_RANK = 16
GLA_TAU = 16.0
POOL_GDIM = 128
POOL_WINDOWS = (2, 4, 8, 16)
POOL_W = POOL_GDIM * len(POOL_WINDOWS)
POOL_BUF = 15
N_BRANCH = 3
N_EXPERTS = 8
EPS = 1e-6
NEG = -1e30
LANES = 128

C_MQ, C_MK, C_MV, C_MO = 0, 512, 1024, 1536
C_GQ, C_GK, C_GV, C_GO, C_PU, C_GATES = 2048, 2304, 2560, 3072, 3584, 4096
Z_MAIN = C_GATES + N_BRANCH * D_MODEL
S_MI, S_MF, S_GLR = 0, 4, 8

VMEM_LIMIT = 56 * 1024 * 1024

_NT = (((1,), (1,)), ((), ()))


def _cparams(sem):
    return pltpu.CompilerParams(dimension_semantics=sem, vmem_limit_bytes=VMEM_LIMIT)


def _log_sigmoid(x):
    return jnp.minimum(x, 0.0) - jnp.log1p(jnp.exp(-jnp.abs(x)))


def _sigmoid(x):
    return 1.0 / (1.0 + jnp.exp(-x))


def _rms(x, g):
    ms = jnp.mean(x * x, axis=-1, keepdims=True)
    return x * lax.rsqrt(ms + EPS) * g


def _lower_tri(n):
    r = lax.broadcasted_iota(jnp.int32, (n, n), 0)
    c = lax.broadcasted_iota(jnp.int32, (n, n), 1)
    return c <= r


def _cumsum_rows(tri_bf16, a):
    a1 = a.astype(bf16)
    r = a - a1.astype(f32)
    a2 = r.astype(bf16)
    a3 = (r - a2.astype(f32)).astype(bf16)
    d = lambda y: jnp.dot(tri_bf16, y, preferred_element_type=f32)
    return d(a1) + d(a2) + d(a3)


def _norm_matmul_kernel(x_ref, g_ref, w_ref, ws_ref, z_ref, zs_ref, h_ref):
    @pl.when(pl.program_id(1) == 0)
    def _():
        h = _rms(x_ref[...], g_ref[...]).astype(bf16)
        h_ref[...] = h
        zs_ref[...] = jnp.dot(h, ws_ref[...], preferred_element_type=f32)

    z_ref[...] = jnp.dot(h_ref[...], w_ref[...], preferred_element_type=f32)


def _norm_matmul(x, g, w_main, w_small, tm, tn):
    mp = x.shape[0]
    n = w_main.shape[1]
    return pl.pallas_call(
        _norm_matmul_kernel,
        grid=(mp // tm, n // tn),
        in_specs=[
            pl.BlockSpec((tm, D_MODEL), lambda i, j: (i, 0)),
            pl.BlockSpec((1, D_MODEL), lambda i, j: (0, 0)),
            pl.BlockSpec((D_MODEL, tn), lambda i, j: (0, j)),
            pl.BlockSpec((D_MODEL, LANES), lambda i, j: (0, 0)),
        ],
        out_specs=[
            pl.BlockSpec((tm, tn), lambda i, j: (i, j)),
            pl.BlockSpec((tm, LANES), lambda i, j: (i, 0)),
        ],
        out_shape=[jax.ShapeDtypeStruct((mp, n), f32), jax.ShapeDtypeStruct((mp, LANES), f32)],
        scratch_shapes=[pltpu.VMEM((tm, D_MODEL), bf16)],
        compiler_params=_cparams(("parallel", "arbitrary")),
        name="norm_matmul",
    )(x, g, w_main, w_small)


def _head_norm(h, g):
    outs = []
    for j in range(h.shape[1] // LANES):
        hj = h[:, j * LANES:(j + 1) * LANES]
        outs.append(hj * lax.rsqrt(jnp.mean(hj * hj, axis=-1, keepdims=True) + EPS))
    return jnp.concatenate(outs, axis=1) * g


def _gla_log_decay(sm, gw_ref, gb_ref):
    xg = jnp.dot(sm.astype(bf16), gw_ref[...], preferred_element_type=f32) + gb_ref[...]
    return _log_sigmoid(xg) * (1.0 / GLA_TAU)


def _prompt_mix_kernel(q_ref, k_ref, v_ref, mo_ref, gqk_ref, gv_ref, go_ref, pu_ref, sm_ref,
                       ifb_ref, mlg_ref, gw_ref, gb_ref, glag_ref, pw_ref, ps_ref,
                       br_ref, c_out, n_out, m_out, s_out, buf_out,
                       c_s, n_s, m_s, s_s, ext_s, *, tt, chunk, sub, n_t):
    t_idx = pl.program_id(1)
    L = chunk

    @pl.when(t_idx == 0)
    def _():
        c_s[...] = jnp.zeros_like(c_s)
        n_s[...] = jnp.zeros_like(n_s)
        m_s[...] = jnp.zeros_like(m_s)
        s_s[...] = jnp.zeros_like(s_s)
        ext_s[0:16, :] = jnp.zeros((16, POOL_W), f32)

    tri = _lower_tri(L)
    tri_b = tri.astype(bf16)
    ifb = ifb_ref[...]
    k_scale = ML_DH ** -0.5
    q_scale = GLA_DK ** -0.5

    def chunk_body(c, carry):
        r0 = pl.multiple_of(c * L, L)
        rows = pl.ds(r0, L)
        sm = sm_ref[rows, :]

        y0 = sm + ifb
        bc = _cumsum_rows(tri_b, _log_sigmoid(y0))
        y0t = y0.T
        bct = bc.T
        hm = []
        for h in range(ML_HEADS):
            hs = slice(h * ML_DH, (h + 1) * ML_DH)
            q = q_ref[rows, hs]
            k = k_ref[rows, hs] * k_scale
            v = v_ref[rows, hs]
            qb, kb, vb = q.astype(bf16), k.astype(bf16), v.astype(bf16)
            bcol = bc[:, S_MF + h:S_MF + h + 1]
            icol = y0[:, S_MI + h:S_MI + h + 1]
            brow = bct[S_MF + h:S_MF + h + 1, :]
            irow = y0t[S_MI + h:S_MI + h + 1, :]
            m = m_s[h:h + 1, 0:1]
            cmat = c_s[h]
            nrow = n_s[h:h + 1, :]
            dm = jnp.where(tri, bcol - brow + irow, NEG)
            inter = bcol + m
            m_t = jnp.maximum(inter, jnp.max(dm, axis=1, keepdims=True))
            w_intra = jnp.exp(dm - m_t)
            w_inter = jnp.exp(inter - m_t)
            s = lax.dot_general(qb, kb, _NT, preferred_element_type=f32) * w_intra
            num = (jnp.dot(s.astype(bf16), vb, preferred_element_type=f32)
                   + w_inter * lax.dot_general(qb, cmat.astype(bf16), _NT, preferred_element_type=f32))
            den = (jnp.sum(s, axis=1, keepdims=True)
                   + w_inter * jnp.sum(q * nrow, axis=1, keepdims=True))
            hm.append(num / jnp.maximum(jnp.abs(den), jnp.exp(-m_t)))
            b_last = bcol[L - 1:L, :]
            g = b_last - bcol + icol
            m_new = jnp.maximum(b_last + m, jnp.max(g, axis=0, keepdims=True))
            w_s = jnp.exp(g - m_new)
            w_c = jnp.exp(b_last + m - m_new)
            vwt = (v * w_s).T.astype(bf16)
            c_s[h] = w_c * cmat + jnp.dot(vwt, kb, preferred_element_type=f32)
            n_s[h:h + 1, :] = w_c * nrow + jnp.sum(w_s * k, axis=0, keepdims=True)
            m_s[h:h + 1, :] = jnp.broadcast_to(m_new, (1, LANES))
        y_ml = (_head_norm(jnp.concatenate(hm, axis=1), mlg_ref[...])
                * _sigmoid(mo_ref[rows, :]))
        br_ref[rows, 0:ML_W] = y_ml.astype(br_ref.dtype)

        log_a = _gla_log_decay(sm, gw_ref, gb_ref)
        b = _cumsum_rows(tri_b, log_a)
        q2 = gqk_ref[rows, 0:GLA_KW] * q_scale
        k2 = gqk_ref[rows, GLA_KW:2 * GLA_KW]
        gv = gv_ref[rows, :].astype(bf16)
        qe_chunk = (q2 * jnp.exp(b)).astype(bf16)
        b_last = b[L - 1:L, :]
        kdt = (k2 * jnp.exp(b_last - b)).T.astype(bf16)
        decay_col = jnp.exp(b.T[:, L - 1:L])
        o_blocks = []
        for blk in range(L // sub):
            s0, s1 = blk * sub, (blk + 1) * sub
            b_ref = b[s0 - 1:s0, :] if blk > 0 else jnp.zeros((1, GLA_KW), f32)
            qe = (q2[s0:s1] * jnp.exp(b[s0:s1] - b_ref)).astype(bf16)
            ke = (k2[0:s1] * jnp.exp(b_ref - b[0:s1])).astype(bf16)
            mask = (lax.broadcasted_iota(jnp.int32, (sub, s1), 1)
                    <= lax.broadcasted_iota(jnp.int32, (sub, s1), 0) + s0)
            o_heads = []
            for h in range(GLA_HEADS):
                ks = slice(h * GLA_DK, (h + 1) * GLA_DK)
                vs = slice(h * GLA_DV, (h + 1) * GLA_DV)
                a = lax.dot_general(qe[:, ks], ke[:, ks], _NT, preferred_element_type=f32)
                a = jnp.where(mask, a, 0.0)
                o = (jnp.dot(a.astype(bf16), gv[0:s1, vs], preferred_element_type=f32)
                     + jnp.dot(qe_chunk[s0:s1, ks], s_s[h].astype(bf16), preferred_element_type=f32))
                o_heads.append(o)
            o_blocks.append(jnp.concatenate(o_heads, axis=1))
        og = jnp.concatenate(o_blocks, axis=0)
        for h in range(GLA_HEADS):
            ks = slice(h * GLA_DK, (h + 1) * GLA_DK)
            vs = slice(h * GLA_DV, (h + 1) * GLA_DV)
            s_s[h] = (decay_col[ks, :] * s_s[h]
                      + jnp.dot(kdt[ks, :], gv[:, vs], preferred_element_type=f32))
        go = go_ref[rows, :]
        y_gla = _head_norm(og, glag_ref[...]) * (go * _sigmoid(go))
        br_ref[rows, ML_W:ML_W + GLA_VW] = y_gla.astype(br_ref.dtype)
        return carry

    lax.fori_loop(0, tt // L, chunk_body, 0)

    ext_s[16:16 + tt, :] = pu_ref[...]
    pos = t_idx * tt + lax.broadcasted_iota(jnp.int32, (tt, 1), 0)
    for g, w in enumerate(POOL_WINDOWS):
        gs = slice(g * POOL_GDIM, (g + 1) * POOL_GDIM)
        u = ext_s[16:16 + tt, gs]
        acc = u
        for j in range(1, w):
            acc = acc + ext_s[16 - j:16 - j + tt, gs]
        cnt = jnp.minimum(pos + 1, w).astype(f32)
        d = acc / cnt - u
        y = jnp.dot(d.astype(bf16), pw_ref[g], preferred_element_type=f32) * ps_ref[:, gs]
        br_ref[:, ML_W + GLA_VW + g * POOL_GDIM:ML_W + GLA_VW + (g + 1) * POOL_GDIM] = y.astype(br_ref.dtype)
    ext_s[0:16, :] = ext_s[tt:tt + 16, :]

    @pl.when(t_idx == n_t - 1)
    def _():
        c_out[0] = c_s[...]
        n_out[0] = n_s[0:ML_HEADS, :]
        m_out[0] = m_s[...]
        s_out[0] = s_s[...]
        buf_out[0] = ext_s[1:16, :]


def _prompt_mix(z, zs, lw, batch, seq, mp, tt, chunk, sub):
    n_t = seq // tt
    row = lambda b, t: b * n_t + t

    def zspec(col, width):
        blk = col // width
        return pl.BlockSpec((tt, width), lambda b, t: (row(b, t), blk))

    def full(shape):
        nd = len(shape)
        return pl.BlockSpec(shape, lambda b, t: (0,) * nd)

    kern = functools.partial(_prompt_mix_kernel, tt=tt, chunk=chunk, sub=sub, n_t=n_t)
    return pl.pallas_call(
        kern,
        grid=(batch, n_t),
        in_specs=[
            zspec(C_MQ, 512), zspec(C_MK, 512), zspec(C_MV, 512), zspec(C_MO, 512),
            zspec(C_GQ, 512), zspec(C_GV, 512), zspec(C_GO, 512), zspec(C_PU, 512),
            pl.BlockSpec((tt, LANES), lambda b, t: (row(b, t), 0)),
            full((1, LANES)), full((1, ML_W)), full((LANES, GLA_KW)), full((1, GLA_KW)),
            full((1, GLA_VW)), full((len(POOL_WINDOWS), POOL_GDIM, POOL_GDIM)), full((1, POOL_W)),
        ],
        out_specs=[
            pl.BlockSpec((tt, N_BRANCH * ML_W), lambda b, t: (row(b, t), 0)),
            pl.BlockSpec((1, ML_HEADS, ML_DH, ML_DH), lambda b, t: (b, 0, 0, 0)),
            pl.BlockSpec((1, ML_HEADS, ML_DH), lambda b, t: (b, 0, 0)),
            pl.BlockSpec((1, 8, LANES), lambda b, t: (b, 0, 0)),
            pl.BlockSpec((1, GLA_HEADS, GLA_DK, GLA_DV), lambda b, t: (b, 0, 0, 0)),
            pl.BlockSpec((1, POOL_BUF, POOL_W), lambda b, t: (b, 0, 0)),
        ],
        out_shape=[
            jax.ShapeDtypeStruct((mp, N_BRANCH * ML_W), bf16),
            jax.ShapeDtypeStruct((batch, ML_HEADS, ML_DH, ML_DH), f32),
            jax.ShapeDtypeStruct((batch, ML_HEADS, ML_DH), f32),
            jax.ShapeDtypeStruct((batch, 8, LANES), f32),
            jax.ShapeDtypeStruct((batch, GLA_HEADS, GLA_DK, GLA_DV), f32),
            jax.ShapeDtypeStruct((batch, POOL_BUF, POOL_W), f32),
        ],
        scratch_shapes=[
            pltpu.VMEM((ML_HEADS, ML_DH, ML_DH), f32),
            pltpu.VMEM((8, LANES), f32),
            pltpu.VMEM((8, LANES), f32),
            pltpu.VMEM((GLA_HEADS, GLA_DK, GLA_DV), f32),
            pltpu.VMEM((tt + 16, POOL_W), f32),
        ],
        compiler_params=_cparams(("parallel", "arbitrary")),
        name="prompt_mix",
    )(z, z, z, z, z, z, z, z, zs,
      lw["ifb"], lw["ml_g"], lw["gla_gw"], lw["gla_gb"], lw["gla_g"], lw["pool_w"], lw["pool_scale"])


def _sample_mix_kernel(br_in, q_ref, k_ref, v_ref, mo_ref, gqk_ref, gv_ref, go_ref, pu_ref, sm_ref,
                       c_in, n_in, m_in, s_in, buf_in,
                       ifb_ref, mlg_ref, gw_ref, gb_ref, glag_ref, pw_ref, ps_ref,
                       br_ref, c_out, n_out, m_out, s_out, buf_out, *, sb):
    del br_in
    sm = sm_ref[...]
    eye = (lax.broadcasted_iota(jnp.int32, (LANES, LANES), 0)
           == lax.broadcasted_iota(jnp.int32, (LANES, LANES), 1)).astype(f32)

    def to_cols(x):
        parts = [lax.dot_general(eye, x[:, j * LANES:(j + 1) * LANES], _NT,
                                 preferred_element_type=f32, precision=lax.Precision.HIGHEST)
                 for j in range(x.shape[1] // LANES)]
        return jnp.concatenate(parts, axis=0)

    y0 = sm + ifb_ref[...]
    logf_all = _log_sigmoid(y0)
    k_scale = ML_DH ** -0.5
    m_all = m_in[...]
    hm, m_new_cols = [], []
    for h in range(ML_HEADS):
        hs = slice(h * ML_DH, (h + 1) * ML_DH)
        q = q_ref[:, hs]
        k = k_ref[:, hs] * k_scale
        v = v_ref[:, hs]
        i_pre = y0[:, S_MI + h:S_MI + h + 1]
        logf = logf_all[:, S_MF + h:S_MF + h + 1]
        m = m_all[:, h:h + 1]
        inter = logf + m
        m_t = jnp.maximum(inter, i_pre)
        w_intra = jnp.exp(i_pre - m_t)
        w_inter = jnp.exp(inter - m_t)
        s = jnp.sum(q * k, axis=1, keepdims=True) * w_intra
        qb = q.astype(bf16)
        n_h = n_in[:, h, :]
        cq = jnp.concatenate(
            [lax.dot_general(qb, c_in[j, h].astype(bf16), _NT, preferred_element_type=f32)[j:j + 1, :]
             for j in range(sb)], axis=0)
        num = s * v + w_inter * cq
        den = s + w_inter * jnp.sum(n_h * q, axis=1, keepdims=True)
        hm.append(num / jnp.maximum(jnp.abs(den), jnp.exp(-m_t)))
        m_new = m_t
        w_s = w_intra
        w_c = w_inter
        n_out[:, h, :] = w_c * n_h + w_s * k
        m_new_cols.append(m_new)
        vw_cols = to_cols(v * w_s)
        for j in range(sb):
            c_out[j, h] = w_c[j:j + 1, :] * c_in[j, h] + vw_cols[:, j:j + 1] * k[j:j + 1, :]
    lane = lax.broadcasted_iota(jnp.int32, (sb, LANES), 1)
    m_pack = jnp.zeros((sb, LANES), f32)
    for h in range(ML_HEADS):
        m_pack = jnp.where(lane == h, m_new_cols[h], m_pack)
    m_out[...] = m_pack
    y_ml = _head_norm(jnp.concatenate(hm, axis=1), mlg_ref[...]) * _sigmoid(mo_ref[...])
    br_ref[:, 0:ML_W] = y_ml.astype(br_ref.dtype)

    log_a = _gla_log_decay(sm, gw_ref, gb_ref)
    decay = jnp.exp(log_a)
    q2 = gqk_ref[:, 0:GLA_KW] * (GLA_DK ** -0.5)
    k2 = gqk_ref[:, GLA_KW:2 * GLA_KW]
    gv = gv_ref[...]
    qe = (q2 * decay).astype(bf16)
    qk = q2 * k2
    k_cols = to_cols(k2)
    decay_cols = to_cols(decay)
    og = []
    for h in range(GLA_HEADS):
        ks = slice(h * GLA_DK, (h + 1) * GLA_DK)
        vs = slice(h * GLA_DV, (h + 1) * GLA_DV)
        a = jnp.sum(qk[:, ks], axis=1, keepdims=True)
        inter = jnp.concatenate(
            [jnp.dot(qe[:, ks], s_in[j, h].astype(bf16), preferred_element_type=f32)[j:j + 1, :]
             for j in range(sb)], axis=0)
        og.append(a * gv[:, vs] + inter)
        for j in range(sb):
            s_out[j, h] = (decay_cols[ks, j:j + 1] * s_in[j, h]
                           + k_cols[ks, j:j + 1] * gv[j:j + 1, vs])
    go = go_ref[...]
    y_gla = _head_norm(jnp.concatenate(og, axis=1), glag_ref[...]) * (go * _sigmoid(go))
    br_ref[:, ML_W:ML_W + GLA_VW] = y_gla.astype(br_ref.dtype)

    u = pu_ref[...]
    rowi = lax.broadcasted_iota(jnp.int32, (POOL_BUF + 1, POOL_GDIM), 0)
    d_rows = []
    for j in range(sb):
        ext = jnp.concatenate([buf_in[j], u[j:j + 1, :]], axis=0)
        buf_out[j] = ext[1:POOL_BUF + 1, :]
        parts = []
        for g, w in enumerate(POOL_WINDOWS):
            gs = slice(g * POOL_GDIM, (g + 1) * POOL_GDIM)
            win = jnp.sum(jnp.where(rowi >= POOL_BUF + 1 - w, ext[:, gs], 0.0), axis=0, keepdims=True)
            parts.append(win / float(w) - u[j:j + 1, gs])
        d_rows.append(jnp.concatenate(parts, axis=1))
    d = jnp.concatenate(d_rows, axis=0)
    for g in range(len(POOL_WINDOWS)):
        gs = slice(g * POOL_GDIM, (g + 1) * POOL_GDIM)
        y = jnp.dot(d[:, gs].astype(bf16), pw_ref[g], preferred_element_type=f32) * ps_ref[:, gs]
        br_ref[:, ML_W + GLA_VW + g * POOL_GDIM:ML_W + GLA_VW + (g + 1) * POOL_GDIM] = y.astype(br_ref.dtype)


def _sample_mix(br, z, zs, st, lw, row0, dec, sb):
    base = row0 // sb

    def zspec(col, width):
        blk = col // width
        return pl.BlockSpec((sb, width), lambda i: (base + i, blk))

    def full(shape):
        nd = len(shape)
        return pl.BlockSpec(shape, lambda i: (0,) * nd)

    def st_spec(shape):
        nd = len(shape)
        return pl.BlockSpec((sb,) + shape, lambda i: (i,) + (0,) * nd)

    c0, n0, m0, s0, buf0 = st
    st_specs = [st_spec((ML_HEADS, ML_DH, ML_DH)), st_spec((ML_HEADS, ML_DH)), st_spec((LANES,)),
                st_spec((GLA_HEADS, GLA_DK, GLA_DV)), st_spec((POOL_BUF, POOL_W))]
    br_spec = pl.BlockSpec((sb, N_BRANCH * ML_W), lambda i: (base + i, 0))
    return pl.pallas_call(
        functools.partial(_sample_mix_kernel, sb=sb),
        grid=(dec // sb,),
        in_specs=[pl.BlockSpec(memory_space=pl.ANY),
                  zspec(C_MQ, 512), zspec(C_MK, 512), zspec(C_MV, 512), zspec(C_MO, 512),
                  zspec(C_GQ, 512), zspec(C_GV, 512), zspec(C_GO, 512), zspec(C_PU, 512),
                  pl.BlockSpec((sb, LANES), lambda i: (base + i, 0))]
                 + st_specs
                 + [full((1, LANES)), full((1, ML_W)), full((LANES, GLA_KW)), full((1, GLA_KW)),
                    full((1, GLA_VW)), full((len(POOL_WINDOWS), POOL_GDIM, POOL_GDIM)), full((1, POOL_W))],
        out_specs=[br_spec] + st_specs,
        out_shape=[jax.ShapeDtypeStruct(br.shape, br.dtype),
                   jax.ShapeDtypeStruct(c0.shape, f32), jax.ShapeDtypeStruct(n0.shape, f32),
                   jax.ShapeDtypeStruct(m0.shape, f32), jax.ShapeDtypeStruct(s0.shape, f32),
                   jax.ShapeDtypeStruct(buf0.shape, f32)],
        input_output_aliases={0: 0},
        compiler_params=_cparams(("parallel",)),
        name="sample_mix",
    )(br, z, z, z, z, z, z, z, z, zs, c0, n0, m0, s0, buf0,
      lw["ifb"], lw["ml_g"], lw["gla_gw"], lw["gla_gb"], lw["gla_g"], lw["pool_w"], lw["pool_scale"])


def _merge_kernel(x_ref, br_ref, g0_ref, g1_ref, g2_ref, wb_ref, wo_ref, o_ref, *, tm, n_valid):
    mixed = None
    for n, g_ref in enumerate((g0_ref, g1_ref, g2_ref)):
        proj = jnp.dot(br_ref[:, n * ML_W:(n + 1) * ML_W], wb_ref[n], preferred_element_type=f32)
        term = _sigmoid(g_ref[...]) * proj
        mixed = term if mixed is None else mixed + term
    out = x_ref[...] + jnp.dot(mixed.astype(bf16), wo_ref[...], preferred_element_type=f32)
    row = pl.program_id(0) * tm + lax.broadcasted_iota(jnp.int32, (tm, 1), 0)
    o_ref[...] = jnp.where(row < n_valid, out, 0.0)


def _merge(x, br, z, w_branch, w_out, tm, n_valid):
    mp = x.shape[0]
    gate_specs = [pl.BlockSpec((tm, D_MODEL), functools.partial(lambda i, n: (i, C_GATES // D_MODEL + n), n=n))
                  for n in range(N_BRANCH)]
    return pl.pallas_call(
        functools.partial(_merge_kernel, tm=tm, n_valid=n_valid),
        grid=(mp // tm,),
        in_specs=[
            pl.BlockSpec((tm, D_MODEL), lambda i: (i, 0)),
            pl.BlockSpec((tm, N_BRANCH * ML_W), lambda i: (i, 0)),
            *gate_specs,
            pl.BlockSpec((N_BRANCH, ML_W, D_MODEL), lambda i: (0, 0, 0)),
            pl.BlockSpec((D_MODEL, D_MODEL), lambda i: (0, 0)),
        ],
        out_specs=pl.BlockSpec((tm, D_MODEL), lambda i: (i, 0)),
        out_shape=jax.ShapeDtypeStruct((mp, D_MODEL), f32),
        compiler_params=_cparams(("parallel",)),
        name="merge",
    )(x, br, z, z, z, w_branch, w_out)


def _router_kernel(x_ref, g_ref, wr_ref, hb_ref, comb_ref, rnk_ref, rnkt_ref, cnt_ref, *, tm, n_valid):
    h = _rms(x_ref[...], g_ref[...]).astype(bf16)
    hb_ref[...] = h
    logits = jnp.dot(h, wr_ref[...], preferred_element_type=f32)
    lane = lax.broadcasted_iota(jnp.int32, logits.shape, 1)
    valid = lane < N_EXPERTS
    logits = jnp.where(valid, logits, NEG)
    mx = jnp.max(logits, axis=1, keepdims=True)
    e = jnp.where(valid, jnp.exp(logits - mx), 0.0)
    probs = e / jnp.sum(e, axis=1, keepdims=True)
    p1 = jnp.max(probs, axis=1, keepdims=True)
    i1 = jnp.min(jnp.where(probs == p1, lane, LANES), axis=1, keepdims=True)
    rest = jnp.where((lane == i1) | ~valid, -1.0, probs)
    p2 = jnp.max(rest, axis=1, keepdims=True)
    i2 = jnp.min(jnp.where(rest == p2, lane, LANES), axis=1, keepdims=True)
    tot = p1 + p2
    comb_ref[...] = jnp.where(lane == i1, p1 / tot, 0.0) + jnp.where(lane == i2, p2 / tot, 0.0)
    row = pl.program_id(0) * tm + lax.broadcasted_iota(jnp.int32, logits.shape, 0)
    sel = ((lane == i1) | (lane == i2)) & (row < n_valid)
    r = lax.broadcasted_iota(jnp.int32, (tm, tm), 0)
    c = lax.broadcasted_iota(jnp.int32, (tm, tm), 1)
    rank = jnp.dot((c < r).astype(bf16), sel.astype(bf16), preferred_element_type=f32)
    rnk = jnp.where(sel, rank, -1.0)
    rnk_ref[...] = rnk
    rnkt_ref[0] = rnk.T[0:N_EXPERTS, :]
    cnt_ref[0] = jnp.sum(sel.astype(f32), axis=0, keepdims=True)


def _router(x, g, wr, tm, n_valid):
    mp = x.shape[0]
    nt = mp // tm
    return pl.pallas_call(
        functools.partial(_router_kernel, tm=tm, n_valid=n_valid),
        grid=(nt,),
        in_specs=[pl.BlockSpec((tm, D_MODEL), lambda i: (i, 0)),
                  pl.BlockSpec((1, D_MODEL), lambda i: (0, 0)),
                  pl.BlockSpec((D_MODEL, LANES), lambda i: (0, 0))],
        out_specs=[pl.BlockSpec((tm, D_MODEL), lambda i: (i, 0)),
                   pl.BlockSpec((tm, LANES), lambda i: (i, 0)),
                   pl.BlockSpec((tm, LANES), lambda i: (i, 0)),
                   pl.BlockSpec((1, N_EXPERTS, tm), lambda i: (i, 0, 0)),
                   pl.BlockSpec((1, 1, LANES), lambda i: (i, 0, 0))],
        out_shape=[jax.ShapeDtypeStruct((mp, D_MODEL), bf16),
                   jax.ShapeDtypeStruct((mp, LANES), f32),
                   jax.ShapeDtypeStruct((mp, LANES), f32),
                   jax.ShapeDtypeStruct((nt, N_EXPERTS, tm), f32),
                   jax.ShapeDtypeStruct((nt, 1, LANES), f32)],
        compiler_params=_cparams(("parallel",)),
        name="router",
    )(x, g, wr)


def _moe_kernel(rounds_ref, hb_ref, rnk_ref, rnkt_ref, comb_ref, wg_ref, wu_ref, wd_ref, o_ref,
                *, cap, n_sub, ts):
    i, e = pl.program_id(0), pl.program_id(1)

    @pl.when(e == 0)
    def _():
        o_ref[...] = jnp.zeros_like(o_ref)

    lane = lax.broadcasted_iota(jnp.int32, (ts, LANES), 1)
    slot_rows = lax.broadcasted_iota(jnp.int32, (cap, ts), 0).astype(f32)
    slot_cols = lax.broadcasted_iota(jnp.int32, (ts, cap), 1).astype(f32)

    def round_body(r, carry):
        base = (r * cap).astype(f32)
        xs = []
        for j in range(n_sub):
            rt = rnkt_ref[j, pl.ds(e, 1), :]
            p = (rt - base == slot_rows).astype(bf16)
            xs.append(jnp.dot(p, hb_ref[j * ts:(j + 1) * ts, :],
                              preferred_element_type=f32).astype(bf16))
        xs = jnp.concatenate(xs, axis=0)
        a = jnp.dot(xs, wg_ref[0], preferred_element_type=f32)
        a = (a * _sigmoid(a)) * jnp.dot(xs, wu_ref[0], preferred_element_type=f32)
        y = jnp.dot(a.astype(bf16), wd_ref[0], preferred_element_type=f32).astype(bf16)
        for j in range(n_sub):
            rows = slice(j * ts, (j + 1) * ts)
            col = jnp.sum(jnp.where(lane == e, rnk_ref[rows, :], 0.0), axis=1, keepdims=True)
            w = jnp.sum(jnp.where(lane == e, comb_ref[rows, :], 0.0), axis=1, keepdims=True)
            pt = (col - base == slot_cols).astype(bf16)
            o_ref[rows, :] += w * jnp.dot(pt, y[j * cap:(j + 1) * cap, :], preferred_element_type=f32)
        return carry

    lax.fori_loop(0, rounds_ref[i * N_EXPERTS + e], round_body, 0)


def _moe(hb, rnk, rnkt, comb, rounds, wg, wu, wd, ts, n_sub, cap):
    mp = hb.shape[0]
    n_e, _, d_ff = wg.shape
    tsup = ts * n_sub
    grid_spec = pltpu.PrefetchScalarGridSpec(
        num_scalar_prefetch=1,
        grid=(mp // tsup, n_e),
        in_specs=[pl.BlockSpec((tsup, D_MODEL), lambda i, e, r: (i, 0)),
                  pl.BlockSpec((tsup, LANES), lambda i, e, r: (i, 0)),
                  pl.BlockSpec((n_sub, N_EXPERTS, ts), lambda i, e, r: (i, 0, 0)),
                  pl.BlockSpec((tsup, LANES), lambda i, e, r: (i, 0)),
                  pl.BlockSpec((1, D_MODEL, d_ff), lambda i, e, r: (e, 0, 0)),
                  pl.BlockSpec((1, D_MODEL, d_ff), lambda i, e, r: (e, 0, 0)),
                  pl.BlockSpec((1, d_ff, D_MODEL), lambda i, e, r: (e, 0, 0))],
        out_specs=pl.BlockSpec((tsup, D_MODEL), lambda i, e, r: (i, 0)),
    )
    return pl.pallas_call(
        functools.partial(_moe_kernel, cap=cap, n_sub=n_sub, ts=ts),
        grid_spec=grid_spec,
        out_shape=jax.ShapeDtypeStruct((mp, D_MODEL), f32),
        compiler_params=_cparams(("parallel", "arbitrary")),
        name="moe",
    )(rounds, hb, rnk, rnkt, comb, wg, wu, wd)


def _ffn_kernel(x_ref, g_ref, wg_ref, wu_ref, wd_ref, o_ref, h_ref, acc_ref, *, n_f):
    f = pl.program_id(1)

    @pl.when(f == 0)
    def _():
        h_ref[...] = _rms(x_ref[...], g_ref[...]).astype(bf16)
        acc_ref[...] = x_ref[...]

    h = h_ref[...]
    a = jnp.dot(h, wg_ref[...], preferred_element_type=f32)
    a = (a * _sigmoid(a)) * jnp.dot(h, wu_ref[...], preferred_element_type=f32)
    acc_ref[...] += jnp.dot(a.astype(bf16), wd_ref[...], preferred_element_type=f32)

    @pl.when(f == n_f - 1)
    def _():
        o_ref[...] = acc_ref[...]


def _ffn(x, g, wg, wu, wd, tm, tf):
    mp = x.shape[0]
    d_ff = wg.shape[1]
    n_f = d_ff // tf
    return pl.pallas_call(
        functools.partial(_ffn_kernel, n_f=n_f),
        grid=(mp // tm, n_f),
        in_specs=[pl.BlockSpec((tm, D_MODEL), lambda i, f: (i, 0)),
                  pl.BlockSpec((1, D_MODEL), lambda i, f: (0, 0)),
                  pl.BlockSpec((D_MODEL, tf), lambda i, f: (0, f)),
                  pl.BlockSpec((D_MODEL, tf), lambda i, f: (0, f)),
                  pl.BlockSpec((tf, D_MODEL), lambda i, f: (f, 0))],
        out_specs=pl.BlockSpec((tm, D_MODEL), lambda i, f: (i, 0)),
        out_shape=jax.ShapeDtypeStruct((mp, D_MODEL), f32),
        scratch_shapes=[pltpu.VMEM((tm, D_MODEL), bf16), pltpu.VMEM((tm, D_MODEL), f32)],
        compiler_params=_cparams(("parallel", "arbitrary")),
        name="ffn",
    )(x, g, wg, wu, wd)


def _final_norm_kernel(x_ref, y_ref, g_ref, o_ref):
    o_ref[...] = _rms(x_ref[...] + y_ref[...], g_ref[...])


def _final_norm(x, y, g, tm):
    mp = x.shape[0]
    return pl.pallas_call(
        _final_norm_kernel,
        grid=(mp // tm,),
        in_specs=[pl.BlockSpec((tm, D_MODEL), lambda i: (i, 0)),
                  pl.BlockSpec((tm, D_MODEL), lambda i: (i, 0)),
                  pl.BlockSpec((1, D_MODEL), lambda i: (0, 0))],
        out_specs=pl.BlockSpec((tm, D_MODEL), lambda i: (i, 0)),
        out_shape=jax.ShapeDtypeStruct((mp, D_MODEL), f32),
        compiler_params=_cparams(("parallel",)),
        name="final_norm",
    )(x, y, g)


def _regroup_w_in(w):
    o_mi = 3 * ML_W
    o_mo = o_mi + 2 * ML_HEADS
    o_gq = o_mo + ML_W
    o_glr = o_gq + 2 * GLA_KW + GLA_VW
    o_go = o_glr + GLA_RANK
    main = jnp.concatenate([w[:, :o_mi], w[:, o_mo:o_glr], w[:, o_go:]], axis=1)
    small = jnp.concatenate([w[:, o_mi:o_mo], w[:, o_glr:o_go],
                             jnp.zeros((w.shape[0], LANES - 2 * ML_HEADS - GLA_RANK), w.dtype)], axis=1)
    return main.astype(bf16), small.astype(bf16)


def _layer_weights(l, norm1_g, w_in, if_bias, ml_g, gla_gw, gla_gb, gla_g, pool_w, pool_scale,
                   w_branch, w_out):
    w_main, w_small = _regroup_w_in(w_in[l])
    ifb = jnp.zeros((1, LANES), f32)
    ifb = ifb.at[0, S_MI:S_MI + ML_HEADS].set(if_bias[l, 0]).at[0, S_MF:S_MF + ML_HEADS].set(if_bias[l, 1])
    gw = jnp.zeros((LANES, GLA_KW), f32).at[S_GLR:S_GLR + GLA_RANK].set(gla_gw[l]).astype(bf16)
    return dict(
        norm1_g=norm1_g[l][None], w_main=w_main, w_small=w_small, ifb=ifb, ml_g=ml_g[l][None],
        gla_gw=gw, gla_gb=gla_gb[l][None], gla_g=gla_g[l][None], pool_w=pool_w[l].astype(bf16),
        pool_scale=pool_scale[l][None], w_branch=w_branch[l].astype(bf16), w_out=w_out[l].astype(bf16))


def _forward(x_prompt, x_sample, state_mlstm_C, state_mlstm_n, state_mlstm_m, state_gla_S,
             state_pool_buf, norm1_g, w_in, mlstm_if_bias, mlstm_norm_g, gla_gate_w, gla_gate_b,
             gla_norm_g, pool_w, pool_scale, w_branch, w_out, norm2_g, ffn_wg, ffn_wu, ffn_wd,
             router_w, moe_wg, moe_wu, moe_wd, final_norm_g, *, tm, tn, tt, chunk, sub, sb, tf_dense, moe_sub, moe_cap):
    batch, seq, _ = x_prompt.shape
    dec = x_sample.shape[0]
    depth = w_in.shape[0]
    m_prompt = batch * seq
    m_all = m_prompt + dec
    mp = -(-m_all // tm) * tm
    x = jnp.concatenate([x_prompt.reshape(m_prompt, D_MODEL), x_sample.reshape(dec, D_MODEL),
                         jnp.zeros((mp - m_all, D_MODEL), f32)], axis=0)
    outs = [[] for _ in range(10)]
    moe_out = None
    for l in range(depth):
        lw = _layer_weights(l, norm1_g, w_in, mlstm_if_bias, mlstm_norm_g, gla_gate_w, gla_gate_b,
                            gla_norm_g, pool_w, pool_scale, w_branch, w_out)
        z, zs = _norm_matmul(x, lw["norm1_g"], lw["w_main"], lw["w_small"], tm, tn)
        br, c_p, n_p, m_p, s_p, buf_p = _prompt_mix(z, zs, lw, batch, seq, mp, tt, chunk, sub)
        m_state = jnp.pad(state_mlstm_m[l], ((0, 0), (0, LANES - ML_HEADS)))
        st = (state_mlstm_C[l], state_mlstm_n[l], m_state, state_gla_S[l], state_pool_buf[l])
        br, c_s, n_s, m_s, s_s, buf_s = _sample_mix(br, z, zs, st, lw, m_prompt, dec, sb)
        x = _merge(x, br, z, lw["w_branch"], lw["w_out"], tm, m_all)
        j = l // 2
        if l % 2 == 0:
            x = _ffn(x, norm2_g[l][None], ffn_wg[j].astype(bf16), ffn_wu[j].astype(bf16),
                     ffn_wd[j].astype(bf16), tm, tf_dense)
        else:
            wr = jnp.pad(router_w[j], ((0, 0), (0, LANES - N_EXPERTS))).astype(bf16)
            hb, comb, rnk, rnkt, cnt = _router(x, norm2_g[l][None], wr, tm, m_all)
            n_sup = mp // (tm * moe_sub)
            cnt = cnt[:, 0, :N_EXPERTS].reshape(n_sup, moe_sub, N_EXPERTS).max(axis=1)
            rounds = jnp.ceil(cnt / moe_cap).astype(jnp.int32).reshape(-1)
            moe_out = _moe(hb, rnk, rnkt, comb, rounds, moe_wg[j].astype(bf16), moe_wu[j].astype(bf16),
                           moe_wd[j].astype(bf16), tm, moe_sub, moe_cap)
            if l + 1 < depth:
                x = x + moe_out
                moe_out = None
        for lst, val in zip(outs, (c_p, c_s, n_p, n_s, m_p[:, :ML_HEADS, 0], m_s[:, :ML_HEADS],
                                   s_p, s_s, buf_p, buf_s)):
            lst.append(val)
    if moe_out is None:
        moe_out = jnp.zeros_like(x)
    y = _final_norm(x, moe_out, final_norm_g[None], tm)
    y_prompt = y[:m_prompt].reshape(batch, seq, D_MODEL)
    y_sample = y[m_prompt:m_all].reshape(dec, 1, D_MODEL)
    return (y_prompt, y_sample) + tuple(jnp.stack(o) for o in outs)


def kernel(x_prompt, x_sample, state_mlstm_C, state_mlstm_n, state_mlstm_m, state_gla_S, state_pool_buf, norm1_g, w_in, mlstm_if_bias, mlstm_norm_g, gla_gate_w, gla_gate_b, gla_norm_g, pool_w, pool_scale, w_branch, w_out, norm2_g, ffn_wg, ffn_wu, ffn_wd, router_w, moe_wg, moe_wu, moe_wd, final_norm_g):
    return _forward(x_prompt, x_sample, state_mlstm_C, state_mlstm_n, state_mlstm_m, state_gla_S,
                    state_pool_buf, norm1_g, w_in, mlstm_if_bias, mlstm_norm_g, gla_gate_w, gla_gate_b,
                    gla_norm_g, pool_w, pool_scale, w_branch, w_out, norm2_g, ffn_wg, ffn_wu, ffn_wd,
                    router_w, moe_wg, moe_wu, moe_wd, final_norm_g,
                    tm=512, tn=3584, tt=512, chunk=64, sub=32, sb=8, tf_dense=1408, moe_sub=3, moe_cap=160)
```

```python
import functools

import jax
import jax.numpy as jnp
from jax import lax
from jax.experimental import pallas as pl
from jax.experimental.pallas import tpu as pltpu

f32 = jnp.float32
bf16 = jnp.bfloat16

D_MODEL = 1024
ML_HEADS, ML_DH = 4, 128
ML_W = ML_HEADS * ML_DH
GLA_HEADS, GLA_DK, GLA_DV = 4, 64, 128
GLA_KW, GLA_VW = GLA_HEADS * GLA_DK, GLA_HEADS * GLA_DV
GLA_RANK = 16
GLA_TAU = 16.0
POOL_GDIM = 128
POOL_WINDOWS = (2, 4, 8, 16)
POOL_W = POOL_GDIM * len(POOL_WINDOWS)
POOL_BUF = 15
N_BRANCH = 3
N_EXPERTS = 8
EPS = 1e-6
NEG = -1e30
LANES = 128

C_MQ, C_MK, C_MV, C_MO = 0, 512, 1024, 1536
C_GQ, C_GK, C_GV, C_GO, C_PU, C_GATES = 2048, 2304, 2560, 3072, 3584, 4096
Z_MAIN = C_GATES + N_BRANCH * D_MODEL
S_MI, S_MF, S_GLR = 0, 4, 8

VMEM_LIMIT = 56 * 1024 * 1024

_NT = (((1,), (1,)), ((), ()))


def _cparams(sem):
    return pltpu.CompilerParams(dimension_semantics=sem, vmem_limit_bytes=VMEM_LIMIT)


def _log_sigmoid(x):
    return jnp.minimum(x, 0.0) - jnp.log1p(jnp.exp(-jnp.abs(x)))


def _sigmoid(x):
    return 1.0 / (1.0 + jnp.exp(-x))


def _rms(x, g):
    ms = jnp.mean(x * x, axis=-1, keepdims=True)
    return x * lax.rsqrt(ms + EPS) * g


def _lower_tri(n):
    r = lax.broadcasted_iota(jnp.int32, (n, n), 0)
    c = lax.broadcasted_iota(jnp.int32, (n, n), 1)
    return c <= r


def _cumsum_rows(tri_bf16, a):
    a1 = a.astype(bf16)
    r = a - a1.astype(f32)
    a2 = r.astype(bf16)
    a3 = (r - a2.astype(f32)).astype(bf16)
    d = lambda y: jnp.dot(tri_bf16, y, preferred_element_type=f32)
    return d(a1) + d(a2) + d(a3)


def _norm_matmul_kernel(x_ref, g_ref, w_ref, ws_ref, z_ref, zs_ref, h_ref):
    @pl.when(pl.program_id(1) == 0)
    def _():
        h = _rms(x_ref[...], g_ref[...]).astype(bf16)
        h_ref[...] = h
        zs_ref[...] = jnp.dot(h, ws_ref[...], preferred_element_type=f32)

    z_ref[...] = jnp.dot(h_ref[...], w_ref[...], preferred_element_type=f32)


def _norm_matmul(x, g, w_main, w_small, tm, tn):
    mp = x.shape[0]
    n = w_main.shape[1]
    return pl.pallas_call(
        _norm_matmul_kernel,
        grid=(mp // tm, n // tn),
        in_specs=[
            pl.BlockSpec((tm, D_MODEL), lambda i, j: (i, 0)),
            pl.BlockSpec((1, D_MODEL), lambda i, j: (0, 0)),
            pl.BlockSpec((D_MODEL, tn), lambda i, j: (0, j)),
            pl.BlockSpec((D_MODEL, LANES), lambda i, j: (0, 0)),
        ],
        out_specs=[
            pl.BlockSpec((tm, tn), lambda i, j: (i, j)),
            pl.BlockSpec((tm, LANES), lambda i, j: (i, 0)),
        ],
        out_shape=[jax.ShapeDtypeStruct((mp, n), f32), jax.ShapeDtypeStruct((mp, LANES), f32)],
        scratch_shapes=[pltpu.VMEM((tm, D_MODEL), bf16)],
        compiler_params=_cparams(("parallel", "arbitrary")),
        name="norm_matmul",
    )(x, g, w_main, w_small)


def _head_norm(h, g):
    outs = []
    for j in range(h.shape[1] // LANES):
        hj = h[:, j * LANES:(j + 1) * LANES]
        outs.append(hj * lax.rsqrt(jnp.mean(hj * hj, axis=-1, keepdims=True) + EPS))
    return jnp.concatenate(outs, axis=1) * g


def _gla_log_decay(sm, gw_ref, gb_ref):
    xg = jnp.dot(sm.astype(bf16), gw_ref[...], preferred_element_type=f32) + gb_ref[...]
    return _log_sigmoid(xg) * (1.0 / GLA_TAU)


def _prompt_mix_kernel(q_ref, k_ref, v_ref, mo_ref, gqk_ref, gv_ref, go_ref, pu_ref, sm_ref,
                       ifb_ref, mlg_ref, gw_ref, gb_ref, glag_ref, pw_ref, ps_ref,
                       br_ref, c_out, n_out, m_out, s_out, buf_out,
                       c_s, n_s, m_s, s_s, ext_s, *, tt, chunk, sub, n_t):
    t_idx = pl.program_id(1)
    L = chunk

    @pl.when(t_idx == 0)
    def _():
        c_s[...] = jnp.zeros_like(c_s)
        n_s[...] = jnp.zeros_like(n_s)
        m_s[...] = jnp.zeros_like(m_s)
        s_s[...] = jnp.zeros_like(s_s)
        ext_s[0:16, :] = jnp.zeros((16, POOL_W), f32)

    tri = _lower_tri(L)
    tri_b = tri.astype(bf16)
    causal_sub = _lower_tri(sub)
    ifb = ifb_ref[...]
    k_scale = ML_DH ** -0.5
    q_scale = GLA_DK ** -0.5

    def chunk_body(c, carry):
        r0 = pl.multiple_of(c * L, L)
        rows = pl.ds(r0, L)
        sm = sm_ref[rows, :]

        y0 = sm + ifb
        bc = _cumsum_rows(tri_b, _log_sigmoid(y0))
        y0t = y0.T
        bct = bc.T
        hm = []
        for h in range(ML_HEADS):
            hs = slice(h * ML_DH, (h + 1) * ML_DH)
            q = q_ref[rows, hs]
            k = k_ref[rows, hs] * k_scale
            v = v_ref[rows, hs]
            qb, kb, vb = q.astype(bf16), k.astype(bf16), v.astype(bf16)
            bcol = bc[:, S_MF + h:S_MF + h + 1]
            icol = y0[:, S_MI + h:S_MI + h + 1]
            brow = bct[S_MF + h:S_MF + h + 1, :]
            irow = y0t[S_MI + h:S_MI + h + 1, :]
            m = m_s[h:h + 1, 0:1]
            cmat = c_s[h]
            nrow = n_s[h:h + 1, :]
            dm = jnp.where(tri, bcol - brow + irow, NEG)
            inter = bcol + m
            m_t = jnp.maximum(inter, jnp.max(dm, axis=1, keepdims=True))
            w_intra = jnp.exp(dm - m_t)
            w_inter = jnp.exp(inter - m_t)
            s = lax.dot_general(qb, kb, _NT, preferred_element_type=f32) * w_intra
            num = (jnp.dot(s.astype(bf16), vb, preferred_element_type=f32)
                   + w_inter * lax.dot_general(qb, cmat.astype(bf16), _NT, preferred_element_type=f32))
            den = (jnp.sum(s, axis=1, keepdims=True)
                   + w_inter * jnp.sum(q * nrow, axis=1, keepdims=True))
            hm.append(num / jnp.maximum(jnp.abs(den), jnp.exp(-m_t)))
            b_last = bcol[L - 1:L, :]
            g = b_last - bcol + icol
            m_new = jnp.maximum(b_last + m, jnp.max(g, axis=0, keepdims=True))
            w_s = jnp.exp(g - m_new)
            w_c = jnp.exp(b_last + m - m_new)
            vwt = (v * w_s).T.astype(bf16)
            c_s[h] = w_c * cmat + jnp.dot(vwt, kb, preferred_element_type=f32)
            n_s[h:h + 1, :] = w_c * nrow + jnp.sum(w_s * k, axis=0, keepdims=True)
            m_s[h:h + 1, :] = jnp.broadcast_to(m_new, (1, LANES))
        y_ml = (_head_norm(jnp.concatenate(hm, axis=1), mlg_ref[...])
                * _sigmoid(mo_ref[rows, :]))
        br_ref[rows, 0:ML_W] = y_ml.astype(br_ref.dtype)

        log_a = _gla_log_decay(sm, gw_ref, gb_ref)
        b = _cumsum_rows(tri_b, log_a)
        q2 = gqk_ref[rows, 0:GLA_KW] * q_scale
        k2 = gqk_ref[rows, GLA_KW:2 * GLA_KW]
        gv = gv_ref[rows, :].astype(bf16)
        qe_chunk = (q2 * jnp.exp(b)).astype(bf16)
        b_last = b[L - 1:L, :]
        kdt = (k2 * jnp.exp(b_last - b)).T.astype(bf16)
        decay_col = jnp.exp(b.T[:, L - 1:L])
        o_blocks = []
        for blk in range(L // sub):
            s0, s1 = blk * sub, (blk + 1) * sub
            mid = s0 + sub // 2
            b_blk = b[s0:s1]
            b_mid = b[mid - 1:mid, :]
            qe_d = (q2[s0:s1] * jnp.exp(b_blk - b_mid)).astype(bf16)
            ke_d = (k2[s0:s1] * jnp.exp(b_mid - b_blk)).astype(bf16)
            if blk > 0:
                b_start = b[s0 - 1:s0, :]
                qe_o = (q2[s0:s1] * jnp.exp(b_blk - b_start)).astype(bf16)
                ke_o = (k2[0:s0] * jnp.exp(b_start - b[0:s0])).astype(bf16)
            o_heads = []
            for h in range(GLA_HEADS):
                ks = slice(h * GLA_DK, (h + 1) * GLA_DK)
                vs = slice(h * GLA_DV, (h + 1) * GLA_DV)
                a = lax.dot_general(qe_d[:, ks], ke_d[:, ks], _NT, preferred_element_type=f32)
                a = jnp.where(causal_sub, a, 0.0)
                o = (jnp.dot(a.astype(bf16), gv[s0:s1, vs], preferred_element_type=f32)
                     + jnp.dot(qe_chunk[s0:s1, ks], s_s[h].astype(bf16), preferred_element_type=f32))
                if blk > 0:
                    a = lax.dot_general(qe_o[:, ks], ke_o[:, ks], _NT, preferred_element_type=f32)
                    o = o + jnp.dot(a.astype(bf16), gv[0:s0, vs], preferred_element_type=f32)
                o_heads.append(o)
            o_blocks.append(jnp.concatenate(o_heads, axis=1))
        og = jnp.concatenate(o_blocks, axis=0)
        for h in range(GLA_HEADS):
            ks = slice(h * GLA_DK, (h + 1) * GLA_DK)
            vs = slice(h * GLA_DV, (h + 1) * GLA_DV)
            s_s[h] = (decay_col[ks, :] * s_s[h]
                      + jnp.dot(kdt[ks, :], gv[:, vs], preferred_element_type=f32))
        go = go_ref[rows, :]
        y_gla = _head_norm(og, glag_ref[...]) * (go * _sigmoid(go))
        br_ref[rows, ML_W:ML_W + GLA_VW] = y_gla.astype(br_ref.dtype)
        return carry

    lax.fori_loop(0, tt // L, chunk_body, 0)

    ext_s[16:16 + tt, :] = pu_ref[...]
    pos = t_idx * tt + lax.broadcasted_iota(jnp.int32, (tt, 1), 0)
    for g, w in enumerate(POOL_WINDOWS):
        gs = slice(g * POOL_GDIM, (g + 1) * POOL_GDIM)
        u = ext_s[16:16 + tt, gs]
        acc = u
        for j in range(1, w):
            acc = acc + ext_s[16 - j:16 - j + tt, gs]
        cnt = jnp.minimum(pos + 1, w).astype(f32)
        d = acc / cnt - u
        y = jnp.dot(d.astype(bf16), pw_ref[g], preferred_element_type=f32) * ps_ref[:, gs]
        br_ref[:, ML_W + GLA_VW + g * POOL_GDIM:ML_W + GLA_VW + (g + 1) * POOL_GDIM] = y.astype(br_ref.dtype)
    ext_s[0:16, :] = ext_s[tt:tt + 16, :]

    @pl.when(t_idx == n_t - 1)
    def _():
        c_out[0] = c_s[...]
        n_out[0] = n_s[0:ML_HEADS, :]
        m_out[0] = m_s[...]
        s_out[0] = s_s[...]
        buf_out[0] = ext_s[1:16, :]


def _prompt_mix(z, zs, lw, batch, seq, tt, chunk, sub):
    n_t = seq // tt
    row = lambda b, t: b * n_t + t

    def zspec(col, width):
        blk = col // width
        return pl.BlockSpec((tt, width), lambda b, t: (row(b, t), blk))

    def full(shape):
        nd = len(shape)
        return pl.BlockSpec(shape, lambda b, t: (0,) * nd)

    kern = functools.partial(_prompt_mix_kernel, tt=tt, chunk=chunk, sub=sub, n_t=n_t)
    return pl.pallas_call(
        kern,
        grid=(batch, n_t),
        in_specs=[
            zspec(C_MQ, 512), zspec(C_MK, 512), zspec(C_MV, 512), zspec(C_MO, 512),
            zspec(C_GQ, 512), zspec(C_GV, 512), zspec(C_GO, 512), zspec(C_PU, 512),
            pl.BlockSpec((tt, LANES), lambda b, t: (row(b, t), 0)),
            full((1, LANES)), full((1, ML_W)), full((LANES, GLA_KW)), full((1, GLA_KW)),
            full((1, GLA_VW)), full((len(POOL_WINDOWS), POOL_GDIM, POOL_GDIM)), full((1, POOL_W)),
        ],
        out_specs=[
            pl.BlockSpec((tt, N_BRANCH * ML_W), lambda b, t: (row(b, t), 0)),
            pl.BlockSpec((1, ML_HEADS, ML_DH, ML_DH), lambda b, t: (b, 0, 0, 0)),
            pl.BlockSpec((1, ML_HEADS, ML_DH), lambda b, t: (b, 0, 0)),
            pl.BlockSpec((1, 8, LANES), lambda b, t: (b, 0, 0)),
            pl.BlockSpec((1, GLA_HEADS, GLA_DK, GLA_DV), lambda b, t: (b, 0, 0, 0)),
            pl.BlockSpec((1, POOL_BUF, POOL_W), lambda b, t: (b, 0, 0)),
        ],
        out_shape=[
            jax.ShapeDtypeStruct((batch * seq, N_BRANCH * ML_W), bf16),
            jax.ShapeDtypeStruct((batch, ML_HEADS, ML_DH, ML_DH), f32),
            jax.ShapeDtypeStruct((batch, ML_HEADS, ML_DH), f32),
            jax.ShapeDtypeStruct((batch, 8, LANES), f32),
            jax.ShapeDtypeStruct((batch, GLA_HEADS, GLA_DK, GLA_DV), f32),
            jax.ShapeDtypeStruct((batch, POOL_BUF, POOL_W), f32),
        ],
        scratch_shapes=[
            pltpu.VMEM((ML_HEADS, ML_DH, ML_DH), f32),
            pltpu.VMEM((8, LANES), f32),
            pltpu.VMEM((8, LANES), f32),
            pltpu.VMEM((GLA_HEADS, GLA_DK, GLA_DV), f32),
            pltpu.VMEM((tt + 16, POOL_W), f32),
        ],
        compiler_params=_cparams(("parallel", "arbitrary")),
        name="prompt_mix",
    )(z, z, z, z, z, z, z, z, zs,
      lw["ifb"], lw["ml_g"], lw["gla_gw"], lw["gla_gb"], lw["gla_g"], lw["pool_w"], lw["pool_scale"])


def _sample_mix_kernel(q_ref, k_ref, v_ref, mo_ref, gqk_ref, gv_ref, go_ref, pu_ref, sm_ref,
                       c_in, n_in, m_in, s_in, buf_in,
                       ifb_ref, mlg_ref, gw_ref, gb_ref, glag_ref, pw_ref, ps_ref,
                       br_all, c_out, n_out, m_out, s_out, buf_out, *, sb, dec):
    step = pl.program_id(0)
    c_in, n_in, m_in, s_in, buf_in = (r.at[0] for r in (c_in, n_in, m_in, s_in, buf_in))
    br_ref = br_all.at[pl.ds(pl.multiple_of(step * sb, sb), sb), :]

    @pl.when(step == 0)
    def _():
        br_all[dec:, :] = jnp.zeros((br_all.shape[0] - dec, br_all.shape[1]), br_all.dtype)

    sm = sm_ref[...]
    eye = (lax.broadcasted_iota(jnp.int32, (LANES, LANES), 0)
           == lax.broadcasted_iota(jnp.int32, (LANES, LANES), 1)).astype(f32)

    def to_cols(x):
        parts = [lax.dot_general(eye, x[:, j * LANES:(j + 1) * LANES], _NT,
                                 preferred_element_type=f32, precision=lax.Precision.HIGHEST)
                 for j in range(x.shape[1] // LANES)]
        return jnp.concatenate(parts, axis=0)

    y0 = sm + ifb_ref[...]
    logf_all = _log_sigmoid(y0)
    k_scale = ML_DH ** -0.5
    m_all = m_in[...]
    hm, m_new_cols = [], []
    for h in range(ML_HEADS):
        hs = slice(h * ML_DH, (h + 1) * ML_DH)
        q = q_ref[:, hs]
        k = k_ref[:, hs] * k_scale
        v = v_ref[:, hs]
        i_pre = y0[:, S_MI + h:S_MI + h + 1]
        logf = logf_all[:, S_MF + h:S_MF + h + 1]
        m = m_all[:, h:h + 1]
        inter = logf + m
        m_t = jnp.maximum(inter, i_pre)
        w_intra = jnp.exp(i_pre - m_t)
        w_inter = jnp.exp(inter - m_t)
        s = jnp.sum(q * k, axis=1, keepdims=True) * w_intra
        qb = q.astype(bf16)
        n_h = n_in[:, h, :]
        cq = jnp.concatenate(
            [lax.dot_general(qb, c_in[j, h].astype(bf16), _NT, preferred_element_type=f32)[j:j + 1, :]
             for j in range(sb)], axis=0)
        num = s * v + w_inter * cq
        den = s + w_inter * jnp.sum(n_h * q, axis=1, keepdims=True)
        hm.append(num / jnp.maximum(jnp.abs(den), jnp.exp(-m_t)))
        m_new = m_t
        w_s = w_intra
        w_c = w_inter
        n_out[:, h, :] = w_c * n_h + w_s * k
        m_new_cols.append(m_new)
        vw_cols = to_cols(v * w_s)
        for j in range(sb):
            c_out[j, h] = w_c[j:j + 1, :] * c_in[j, h] + vw_cols[:, j:j + 1] * k[j:j + 1, :]
    lane = lax.broadcasted_iota(jnp.int32, (sb, LANES), 1)
    m_pack = jnp.zeros((sb, LANES), f32)
    for h in range(ML_HEADS):
        m_pack = jnp.where(lane == h, m_new_cols[h], m_pack)
    m_out[...] = m_pack
    y_ml = _head_norm(jnp.concatenate(hm, axis=1), mlg_ref[...]) * _sigmoid(mo_ref[...])
    br_ref[:, 0:ML_W] = y_ml.astype(br_ref.dtype)

    log_a = _gla_log_decay(sm, gw_ref, gb_ref)
    decay = jnp.exp(log_a)
    q2 = gqk_ref[:, 0:GLA_KW] * (GLA_DK ** -0.5)
    k2 = gqk_ref[:, GLA_KW:2 * GLA_KW]
    gv = gv_ref[...]
    qe = (q2 * decay).astype(bf16)
    qk = q2 * k2
    k_cols = to_cols(k2)
    decay_cols = to_cols(decay)
    og = []
    for h in range(GLA_HEADS):
        ks = slice(h * GLA_DK, (h + 1) * GLA_DK)
        vs = slice(h * GLA_DV, (h + 1) * GLA_DV)
        a = jnp.sum(qk[:, ks], axis=1, keepdims=True)
        inter = jnp.concatenate(
            [jnp.dot(qe[:, ks], s_in[j, h].astype(bf16), preferred_element_type=f32)[j:j + 1, :]
             for j in range(sb)], axis=0)
        og.append(a * gv[:, vs] + inter)
        for j in range(sb):
            s_out[j, h] = (decay_cols[ks, j:j + 1] * s_in[j, h]
                           + k_cols[ks, j:j + 1] * gv[j:j + 1, vs])
    go = go_ref[...]
    y_gla = _head_norm(jnp.concatenate(og, axis=1), glag_ref[...]) * (go * _sigmoid(go))
    br_ref[:, ML_W:ML_W + GLA_VW] = y_gla.astype(br_ref.dtype)

    u = pu_ref[...]
    rowi = lax.broadcasted_iota(jnp.int32, (POOL_BUF + 1, POOL_GDIM), 0)
    d_rows = []
    for j in range(sb):
        ext = jnp.concatenate([buf_in[j], u[j:j + 1, :]], axis=0)
        buf_out[j] = ext[1:POOL_BUF + 1, :]
        parts = []
        for g, w in enumerate(POOL_WINDOWS):
            gs = slice(g * POOL_GDIM, (g + 1) * POOL_GDIM)
            win = jnp.sum(jnp.where(rowi >= POOL_BUF + 1 - w, ext[:, gs], 0.0), axis=0, keepdims=True)
            parts.append(win / float(w) - u[j:j + 1, gs])
        d_rows.append(jnp.concatenate(parts, axis=1))
    d = jnp.concatenate(d_rows, axis=0)
    for g in range(len(POOL_WINDOWS)):
        gs = slice(g * POOL_GDIM, (g + 1) * POOL_GDIM)
        y = jnp.dot(d[:, gs].astype(bf16), pw_ref[g], preferred_element_type=f32) * ps_ref[:, gs]
        br_ref[:, ML_W + GLA_VW + g * POOL_GDIM:ML_W + GLA_VW + (g + 1) * POOL_GDIM] = y.astype(br_ref.dtype)


def _sample_mix(z, zs, st, layer, lw, row0, dec, sb):
    base = row0 // sb
    tail_rows = z.shape[0] - row0

    def zspec(col, width):
        blk = col // width
        return pl.BlockSpec((sb, width), lambda i: (base + i, blk))

    def full(shape):
        nd = len(shape)
        return pl.BlockSpec(shape, lambda i: (0,) * nd)

    st_shapes = [(ML_HEADS, ML_DH, ML_DH), (ML_HEADS, ML_DH), (LANES,),
                 (GLA_HEADS, GLA_DK, GLA_DV), (POOL_BUF, POOL_W)]
    in_st = [pl.BlockSpec((1, sb) + s, functools.partial(lambda i, nd: (layer, i) + (0,) * nd, nd=len(s)))
             for s in st_shapes]
    out_st = [pl.BlockSpec((sb,) + s, functools.partial(lambda i, nd: (i,) + (0,) * nd, nd=len(s)))
              for s in st_shapes]
    return pl.pallas_call(
        functools.partial(_sample_mix_kernel, sb=sb, dec=dec),
        grid=(dec // sb,),
        in_specs=[zspec(C_MQ, 512), zspec(C_MK, 512), zspec(C_MV, 512), zspec(C_MO, 512),
                  zspec(C_GQ, 512), zspec(C_GV, 512), zspec(C_GO, 512), zspec(C_PU, 512),
                  pl.BlockSpec((sb, LANES), lambda i: (base + i, 0))]
                 + in_st
                 + [full((1, LANES)), full((1, ML_W)), full((LANES, GLA_KW)), full((1, GLA_KW)),
                    full((1, GLA_VW)), full((len(POOL_WINDOWS), POOL_GDIM, POOL_GDIM)), full((1, POOL_W))],
        out_specs=[pl.BlockSpec((tail_rows, N_BRANCH * ML_W), lambda i: (0, 0))] + out_st,
        out_shape=[jax.ShapeDtypeStruct((tail_rows, N_BRANCH * ML_W), bf16)]
                  + [jax.ShapeDtypeStruct((dec,) + s, f32) for s in st_shapes],
        compiler_params=_cparams(("arbitrary",)),
        name="sample_mix",
    )(z, z, z, z, z, z, z, z, zs, *st,
      lw["ifb"], lw["ml_g"], lw["gla_gw"], lw["gla_gb"], lw["gla_g"], lw["pool_w"], lw["pool_scale"])


def _merge_kernel(x_ref, brp_ref, brt_ref, g0_ref, g1_ref, g2_ref, wb_ref, wo_ref, o_ref,
                  *, tm, n_valid, n_prompt_tiles):
    in_tail = pl.program_id(0) >= n_prompt_tiles
    mixed = None
    for n, g_ref in enumerate((g0_ref, g1_ref, g2_ref)):
        cols = slice(n * ML_W, (n + 1) * ML_W)
        br = jnp.where(in_tail, brt_ref[:, cols], brp_ref[:, cols])
        proj = jnp.dot(br, wb_ref[n], preferred_element_type=f32)
        term = _sigmoid(g_ref[...]) * proj
        mixed = term if mixed is None else mixed + term
    out = x_ref[...] + jnp.dot(mixed.astype(bf16), wo_ref[...], preferred_element_type=f32)
    row = pl.program_id(0) * tm + lax.broadcasted_iota(jnp.int32, (tm, 1), 0)
    o_ref[...] = jnp.where(row < n_valid, out, 0.0)


def _merge(x, br_prompt, br_tail, z, w_branch, w_out, tm, n_valid):
    mp = x.shape[0]
    n_p = br_prompt.shape[0] // tm
    gate_specs = [pl.BlockSpec((tm, D_MODEL), functools.partial(lambda i, n: (i, C_GATES // D_MODEL + n), n=n))
                  for n in range(N_BRANCH)]
    return pl.pallas_call(
        functools.partial(_merge_kernel, tm=tm, n_valid=n_valid, n_prompt_tiles=n_p),
        grid=(mp // tm,),
        in_specs=[
            pl.BlockSpec((tm, D_MODEL), lambda i: (i, 0)),
            pl.BlockSpec((tm, N_BRANCH * ML_W), lambda i: (jnp.minimum(i, n_p - 1), 0)),
            pl.BlockSpec((tm, N_BRANCH * ML_W), lambda i: (jnp.maximum(i - n_p, 0), 0)),
            *gate_specs,
            pl.BlockSpec((N_BRANCH, ML_W, D_MODEL), lambda i: (0, 0, 0)),
            pl.BlockSpec((D_MODEL, D_MODEL), lambda i: (0, 0)),
        ],
        out_specs=pl.BlockSpec((tm, D_MODEL), lambda i: (i, 0)),
        out_shape=jax.ShapeDtypeStruct((mp, D_MODEL), f32),
        compiler_params=_cparams(("parallel",)),
        name="merge",
    )(x, br_prompt, br_tail, z, z, z, w_branch, w_out)


def _router_kernel(x_ref, g_ref, wr_ref, hb_ref, comb_ref, rnk_ref, rnkt_ref, cnt_ref, *, tm, n_valid):
    h = _rms(x_ref[...], g_ref[...]).astype(bf16)
    hb_ref[...] = h
    logits = jnp.dot(h, wr_ref[...], preferred_element_type=f32)
    lane = lax.broadcasted_iota(jnp.int32, logits.shape, 1)
    valid = lane < N_EXPERTS
    logits = jnp.where(valid, logits, NEG)
    mx = jnp.max(logits, axis=1, keepdims=True)
    e = jnp.where(valid, jnp.exp(logits - mx), 0.0)
    probs = e / jnp.sum(e, axis=1, keepdims=True)
    p1 = jnp.max(probs, axis=1, keepdims=True)
    i1 = jnp.min(jnp.where(probs == p1, lane, LANES), axis=1, keepdims=True)
    rest = jnp.where((lane == i1) | ~valid, -1.0, probs)
    p2 = jnp.max(rest, axis=1, keepdims=True)
    i2 = jnp.min(jnp.where(rest == p2, lane, LANES), axis=1, keepdims=True)
    tot = p1 + p2
    comb_ref[...] = jnp.where(lane == i1, p1 / tot, 0.0) + jnp.where(lane == i2, p2 / tot, 0.0)
    row = pl.program_id(0) * tm + lax.broadcasted_iota(jnp.int32, logits.shape, 0)
    sel = ((lane == i1) | (lane == i2)) & (row < n_valid)
    r = lax.broadcasted_iota(jnp.int32, (tm, tm), 0)
    c = lax.broadcasted_iota(jnp.int32, (tm, tm), 1)
    rank = jnp.dot((c < r).astype(bf16), sel.astype(bf16), preferred_element_type=f32)
    rnk = jnp.where(sel, rank, -1.0)
    rnk_ref[...] = rnk
    rnkt_ref[0] = rnk.T[0:N_EXPERTS, :]
    cnt_ref[0] = jnp.sum(sel.astype(f32), axis=0, keepdims=True)


def _router(x, g, wr, tm, n_valid):
    mp = x.shape[0]
    nt = mp // tm
    return pl.pallas_call(
        functools.partial(_router_kernel, tm=tm, n_valid=n_valid),
        grid=(nt,),
        in_specs=[pl.BlockSpec((tm, D_MODEL), lambda i: (i, 0)),
                  pl.BlockSpec((1, D_MODEL), lambda i: (0, 0)),
                  pl.BlockSpec((D_MODEL, LANES), lambda i: (0, 0))],
        out_specs=[pl.BlockSpec((tm, D_MODEL), lambda i: (i, 0)),
                   pl.BlockSpec((tm, LANES), lambda i: (i, 0)),
                   pl.BlockSpec((tm, LANES), lambda i: (i, 0)),
                   pl.BlockSpec((1, N_EXPERTS, tm), lambda i: (i, 0, 0)),
                   pl.BlockSpec((1, 1, LANES), lambda i: (i, 0, 0))],
        out_shape=[jax.ShapeDtypeStruct((mp, D_MODEL), bf16),
                   jax.ShapeDtypeStruct((mp, LANES), f32),
                   jax.ShapeDtypeStruct((mp, LANES), f32),
                   jax.ShapeDtypeStruct((nt, N_EXPERTS, tm), f32),
                   jax.ShapeDtypeStruct((nt, 1, LANES), f32)],
        compiler_params=_cparams(("parallel",)),
        name="router",
    )(x, g, wr)


def _moe_kernel(rounds_ref, hb_ref, rnk_ref, rnkt_ref, comb_ref, wg_ref, wu_ref, wd_ref, o_ref,
                *, cap, n_sub, ts):
    i, e = pl.program_id(0), pl.program_id(1)

    @pl.when(e == 0)
    def _():
        o_ref[...] = jnp.zeros_like(o_ref)

    lane = lax.broadcasted_iota(jnp.int32, (ts, LANES), 1)
    slot_rows = lax.broadcasted_iota(jnp.int32, (cap, ts), 0).astype(f32)
    slot_cols = lax.broadcasted_iota(jnp.int32, (ts, cap), 1).astype(f32)

    def round_body(r, carry):
        base = (r * cap).astype(f32)
        xs = []
        for j in range(n_sub):
            rt = rnkt_ref[j, pl.ds(e, 1), :]
            p = (rt - base == slot_rows).astype(bf16)
            xs.append(jnp.dot(p, hb_ref[j * ts:(j + 1) * ts, :],
                              preferred_element_type=f32).astype(bf16))
        xs = jnp.concatenate(xs, axis=0)
        a = jnp.dot(xs, wg_ref[0], preferred_element_type=f32)
        a = (a * _sigmoid(a)) * jnp.dot(xs, wu_ref[0], preferred_element_type=f32)
        y = jnp.dot(a.astype(bf16), wd_ref[0], preferred_element_type=f32).astype(bf16)
        for j in range(n_sub):
            rows = slice(j * ts, (j + 1) * ts)
            col = jnp.sum(jnp.where(lane == e, rnk_ref[rows, :], 0.0), axis=1, keepdims=True)
            w = jnp.sum(jnp.where(lane == e, comb_ref[rows, :], 0.0), axis=1, keepdims=True)
            pt = (col - base == slot_cols).astype(bf16)
            o_ref[rows, :] += w * jnp.dot(pt, y[j * cap:(j + 1) * cap, :], preferred_element_type=f32)
        return carry

    lax.fori_loop(0, rounds_ref[i * N_EXPERTS + e], round_body, 0)


def _moe(hb, rnk, rnkt, comb, rounds, wg, wu, wd, ts, n_sub, cap):
    mp = hb.shape[0]
    n_e, _, d_ff = wg.shape
    tsup = ts * n_sub
    grid_spec = pltpu.PrefetchScalarGridSpec(
        num_scalar_prefetch=1,
        grid=(mp // tsup, n_e),
        in_specs=[pl.BlockSpec((tsup, D_MODEL), lambda i, e, r: (i, 0)),
                  pl.BlockSpec((tsup, LANES), lambda i, e, r: (i, 0)),
                  pl.BlockSpec((n_sub, N_EXPERTS, ts), lambda i, e, r: (i, 0, 0)),
                  pl.BlockSpec((tsup, LANES), lambda i, e, r: (i, 0)),
                  pl.BlockSpec((1, D_MODEL, d_ff), lambda i, e, r: (e, 0, 0)),
                  pl.BlockSpec((1, D_MODEL, d_ff), lambda i, e, r: (e, 0, 0)),
                  pl.BlockSpec((1, d_ff, D_MODEL), lambda i, e, r: (e, 0, 0))],
        out_specs=pl.BlockSpec((tsup, D_MODEL), lambda i, e, r: (i, 0)),
    )
    return pl.pallas_call(
        functools.partial(_moe_kernel, cap=cap, n_sub=n_sub, ts=ts),
        grid_spec=grid_spec,
        out_shape=jax.ShapeDtypeStruct((mp, D_MODEL), f32),
        compiler_params=_cparams(("parallel", "arbitrary")),
        name="moe",
    )(rounds, hb, rnk, rnkt, comb, wg, wu, wd)


def _ffn_kernel(x_ref, g_ref, wg_ref, wu_ref, wd_ref, o_ref, h_ref, acc_ref, *, n_f):
    f = pl.program_id(1)

    @pl.when(f == 0)
    def _():
        h_ref[...] = _rms(x_ref[...], g_ref[...]).astype(bf16)
        acc_ref[...] = x_ref[...]

    h = h_ref[...]
    a = jnp.dot(h, wg_ref[...], preferred_element_type=f32)
    a = (a * _sigmoid(a)) * jnp.dot(h, wu_ref[...], preferred_element_type=f32)
    acc_ref[...] += jnp.dot(a.astype(bf16), wd_ref[...], preferred_element_type=f32)

    @pl.when(f == n_f - 1)
    def _():
        o_ref[...] = acc_ref[...]


def _ffn(x, g, wg, wu, wd, tm, tf):
    mp = x.shape[0]
    d_ff = wg.shape[1]
    n_f = d_ff // tf
    return pl.pallas_call(
        functools.partial(_ffn_kernel, n_f=n_f),
        grid=(mp // tm, n_f),
        in_specs=[pl.BlockSpec((tm, D_MODEL), lambda i, f: (i, 0)),
                  pl.BlockSpec((1, D_MODEL), lambda i, f: (0, 0)),
                  pl.BlockSpec((D_MODEL, tf), lambda i, f: (0, f)),
                  pl.BlockSpec((D_MODEL, tf), lambda i, f: (0, f)),
                  pl.BlockSpec((tf, D_MODEL), lambda i, f: (f, 0))],
        out_specs=pl.BlockSpec((tm, D_MODEL), lambda i, f: (i, 0)),
        out_shape=jax.ShapeDtypeStruct((mp, D_MODEL), f32),
        scratch_shapes=[pltpu.VMEM((tm, D_MODEL), bf16), pltpu.VMEM((tm, D_MODEL), f32)],
        compiler_params=_cparams(("parallel", "arbitrary")),
        name="ffn",
    )(x, g, wg, wu, wd)


def _final_norm_kernel(x_ref, y_ref, g_ref, op_ref, os_ref, *, n_prompt_tiles, dec):
    i = pl.program_id(0)
    out = _rms(x_ref[...] + y_ref[...], g_ref[...])

    @pl.when(i < n_prompt_tiles)
    def _():
        op_ref[...] = out

    @pl.when(i == n_prompt_tiles)
    def _():
        os_ref[...] = out[0:dec]


def _final_norm(x, y, g, tm, m_prompt, dec):
    n_p = m_prompt // tm
    assert n_p * tm == m_prompt and dec <= tm
    return pl.pallas_call(
        functools.partial(_final_norm_kernel, n_prompt_tiles=n_p, dec=dec),
        grid=(n_p + 1,),
        in_specs=[pl.BlockSpec((tm, D_MODEL), lambda i: (i, 0)),
                  pl.BlockSpec((tm, D_MODEL), lambda i: (i, 0)),
                  pl.BlockSpec((1, D_MODEL), lambda i: (0, 0))],
        out_specs=[pl.BlockSpec((tm, D_MODEL), lambda i: (jnp.minimum(i, n_p - 1), 0)),
                   pl.BlockSpec((dec, D_MODEL), lambda i: (0, 0))],
        out_shape=[jax.ShapeDtypeStruct((m_prompt, D_MODEL), f32),
                   jax.ShapeDtypeStruct((dec, D_MODEL), f32)],
        compiler_params=_cparams(("arbitrary",)),
        name="final_norm",
    )(x, y, g)


def _regroup_w_in(w):
    o_mi = 3 * ML_W
    o_mo = o_mi + 2 * ML_HEADS
    o_gq = o_mo + ML_W
    o_glr = o_gq + 2 * GLA_KW + GLA_VW
    o_go = o_glr + GLA_RANK
    main = jnp.concatenate([w[:, :o_mi], w[:, o_mo:o_glr], w[:, o_go:]], axis=1)
    small = jnp.concatenate([w[:, o_mi:o_mo], w[:, o_glr:o_go],
                             jnp.zeros((w.shape[0], LANES - 2 * ML_HEADS - GLA_RANK), w.dtype)], axis=1)
    return main.astype(bf16), small.astype(bf16)


def _layer_weights(l, norm1_g, w_in, if_bias, ml_g, gla_gw, gla_gb, gla_g, pool_w, pool_scale,
                   w_branch, w_out):
    w_main, w_small = _regroup_w_in(w_in[l])
    ifb = jnp.zeros((1, LANES), f32)
    ifb = ifb.at[0, S_MI:S_MI + ML_HEADS].set(if_bias[l, 0]).at[0, S_MF:S_MF + ML_HEADS].set(if_bias[l, 1])
    gw = jnp.zeros((LANES, GLA_KW), f32).at[S_GLR:S_GLR + GLA_RANK].set(gla_gw[l]).astype(bf16)
    return dict(
        norm1_g=norm1_g[l][None], w_main=w_main, w_small=w_small, ifb=ifb, ml_g=ml_g[l][None],
        gla_gw=gw, gla_gb=gla_gb[l][None], gla_g=gla_g[l][None], pool_w=pool_w[l].astype(bf16),
        pool_scale=pool_scale[l][None], w_branch=w_branch[l].astype(bf16), w_out=w_out[l].astype(bf16))


def _forward(x_prompt, x_sample, state_mlstm_C, state_mlstm_n, state_mlstm_m, state_gla_S,
             state_pool_buf, norm1_g, w_in, mlstm_if_bias, mlstm_norm_g, gla_gate_w, gla_gate_b,
             gla_norm_g, pool_w, pool_scale, w_branch, w_out, norm2_g, ffn_wg, ffn_wu, ffn_wd,
             router_w, moe_wg, moe_wu, moe_wd, final_norm_g, *, tm, tn, tt, chunk, sub, sb, tf_dense, moe_sub, moe_cap):
    batch, seq, _ = x_prompt.shape
    dec = x_sample.shape[0]
    depth = w_in.shape[0]
    m_prompt = batch * seq
    m_all = m_prompt + dec
    mp = -(-m_all // tm) * tm
    x = jnp.concatenate([x_prompt.reshape(m_prompt, D_MODEL), x_sample.reshape(dec, D_MODEL),
                         jnp.zeros((mp - m_all, D_MODEL), f32)], axis=0)
    outs = [[] for _ in range(10)]
    moe_out = None
    st = (state_mlstm_C, state_mlstm_n, jnp.pad(state_mlstm_m, ((0, 0), (0, 0), (0, LANES - ML_HEADS))),
          state_gla_S, state_pool_buf)
    for l in range(depth):
        lw = _layer_weights(l, norm1_g, w_in, mlstm_if_bias, mlstm_norm_g, gla_gate_w, gla_gate_b,
                            gla_norm_g, pool_w, pool_scale, w_branch, w_out)
        z, zs = _norm_matmul(x, lw["norm1_g"], lw["w_main"], lw["w_small"], tm, tn)
        br_p, c_p, n_p, m_p, s_p, buf_p = _prompt_mix(z, zs, lw, batch, seq, tt, chunk, sub)
        br_t, c_s, n_s, m_s, s_s, buf_s = _sample_mix(z, zs, st, l, lw, m_prompt, dec, sb)
        x = _merge(x, br_p, br_t, z, lw["w_branch"], lw["w_out"], tm, m_all)
        j = l // 2
        if l % 2 == 0:
            x = _ffn(x, norm2_g[l][None], ffn_wg[j].astype(bf16), ffn_wu[j].astype(bf16),
                     ffn_wd[j].astype(bf16), tm, tf_dense)
        else:
            wr = jnp.pad(router_w[j], ((0, 0), (0, LANES - N_EXPERTS))).astype(bf16)
            hb, comb, rnk, rnkt, cnt = _router(x, norm2_g[l][None], wr, tm, m_all)
            n_sup = mp // (tm * moe_sub)
            cnt = cnt[:, 0, :N_EXPERTS].reshape(n_sup, moe_sub, N_EXPERTS).max(axis=1)
            rounds = jnp.ceil(cnt / moe_cap).astype(jnp.int32).reshape(-1)
            moe_out = _moe(hb, rnk, rnkt, comb, rounds, moe_wg[j].astype(bf16), moe_wu[j].astype(bf16),
                           moe_wd[j].astype(bf16), tm, moe_sub, moe_cap)
            if l + 1 < depth:
                x = x + moe_out
                moe_out = None
        for lst, val in zip(outs, (c_p, c_s, n_p, n_s, m_p[:, :ML_HEADS, 0], m_s[:, :ML_HEADS],
                                   s_p, s_s, buf_p, buf_s)):
            lst.append(val)
    if moe_out is None:
        moe_out = jnp.zeros_like(x)
    y_prompt, y_sample = _final_norm(x, moe_out, final_norm_g[None], tm, m_prompt, dec)
    return ((y_prompt.reshape(batch, seq, D_MODEL), y_sample.reshape(dec, 1, D_MODEL))
            + tuple(jnp.stack(o) for o in outs))


def kernel(x_prompt, x_sample, state_mlstm_C, state_mlstm_n, state_mlstm_m, state_gla_S, state_pool_buf, norm1_g, w_in, mlstm_if_bias, mlstm_norm_g, gla_gate_w, gla_gate_b, gla_norm_g, pool_w, pool_scale, w_branch, w_out, norm2_g, ffn_wg, ffn_wu, ffn_wd, router_w, moe_wg, moe_wu, moe_wd, final_norm_g):
    return _forward(x_prompt, x_sample, state_mlstm_C, state_mlstm_n, state_mlstm_m, state_gla_S,
                    state_pool_buf, norm1_g, w_in, mlstm_if_bias, mlstm_norm_g, gla_gate_w, gla_gate_b,
                    gla_norm_g, pool_w, pool_scale, w_branch, w_out, norm2_g, ffn_wg, ffn_wu, ffn_wd,
                    router_w, moe_wg, moe_wu, moe_wd, final_norm_g,
                    tm=512, tn=3584, tt=512, chunk=128, sub=64, sb=16, tf_dense=1408, moe_sub=3, moe_cap=160)
```

```python
import functools

import jax
import jax.numpy as jnp
from jax import lax
from jax.experimental import pallas as pl
from jax.experimental.pallas import tpu as pltpu

f32 = jnp.float32
bf16 = jnp.bfloat16

D_MODEL = 1024
ML_HEADS, ML_DH = 4, 128
ML_W = ML_HEADS * ML_DH
GLA_HEADS, GLA_DK, GLA_DV = 4, 64, 128
GLA_KW, GLA_VW = GLA_HEADS * GLA_DK, GLA_HEADS * GLA_DV
GLA_RANK = 16
GLA_TAU = 16.0
POOL_GDIM = 128
POOL_WINDOWS = (2, 4, 8, 16)
POOL_W = POOL_GDIM * len(POOL_WINDOWS)
POOL_BUF = 15
N_BRANCH = 3
N_EXPERTS = 8
EPS = 1e-6
NEG = -1e30
LANES = 128

C_MQ, C_MK, C_MV, C_MO = 0, 512, 1024, 1536
C_GQ, C_GK, C_GV, C_GO, C_PU, C_GATES = 2048, 2304, 2560, 3072, 3584, 4096
Z_MAIN = C_GATES + N_BRANCH * D_MODEL
S_MI, S_MF, S_GLR = 0, 4, 8

VMEM_LIMIT = 56 * 1024 * 1024

_NT = (((1,), (1,)), ((), ()))


def _cparams(sem):
    return pltpu.CompilerParams(dimension_semantics=sem, vmem_limit_bytes=VMEM_LIMIT)


def _log_sigmoid(x):
    return jnp.minimum(x, 0.0) - jnp.log(1.0 + jnp.exp(-jnp.abs(x)))


def _sigmoid(x):
    return 1.0 / (1.0 + jnp.exp(-x))


def _rms(x, g):
    ms = jnp.mean(x * x, axis=-1, keepdims=True)
    return x * lax.rsqrt(ms + EPS) * g


def _lower_tri(n):
    r = lax.broadcasted_iota(jnp.int32, (n, n), 0)
    c = lax.broadcasted_iota(jnp.int32, (n, n), 1)
    return c <= r


def _cumsum_rows(tri_bf16, a):
    a1 = a.astype(bf16)
    r = a - a1.astype(f32)
    a2 = r.astype(bf16)
    a3 = (r - a2.astype(f32)).astype(bf16)
    d = lambda y: jnp.dot(tri_bf16, y, preferred_element_type=f32)
    return d(a1) + d(a2) + d(a3)


def _resident(shape):
    nd = len(shape)
    return pl.BlockSpec(shape, lambda *_: (0,) * nd, pipeline_mode=pl.Buffered(1))


def _norm_matmul_kernel(x_ref, g_ref, w_ref, ws_ref, z_ref, zs_ref, *, tn):
    h = _rms(x_ref[...], g_ref[...]).astype(bf16)
    zs_ref[...] = jnp.dot(h, ws_ref[...], preferred_element_type=f32)
    for c in range(w_ref.shape[1] // tn):
        cols = slice(c * tn, (c + 1) * tn)
        z_ref[:, cols] = jnp.dot(h, w_ref[:, cols], preferred_element_type=f32).astype(z_ref.dtype)


def _norm_matmul(x, g, w_main, w_small, tm, tn):
    mp = x.shape[0]
    n = w_main.shape[1]
    return pl.pallas_call(
        functools.partial(_norm_matmul_kernel, tn=tn),
        grid=(mp // tm,),
        in_specs=[
            pl.BlockSpec((tm, D_MODEL), lambda i: (i, 0)),
            _resident((1, D_MODEL)),
            _resident((D_MODEL, n)),
            _resident((D_MODEL, LANES)),
        ],
        out_specs=[
            pl.BlockSpec((tm, n), lambda i: (i, 0)),
            pl.BlockSpec((tm, LANES), lambda i: (i, 0)),
        ],
        out_shape=[jax.ShapeDtypeStruct((mp, n), bf16), jax.ShapeDtypeStruct((mp, LANES), f32)],
        compiler_params=_cparams(("parallel",)),
        name="norm_matmul",
    )(x, g, w_main, w_small)


def _head_norm(h, g):
    outs = []
    for j in range(h.shape[1] // LANES):
        hj = h[:, j * LANES:(j + 1) * LANES]
        outs.append(hj * lax.rsqrt(jnp.mean(hj * hj, axis=-1, keepdims=True) + EPS))
    return jnp.concatenate(outs, axis=1) * g


def _gla_log_decay(sm, gw_ref, gb_ref):
    xg = jnp.dot(sm.astype(bf16), gw_ref[...], preferred_element_type=f32) + gb_ref[...]
    return _log_sigmoid(xg) * (1.0 / GLA_TAU)


N_ZBLOCKS = 9


def _prompt_mix_kernel(*refs, nb, tt, chunk, sub, n_t):
    z_refs = [refs[s * N_ZBLOCKS:(s + 1) * N_ZBLOCKS] for s in range(nb)]
    ifb_ref, mlg_ref, gw_ref, gb_ref, glag_ref, pw_ref, ps_ref = refs[nb * N_ZBLOCKS:nb * N_ZBLOCKS + 7]
    br_all, c_out, n_out, m_out, s_out, buf_out = refs[nb * N_ZBLOCKS + 7:nb * N_ZBLOCKS + 13]
    scratch = refs[nb * N_ZBLOCKS + 13:]
    per = ML_HEADS + GLA_HEADS + 3
    c_refs = [scratch[s * per:s * per + ML_HEADS] for s in range(nb)]
    s_refs = [scratch[s * per + ML_HEADS:s * per + ML_HEADS + GLA_HEADS] for s in range(nb)]
    n_refs = [scratch[s * per + ML_HEADS + GLA_HEADS] for s in range(nb)]
    m_refs = [scratch[s * per + ML_HEADS + GLA_HEADS + 1] for s in range(nb)]
    ext_refs = [scratch[s * per + ML_HEADS + GLA_HEADS + 2] for s in range(nb)]
    t_idx = pl.program_id(1)
    L = chunk

    @pl.when(t_idx == 0)
    def _():
        for s in range(nb):
            for r in (*c_refs[s], *s_refs[s], n_refs[s], m_refs[s]):
                r[...] = jnp.zeros_like(r)
            ext_refs[s][0:16, :] = jnp.zeros((16, POOL_W), f32)

    tri = _lower_tri(L)
    tri_b = tri.astype(bf16)
    causal_sub = _lower_tri(sub)
    assert L == LANES
    ones_b = jnp.ones((L, LANES), bf16)
    sel8 = (lax.broadcasted_iota(jnp.int32, (8, LANES), 0)
            == lax.broadcasted_iota(jnp.int32, (8, LANES), 1)).astype(f32)
    ifb = ifb_ref[...]
    k_scale = ML_DH ** -0.5
    q_scale = GLA_DK ** -0.5

    def one_seq(c, seq):
        q_ref, k_ref, v_ref, mo_ref, gqk_ref, gv_ref, go_ref, _, sm_ref = z_refs[seq]
        br_ref, c_s, s_s = br_all.at[seq], c_refs[seq], s_refs[seq]
        n_old, m_old = n_refs[seq][...], m_refs[seq][...]
        n_rows, m_rows = [], []
        r0 = pl.multiple_of(c * L, L)
        rows = pl.ds(r0, L)
        sm = sm_ref[rows, :]

        y0 = sm + ifb
        bc = _cumsum_rows(tri_b, _log_sigmoid(y0))
        y0t = lax.dot_general(sel8, y0, _NT, preferred_element_type=f32, precision=lax.Precision.HIGHEST)
        bct = lax.dot_general(sel8, bc, _NT, preferred_element_type=f32, precision=lax.Precision.HIGHEST)
        hm = []
        for h in range(ML_HEADS):
            hs = slice(h * ML_DH, (h + 1) * ML_DH)
            v = v_ref[rows, hs].astype(f32)
            kb = k_ref[rows, hs].astype(bf16)
            qb, vb = q_ref[rows, hs].astype(bf16), v.astype(bf16)
            bcol = bc[:, S_MF + h:S_MF + h + 1]
            icol = y0[:, S_MI + h:S_MI + h + 1]
            brow = bct[S_MF + h:S_MF + h + 1, :]
            irow = y0t[S_MI + h:S_MI + h + 1, :]
            m_row = m_old[h:h + 1, :]
            cmat = c_s[h][...]
            nrow = n_old[h:h + 1, :]
            bcol_r = jnp.broadcast_to(bcol, (L, LANES))
            dm = jnp.where(tri, bcol_r - brow + irow, NEG)
            inter_r = bcol_r + m_row
            m_t_r = jnp.maximum(inter_r, jnp.broadcast_to(jnp.max(dm, axis=1, keepdims=True), (L, LANES)))
            w_intra = jnp.exp(dm - m_t_r) * k_scale
            w_inter_r = jnp.exp(inter_r - m_t_r)
            s = lax.dot_general(qb, kb, _NT, preferred_element_type=f32) * w_intra
            s_hi = s.astype(bf16)
            s_lo = (s - s_hi.astype(f32)).astype(bf16)
            r_intra = jnp.dot(s_hi, jnp.concatenate([vb, ones_b], axis=1), preferred_element_type=f32)
            c_aug = jnp.concatenate([cmat, jnp.broadcast_to(nrow, (LANES, LANES))], axis=0).astype(bf16)
            r_inter = lax.dot_general(qb, c_aug, _NT, preferred_element_type=f32)
            rs_lo = jnp.dot(s_lo, ones_b, preferred_element_type=f32)
            num = r_intra[:, 0:LANES] + w_inter_r * r_inter[:, 0:LANES]
            den_r = r_intra[:, LANES:] + rs_lo + w_inter_r * r_inter[:, LANES:]
            hm.append(num / jnp.maximum(jnp.abs(den_r), jnp.exp(-m_t_r)))
            b_last = bcol[L - 1:L, :]
            m = m_row[:, 0:1]
            g = b_last - bcol + icol
            m_new = jnp.maximum(b_last + m, jnp.max(g, axis=0, keepdims=True))
            w_s = jnp.broadcast_to(jnp.exp(g - m_new) * k_scale, (L, LANES))
            w_c = jnp.exp(b_last + m - m_new)
            vwt = (v * w_s).T.astype(bf16)
            c_s[h][...] = w_c * cmat + jnp.dot(vwt, kb, preferred_element_type=f32)
            n_rows.append(w_c * nrow + jnp.sum(w_s * k_ref[rows, hs].astype(f32), axis=0, keepdims=True))
            m_rows.append(jnp.broadcast_to(m_new, (1, LANES)))
        pad_rows = [jnp.zeros((8 - ML_HEADS, LANES), f32)]
        n_refs[seq][...] = jnp.concatenate(n_rows + pad_rows, axis=0)
        m_refs[seq][...] = jnp.concatenate(m_rows + pad_rows, axis=0)
        y_ml = (_head_norm(jnp.concatenate(hm, axis=1), mlg_ref[...])
                * _sigmoid(mo_ref[rows, :].astype(f32)))
        br_ref[rows, 0:ML_W] = y_ml.astype(br_ref.dtype)

        log_a = _gla_log_decay(sm, gw_ref, gb_ref)
        b = _cumsum_rows(tri_b, log_a)
        q2 = gqk_ref[rows, 0:GLA_KW].astype(f32) * q_scale
        k2 = gqk_ref[rows, GLA_KW:2 * GLA_KW].astype(f32)
        gv = gv_ref[rows, :].astype(bf16)
        s_old = [s_s[h][...] for h in range(GLA_HEADS)]
        qe_chunk = (q2 * jnp.exp(b)).astype(bf16)
        b_last = b[L - 1:L, :]
        kdt = (k2 * jnp.exp(b_last - b)).T.astype(bf16)
        decay_col = jnp.exp(b.T[:, L - 1:L])
        o_blocks = []
        for blk in range(L // sub):
            s0, s1 = blk * sub, (blk + 1) * sub
            mid = s0 + sub // 2
            b_blk = b[s0:s1]
            b_mid = b[mid - 1:mid, :]
            qe_d = (q2[s0:s1] * jnp.exp(b_blk - b_mid)).astype(bf16)
            ke_d = (k2[s0:s1] * jnp.exp(b_mid - b_blk)).astype(bf16)
            if blk > 0:
                b_start = b[s0 - 1:s0, :]
                qe_o = (q2[s0:s1] * jnp.exp(b_blk - b_start)).astype(bf16)
                ke_o = (k2[0:s0] * jnp.exp(b_start - b[0:s0])).astype(bf16)
            o_heads = []
            for h in range(GLA_HEADS):
                ks = slice(h * GLA_DK, (h + 1) * GLA_DK)
                vs = slice(h * GLA_DV, (h + 1) * GLA_DV)
                a = lax.dot_general(qe_d[:, ks], ke_d[:, ks], _NT, preferred_element_type=f32)
                a = jnp.where(causal_sub, a, 0.0)
                o = (jnp.dot(a.astype(bf16), gv[s0:s1, vs], preferred_element_type=f32)
                     + jnp.dot(qe_chunk[s0:s1, ks], s_old[h].astype(bf16), preferred_element_type=f32))
                if blk > 0:
                    a = lax.dot_general(qe_o[:, ks], ke_o[:, ks], _NT, preferred_element_type=f32)
                    o = o + jnp.dot(a.astype(bf16), gv[0:s0, vs], preferred_element_type=f32)
                o_heads.append(o)
            o_blocks.append(jnp.concatenate(o_heads, axis=1))
        og = jnp.concatenate(o_blocks, axis=0)
        for h in range(GLA_HEADS):
            ks = slice(h * GLA_DK, (h + 1) * GLA_DK)
            vs = slice(h * GLA_DV, (h + 1) * GLA_DV)
            s_s[h][...] = (decay_col[ks, :] * s_old[h]
                           + jnp.dot(kdt[ks, :], gv[:, vs], preferred_element_type=f32))
        go = go_ref[rows, :].astype(f32)
        y_gla = _head_norm(og, glag_ref[...]) * (go * _sigmoid(go))
        br_ref[rows, ML_W:ML_W + GLA_VW] = y_gla.astype(br_ref.dtype)

    def chunk_body(c, carry):
        for s in range(nb):
            one_seq(c, s)
        return carry

    lax.fori_loop(0, tt // L, chunk_body, 0)

    pos = t_idx * tt + lax.broadcasted_iota(jnp.int32, (tt, 1), 0)
    for s in range(nb):
        pu_ref, br_ref, ext_s = z_refs[s][7], br_all.at[s], ext_refs[s]
        ext_s[16:16 + tt, :] = pu_ref[...].astype(f32)
        for g, w in enumerate(POOL_WINDOWS):
            gs = slice(g * POOL_GDIM, (g + 1) * POOL_GDIM)
            u = ext_s[16:16 + tt, gs]
            acc = u
            for j in range(1, w):
                acc = acc + ext_s[16 - j:16 - j + tt, gs]
            cnt = jnp.minimum(pos + 1, w).astype(f32)
            d = acc / cnt - u
            y = jnp.dot(d.astype(bf16), pw_ref[g], preferred_element_type=f32) * ps_ref[:, gs]
            br_ref[:, ML_W + GLA_VW + g * POOL_GDIM:ML_W + GLA_VW + (g + 1) * POOL_GDIM] = y.astype(br_ref.dtype)
        ext_s[0:16, :] = ext_s[tt:tt + 16, :]

    @pl.when(t_idx == n_t - 1)
    def _():
        for s in range(nb):
            for h in range(ML_HEADS):
                c_out[s, h] = c_refs[s][h][...]
            for h in range(GLA_HEADS):
                s_out[s, h] = s_refs[s][h][...]
            n_out[s] = n_refs[s][0:ML_HEADS, :]
            m_out[s] = m_refs[s][...]
            buf_out[s] = ext_refs[s][1:16, :]


def _prompt_mix(z, zs, lw, batch, seq, tt, chunk, sub, nb):
    n_t = seq // tt
    assert batch % nb == 0

    def zspecs(s):
        row = lambda b, t: (b * nb + s) * n_t + t
        spec = lambda col, width: pl.BlockSpec((tt, width), lambda b, t: (row(b, t), col // width))
        return [spec(C_MQ, 512), spec(C_MK, 512), spec(C_MV, 512), spec(C_MO, 512),
                spec(C_GQ, 512), spec(C_GV, 512), spec(C_GO, 512), spec(C_PU, 512),
                pl.BlockSpec((tt, LANES), lambda b, t: (row(b, t), 0))]

    def full(shape):
        nd = len(shape)
        return pl.BlockSpec(shape, lambda b, t: (0,) * nd)

    def per_seq(shape):
        nd = len(shape)
        return pl.BlockSpec((nb,) + shape, lambda b, t: (b,) + (0,) * nd)

    kern = functools.partial(_prompt_mix_kernel, nb=nb, tt=tt, chunk=chunk, sub=sub, n_t=n_t)
    outs = pl.pallas_call(
        kern,
        grid=(batch // nb, n_t),
        in_specs=[sp for s in range(nb) for sp in zspecs(s)]
                 + [full((1, LANES)), full((1, ML_W)), full((LANES, GLA_KW)), full((1, GLA_KW)),
                    full((1, GLA_VW)), full((len(POOL_WINDOWS), POOL_GDIM, POOL_GDIM)), full((1, POOL_W))],
        out_specs=[
            pl.BlockSpec((nb, tt, N_BRANCH * ML_W), lambda b, t: (b, t, 0)),
            per_seq((ML_HEADS, ML_DH, ML_DH)), per_seq((ML_HEADS, ML_DH)), per_seq((8, LANES)),
            per_seq((GLA_HEADS, GLA_DK, GLA_DV)), per_seq((POOL_BUF, POOL_W)),
        ],
        out_shape=[
            jax.ShapeDtypeStruct((batch, seq, N_BRANCH * ML_W), bf16),
            jax.ShapeDtypeStruct((batch, ML_HEADS, ML_DH, ML_DH), f32),
            jax.ShapeDtypeStruct((batch, ML_HEADS, ML_DH), f32),
            jax.ShapeDtypeStruct((batch, 8, LANES), f32),
            jax.ShapeDtypeStruct((batch, GLA_HEADS, GLA_DK, GLA_DV), f32),
            jax.ShapeDtypeStruct((batch, POOL_BUF, POOL_W), f32),
        ],
        scratch_shapes=([pltpu.VMEM((ML_DH, ML_DH), f32)] * ML_HEADS
                        + [pltpu.VMEM((GLA_DK, GLA_DV), f32)] * GLA_HEADS
                        + [pltpu.VMEM((8, LANES), f32), pltpu.VMEM((8, LANES), f32),
                           pltpu.VMEM((tt + 16, POOL_W), f32)]) * nb,
        compiler_params=_cparams(("parallel", "arbitrary")),
        name="prompt_mix",
    )(*([z] * 8 + [zs]) * nb,
      lw["ifb"], lw["ml_g"], lw["gla_gw"], lw["gla_gb"], lw["gla_g"], lw["pool_w"], lw["pool_scale"])
    return (outs[0].reshape(batch * seq, N_BRANCH * ML_W),) + tuple(outs[1:])


def _sample_mix_kernel(q_ref, k_ref, v_ref, mo_ref, gqk_ref, gv_ref, go_ref, pu_ref, sm_ref,
                       c_in, n_in, m_in, s_in, buf_in,
                       ifb_ref, mlg_ref, gw_ref, gb_ref, glag_ref, pw_ref, ps_ref, *tail, sb, dec):
    br_all = tail[-6]
    c_out, n_out, m_out, s_out, buf_out = (r.at[0] for r in tail[-5:])
    step = pl.program_id(0)
    c_in, n_in, m_in, s_in, buf_in = (r.at[0] for r in (c_in, n_in, m_in, s_in, buf_in))
    br_ref = br_all.at[pl.ds(pl.multiple_of(step * sb, sb), sb), :]

    @pl.when(step == 0)
    def _():
        br_all[dec:, :] = jnp.zeros((br_all.shape[0] - dec, br_all.shape[1]), br_all.dtype)

    sm = sm_ref[...]
    eye = (lax.broadcasted_iota(jnp.int32, (LANES, LANES), 0)
           == lax.broadcasted_iota(jnp.int32, (LANES, LANES), 1)).astype(f32)

    def to_cols(x):
        parts = [lax.dot_general(eye, x[:, j * LANES:(j + 1) * LANES], _NT,
                                 preferred_element_type=f32, precision=lax.Precision.HIGHEST)
                 for j in range(x.shape[1] // LANES)]
        return jnp.concatenate(parts, axis=0)

    y0 = sm + ifb_ref[...]
    logf_all = _log_sigmoid(y0)
    k_scale = ML_DH ** -0.5
    m_all = m_in[...]
    hm, m_new_cols = [], []
    for h in range(ML_HEADS):
        hs = slice(h * ML_DH, (h + 1) * ML_DH)
        q = q_ref[:, hs].astype(f32)
        k = k_ref[:, hs].astype(f32) * k_scale
        v = v_ref[:, hs].astype(f32)
        i_pre = y0[:, S_MI + h:S_MI + h + 1]
        logf = logf_all[:, S_MF + h:S_MF + h + 1]
        m = m_all[:, h:h + 1]
        inter = logf + m
        m_t = jnp.maximum(inter, i_pre)
        w_intra = jnp.exp(i_pre - m_t)
        w_inter = jnp.exp(inter - m_t)
        s = jnp.sum(q * k, axis=1, keepdims=True) * w_intra
        qb = q.astype(bf16)
        n_h = n_in[:, h, :]
        cq = jnp.concatenate(
            [lax.dot_general(qb, c_in[j, h].astype(bf16), _NT, preferred_element_type=f32)[j:j + 1, :]
             for j in range(sb)], axis=0)
        num = s * v + w_inter * cq
        den = s + w_inter * jnp.sum(n_h * q, axis=1, keepdims=True)
        hm.append(num / jnp.maximum(jnp.abs(den), jnp.exp(-m_t)))
        m_new = m_t
        w_s = w_intra
        w_c = w_inter
        n_out[:, h, :] = w_c * n_h + w_s * k
        m_new_cols.append(m_new)
        vw_cols = to_cols(v * w_s)
        for j in range(sb):
            c_out[j, h] = w_c[j:j + 1, :] * c_in[j, h] + vw_cols[:, j:j + 1] * k[j:j + 1, :]
    lane = lax.broadcasted_iota(jnp.int32, (sb, LANES), 1)
    m_pack = jnp.zeros((sb, LANES), f32)
    for h in range(ML_HEADS):
        m_pack = jnp.where(lane == h, m_new_cols[h], m_pack)
    m_out[...] = m_pack
    y_ml = _head_norm(jnp.concatenate(hm, axis=1), mlg_ref[...]) * _sigmoid(mo_ref[...].astype(f32))
    br_ref[:, 0:ML_W] = y_ml.astype(br_ref.dtype)

    log_a = _gla_log_decay(sm, gw_ref, gb_ref)
    decay = jnp.exp(log_a)
    q2 = gqk_ref[:, 0:GLA_KW].astype(f32) * (GLA_DK ** -0.5)
    k2 = gqk_ref[:, GLA_KW:2 * GLA_KW].astype(f32)
    gv = gv_ref[...].astype(f32)
    qe = (q2 * decay).astype(bf16)
    qk = q2 * k2
    k_cols = to_cols(k2)
    decay_cols = to_cols(decay)
    og = []
    for h in range(GLA_HEADS):
        ks = slice(h * GLA_DK, (h + 1) * GLA_DK)
        vs = slice(h * GLA_DV, (h + 1) * GLA_DV)
        a = jnp.sum(qk[:, ks], axis=1, keepdims=True)
        inter = jnp.concatenate(
            [jnp.dot(qe[:, ks], s_in[j, h].astype(bf16), preferred_element_type=f32)[j:j + 1, :]
             for j in range(sb)], axis=0)
        og.append(a * gv[:, vs] + inter)
        for j in range(sb):
            s_out[j, h] = (decay_cols[ks, j:j + 1] * s_in[j, h]
                           + k_cols[ks, j:j + 1] * gv[j:j + 1, vs])
    go = go_ref[...].astype(f32)
    y_gla = _head_norm(jnp.concatenate(og, axis=1), glag_ref[...]) * (go * _sigmoid(go))
    br_ref[:, ML_W:ML_W + GLA_VW] = y_gla.astype(br_ref.dtype)

    u = pu_ref[...].astype(f32)
    rowi = lax.broadcasted_iota(jnp.int32, (POOL_BUF + 1, POOL_GDIM), 0)
    d_rows = []
    for j in range(sb):
        ext = jnp.concatenate([buf_in[j], u[j:j + 1, :]], axis=0)
        buf_out[j] = ext[1:POOL_BUF + 1, :]
        parts = []
        for g, w in enumerate(POOL_WINDOWS):
            gs = slice(g * POOL_GDIM, (g + 1) * POOL_GDIM)
            win = jnp.sum(jnp.where(rowi >= POOL_BUF + 1 - w, ext[:, gs], 0.0), axis=0, keepdims=True)
            parts.append(win / float(w) - u[j:j + 1, gs])
        d_rows.append(jnp.concatenate(parts, axis=1))
    d = jnp.concatenate(d_rows, axis=0)
    for g in range(len(POOL_WINDOWS)):
        gs = slice(g * POOL_GDIM, (g + 1) * POOL_GDIM)
        y = jnp.dot(d[:, gs].astype(bf16), pw_ref[g], preferred_element_type=f32) * ps_ref[:, gs]
        br_ref[:, ML_W + GLA_VW + g * POOL_GDIM:ML_W + GLA_VW + (g + 1) * POOL_GDIM] = y.astype(br_ref.dtype)


def _sample_mix(z, zs, st, prev, layer, lw, row0, dec, sb):
    depth = st[0].shape[0]
    base = row0 // sb
    tail_rows = z.shape[0] - row0

    def zspec(col, width):
        blk = col // width
        return pl.BlockSpec((sb, width), lambda i: (base + i, blk))

    def full(shape):
        nd = len(shape)
        return pl.BlockSpec(shape, lambda i: (0,) * nd)

    st_shapes = [(ML_HEADS, ML_DH, ML_DH), (ML_HEADS, ML_DH), (LANES,),
                 (GLA_HEADS, GLA_DK, GLA_DV), (POOL_BUF, POOL_W)]
    st_specs = [pl.BlockSpec((1, sb) + s, functools.partial(lambda i, nd: (layer, i) + (0,) * nd, nd=len(s)))
                for s in st_shapes]
    n_in = N_ZBLOCKS + len(st_shapes) + 7
    prev = () if prev is None else tuple(prev)
    return pl.pallas_call(
        functools.partial(_sample_mix_kernel, sb=sb, dec=dec),
        grid=(dec // sb,),
        in_specs=[zspec(C_MQ, 512), zspec(C_MK, 512), zspec(C_MV, 512), zspec(C_MO, 512),
                  zspec(C_GQ, 512), zspec(C_GV, 512), zspec(C_GO, 512), zspec(C_PU, 512),
                  pl.BlockSpec((sb, LANES), lambda i: (base + i, 0))]
                 + st_specs
                 + [full((1, LANES)), full((1, ML_W)), full((LANES, GLA_KW)), full((1, GLA_KW)),
                    full((1, GLA_VW)), full((len(POOL_WINDOWS), POOL_GDIM, POOL_GDIM)), full((1, POOL_W))]
                 + [pl.BlockSpec(memory_space=pl.ANY)] * len(prev),
        out_specs=[pl.BlockSpec((tail_rows, N_BRANCH * ML_W), lambda i: (0, 0))] + st_specs,
        out_shape=[jax.ShapeDtypeStruct((tail_rows, N_BRANCH * ML_W), bf16)]
                  + [jax.ShapeDtypeStruct((depth, dec) + s, f32) for s in st_shapes],
        input_output_aliases={n_in + j: 1 + j for j in range(len(prev))},
        compiler_params=_cparams(("arbitrary",)),
        name="sample_mix",
    )(z, z, z, z, z, z, z, z, zs, *st,
      lw["ifb"], lw["ml_g"], lw["gla_gw"], lw["gla_gb"], lw["gla_g"], lw["pool_w"], lw["pool_scale"], *prev)


def _merge_kernel(x_ref, brp_ref, brt_ref, g0_ref, g1_ref, g2_ref, wb_ref, wo_ref, o_ref,
                  *, tm, n_valid, n_prompt_tiles):
    in_tail = pl.program_id(0) >= n_prompt_tiles
    mixed = None
    for n, g_ref in enumerate((g0_ref, g1_ref, g2_ref)):
        cols = slice(n * ML_W, (n + 1) * ML_W)
        br = jnp.where(in_tail, brt_ref[:, cols], brp_ref[:, cols])
        proj = jnp.dot(br, wb_ref[n], preferred_element_type=f32)
        term = _sigmoid(g_ref[...].astype(f32)) * proj
        mixed = term if mixed is None else mixed + term
    out = x_ref[...] + jnp.dot(mixed.astype(bf16), wo_ref[...], preferred_element_type=f32)
    row = pl.program_id(0) * tm + lax.broadcasted_iota(jnp.int32, (tm, 1), 0)
    o_ref[...] = jnp.where(row < n_valid, out, 0.0)


def _merge(x, br_prompt, br_tail, z, w_branch, w_out, tm, n_valid):
    mp = x.shape[0]
    n_p = br_prompt.shape[0] // tm
    gate_specs = [pl.BlockSpec((tm, D_MODEL), functools.partial(lambda i, n: (i, C_GATES // D_MODEL + n), n=n))
                  for n in range(N_BRANCH)]
    return pl.pallas_call(
        functools.partial(_merge_kernel, tm=tm, n_valid=n_valid, n_prompt_tiles=n_p),
        grid=(mp // tm,),
        in_specs=[
            pl.BlockSpec((tm, D_MODEL), lambda i: (i, 0)),
            pl.BlockSpec((tm, N_BRANCH * ML_W), lambda i: (jnp.minimum(i, n_p - 1), 0)),
            pl.BlockSpec((tm, N_BRANCH * ML_W), lambda i: (jnp.maximum(i - n_p, 0), 0)),
            *gate_specs,
            _resident((N_BRANCH, ML_W, D_MODEL)),
            _resident((D_MODEL, D_MODEL)),
        ],
        out_specs=pl.BlockSpec((tm, D_MODEL), lambda i: (i, 0)),
        out_shape=jax.ShapeDtypeStruct((mp, D_MODEL), f32),
        compiler_params=_cparams(("parallel",)),
        name="merge",
    )(x, br_prompt, br_tail, z, z, z, w_branch, w_out)


def _router_kernel(x_ref, g_ref, wr_ref, hb_ref, comb_ref, rnk_ref, rnkt_ref, cnt_ref, *, tm, n_valid):
    h = _rms(x_ref[...], g_ref[...]).astype(bf16)
    hb_ref[...] = h
    logits = jnp.dot(h, wr_ref[...], preferred_element_type=f32)
    lane = lax.broadcasted_iota(jnp.int32, logits.shape, 1)
    valid = lane < N_EXPERTS
    logits = jnp.where(valid, logits, NEG)
    mx = jnp.max(logits, axis=1, keepdims=True)
    e = jnp.where(valid, jnp.exp(logits - mx), 0.0)
    probs = e / jnp.sum(e, axis=1, keepdims=True)
    p1 = jnp.max(probs, axis=1, keepdims=True)
    i1 = jnp.min(jnp.where(probs == p1, lane, LANES), axis=1, keepdims=True)
    rest = jnp.where((lane == i1) | ~valid, -1.0, probs)
    p2 = jnp.max(rest, axis=1, keepdims=True)
    i2 = jnp.min(jnp.where(rest == p2, lane, LANES), axis=1, keepdims=True)
    tot = p1 + p2
    comb_ref[...] = jnp.where(lane == i1, p1 / tot, 0.0) + jnp.where(lane == i2, p2 / tot, 0.0)
    row = pl.program_id(0) * tm + lax.broadcasted_iota(jnp.int32, logits.shape, 0)
    sel = ((lane == i1) | (lane == i2)) & (row < n_valid)
    r = lax.broadcasted_iota(jnp.int32, (tm, tm), 0)
    c = lax.broadcasted_iota(jnp.int32, (tm, tm), 1)
    rank = jnp.dot((c < r).astype(bf16), sel.astype(bf16), preferred_element_type=f32)
    rnk = jnp.where(sel, rank, -1.0)
    rnk_ref[...] = rnk
    rnkt_ref[0] = rnk.T[0:N_EXPERTS, :]
    cnt_ref[0] = jnp.sum(sel.astype(f32), axis=0, keepdims=True)


def _router(x, g, wr, tm, n_valid):
    mp = x.shape[0]
    nt = mp // tm
    return pl.pallas_call(
        functools.partial(_router_kernel, tm=tm, n_valid=n_valid),
        grid=(nt,),
        in_specs=[pl.BlockSpec((tm, D_MODEL), lambda i: (i, 0)),
                  pl.BlockSpec((1, D_MODEL), lambda i: (0, 0)),
                  pl.BlockSpec((D_MODEL, LANES), lambda i: (0, 0))],
        out_specs=[pl.BlockSpec((tm, D_MODEL), lambda i: (i, 0)),
                   pl.BlockSpec((tm, LANES), lambda i: (i, 0)),
                   pl.BlockSpec((tm, LANES), lambda i: (i, 0)),
                   pl.BlockSpec((1, N_EXPERTS, tm), lambda i: (i, 0, 0)),
                   pl.BlockSpec((1, 1, LANES), lambda i: (i, 0, 0))],
        out_shape=[jax.ShapeDtypeStruct((mp, D_MODEL), bf16),
                   jax.ShapeDtypeStruct((mp, LANES), f32),
                   jax.ShapeDtypeStruct((mp, LANES), f32),
                   jax.ShapeDtypeStruct((nt, N_EXPERTS, tm), f32),
                   jax.ShapeDtypeStruct((nt, 1, LANES), f32)],
        compiler_params=_cparams(("parallel",)),
        name="router",
    )(x, g, wr)


def _moe_kernel(rounds_ref, hb_ref, rnk_ref, rnkt_ref, comb_ref, wg_ref, wu_ref, wd_ref, o_ref,
                *, cap, n_sub, ts):
    i, e = pl.program_id(0), pl.program_id(1)

    @pl.when(e == 0)
    def _():
        o_ref[...] = jnp.zeros_like(o_ref)

    lane = lax.broadcasted_iota(jnp.int32, (ts, LANES), 1)
    slot_rows = lax.broadcasted_iota(jnp.int32, (cap, ts), 0).astype(f32)
    slot_cols = lax.broadcasted_iota(jnp.int32, (ts, cap), 1).astype(f32)

    def round_body(r, carry):
        base = (r * cap).astype(f32)
        xs = []
        for j in range(n_sub):
            rt = rnkt_ref[j, pl.ds(e, 1), :]
            p = (rt - base == slot_rows).astype(bf16)
            xs.append(jnp.dot(p, hb_ref[j * ts:(j + 1) * ts, :],
                              preferred_element_type=f32).astype(bf16))
        xs = jnp.concatenate(xs, axis=0)
        a = jnp.dot(xs, wg_ref[0], preferred_element_type=f32)
        a = (a * _sigmoid(a)) * jnp.dot(xs, wu_ref[0], preferred_element_type=f32)
        y = jnp.dot(a.astype(bf16), wd_ref[0], preferred_element_type=f32).astype(bf16)
        for j in range(n_sub):
            rows = slice(j * ts, (j + 1) * ts)
            col = jnp.sum(jnp.where(lane == e, rnk_ref[rows, :], 0.0), axis=1, keepdims=True)
            w = jnp.sum(jnp.where(lane == e, comb_ref[rows, :], 0.0), axis=1, keepdims=True)
            pt = (col - base == slot_cols).astype(bf16)
            o_ref[rows, :] += w * jnp.dot(pt, y[j * cap:(j + 1) * cap, :], preferred_element_type=f32)
        return carry

    lax.fori_loop(0, rounds_ref[i * N_EXPERTS + e], round_body, 0)


def _moe(hb, rnk, rnkt, comb, rounds, wg, wu, wd, ts, n_sub, cap):
    mp = hb.shape[0]
    n_e, _, d_ff = wg.shape
    tsup = ts * n_sub
    grid_spec = pltpu.PrefetchScalarGridSpec(
        num_scalar_prefetch=1,
        grid=(mp // tsup, n_e),
        in_specs=[pl.BlockSpec((tsup, D_MODEL), lambda i, e, r: (i, 0)),
                  pl.BlockSpec((tsup, LANES), lambda i, e, r: (i, 0)),
                  pl.BlockSpec((n_sub, N_EXPERTS, ts), lambda i, e, r: (i, 0, 0)),
                  pl.BlockSpec((tsup, LANES), lambda i, e, r: (i, 0)),
                  pl.BlockSpec((1, D_MODEL, d_ff), lambda i, e, r: (e, 0, 0)),
                  pl.BlockSpec((1, D_MODEL, d_ff), lambda i, e, r: (e, 0, 0)),
                  pl.BlockSpec((1, d_ff, D_MODEL), lambda i, e, r: (e, 0, 0))],
        out_specs=pl.BlockSpec((tsup, D_MODEL), lambda i, e, r: (i, 0)),
    )
    return pl.pallas_call(
        functools.partial(_moe_kernel, cap=cap, n_sub=n_sub, ts=ts),
        grid_spec=grid_spec,
        out_shape=jax.ShapeDtypeStruct((mp, D_MODEL), f32),
        compiler_params=_cparams(("parallel", "arbitrary")),
        name="moe",
    )(rounds, hb, rnk, rnkt, comb, wg, wu, wd)


def _ffn_kernel(x_ref, g_ref, wg_ref, wu_ref, wd_ref, o_ref, *, tf):
    x = x_ref[...]
    h = _rms(x, g_ref[...]).astype(bf16)
    acc = x
    for c in range(wg_ref.shape[1] // tf):
        cols = slice(c * tf, (c + 1) * tf)
        a = jnp.dot(h, wg_ref[:, cols], preferred_element_type=f32)
        a = (a * _sigmoid(a)) * jnp.dot(h, wu_ref[:, cols], preferred_element_type=f32)
        acc = acc + jnp.dot(a.astype(bf16), wd_ref[cols, :], preferred_element_type=f32)
    o_ref[...] = acc


def _ffn(x, g, wg, wu, wd, tm, tf):
    mp = x.shape[0]
    d_ff = wg.shape[1]
    return pl.pallas_call(
        functools.partial(_ffn_kernel, tf=tf),
        grid=(mp // tm,),
        in_specs=[pl.BlockSpec((tm, D_MODEL), lambda i: (i, 0)),
                  _resident((1, D_MODEL)),
                  _resident((D_MODEL, d_ff)),
                  _resident((D_MODEL, d_ff)),
                  _resident((d_ff, D_MODEL))],
        out_specs=pl.BlockSpec((tm, D_MODEL), lambda i: (i, 0)),
        out_shape=jax.ShapeDtypeStruct((mp, D_MODEL), f32),
        compiler_params=_cparams(("parallel",)),
        name="ffn",
    )(x, g, wg, wu, wd)


def _final_norm_kernel(x_ref, y_ref, g_ref, op_ref, os_ref, *, n_prompt_tiles, dec):
    i = pl.program_id(0)
    out = _rms(x_ref[...] + y_ref[...], g_ref[...])

    @pl.when(i < n_prompt_tiles)
    def _():
        op_ref[...] = out

    @pl.when(i == n_prompt_tiles)
    def _():
        os_ref[...] = out[0:dec]


def _final_norm(x, y, g, tm, m_prompt, dec):
    n_p = m_prompt // tm
    assert n_p * tm == m_prompt and dec <= tm
    return pl.pallas_call(
        functools.partial(_final_norm_kernel, n_prompt_tiles=n_p, dec=dec),
        grid=(n_p + 1,),
        in_specs=[pl.BlockSpec((tm, D_MODEL), lambda i: (i, 0)),
                  pl.BlockSpec((tm, D_MODEL), lambda i: (i, 0)),
                  pl.BlockSpec((1, D_MODEL), lambda i: (0, 0))],
        out_specs=[pl.BlockSpec((tm, D_MODEL), lambda i: (jnp.minimum(i, n_p - 1), 0)),
                   pl.BlockSpec((dec, D_MODEL), lambda i: (0, 0))],
        out_shape=[jax.ShapeDtypeStruct((m_prompt, D_MODEL), f32),
                   jax.ShapeDtypeStruct((dec, D_MODEL), f32)],
        compiler_params=_cparams(("arbitrary",)),
        name="final_norm",
    )(x, y, g)


def _regroup_w_in(w):
    o_mi = 3 * ML_W
    o_mo = o_mi + 2 * ML_HEADS
    o_gq = o_mo + ML_W
    o_glr = o_gq + 2 * GLA_KW + GLA_VW
    o_go = o_glr + GLA_RANK
    main = jnp.concatenate([w[:, :o_mi], w[:, o_mo:o_glr], w[:, o_go:]], axis=1)
    small = jnp.concatenate([w[:, o_mi:o_mo], w[:, o_glr:o_go],
                             jnp.zeros((w.shape[0], LANES - 2 * ML_HEADS - GLA_RANK), w.dtype)], axis=1)
    return main.astype(bf16), small.astype(bf16)


def _layer_weights(l, norm1_g, w_in, if_bias, ml_g, gla_gw, gla_gb, gla_g, pool_w, pool_scale,
                   w_branch, w_out):
    w_main, w_small = _regroup_w_in(w_in[l])
    ifb = jnp.zeros((1, LANES), f32)
    ifb = ifb.at[0, S_MI:S_MI + ML_HEADS].set(if_bias[l, 0]).at[0, S_MF:S_MF + ML_HEADS].set(if_bias[l, 1])
    gw = jnp.zeros((LANES, GLA_KW), f32).at[S_GLR:S_GLR + GLA_RANK].set(gla_gw[l]).astype(bf16)
    return dict(
        norm1_g=norm1_g[l][None], w_main=w_main, w_small=w_small, ifb=ifb, ml_g=ml_g[l][None],
        gla_gw=gw, gla_gb=gla_gb[l][None], gla_g=gla_g[l][None], pool_w=pool_w[l].astype(bf16),
        pool_scale=pool_scale[l][None], w_branch=w_branch[l].astype(bf16), w_out=w_out[l].astype(bf16))


def _forward(x_prompt, x_sample, state_mlstm_C, state_mlstm_n, state_mlstm_m, state_gla_S,
             state_pool_buf, norm1_g, w_in, mlstm_if_bias, mlstm_norm_g, gla_gate_w, gla_gate_b,
             gla_norm_g, pool_w, pool_scale, w_branch, w_out, norm2_g, ffn_wg, ffn_wu, ffn_wd,
             router_w, moe_wg, moe_wu, moe_wd, final_norm_g, *, tm, tn, tt, chunk, sub, nb, sb, tf_dense, moe_sub, moe_cap):
    batch, seq, _ = x_prompt.shape
    dec = x_sample.shape[0]
    depth = w_in.shape[0]
    m_prompt = batch * seq
    m_all = m_prompt + dec
    mp = -(-m_all // tm) * tm
    x = jnp.concatenate([x_prompt.reshape(m_prompt, D_MODEL), x_sample.reshape(dec, D_MODEL),
                         jnp.zeros((mp - m_all, D_MODEL), f32)], axis=0)
    outs = [[] for _ in range(5)]
    moe_out = None
    st = (state_mlstm_C, state_mlstm_n, jnp.pad(state_mlstm_m, ((0, 0), (0, 0), (0, LANES - ML_HEADS))),
          state_gla_S, state_pool_buf)
    st_new = None
    for l in range(depth):
        lw = _layer_weights(l, norm1_g, w_in, mlstm_if_bias, mlstm_norm_g, gla_gate_w, gla_gate_b,
                            gla_norm_g, pool_w, pool_scale, w_branch, w_out)
        z, zs = _norm_matmul(x, lw["norm1_g"], lw["w_main"], lw["w_small"], tm, tn)
        br_p, c_p, n_p, m_p, s_p, buf_p = _prompt_mix(z, zs, lw, batch, seq, tt, chunk, sub, nb)
        br_t, *st_new = _sample_mix(z, zs, st, st_new, l, lw, m_prompt, dec, sb)
        x = _merge(x, br_p, br_t, z, lw["w_branch"], lw["w_out"], tm, m_all)
        j = l // 2
        if l % 2 == 0:
            x = _ffn(x, norm2_g[l][None], ffn_wg[j].astype(bf16), ffn_wu[j].astype(bf16),
                     ffn_wd[j].astype(bf16), tm, tf_dense)
        else:
            wr = jnp.pad(router_w[j], ((0, 0), (0, LANES - N_EXPERTS))).astype(bf16)
            hb, comb, rnk, rnkt, cnt = _router(x, norm2_g[l][None], wr, tm, m_all)
            n_sup = mp // (tm * moe_sub)
            cnt = cnt[:, 0, :N_EXPERTS].reshape(n_sup, moe_sub, N_EXPERTS).max(axis=1)
            rounds = jnp.ceil(cnt / moe_cap).astype(jnp.int32).reshape(-1)
            moe_out = _moe(hb, rnk, rnkt, comb, rounds, moe_wg[j].astype(bf16), moe_wu[j].astype(bf16),
                           moe_wd[j].astype(bf16), tm, moe_sub, moe_cap)
            if l + 1 < depth:
                x = x + moe_out
                moe_out = None
        for lst, val in zip(outs, (c_p, n_p, m_p[:, :ML_HEADS, 0], s_p, buf_p)):
            lst.append(val)
    if moe_out is None:
        moe_out = jnp.zeros_like(x)
    y_prompt, y_sample = _final_norm(x, moe_out, final_norm_g[None], tm, m_prompt, dec)
    c_p, n_p, m_p, s_p, buf_p = (jnp.stack(o) for o in outs)
    c_s, n_s, m_s, s_s, buf_s = st_new
    return (y_prompt.reshape(batch, seq, D_MODEL), y_sample.reshape(dec, 1, D_MODEL),
            c_p, c_s, n_p, n_s, m_p, m_s[:, :, :ML_HEADS], s_p, s_s, buf_p, buf_s)


def kernel(x_prompt, x_sample, state_mlstm_C, state_mlstm_n, state_mlstm_m, state_gla_S, state_pool_buf, norm1_g, w_in, mlstm_if_bias, mlstm_norm_g, gla_gate_w, gla_gate_b, gla_norm_g, pool_w, pool_scale, w_branch, w_out, norm2_g, ffn_wg, ffn_wu, ffn_wd, router_w, moe_wg, moe_wu, moe_wd, final_norm_g):
    return _forward(x_prompt, x_sample, state_mlstm_C, state_mlstm_n, state_mlstm_m, state_gla_S,
                    state_pool_buf, norm1_g, w_in, mlstm_if_bias, mlstm_norm_g, gla_gate_w, gla_gate_b,
                    gla_norm_g, pool_w, pool_scale, w_branch, w_out, norm2_g, ffn_wg, ffn_wu, ffn_wd,
                    router_w, moe_wg, moe_wu, moe_wd, final_norm_g,
                    tm=512, tn=1024, tt=512, chunk=128, sub=64, nb=2, sb=16, tf_dense=1408, moe_sub=3, moe_cap=160)
```

```python
import functools

import jax
import jax.numpy as jnp
from jax import lax
from jax.experimental import pallas as pl
from jax.experimental.pallas import tpu as pltpu

f32 = jnp.float32
bf16 = jnp.bfloat16

D_MODEL = 1024
ML_HEADS, ML_DH = 4, 128
ML_W = ML_HEADS * ML_DH
GLA_HEADS, GLA_DK, GLA_DV = 4, 64, 128
GLA_KW, GLA_VW = GLA_HEADS * GLA_DK, GLA_HEADS * GLA_DV
GLA_RANK = 16
GLA_TAU = 16.0
POOL_GDIM = 128
POOL_WINDOWS = (2, 4, 8, 16)
POOL_W = POOL_GDIM * len(POOL_WINDOWS)
POOL_BUF = 15
N_BRANCH = 3
N_EXPERTS = 8
EPS = 1e-6
NEG = -1e30
LANES = 128

C_MQ, C_MK, C_MV, C_MO = 0, 512, 1024, 1536
C_GQ, C_GK, C_GV, C_GO, C_PU, C_GATES = 2048, 2304, 2560, 3072, 3584, 4096
Z_MAIN = C_GATES + N_BRANCH * D_MODEL
S_MI, S_MF, S_GLR = 0, 4, 8

VMEM_LIMIT = 56 * 1024 * 1024

_NT = (((1,), (1,)), ((), ()))


def _cparams(sem):
    return pltpu.CompilerParams(dimension_semantics=sem, vmem_limit_bytes=VMEM_LIMIT)


def _log_sigmoid(x):
    return jnp.minimum(x, 0.0) - jnp.log(1.0 + jnp.exp(-jnp.abs(x)))


def _sigmoid(x):
    return 0.5 * jnp.tanh(0.5 * x) + 0.5


def _rms(x, g):
    ms = jnp.mean(x * x, axis=-1, keepdims=True)
    return x * lax.rsqrt(ms + EPS) * g


def _lower_tri(n):
    r = lax.broadcasted_iota(jnp.int32, (n, n), 0)
    c = lax.broadcasted_iota(jnp.int32, (n, n), 1)
    return c <= r


def _cumsum_rows(tri_bf16, a):
    a1 = a.astype(bf16)
    r = a - a1.astype(f32)
    a2 = r.astype(bf16)
    a3 = (r - a2.astype(f32)).astype(bf16)
    d = lambda y: jnp.dot(tri_bf16, y, preferred_element_type=f32)
    return d(a1) + d(a2) + d(a3)


def _cumsum_lanes(triu_bf16, a):
    a1 = a.astype(bf16)
    r = a - a1.astype(f32)
    a2 = r.astype(bf16)
    a3 = (r - a2.astype(f32)).astype(bf16)
    d = lambda y: jnp.dot(y, triu_bf16, preferred_element_type=f32)
    return d(a1) + d(a2) + d(a3)


def _resident(shape):
    nd = len(shape)
    return pl.BlockSpec(shape, lambda *_: (0,) * nd, pipeline_mode=pl.Buffered(1))


def _split_rows_specs(parts, tm, width):
    specs, start = [], 0
    for p in parts:
        n_t = p.shape[0] // tm
        assert n_t * tm == p.shape[0]
        specs.append(pl.BlockSpec(
            (tm, width), functools.partial(lambda i, s, n: (jnp.clip(i - s, 0, n - 1), 0), s=start, n=n_t)))
        start += n_t
    return specs


def _pick_rows(refs, starts):
    i = pl.program_id(0)
    x = refs[0][...]
    for r, s in zip(refs[1:], starts[1:]):
        x = jnp.where(i >= s, r[...], x)
    return x


def _norm_matmul_kernel(*refs, tn, starts):
    x_refs = refs[:len(starts)]
    g_ref, w_ref, ws_ref, z_ref, zs_ref = refs[len(starts):]
    h = _rms(_pick_rows(x_refs, starts), g_ref[...]).astype(bf16)
    zs_ref[...] = jnp.dot(h, ws_ref[...], preferred_element_type=f32)
    for c in range(w_ref.shape[1] // tn):
        cols = slice(c * tn, (c + 1) * tn)
        z_ref[:, cols] = jnp.dot(h, w_ref[:, cols], preferred_element_type=f32).astype(z_ref.dtype)


def _norm_matmul(x_parts, g, w_main, w_small, tm, tn):
    mp = sum(p.shape[0] for p in x_parts)
    n = w_main.shape[1]
    starts = tuple(sum(p.shape[0] for p in x_parts[:j]) // tm for j in range(len(x_parts)))
    return pl.pallas_call(
        functools.partial(_norm_matmul_kernel, tn=tn, starts=starts),
        grid=(mp // tm,),
        in_specs=[
            *_split_rows_specs(x_parts, tm, D_MODEL),
            _resident((1, D_MODEL)),
            _resident((D_MODEL, n)),
            _resident((D_MODEL, LANES)),
        ],
        out_specs=[
            pl.BlockSpec((tm, n), lambda i: (i, 0)),
            pl.BlockSpec((tm, LANES), lambda i: (i, 0)),
        ],
        out_shape=[jax.ShapeDtypeStruct((mp, n), bf16), jax.ShapeDtypeStruct((mp, LANES), f32)],
        compiler_params=_cparams(("parallel",)),
        name="norm_matmul",
    )(*x_parts, g, w_main, w_small)


def _head_norm(h, g):
    outs = []
    for j in range(h.shape[1] // LANES):
        hj = h[:, j * LANES:(j + 1) * LANES]
        outs.append(hj * lax.rsqrt(jnp.mean(hj * hj, axis=-1, keepdims=True) + EPS))
    return jnp.concatenate(outs, axis=1) * g


def _gla_log_decay(sm, gw_ref, gb_ref):
    xg = jnp.dot(sm.astype(bf16), gw_ref[...], preferred_element_type=f32) + gb_ref[...]
    return _log_sigmoid(xg) * (1.0 / GLA_TAU)


N_ZBLOCKS = 9


def _prompt_mix_kernel(*refs, nb, tt, chunk, sub, n_t):
    z_refs = [refs[s * N_ZBLOCKS:(s + 1) * N_ZBLOCKS] for s in range(nb)]
    ifb_ref, mlg_ref, gw_ref, gb_ref, glag_ref, pw_ref, ps_ref = refs[nb * N_ZBLOCKS:nb * N_ZBLOCKS + 7]
    br_all, c_out, n_out, m_out, s_out, buf_out = refs[nb * N_ZBLOCKS + 7:nb * N_ZBLOCKS + 13]
    scratch = refs[nb * N_ZBLOCKS + 13:]
    per = ML_HEADS + GLA_HEADS + 7
    c_refs = [scratch[s * per:s * per + ML_HEADS] for s in range(nb)]
    s_refs = [scratch[s * per + ML_HEADS:s * per + ML_HEADS + GLA_HEADS] for s in range(nb)]
    n_refs = [scratch[s * per + ML_HEADS + GLA_HEADS] for s in range(nb)]
    m_refs = [scratch[s * per + ML_HEADS + GLA_HEADS + 1] for s in range(nb)]
    ext_refs = [scratch[s * per + ML_HEADS + GLA_HEADS + 2] for s in range(nb)]
    prep_refs = [scratch[s * per + ML_HEADS + GLA_HEADS + 3:(s + 1) * per] for s in range(nb)]
    t_idx = pl.program_id(1)
    L = chunk

    @pl.when(t_idx == 0)
    def _():
        for s in range(nb):
            for r in (*c_refs[s], *s_refs[s], n_refs[s], m_refs[s]):
                r[...] = jnp.zeros_like(r)
            ext_refs[s][0:16, :] = jnp.zeros((16, POOL_W), f32)

    tri = _lower_tri(L)
    tri_b = tri.astype(bf16)
    causal_sub = _lower_tri(sub)
    assert L == LANES
    ones_b = jnp.ones((L, LANES), bf16)
    triu_b = (lax.broadcasted_iota(jnp.int32, (L, L), 0)
              <= lax.broadcasted_iota(jnp.int32, (L, L), 1)).astype(bf16)
    ifb = ifb_ref[...]
    k_scale = ML_DH ** -0.5
    q_scale = GLA_DK ** -0.5

    def gate_prep(c, seq):
        sm_ref = z_refs[seq][8]
        y0_p, bc_p, rows_p, b_p = prep_refs[seq]
        sm = sm_ref[pl.ds(pl.multiple_of(c * L, L), L), :]
        y0 = sm + ifb
        y0_p[...] = y0
        bc_p[...] = _cumsum_rows(tri_b, _log_sigmoid(y0))
        y0t = y0.T[0:8, :]
        rows_p[0:8, :] = y0t
        rows_p[8:16, :] = _cumsum_lanes(triu_b, _log_sigmoid(y0t))
        b_p[...] = _cumsum_rows(tri_b, _gla_log_decay(sm, gw_ref, gb_ref))

    def one_seq(c, seq):
        q_ref, k_ref, v_ref, mo_ref, gqk_ref, gv_ref, go_ref, _, sm_ref = z_refs[seq]
        br_ref, c_s, s_s = br_all.at[seq], c_refs[seq], s_refs[seq]
        n_old, m_old = n_refs[seq][...], m_refs[seq][...]
        n_rows, m_rows = [], []
        r0 = pl.multiple_of(c * L, L)
        rows = pl.ds(r0, L)
        y0_p, bc_p, rows_p, b_p = prep_refs[seq]

        y0, bc = y0_p[...], bc_p[...]
        y0t, bct = rows_p[0:8, :], rows_p[8:16, :]
        b = b_p[...]
        hm = []
        for h in range(ML_HEADS):
            hs = slice(h * ML_DH, (h + 1) * ML_DH)
            v = v_ref[rows, hs].astype(f32)
            kb = k_ref[rows, hs].astype(bf16)
            qb, vb = q_ref[rows, hs].astype(bf16), v.astype(bf16)
            bcol = bc[:, S_MF + h:S_MF + h + 1]
            icol = y0[:, S_MI + h:S_MI + h + 1]
            brow = bct[S_MF + h:S_MF + h + 1, :]
            irow = y0t[S_MI + h:S_MI + h + 1, :]
            m_row = m_old[h:h + 1, :]
            cmat = c_s[h][...]
            nrow = n_old[h:h + 1, :]
            bcol_r = jnp.broadcast_to(bcol, (L, LANES))
            dm = jnp.where(tri, bcol_r - brow + irow, NEG)
            inter_r = bcol_r + m_row
            m_t_r = jnp.maximum(inter_r, jnp.broadcast_to(jnp.max(dm, axis=1, keepdims=True), (L, LANES)))
            w_intra = jnp.exp(dm - m_t_r) * k_scale
            w_inter_r = jnp.exp(inter_r - m_t_r)
            s = lax.dot_general(qb, kb, _NT, preferred_element_type=f32) * w_intra
            s_hi = s.astype(bf16)
            s_lo = (s - s_hi.astype(f32)).astype(bf16)
            r_intra = jnp.dot(s_hi, jnp.concatenate([vb, ones_b], axis=1), preferred_element_type=f32)
            c_aug = jnp.concatenate([cmat, jnp.broadcast_to(nrow, (LANES, LANES))], axis=0).astype(bf16)
            r_inter = lax.dot_general(qb, c_aug, _NT, preferred_element_type=f32)
            rs_lo = jnp.dot(s_lo, ones_b, preferred_element_type=f32)
            num = r_intra[:, 0:LANES] + w_inter_r * r_inter[:, 0:LANES]
            den_r = r_intra[:, LANES:] + rs_lo + w_inter_r * r_inter[:, LANES:]
            hm.append(num / jnp.maximum(jnp.abs(den_r), jnp.exp(-m_t_r)))
            b_last = bcol[L - 1:L, :]
            m = m_row[:, 0:1]
            g = b_last - bcol + icol
            m_new = jnp.maximum(b_last + m, jnp.max(g, axis=0, keepdims=True))
            w_s = jnp.broadcast_to(jnp.exp(g - m_new) * k_scale, (L, LANES))
            w_c = jnp.exp(b_last + m - m_new)
            vwt = (v * w_s).T.astype(bf16)
            c_s[h][...] = w_c * cmat + jnp.dot(vwt, kb, preferred_element_type=f32)
            n_rows.append(w_c * nrow + jnp.sum(w_s * k_ref[rows, hs].astype(f32), axis=0, keepdims=True))
            m_rows.append(jnp.broadcast_to(m_new, (1, LANES)))
        pad_rows = [jnp.zeros((8 - ML_HEADS, LANES), f32)]
        n_refs[seq][...] = jnp.concatenate(n_rows + pad_rows, axis=0)
        m_refs[seq][...] = jnp.concatenate(m_rows + pad_rows, axis=0)
        y_ml = (_head_norm(jnp.concatenate(hm, axis=1), mlg_ref[...])
                * _sigmoid(mo_ref[rows, :].astype(f32)))
        br_ref[rows, 0:ML_W] = y_ml.astype(br_ref.dtype)

        q2 = gqk_ref[rows, 0:GLA_KW].astype(f32) * q_scale
        k2 = gqk_ref[rows, GLA_KW:2 * GLA_KW].astype(f32)
        gv = gv_ref[rows, :].astype(bf16)
        s_old = [s_s[h][...] for h in range(GLA_HEADS)]
        qe_chunk = (q2 * jnp.exp(b)).astype(bf16)
        b_last = b[L - 1:L, :]
        kdt = (k2 * jnp.exp(b_last - b)).T.astype(bf16)
        decay_col = jnp.exp(b.T[:, L - 1:L])
        o_blocks = []
        for blk in range(L // sub):
            s0, s1 = blk * sub, (blk + 1) * sub
            mid = s0 + sub // 2
            b_blk = b[s0:s1]
            b_mid = b[mid - 1:mid, :]
            qe_d = (q2[s0:s1] * jnp.exp(b_blk - b_mid)).astype(bf16)
            ke_d = (k2[s0:s1] * jnp.exp(b_mid - b_blk)).astype(bf16)
            if blk > 0:
                b_start = b[s0 - 1:s0, :]
                qe_o = (q2[s0:s1] * jnp.exp(b_blk - b_start)).astype(bf16)
                ke_o = (k2[0:s0] * jnp.exp(b_start - b[0:s0])).astype(bf16)
            o_heads = []
            for h in range(GLA_HEADS):
                ks = slice(h * GLA_DK, (h + 1) * GLA_DK)
                vs = slice(h * GLA_DV, (h + 1) * GLA_DV)
                a = lax.dot_general(qe_d[:, ks], ke_d[:, ks], _NT, preferred_element_type=f32)
                a = jnp.where(causal_sub, a, 0.0)
                o = (jnp.dot(a.astype(bf16), gv[s0:s1, vs], preferred_element_type=f32)
                     + jnp.dot(qe_chunk[s0:s1, ks], s_old[h].astype(bf16), preferred_element_type=f32))
                if blk > 0:
                    a = lax.dot_general(qe_o[:, ks], ke_o[:, ks], _NT, preferred_element_type=f32)
                    o = o + jnp.dot(a.astype(bf16), gv[0:s0, vs], preferred_element_type=f32)
                o_heads.append(o)
            o_blocks.append(jnp.concatenate(o_heads, axis=1))
        og = jnp.concatenate(o_blocks, axis=0)
        gate_prep(jnp.minimum(c + 1, tt // L - 1), seq)
        for h in range(GLA_HEADS):
            ks = slice(h * GLA_DK, (h + 1) * GLA_DK)
            vs = slice(h * GLA_DV, (h + 1) * GLA_DV)
            s_s[h][...] = (decay_col[ks, :] * s_old[h]
                           + jnp.dot(kdt[ks, :], gv[:, vs], preferred_element_type=f32))
        go = go_ref[rows, :].astype(f32)
        y_gla = _head_norm(og, glag_ref[...]) * (go * _sigmoid(go))
        br_ref[rows, ML_W:ML_W + GLA_VW] = y_gla.astype(br_ref.dtype)

    def chunk_body(c, carry):
        for s in range(nb):
            one_seq(c, s)
        return carry

    for s in range(nb):
        gate_prep(0, s)
    lax.fori_loop(0, tt // L, chunk_body, 0)

    pos = t_idx * tt + lax.broadcasted_iota(jnp.int32, (tt, 1), 0)
    for s in range(nb):
        pu_ref, br_ref, ext_s = z_refs[s][7], br_all.at[s], ext_refs[s]
        ext_s[16:16 + tt, :] = pu_ref[...].astype(f32)
        for g, w in enumerate(POOL_WINDOWS):
            gs = slice(g * POOL_GDIM, (g + 1) * POOL_GDIM)
            u = ext_s[16:16 + tt, gs]
            acc = u
            for j in range(1, w):
                acc = acc + ext_s[16 - j:16 - j + tt, gs]
            cnt = jnp.minimum(pos + 1, w).astype(f32)
            d = acc / cnt - u
            y = jnp.dot(d.astype(bf16), pw_ref[g], preferred_element_type=f32) * ps_ref[:, gs]
            br_ref[:, ML_W + GLA_VW + g * POOL_GDIM:ML_W + GLA_VW + (g + 1) * POOL_GDIM] = y.astype(br_ref.dtype)
        ext_s[0:16, :] = ext_s[tt:tt + 16, :]

    @pl.when(t_idx == n_t - 1)
    def _():
        for s in range(nb):
            for h in range(ML_HEADS):
                c_out[s, h] = c_refs[s][h][...]
            for h in range(GLA_HEADS):
                s_out[s, h] = s_refs[s][h][...]
            n_out[s] = n_refs[s][0:ML_HEADS, :]
            m_out[s] = m_refs[s][...]
            buf_out[s] = ext_refs[s][1:16, :]


def _prompt_mix(z, zs, lw, batch, seq, tt, chunk, sub, nb):
    n_t = seq // tt
    assert batch % nb == 0

    def zspecs(s):
        row = lambda b, t: (b * nb + s) * n_t + t
        spec = lambda col, width: pl.BlockSpec((tt, width), lambda b, t: (row(b, t), col // width))
        return [spec(C_MQ, 512), spec(C_MK, 512), spec(C_MV, 512), spec(C_MO, 512),
                spec(C_GQ, 512), spec(C_GV, 512), spec(C_GO, 512), spec(C_PU, 512),
                pl.BlockSpec((tt, LANES), lambda b, t: (row(b, t), 0))]

    def full(shape):
        nd = len(shape)
        return pl.BlockSpec(shape, lambda b, t: (0,) * nd)

    def per_seq(shape):
        nd = len(shape)
        return pl.BlockSpec((nb,) + shape, lambda b, t: (b,) + (0,) * nd)

    kern = functools.partial(_prompt_mix_kernel, nb=nb, tt=tt, chunk=chunk, sub=sub, n_t=n_t)
    outs = pl.pallas_call(
        kern,
        grid=(batch // nb, n_t),
        in_specs=[sp for s in range(nb) for sp in zspecs(s)]
                 + [full((1, LANES)), full((1, ML_W)), full((LANES, GLA_KW)), full((1, GLA_KW)),
                    full((1, GLA_VW)), full((len(POOL_WINDOWS), POOL_GDIM, POOL_GDIM)), full((1, POOL_W))],
        out_specs=[
            pl.BlockSpec((nb, tt, N_BRANCH * ML_W), lambda b, t: (b, t, 0)),
            per_seq((ML_HEADS, ML_DH, ML_DH)), per_seq((ML_HEADS, ML_DH)), per_seq((8, LANES)),
            per_seq((GLA_HEADS, GLA_DK, GLA_DV)), per_seq((POOL_BUF, POOL_W)),
        ],
        out_shape=[
            jax.ShapeDtypeStruct((batch, seq, N_BRANCH * ML_W), bf16),
            jax.ShapeDtypeStruct((batch, ML_HEADS, ML_DH, ML_DH), f32),
            jax.ShapeDtypeStruct((batch, ML_HEADS, ML_DH), f32),
            jax.ShapeDtypeStruct((batch, 8, LANES), f32),
            jax.ShapeDtypeStruct((batch, GLA_HEADS, GLA_DK, GLA_DV), f32),
            jax.ShapeDtypeStruct((batch, POOL_BUF, POOL_W), f32),
        ],
        scratch_shapes=([pltpu.VMEM((ML_DH, ML_DH), f32)] * ML_HEADS
                        + [pltpu.VMEM((GLA_DK, GLA_DV), f32)] * GLA_HEADS
                        + [pltpu.VMEM((8, LANES), f32), pltpu.VMEM((8, LANES), f32),
                           pltpu.VMEM((tt + 16, POOL_W), f32),
                           pltpu.VMEM((chunk, LANES), f32), pltpu.VMEM((chunk, LANES), f32),
                           pltpu.VMEM((16, chunk), f32), pltpu.VMEM((chunk, GLA_KW), f32)]) * nb,
        compiler_params=_cparams(("parallel", "arbitrary")),
        name="prompt_mix",
    )(*([z] * 8 + [zs]) * nb,
      lw["ifb"], lw["ml_g"], lw["gla_gw"], lw["gla_gb"], lw["gla_g"], lw["pool_w"], lw["pool_scale"])
    return (outs[0].reshape(batch * seq, N_BRANCH * ML_W),) + tuple(outs[1:])


def _sample_mix_kernel(q_ref, k_ref, v_ref, mo_ref, gqk_ref, gv_ref, go_ref, pu_ref, sm_ref,
                       c_in, n_in, m_in, s_in, buf_in,
                       ifb_ref, mlg_ref, gw_ref, gb_ref, glag_ref, pw_ref, ps_ref, *tail, sb, dec):
    br_all = tail[-6]
    c_out, n_out, m_out, s_out, buf_out = (r.at[0] for r in tail[-5:])
    step = pl.program_id(0)
    c_in, n_in, m_in, s_in, buf_in = (r.at[0] for r in (c_in, n_in, m_in, s_in, buf_in))
    br_ref = br_all.at[pl.ds(pl.multiple_of(step * sb, sb), sb), :]

    @pl.when(step == 0)
    def _():
        br_all[dec:, :] = jnp.zeros((br_all.shape[0] - dec, br_all.shape[1]), br_all.dtype)

    sm = sm_ref[...]
    eye = (lax.broadcasted_iota(jnp.int32, (LANES, LANES), 0)
           == lax.broadcasted_iota(jnp.int32, (LANES, LANES), 1)).astype(f32)

    def to_cols(x):
        parts = [lax.dot_general(eye, x[:, j * LANES:(j + 1) * LANES], _NT,
                                 preferred_element_type=f32, precision=lax.Precision.HIGHEST)
                 for j in range(x.shape[1] // LANES)]
        return jnp.concatenate(parts, axis=0)

    y0 = sm + ifb_ref[...]
    logf_all = _log_sigmoid(y0)
    k_scale = ML_DH ** -0.5
    m_all = m_in[...]
    hm, m_new_cols = [], []
    for h in range(ML_HEADS):
        hs = slice(h * ML_DH, (h + 1) * ML_DH)
        q = q_ref[:, hs].astype(f32)
        k = k_ref[:, hs].astype(f32) * k_scale
        v = v_ref[:, hs].astype(f32)
        i_pre = y0[:, S_MI + h:S_MI + h + 1]
        logf = logf_all[:, S_MF + h:S_MF + h + 1]
        m = m_all[:, h:h + 1]
        inter = logf + m
        m_t = jnp.maximum(inter, i_pre)
        w_intra = jnp.exp(i_pre - m_t)
        w_inter = jnp.exp(inter - m_t)
        s = jnp.sum(q * k, axis=1, keepdims=True) * w_intra
        qb = q.astype(bf16)
        n_h = n_in[:, h, :]
        cq = jnp.concatenate(
            [lax.dot_general(qb, c_in[j, h].astype(bf16), _NT, preferred_element_type=f32)[j:j + 1, :]
             for j in range(sb)], axis=0)
        num = s * v + w_inter * cq
        den = s + w_inter * jnp.sum(n_h * q, axis=1, keepdims=True)
        hm.append(num / jnp.maximum(jnp.abs(den), jnp.exp(-m_t)))
        m_new = m_t
        w_s = w_intra
        w_c = w_inter
        n_out[:, h, :] = w_c * n_h + w_s * k
        m_new_cols.append(m_new)
        vw_cols = to_cols(v * w_s)
        for j in range(sb):
            c_out[j, h] = w_c[j:j + 1, :] * c_in[j, h] + vw_cols[:, j:j + 1] * k[j:j + 1, :]
    lane = lax.broadcasted_iota(jnp.int32, (sb, LANES), 1)
    m_pack = jnp.zeros((sb, LANES), f32)
    for h in range(ML_HEADS):
        m_pack = jnp.where(lane == h, m_new_cols[h], m_pack)
    m_out[...] = m_pack
    y_ml = _head_norm(jnp.concatenate(hm, axis=1), mlg_ref[...]) * _sigmoid(mo_ref[...].astype(f32))
    br_ref[:, 0:ML_W] = y_ml.astype(br_ref.dtype)

    log_a = _gla_log_decay(sm, gw_ref, gb_ref)
    decay = jnp.exp(log_a)
    q2 = gqk_ref[:, 0:GLA_KW].astype(f32) * (GLA_DK ** -0.5)
    k2 = gqk_ref[:, GLA_KW:2 * GLA_KW].astype(f32)
    gv = gv_ref[...].astype(f32)
    qe = (q2 * decay).astype(bf16)
    qk = q2 * k2
    k_cols = to_cols(k2)
    decay_cols = to_cols(decay)
    og = []
    for h in range(GLA_HEADS):
        ks = slice(h * GLA_DK, (h + 1) * GLA_DK)
        vs = slice(h * GLA_DV, (h + 1) * GLA_DV)
        a = jnp.sum(qk[:, ks], axis=1, keepdims=True)
        inter = jnp.concatenate(
            [jnp.dot(qe[:, ks], s_in[j, h].astype(bf16), preferred_element_type=f32)[j:j + 1, :]
             for j in range(sb)], axis=0)
        og.append(a * gv[:, vs] + inter)
        for j in range(sb):
            s_out[j, h] = (decay_cols[ks, j:j + 1] * s_in[j, h]
                           + k_cols[ks, j:j + 1] * gv[j:j + 1, vs])
    go = go_ref[...].astype(f32)
    y_gla = _head_norm(jnp.concatenate(og, axis=1), glag_ref[...]) * (go * _sigmoid(go))
    br_ref[:, ML_W:ML_W + GLA_VW] = y_gla.astype(br_ref.dtype)

    u = pu_ref[...].astype(f32)
    rowi = lax.broadcasted_iota(jnp.int32, (POOL_BUF + 1, POOL_GDIM), 0)
    d_rows = []
    for j in range(sb):
        ext = jnp.concatenate([buf_in[j], u[j:j + 1, :]], axis=0)
        buf_out[j] = ext[1:POOL_BUF + 1, :]
        parts = []
        for g, w in enumerate(POOL_WINDOWS):
            gs = slice(g * POOL_GDIM, (g + 1) * POOL_GDIM)
            win = jnp.sum(jnp.where(rowi >= POOL_BUF + 1 - w, ext[:, gs], 0.0), axis=0, keepdims=True)
            parts.append(win / float(w) - u[j:j + 1, gs])
        d_rows.append(jnp.concatenate(parts, axis=1))
    d = jnp.concatenate(d_rows, axis=0)
    for g in range(len(POOL_WINDOWS)):
        gs = slice(g * POOL_GDIM, (g + 1) * POOL_GDIM)
        y = jnp.dot(d[:, gs].astype(bf16), pw_ref[g], preferred_element_type=f32) * ps_ref[:, gs]
        br_ref[:, ML_W + GLA_VW + g * POOL_GDIM:ML_W + GLA_VW + (g + 1) * POOL_GDIM] = y.astype(br_ref.dtype)


def _sample_mix(z, zs, st, prev, layer, lw, row0, dec, sb):
    depth = st[0].shape[0]
    base = row0 // sb
    tail_rows = z.shape[0] - row0

    def zspec(col, width):
        blk = col // width
        return pl.BlockSpec((sb, width), lambda i: (base + i, blk))

    def full(shape):
        nd = len(shape)
        return pl.BlockSpec(shape, lambda i: (0,) * nd)

    st_shapes = [(ML_HEADS, ML_DH, ML_DH), (ML_HEADS, ML_DH), (LANES,),
                 (GLA_HEADS, GLA_DK, GLA_DV), (POOL_BUF, POOL_W)]
    st_specs = [pl.BlockSpec((1, sb) + s, functools.partial(lambda i, nd: (layer, i) + (0,) * nd, nd=len(s)))
                for s in st_shapes]
    n_in = N_ZBLOCKS + len(st_shapes) + 7
    prev = () if prev is None else tuple(prev)
    return pl.pallas_call(
        functools.partial(_sample_mix_kernel, sb=sb, dec=dec),
        grid=(dec // sb,),
        in_specs=[zspec(C_MQ, 512), zspec(C_MK, 512), zspec(C_MV, 512), zspec(C_MO, 512),
                  zspec(C_GQ, 512), zspec(C_GV, 512), zspec(C_GO, 512), zspec(C_PU, 512),
                  pl.BlockSpec((sb, LANES), lambda i: (base + i, 0))]
                 + st_specs
                 + [full((1, LANES)), full((1, ML_W)), full((LANES, GLA_KW)), full((1, GLA_KW)),
                    full((1, GLA_VW)), full((len(POOL_WINDOWS), POOL_GDIM, POOL_GDIM)), full((1, POOL_W))]
                 + [pl.BlockSpec(memory_space=pl.ANY)] * len(prev),
        out_specs=[pl.BlockSpec((tail_rows, N_BRANCH * ML_W), lambda i: (0, 0))] + st_specs,
        out_shape=[jax.ShapeDtypeStruct((tail_rows, N_BRANCH * ML_W), bf16)]
                  + [jax.ShapeDtypeStruct((depth, dec) + s, f32) for s in st_shapes],
        input_output_aliases={n_in + j: 1 + j for j in range(len(prev))},
        compiler_params=_cparams(("arbitrary",)),
        name="sample_mix",
    )(z, z, z, z, z, z, z, z, zs, *st,
      lw["ifb"], lw["ml_g"], lw["gla_gw"], lw["gla_gb"], lw["gla_g"], lw["pool_w"], lw["pool_scale"], *prev)


def _merge_kernel(*refs, tm, n_valid, n_prompt_tiles, x_starts):
    x_refs = refs[:len(x_starts)]
    brp_ref, brt_ref, g0_ref, g1_ref, g2_ref, wb_ref, wo_ref, o_ref = refs[len(x_starts):]
    in_tail = pl.program_id(0) >= n_prompt_tiles
    mixed = None
    for n, g_ref in enumerate((g0_ref, g1_ref, g2_ref)):
        cols = slice(n * ML_W, (n + 1) * ML_W)
        br = jnp.where(in_tail, brt_ref[:, cols], brp_ref[:, cols])
        proj = jnp.dot(br, wb_ref[n], preferred_element_type=f32)
        term = _sigmoid(g_ref[...].astype(f32)) * proj
        mixed = term if mixed is None else mixed + term
    out = (_pick_rows(x_refs, x_starts)
           + jnp.dot(mixed.astype(bf16), wo_ref[...], preferred_element_type=f32))
    row = pl.program_id(0) * tm + lax.broadcasted_iota(jnp.int32, (tm, 1), 0)
    o_ref[...] = jnp.where(row < n_valid, out, 0.0)


def _merge(x_parts, br_prompt, br_tail, z, w_branch, w_out, tm, n_valid):
    mp = sum(p.shape[0] for p in x_parts)
    x_starts = tuple(sum(p.shape[0] for p in x_parts[:j]) // tm for j in range(len(x_parts)))
    n_p = br_prompt.shape[0] // tm
    gate_specs = [pl.BlockSpec((tm, D_MODEL), functools.partial(lambda i, n: (i, C_GATES // D_MODEL + n), n=n))
                  for n in range(N_BRANCH)]
    return pl.pallas_call(
        functools.partial(_merge_kernel, tm=tm, n_valid=n_valid, n_prompt_tiles=n_p, x_starts=x_starts),
        grid=(mp // tm,),
        in_specs=[
            *_split_rows_specs(x_parts, tm, D_MODEL),
            pl.BlockSpec((tm, N_BRANCH * ML_W), lambda i: (jnp.minimum(i, n_p - 1), 0)),
            pl.BlockSpec((tm, N_BRANCH * ML_W), lambda i: (jnp.maximum(i - n_p, 0), 0)),
            *gate_specs,
            _resident((N_BRANCH, ML_W, D_MODEL)),
            _resident((D_MODEL, D_MODEL)),
        ],
        out_specs=pl.BlockSpec((tm, D_MODEL), lambda i: (i, 0)),
        out_shape=jax.ShapeDtypeStruct((mp, D_MODEL), f32),
        compiler_params=_cparams(("parallel",)),
        name="merge",
    )(*x_parts, br_prompt, br_tail, z, z, z, w_branch, w_out)


def _router_kernel(x_ref, g_ref, wr_ref, hb_ref, comb_ref, rnk_ref, rnkt_ref, cnt_ref, *, tm, n_valid):
    h = _rms(x_ref[...], g_ref[...]).astype(bf16)
    hb_ref[...] = h
    logits = jnp.dot(h, wr_ref[...], preferred_element_type=f32)
    lane = lax.broadcasted_iota(jnp.int32, logits.shape, 1)
    valid = lane < N_EXPERTS
    logits = jnp.where(valid, logits, NEG)
    mx = jnp.max(logits, axis=1, keepdims=True)
    e = jnp.where(valid, jnp.exp(logits - mx), 0.0)
    probs = e / jnp.sum(e, axis=1, keepdims=True)
    p1 = jnp.max(probs, axis=1, keepdims=True)
    i1 = jnp.min(jnp.where(probs == p1, lane, LANES), axis=1, keepdims=True)
    rest = jnp.where((lane == i1) | ~valid, -1.0, probs)
    p2 = jnp.max(rest, axis=1, keepdims=True)
    i2 = jnp.min(jnp.where(rest == p2, lane, LANES), axis=1, keepdims=True)
    tot = p1 + p2
    comb_ref[...] = jnp.where(lane == i1, p1 / tot, 0.0) + jnp.where(lane == i2, p2 / tot, 0.0)
    row = pl.program_id(0) * tm + lax.broadcasted_iota(jnp.int32, logits.shape, 0)
    sel = ((lane == i1) | (lane == i2)) & (row < n_valid)
    r = lax.broadcasted_iota(jnp.int32, (tm, tm), 0)
    c = lax.broadcasted_iota(jnp.int32, (tm, tm), 1)
    rank = jnp.dot((c < r).astype(bf16), sel.astype(bf16), preferred_element_type=f32)
    rnk = jnp.where(sel, rank, -1.0)
    rnk_ref[...] = rnk
    rnkt_ref[0] = rnk.T[0:N_EXPERTS, :]
    cnt_ref[0] = jnp.sum(sel.astype(f32), axis=0, keepdims=True)


def _router(x, g, wr, tm, n_valid):
    mp = x.shape[0]
    nt = mp // tm
    return pl.pallas_call(
        functools.partial(_router_kernel, tm=tm, n_valid=n_valid),
        grid=(nt,),
        in_specs=[pl.BlockSpec((tm, D_MODEL), lambda i: (i, 0)),
                  pl.BlockSpec((1, D_MODEL), lambda i: (0, 0)),
                  pl.BlockSpec((D_MODEL, LANES), lambda i: (0, 0))],
        out_specs=[pl.BlockSpec((tm, D_MODEL), lambda i: (i, 0)),
                   pl.BlockSpec((tm, LANES), lambda i: (i, 0)),
                   pl.BlockSpec((tm, LANES), lambda i: (i, 0)),
                   pl.BlockSpec((1, N_EXPERTS, tm), lambda i: (i, 0, 0)),
                   pl.BlockSpec((1, 1, LANES), lambda i: (i, 0, 0))],
        out_shape=[jax.ShapeDtypeStruct((mp, D_MODEL), bf16),
                   jax.ShapeDtypeStruct((mp, LANES), f32),
                   jax.ShapeDtypeStruct((mp, LANES), f32),
                   jax.ShapeDtypeStruct((nt, N_EXPERTS, tm), f32),
                   jax.ShapeDtypeStruct((nt, 1, LANES), f32)],
        compiler_params=_cparams(("parallel",)),
        name="router",
    )(x, g, wr)


def _moe_kernel(rounds_ref, hb_ref, rnk_ref, rnkt_ref, comb_ref, wg_ref, wu_ref, wd_ref, o_ref,
                *, cap, n_sub, ts):
    i, e = pl.program_id(0), pl.program_id(1)

    @pl.when(e == 0)
    def _():
        o_ref[...] = jnp.zeros_like(o_ref)

    lane = lax.broadcasted_iota(jnp.int32, (ts, LANES), 1)
    slot_rows = lax.broadcasted_iota(jnp.int32, (cap, ts), 0).astype(f32)
    slot_cols = lax.broadcasted_iota(jnp.int32, (ts, cap), 1).astype(f32)

    def round_body(r, carry):
        base = (r * cap).astype(f32)
        xs = []
        for j in range(n_sub):
            rt = rnkt_ref[j, pl.ds(e, 1), :]
            p = (rt - base == slot_rows).astype(bf16)
            xs.append(jnp.dot(p, hb_ref[j * ts:(j + 1) * ts, :],
                              preferred_element_type=f32).astype(bf16))
        xs = jnp.concatenate(xs, axis=0)
        a = jnp.dot(xs, wg_ref[0], preferred_element_type=f32)
        a = (a * _sigmoid(a)) * jnp.dot(xs, wu_ref[0], preferred_element_type=f32)
        y = jnp.dot(a.astype(bf16), wd_ref[0], preferred_element_type=f32).astype(bf16)
        for j in range(n_sub):
            rows = slice(j * ts, (j + 1) * ts)
            col = jnp.sum(jnp.where(lane == e, rnk_ref[rows, :], 0.0), axis=1, keepdims=True)
            w = jnp.sum(jnp.where(lane == e, comb_ref[rows, :], 0.0), axis=1, keepdims=True)
            pt = (col - base == slot_cols).astype(bf16)
            o_ref[rows, :] += w * jnp.dot(pt, y[j * cap:(j + 1) * cap, :], preferred_element_type=f32)
        return carry

    lax.fori_loop(0, rounds_ref[i * N_EXPERTS + e], round_body, 0)


def _moe(hb, rnk, rnkt, comb, rounds, wg, wu, wd, ts, n_sub, cap):
    mp = hb.shape[0]
    n_e, _, d_ff = wg.shape
    tsup = ts * n_sub
    grid_spec = pltpu.PrefetchScalarGridSpec(
        num_scalar_prefetch=1,
        grid=(mp // tsup, n_e),
        in_specs=[pl.BlockSpec((tsup, D_MODEL), lambda i, e, r: (i, 0)),
                  pl.BlockSpec((tsup, LANES), lambda i, e, r: (i, 0)),
                  pl.BlockSpec((n_sub, N_EXPERTS, ts), lambda i, e, r: (i, 0, 0)),
                  pl.BlockSpec((tsup, LANES), lambda i, e, r: (i, 0)),
                  pl.BlockSpec((1, D_MODEL, d_ff), lambda i, e, r: (e, 0, 0)),
                  pl.BlockSpec((1, D_MODEL, d_ff), lambda i, e, r: (e, 0, 0)),
                  pl.BlockSpec((1, d_ff, D_MODEL), lambda i, e, r: (e, 0, 0))],
        out_specs=pl.BlockSpec((tsup, D_MODEL), lambda i, e, r: (i, 0)),
    )
    return pl.pallas_call(
        functools.partial(_moe_kernel, cap=cap, n_sub=n_sub, ts=ts),
        grid_spec=grid_spec,
        out_shape=jax.ShapeDtypeStruct((mp, D_MODEL), f32),
        compiler_params=_cparams(("parallel", "arbitrary")),
        name="moe",
    )(rounds, hb, rnk, rnkt, comb, wg, wu, wd)


def _ffn_kernel(x_ref, g_ref, wg_ref, wu_ref, wd_ref, o_ref, *, tf):
    x = x_ref[...]
    h = _rms(x, g_ref[...]).astype(bf16)
    acc = x
    for c in range(wg_ref.shape[1] // tf):
        cols = slice(c * tf, (c + 1) * tf)
        a = jnp.dot(h, wg_ref[:, cols], preferred_element_type=f32)
        a = (a * _sigmoid(a)) * jnp.dot(h, wu_ref[:, cols], preferred_element_type=f32)
        acc = acc + jnp.dot(a.astype(bf16), wd_ref[cols, :], preferred_element_type=f32)
    o_ref[...] = acc


def _ffn(x, g, wg, wu, wd, tm, tf):
    mp = x.shape[0]
    d_ff = wg.shape[1]
    return pl.pallas_call(
        functools.partial(_ffn_kernel, tf=tf),
        grid=(mp // tm,),
        in_specs=[pl.BlockSpec((tm, D_MODEL), lambda i: (i, 0)),
                  _resident((1, D_MODEL)),
                  _resident((D_MODEL, d_ff)),
                  _resident((D_MODEL, d_ff)),
                  _resident((d_ff, D_MODEL))],
        out_specs=pl.BlockSpec((tm, D_MODEL), lambda i: (i, 0)),
        out_shape=jax.ShapeDtypeStruct((mp, D_MODEL), f32),
        compiler_params=_cparams(("parallel",)),
        name="ffn",
    )(x, g, wg, wu, wd)


def _final_norm_kernel(x_ref, y_ref, g_ref, op_ref, os_ref, *, n_prompt_tiles, dec):
    i = pl.program_id(0)
    out = _rms(x_ref[...] + y_ref[...], g_ref[...])

    @pl.when(i < n_prompt_tiles)
    def _():
        op_ref[...] = out

    @pl.when(i == n_prompt_tiles)
    def _():
        os_ref[...] = out[0:dec]


def _final_norm(x, y, g, tm, m_prompt, dec):
    n_p = m_prompt // tm
    assert n_p * tm == m_prompt and dec <= tm
    return pl.pallas_call(
        functools.partial(_final_norm_kernel, n_prompt_tiles=n_p, dec=dec),
        grid=(n_p + 1,),
        in_specs=[pl.BlockSpec((tm, D_MODEL), lambda i: (i, 0)),
                  pl.BlockSpec((tm, D_MODEL), lambda i: (i, 0)),
                  pl.BlockSpec((1, D_MODEL), lambda i: (0, 0))],
        out_specs=[pl.BlockSpec((tm, D_MODEL), lambda i: (jnp.minimum(i, n_p - 1), 0)),
                   pl.BlockSpec((dec, D_MODEL), lambda i: (0, 0))],
        out_shape=[jax.ShapeDtypeStruct((m_prompt, D_MODEL), f32),
                   jax.ShapeDtypeStruct((dec, D_MODEL), f32)],
        compiler_params=_cparams(("arbitrary",)),
        name="final_norm",
    )(x, y, g)


def _regroup_w_in(w):
    o_mi = 3 * ML_W
    o_mo = o_mi + 2 * ML_HEADS
    o_gq = o_mo + ML_W
    o_glr = o_gq + 2 * GLA_KW + GLA_VW
    o_go = o_glr + GLA_RANK
    main = jnp.concatenate([w[:, :o_mi], w[:, o_mo:o_glr], w[:, o_go:]], axis=1)
    small = jnp.concatenate([w[:, o_mi:o_mo], w[:, o_glr:o_go],
                             jnp.zeros((w.shape[0], LANES - 2 * ML_HEADS - GLA_RANK), w.dtype)], axis=1)
    return main.astype(bf16), small.astype(bf16)


def _layer_weights(l, norm1_g, w_in, if_bias, ml_g, gla_gw, gla_gb, gla_g, pool_w, pool_scale,
                   w_branch, w_out):
    w_main, w_small = _regroup_w_in(w_in[l])
    ifb = jnp.zeros((1, LANES), f32)
    ifb = ifb.at[0, S_MI:S_MI + ML_HEADS].set(if_bias[l, 0]).at[0, S_MF:S_MF + ML_HEADS].set(if_bias[l, 1])
    gw = jnp.zeros((LANES, GLA_KW), f32).at[S_GLR:S_GLR + GLA_RANK].set(gla_gw[l]).astype(bf16)
    return dict(
        norm1_g=norm1_g[l][None], w_main=w_main, w_small=w_small, ifb=ifb, ml_g=ml_g[l][None],
        gla_gw=gw, gla_gb=gla_gb[l][None], gla_g=gla_g[l][None], pool_w=pool_w[l].astype(bf16),
        pool_scale=pool_scale[l][None], w_branch=w_branch[l].astype(bf16), w_out=w_out[l].astype(bf16))


def _forward(x_prompt, x_sample, state_mlstm_C, state_mlstm_n, state_mlstm_m, state_gla_S,
             state_pool_buf, norm1_g, w_in, mlstm_if_bias, mlstm_norm_g, gla_gate_w, gla_gate_b,
             gla_norm_g, pool_w, pool_scale, w_branch, w_out, norm2_g, ffn_wg, ffn_wu, ffn_wd,
             router_w, moe_wg, moe_wu, moe_wd, final_norm_g, *, tm, tn, tt, chunk, sub, nb, sb, tf_dense, moe_sub, moe_cap):
    batch, seq, _ = x_prompt.shape
    dec = x_sample.shape[0]
    depth = w_in.shape[0]
    m_prompt = batch * seq
    m_all = m_prompt + dec
    mp = -(-m_all // tm) * tm
    assert m_prompt % tm == 0
    x_parts = (x_prompt.reshape(m_prompt, D_MODEL),
               jnp.concatenate([x_sample.reshape(dec, D_MODEL), jnp.zeros((mp - m_all, D_MODEL), f32)], axis=0))
    outs = [[] for _ in range(5)]
    moe_out = None
    st = (state_mlstm_C, state_mlstm_n, jnp.pad(state_mlstm_m, ((0, 0), (0, 0), (0, LANES - ML_HEADS))),
          state_gla_S, state_pool_buf)
    st_new = None
    for l in range(depth):
        lw = _layer_weights(l, norm1_g, w_in, mlstm_if_bias, mlstm_norm_g, gla_gate_w, gla_gate_b,
                            gla_norm_g, pool_w, pool_scale, w_branch, w_out)
        z, zs = _norm_matmul(x_parts, lw["norm1_g"], lw["w_main"], lw["w_small"], tm, tn)
        br_p, c_p, n_p, m_p, s_p, buf_p = _prompt_mix(z, zs, lw, batch, seq, tt, chunk, sub, nb)
        br_t, *st_new = _sample_mix(z, zs, st, st_new, l, lw, m_prompt, dec, sb)
        x = _merge(x_parts, br_p, br_t, z, lw["w_branch"], lw["w_out"], tm, m_all)
        j = l // 2
        if l % 2 == 0:
            x = _ffn(x, norm2_g[l][None], ffn_wg[j].astype(bf16), ffn_wu[j].astype(bf16),
                     ffn_wd[j].astype(bf16), tm, tf_dense)
        else:
            wr = jnp.pad(router_w[j], ((0, 0), (0, LANES - N_EXPERTS))).astype(bf16)
            hb, comb, rnk, rnkt, cnt = _router(x, norm2_g[l][None], wr, tm, m_all)
            n_sup = mp // (tm * moe_sub)
            cnt = cnt[:, 0, :N_EXPERTS].reshape(n_sup, moe_sub, N_EXPERTS).max(axis=1)
            rounds = jnp.ceil(cnt / moe_cap).astype(jnp.int32).reshape(-1)
            moe_out = _moe(hb, rnk, rnkt, comb, rounds, moe_wg[j].astype(bf16), moe_wu[j].astype(bf16),
                           moe_wd[j].astype(bf16), tm, moe_sub, moe_cap)
            if l + 1 < depth:
                x = x + moe_out
                moe_out = None
        x_parts = (x,)
        for lst, val in zip(outs, (c_p, n_p, m_p[:, :ML_HEADS, 0], s_p, buf_p)):
            lst.append(val)
    if moe_out is None:
        moe_out = jnp.zeros_like(x)
    y_prompt, y_sample = _final_norm(x, moe_out, final_norm_g[None], tm, m_prompt, dec)
    c_p, n_p, m_p, s_p, buf_p = (jnp.stack(o) for o in outs)
    c_s, n_s, m_s, s_s, buf_s = st_new
    return (y_prompt.reshape(batch, seq, D_MODEL), y_sample.reshape(dec, 1, D_MODEL),
            c_p, c_s, n_p, n_s, m_p, m_s[:, :, :ML_HEADS], s_p, s_s, buf_p, buf_s)


def kernel(x_prompt, x_sample, state_mlstm_C, state_mlstm_n, state_mlstm_m, state_gla_S, state_pool_buf, norm1_g, w_in, mlstm_if_bias, mlstm_norm_g, gla_gate_w, gla_gate_b, gla_norm_g, pool_w, pool_scale, w_branch, w_out, norm2_g, ffn_wg, ffn_wu, ffn_wd, router_w, moe_wg, moe_wu, moe_wd, final_norm_g):
    return _forward(x_prompt, x_sample, state_mlstm_C, state_mlstm_n, state_mlstm_m, state_gla_S,
                    state_pool_buf, norm1_g, w_in, mlstm_if_bias, mlstm_norm_g, gla_gate_w, gla_gate_b,
                    gla_norm_g, pool_w, pool_scale, w_branch, w_out, norm2_g, ffn_wg, ffn_wu, ffn_wd,
                    router_w, moe_wg, moe_wu, moe_wd, final_norm_g,
                    tm=512, tn=1024, tt=512, chunk=128, sub=64, nb=2, sb=16, tf_dense=1408, moe_sub=3, moe_cap=160)
```

```python
import functools

import jax
import jax.numpy as jnp
from jax import lax
from jax.experimental import pallas as pl
from jax.experimental.pallas import tpu as pltpu

f32 = jnp.float32
bf16 = jnp.bfloat16

D_MODEL = 1024
ML_HEADS, ML_DH = 4, 128
ML_W = ML_HEADS * ML_DH
GLA_HEADS, GLA_DK, GLA_DV = 4, 64, 128
GLA_KW, GLA_VW = GLA_HEADS * GLA_DK, GLA_HEADS * GLA_DV
GLA_RANK = 16
GLA_TAU = 16.0
POOL_GDIM = 128
POOL_WINDOWS = (2, 4, 8, 16)
POOL_W = POOL_GDIM * len(POOL_WINDOWS)
POOL_BUF = 15
N_BRANCH = 3
N_EXPERTS = 8
EPS = 1e-6
NEG = -1e30
LANES = 128

C_MQ, C_MK, C_MV, C_MO = 0, 512, 1024, 1536
C_GQ, C_GK, C_GV, C_GO, C_PU, C_GATES = 2048, 2304, 2560, 3072, 3584, 4096
Z_MAIN = C_GATES + N_BRANCH * D_MODEL
S_MI, S_MF, S_GLR = 0, 4, 8

VMEM_LIMIT = 56 * 1024 * 1024

_NT = (((1,), (1,)), ((), ()))


def _cparams(sem):
    return pltpu.CompilerParams(dimension_semantics=sem, vmem_limit_bytes=VMEM_LIMIT)


def _log_sigmoid(x):
    return jnp.minimum(x, 0.0) - jnp.log(1.0 + jnp.exp(-jnp.abs(x)))


def _sigmoid(x):
    return 0.5 * jnp.tanh(0.5 * x) + 0.5


def _rms(x, g):
    ms = jnp.mean(x * x, axis=-1, keepdims=True)
    return x * lax.rsqrt(ms + EPS) * g


def _lower_tri(n):
    r = lax.broadcasted_iota(jnp.int32, (n, n), 0)
    c = lax.broadcasted_iota(jnp.int32, (n, n), 1)
    return c <= r


def _cumsum_rows(tri_bf16, a):
    a1 = a.astype(bf16)
    r = a - a1.astype(f32)
    a2 = r.astype(bf16)
    a3 = (r - a2.astype(f32)).astype(bf16)
    d = lambda y: jnp.dot(tri_bf16, y, preferred_element_type=f32)
    return d(a1) + d(a2) + d(a3)


def _cumsum_lanes(triu_bf16, a):
    a1 = a.astype(bf16)
    r = a - a1.astype(f32)
    a2 = r.astype(bf16)
    a3 = (r - a2.astype(f32)).astype(bf16)
    d = lambda y: jnp.dot(y, triu_bf16, preferred_element_type=f32)
    return d(a1) + d(a2) + d(a3)


def _resident(shape):
    nd = len(shape)
    return pl.BlockSpec(shape, lambda *_: (0,) * nd, pipeline_mode=pl.Buffered(1))


def _split_rows_specs(parts, tm, width):
    specs, start = [], 0
    for p in parts:
        n_t = p.shape[0] // tm
        assert n_t * tm == p.shape[0]
        specs.append(pl.BlockSpec(
            (tm, width), functools.partial(lambda i, s, n: (jnp.clip(i - s, 0, n - 1), 0), s=start, n=n_t)))
        start += n_t
    return specs


def _pick_rows(refs, starts):
    i = pl.program_id(0)
    x = refs[0][...]
    for r, s in zip(refs[1:], starts[1:]):
        x = jnp.where(i >= s, r[...], x)
    return x


def _norm_matmul_kernel(*refs, tn, starts):
    x_refs = refs[:len(starts)]
    g_ref, w_ref, ws_ref, z_ref, zs_ref = refs[len(starts):]
    h = _rms(_pick_rows(x_refs, starts), g_ref[...]).astype(bf16)
    zs_ref[...] = jnp.dot(h, ws_ref[0], preferred_element_type=f32)
    for c in range(w_ref.shape[2] // tn):
        cols = slice(c * tn, (c + 1) * tn)
        z_ref[:, cols] = jnp.dot(h, w_ref[0, :, cols], preferred_element_type=f32).astype(z_ref.dtype)


def _norm_matmul(x_parts, g, w_main, w_small, layer, tm, tn):
    mp = sum(p.shape[0] for p in x_parts)
    n = w_main.shape[2]
    starts = tuple(sum(p.shape[0] for p in x_parts[:j]) // tm for j in range(len(x_parts)))
    layer_block = lambda width: pl.BlockSpec((1, D_MODEL, width), lambda i: (layer, 0, 0),
                                             pipeline_mode=pl.Buffered(1))
    return pl.pallas_call(
        functools.partial(_norm_matmul_kernel, tn=tn, starts=starts),
        grid=(mp // tm,),
        in_specs=[
            *_split_rows_specs(x_parts, tm, D_MODEL),
            _resident((1, D_MODEL)),
            layer_block(n),
            layer_block(LANES),
        ],
        out_specs=[
            pl.BlockSpec((tm, n), lambda i: (i, 0)),
            pl.BlockSpec((tm, LANES), lambda i: (i, 0)),
        ],
        out_shape=[jax.ShapeDtypeStruct((mp, n), bf16), jax.ShapeDtypeStruct((mp, LANES), f32)],
        compiler_params=_cparams(("parallel",)),
        name="norm_matmul",
    )(*x_parts, g, w_main, w_small)


def _head_norm(h, g):
    outs = []
    for j in range(h.shape[1] // LANES):
        hj = h[:, j * LANES:(j + 1) * LANES]
        outs.append(hj * lax.rsqrt(jnp.mean(hj * hj, axis=-1, keepdims=True) + EPS))
    return jnp.concatenate(outs, axis=1) * g


def _gla_log_decay(sm, gw_ref, gb_ref):
    xg = jnp.dot(sm.astype(bf16), gw_ref[...], preferred_element_type=f32) + gb_ref[...]
    return _log_sigmoid(xg) * (1.0 / GLA_TAU)


N_ZBLOCKS = 9


def _prompt_mix_kernel(*refs, nb, tt, chunk, sub, n_t):
    z_refs = [refs[s * N_ZBLOCKS:(s + 1) * N_ZBLOCKS] for s in range(nb)]
    ifb_ref, mlg_ref, gw_ref, gb_ref, glag_ref, pw_ref, ps_ref = refs[nb * N_ZBLOCKS:nb * N_ZBLOCKS + 7]
    br_all, c_out, n_out, m_out, s_out, buf_out = refs[nb * N_ZBLOCKS + 7:nb * N_ZBLOCKS + 13]
    scratch = refs[nb * N_ZBLOCKS + 13:]
    per = ML_HEADS + GLA_HEADS + 7
    c_refs = [scratch[s * per:s * per + ML_HEADS] for s in range(nb)]
    s_refs = [scratch[s * per + ML_HEADS:s * per + ML_HEADS + GLA_HEADS] for s in range(nb)]
    n_refs = [scratch[s * per + ML_HEADS + GLA_HEADS] for s in range(nb)]
    m_refs = [scratch[s * per + ML_HEADS + GLA_HEADS + 1] for s in range(nb)]
    ext_refs = [scratch[s * per + ML_HEADS + GLA_HEADS + 2] for s in range(nb)]
    prep_refs = [scratch[s * per + ML_HEADS + GLA_HEADS + 3:(s + 1) * per] for s in range(nb)]
    t_idx = pl.program_id(1)
    L = chunk

    @pl.when(t_idx == 0)
    def _():
        for s in range(nb):
            for r in (*c_refs[s], *s_refs[s], n_refs[s], m_refs[s]):
                r[...] = jnp.zeros_like(r)
            ext_refs[s][0:16, :] = jnp.zeros((16, POOL_W), f32)

    tri = _lower_tri(L)
    tri_b = tri.astype(bf16)
    causal_sub = _lower_tri(sub)
    assert L == LANES
    ones_b = jnp.ones((L, LANES), bf16)
    triu_b = (lax.broadcasted_iota(jnp.int32, (L, L), 0)
              <= lax.broadcasted_iota(jnp.int32, (L, L), 1)).astype(bf16)
    ifb = ifb_ref[...]
    k_scale = ML_DH ** -0.5
    q_scale = GLA_DK ** -0.5

    def gate_prep(c, seq):
        sm_ref = z_refs[seq][8]
        y0_p, bc_p, rows_p, b_p = prep_refs[seq]
        sm = sm_ref[pl.ds(pl.multiple_of(c * L, L), L), :]
        y0 = sm + ifb
        y0_p[...] = y0
        bc_p[...] = _cumsum_rows(tri_b, _log_sigmoid(y0))
        y0t = y0.T[0:8, :]
        rows_p[0:8, :] = y0t
        rows_p[8:16, :] = _cumsum_lanes(triu_b, _log_sigmoid(y0t))
        b_p[...] = _cumsum_rows(tri_b, _gla_log_decay(sm, gw_ref, gb_ref))

    def one_seq(c, seq):
        q_ref, k_ref, v_ref, mo_ref, gqk_ref, gv_ref, go_ref, _, sm_ref = z_refs[seq]
        br_ref, c_s, s_s = br_all.at[seq], c_refs[seq], s_refs[seq]
        n_old, m_old = n_refs[seq][...], m_refs[seq][...]
        n_rows, m_rows = [], []
        r0 = pl.multiple_of(c * L, L)
        rows = pl.ds(r0, L)
        y0_p, bc_p, rows_p, b_p = prep_refs[seq]

        y0, bc = y0_p[...], bc_p[...]
        y0t, bct = rows_p[0:8, :], rows_p[8:16, :]
        b = b_p[...]
        hm = []
        for h in range(ML_HEADS):
            hs = slice(h * ML_DH, (h + 1) * ML_DH)
            v = v_ref[rows, hs].astype(f32)
            kb = k_ref[rows, hs].astype(bf16)
            qb, vb = q_ref[rows, hs].astype(bf16), v.astype(bf16)
            bcol = bc[:, S_MF + h:S_MF + h + 1]
            icol = y0[:, S_MI + h:S_MI + h + 1]
            brow = bct[S_MF + h:S_MF + h + 1, :]
            irow = y0t[S_MI + h:S_MI + h + 1, :]
            m_row = m_old[h:h + 1, :]
            cmat = c_s[h][...]
            nrow = n_old[h:h + 1, :]
            bcol_r = jnp.broadcast_to(bcol, (L, LANES))
            dm = jnp.where(tri, bcol_r - brow + irow, NEG)
            inter_r = bcol_r + m_row
            m_t_r = jnp.maximum(inter_r, jnp.broadcast_to(jnp.max(dm, axis=1, keepdims=True), (L, LANES)))
            w_intra = jnp.exp(dm - m_t_r) * k_scale
            w_inter_r = jnp.exp(inter_r - m_t_r)
            s = lax.dot_general(qb, kb, _NT, preferred_element_type=f32) * w_intra
            s_hi = s.astype(bf16)
            s_lo = (s - s_hi.astype(f32)).astype(bf16)
            r_intra = jnp.dot(s_hi, jnp.concatenate([vb, ones_b], axis=1), preferred_element_type=f32)
            c_aug = jnp.concatenate([cmat, jnp.broadcast_to(nrow, (LANES, LANES))], axis=0).astype(bf16)
            r_inter = lax.dot_general(qb, c_aug, _NT, preferred_element_type=f32)
            rs_lo = jnp.dot(s_lo, ones_b, preferred_element_type=f32)
            num = r_intra[:, 0:LANES] + w_inter_r * r_inter[:, 0:LANES]
            den_r = r_intra[:, LANES:] + rs_lo + w_inter_r * r_inter[:, LANES:]
            hm.append(num / jnp.maximum(jnp.abs(den_r), jnp.exp(-m_t_r)))
            b_last = bcol[L - 1:L, :]
            m = m_row[:, 0:1]
            g = b_last - bcol + icol
            m_new = jnp.maximum(b_last + m, jnp.max(g, axis=0, keepdims=True))
            w_s = jnp.broadcast_to(jnp.exp(g - m_new) * k_scale, (L, LANES))
            w_c = jnp.exp(b_last + m - m_new)
            vwt = (v * w_s).T.astype(bf16)
            c_s[h][...] = w_c * cmat + jnp.dot(vwt, kb, preferred_element_type=f32)
            n_rows.append(w_c * nrow + jnp.sum(w_s * k_ref[rows, hs].astype(f32), axis=0, keepdims=True))
            m_rows.append(jnp.broadcast_to(m_new, (1, LANES)))
        pad_rows = [jnp.zeros((8 - ML_HEADS, LANES), f32)]
        n_refs[seq][...] = jnp.concatenate(n_rows + pad_rows, axis=0)
        m_refs[seq][...] = jnp.concatenate(m_rows + pad_rows, axis=0)
        y_ml = (_head_norm(jnp.concatenate(hm, axis=1), mlg_ref[...])
                * _sigmoid(mo_ref[rows, :].astype(f32)))
        br_ref[rows, 0:ML_W] = y_ml.astype(br_ref.dtype)

        q2 = gqk_ref[rows, 0:GLA_KW].astype(f32) * q_scale
        k2 = gqk_ref[rows, GLA_KW:2 * GLA_KW].astype(f32)
        gv = gv_ref[rows, :].astype(bf16)
        s_old = [s_s[h][...] for h in range(GLA_HEADS)]
        qe_chunk = (q2 * jnp.exp(b)).astype(bf16)
        b_last = b[L - 1:L, :]
        kdt = (k2 * jnp.exp(b_last - b)).T.astype(bf16)
        decay_col = jnp.exp(b.T[:, L - 1:L])
        o_blocks = []
        for blk in range(L // sub):
            s0, s1 = blk * sub, (blk + 1) * sub
            mid = s0 + sub // 2
            b_blk = b[s0:s1]
            b_mid = b[mid - 1:mid, :]
            qe_d = (q2[s0:s1] * jnp.exp(b_blk - b_mid)).astype(bf16)
            ke_d = (k2[s0:s1] * jnp.exp(b_mid - b_blk)).astype(bf16)
            if blk > 0:
                b_start = b[s0 - 1:s0, :]
                qe_o = (q2[s0:s1] * jnp.exp(b_blk - b_start)).astype(bf16)
                ke_o = (k2[0:s0] * jnp.exp(b_start - b[0:s0])).astype(bf16)
            o_heads = []
            for h in range(GLA_HEADS):
                ks = slice(h * GLA_DK, (h + 1) * GLA_DK)
                vs = slice(h * GLA_DV, (h + 1) * GLA_DV)
                a = lax.dot_general(qe_d[:, ks], ke_d[:, ks], _NT, preferred_element_type=f32)
                a = jnp.where(causal_sub, a, 0.0)
                o = (jnp.dot(a.astype(bf16), gv[s0:s1, vs], preferred_element_type=f32)
                     + jnp.dot(qe_chunk[s0:s1, ks], s_old[h].astype(bf16), preferred_element_type=f32))
                if blk > 0:
                    a = lax.dot_general(qe_o[:, ks], ke_o[:, ks], _NT, preferred_element_type=f32)
                    o = o + jnp.dot(a.astype(bf16), gv[0:s0, vs], preferred_element_type=f32)
                o_heads.append(o)
            o_blocks.append(jnp.concatenate(o_heads, axis=1))
        og = jnp.concatenate(o_blocks, axis=0)
        gate_prep(jnp.minimum(c + 1, tt // L - 1), seq)
        for h in range(GLA_HEADS):
            ks = slice(h * GLA_DK, (h + 1) * GLA_DK)
            vs = slice(h * GLA_DV, (h + 1) * GLA_DV)
            s_s[h][...] = (decay_col[ks, :] * s_old[h]
                           + jnp.dot(kdt[ks, :], gv[:, vs], preferred_element_type=f32))
        go = go_ref[rows, :].astype(f32)
        y_gla = _head_norm(og, glag_ref[...]) * (go * _sigmoid(go))
        br_ref[rows, ML_W:ML_W + GLA_VW] = y_gla.astype(br_ref.dtype)

    def chunk_body(c, carry):
        for s in range(nb):
            one_seq(c, s)
        return carry

    for s in range(nb):
        gate_prep(0, s)
    lax.fori_loop(0, tt // L, chunk_body, 0)

    pos = t_idx * tt + lax.broadcasted_iota(jnp.int32, (tt, 1), 0)
    for s in range(nb):
        pu_ref, br_ref, ext_s = z_refs[s][7], br_all.at[s], ext_refs[s]
        ext_s[16:16 + tt, :] = pu_ref[...].astype(f32)
        for g, w in enumerate(POOL_WINDOWS):
            gs = slice(g * POOL_GDIM, (g + 1) * POOL_GDIM)
            u = ext_s[16:16 + tt, gs]
            acc = u
            for j in range(1, w):
                acc = acc + ext_s[16 - j:16 - j + tt, gs]
            cnt = jnp.minimum(pos + 1, w).astype(f32)
            d = acc / cnt - u
            y = jnp.dot(d.astype(bf16), pw_ref[g], preferred_element_type=f32) * ps_ref[:, gs]
            br_ref[:, ML_W + GLA_VW + g * POOL_GDIM:ML_W + GLA_VW + (g + 1) * POOL_GDIM] = y.astype(br_ref.dtype)
        ext_s[0:16, :] = ext_s[tt:tt + 16, :]

    @pl.when(t_idx == n_t - 1)
    def _():
        for s in range(nb):
            for h in range(ML_HEADS):
                c_out[s, h] = c_refs[s][h][...]
            for h in range(GLA_HEADS):
                s_out[s, h] = s_refs[s][h][...]
            n_out[s] = n_refs[s][0:ML_HEADS, :]
            m_out[s] = m_refs[s][...]
            buf_out[s] = ext_refs[s][1:16, :]


def _prompt_mix(z, zs, lw, batch, seq, tt, chunk, sub, nb):
    n_t = seq // tt
    assert batch % nb == 0

    def zspecs(s):
        row = lambda b, t: (b * nb + s) * n_t + t
        spec = lambda col, width: pl.BlockSpec((tt, width), lambda b, t: (row(b, t), col // width))
        return [spec(C_MQ, 512), spec(C_MK, 512), spec(C_MV, 512), spec(C_MO, 512),
                spec(C_GQ, 512), spec(C_GV, 512), spec(C_GO, 512), spec(C_PU, 512),
                pl.BlockSpec((tt, LANES), lambda b, t: (row(b, t), 0))]

    def full(shape):
        nd = len(shape)
        return pl.BlockSpec(shape, lambda b, t: (0,) * nd)

    def per_seq(shape):
        nd = len(shape)
        return pl.BlockSpec((nb,) + shape, lambda b, t: (b,) + (0,) * nd)

    kern = functools.partial(_prompt_mix_kernel, nb=nb, tt=tt, chunk=chunk, sub=sub, n_t=n_t)
    outs = pl.pallas_call(
        kern,
        grid=(batch // nb, n_t),
        in_specs=[sp for s in range(nb) for sp in zspecs(s)]
                 + [full((1, LANES)), full((1, ML_W)), full((LANES, GLA_KW)), full((1, GLA_KW)),
                    full((1, GLA_VW)), full((len(POOL_WINDOWS), POOL_GDIM, POOL_GDIM)), full((1, POOL_W))],
        out_specs=[
            pl.BlockSpec((nb, tt, N_BRANCH * ML_W), lambda b, t: (b, t, 0)),
            per_seq((ML_HEADS, ML_DH, ML_DH)), per_seq((ML_HEADS, ML_DH)), per_seq((8, LANES)),
            per_seq((GLA_HEADS, GLA_DK, GLA_DV)), per_seq((POOL_BUF, POOL_W)),
        ],
        out_shape=[
            jax.ShapeDtypeStruct((batch, seq, N_BRANCH * ML_W), bf16),
            jax.ShapeDtypeStruct((batch, ML_HEADS, ML_DH, ML_DH), f32),
            jax.ShapeDtypeStruct((batch, ML_HEADS, ML_DH), f32),
            jax.ShapeDtypeStruct((batch, 8, LANES), f32),
            jax.ShapeDtypeStruct((batch, GLA_HEADS, GLA_DK, GLA_DV), f32),
            jax.ShapeDtypeStruct((batch, POOL_BUF, POOL_W), f32),
        ],
        scratch_shapes=([pltpu.VMEM((ML_DH, ML_DH), f32)] * ML_HEADS
                        + [pltpu.VMEM((GLA_DK, GLA_DV), f32)] * GLA_HEADS
                        + [pltpu.VMEM((8, LANES), f32), pltpu.VMEM((8, LANES), f32),
                           pltpu.VMEM((tt + 16, POOL_W), f32),
                           pltpu.VMEM((chunk, LANES), f32), pltpu.VMEM((chunk, LANES), f32),
                           pltpu.VMEM((16, chunk), f32), pltpu.VMEM((chunk, GLA_KW), f32)]) * nb,
        compiler_params=_cparams(("parallel", "arbitrary")),
        name="prompt_mix",
    )(*([z] * 8 + [zs]) * nb,
      lw["ifb"], lw["ml_g"], lw["gla_gw"], lw["gla_gb"], lw["gla_g"], lw["pool_w"], lw["pool_scale"])
    return (outs[0].reshape(batch * seq, N_BRANCH * ML_W),) + tuple(outs[1:])


def _sample_mix_kernel(q_ref, k_ref, v_ref, mo_ref, gqk_ref, gv_ref, go_ref, pu_ref, sm_ref,
                       c_in, n_in, m_in, s_in, buf_in,
                       ifb_ref, mlg_ref, gw_ref, gb_ref, glag_ref, pw_ref, ps_ref, *tail, sb, dec):
    br_all = tail[-6]
    c_out, n_out, m_out, s_out, buf_out = (r.at[0] for r in tail[-5:])
    step = pl.program_id(0)
    c_in, n_in, m_in, s_in, buf_in = (r.at[0] for r in (c_in, n_in, m_in, s_in, buf_in))
    br_ref = br_all.at[pl.ds(pl.multiple_of(step * sb, sb), sb), :]

    @pl.when(step == 0)
    def _():
        br_all[dec:, :] = jnp.zeros((br_all.shape[0] - dec, br_all.shape[1]), br_all.dtype)

    sm = sm_ref[...]
    eye = (lax.broadcasted_iota(jnp.int32, (LANES, LANES), 0)
           == lax.broadcasted_iota(jnp.int32, (LANES, LANES), 1)).astype(f32)

    def to_cols(x):
        parts = [lax.dot_general(eye, x[:, j * LANES:(j + 1) * LANES], _NT,
                                 preferred_element_type=f32, precision=lax.Precision.HIGHEST)
                 for j in range(x.shape[1] // LANES)]
        return jnp.concatenate(parts, axis=0)

    y0 = sm + ifb_ref[...]
    logf_all = _log_sigmoid(y0)
    k_scale = ML_DH ** -0.5
    m_all = m_in[...]
    hm, m_new_cols = [], []
    for h in range(ML_HEADS):
        hs = slice(h * ML_DH, (h + 1) * ML_DH)
        q = q_ref[:, hs].astype(f32)
        k = k_ref[:, hs].astype(f32) * k_scale
        v = v_ref[:, hs].astype(f32)
        i_pre = y0[:, S_MI + h:S_MI + h + 1]
        logf = logf_all[:, S_MF + h:S_MF + h + 1]
        m = m_all[:, h:h + 1]
        inter = logf + m
        m_t = jnp.maximum(inter, i_pre)
        w_intra = jnp.exp(i_pre - m_t)
        w_inter = jnp.exp(inter - m_t)
        s = jnp.sum(q * k, axis=1, keepdims=True) * w_intra
        qb = q.astype(bf16)
        n_h = n_in[:, h, :]
        cq = jnp.concatenate(
            [lax.dot_general(qb, c_in[j, h].astype(bf16), _NT, preferred_element_type=f32)[j:j + 1, :]
             for j in range(sb)], axis=0)
        num = s * v + w_inter * cq
        den = s + w_inter * jnp.sum(n_h * q, axis=1, keepdims=True)
        hm.append(num / jnp.maximum(jnp.abs(den), jnp.exp(-m_t)))
        m_new = m_t
        w_s = w_intra
        w_c = w_inter
        n_out[:, h, :] = w_c * n_h + w_s * k
        m_new_cols.append(m_new)
        vw_cols = to_cols(v * w_s)
        for j in range(sb):
            c_out[j, h] = w_c[j:j + 1, :] * c_in[j, h] + vw_cols[:, j:j + 1] * k[j:j + 1, :]
    lane = lax.broadcasted_iota(jnp.int32, (sb, LANES), 1)
    m_pack = jnp.zeros((sb, LANES), f32)
    for h in range(ML_HEADS):
        m_pack = jnp.where(lane == h, m_new_cols[h], m_pack)
    m_out[...] = m_pack
    y_ml = _head_norm(jnp.concatenate(hm, axis=1), mlg_ref[...]) * _sigmoid(mo_ref[...].astype(f32))
    br_ref[:, 0:ML_W] = y_ml.astype(br_ref.dtype)

    log_a = _gla_log_decay(sm, gw_ref, gb_ref)
    decay = jnp.exp(log_a)
    q2 = gqk_ref[:, 0:GLA_KW].astype(f32) * (GLA_DK ** -0.5)
    k2 = gqk_ref[:, GLA_KW:2 * GLA_KW].astype(f32)
    gv = gv_ref[...].astype(f32)
    qe = (q2 * decay).astype(bf16)
    qk = q2 * k2
    k_cols = to_cols(k2)
    decay_cols = to_cols(decay)
    og = []
    for h in range(GLA_HEADS):
        ks = slice(h * GLA_DK, (h + 1) * GLA_DK)
        vs = slice(h * GLA_DV, (h + 1) * GLA_DV)
        a = jnp.sum(qk[:, ks], axis=1, keepdims=True)
        inter = jnp.concatenate(
            [jnp.dot(qe[:, ks], s_in[j, h].astype(bf16), preferred_element_type=f32)[j:j + 1, :]
             for j in range(sb)], axis=0)
        og.append(a * gv[:, vs] + inter)
        for j in range(sb):
            s_out[j, h] = (decay_cols[ks, j:j + 1] * s_in[j, h]
                           + k_cols[ks, j:j + 1] * gv[j:j + 1, vs])
    go = go_ref[...].astype(f32)
    y_gla = _head_norm(jnp.concatenate(og, axis=1), glag_ref[...]) * (go * _sigmoid(go))
    br_ref[:, ML_W:ML_W + GLA_VW] = y_gla.astype(br_ref.dtype)

    u = pu_ref[...].astype(f32)
    rowi = lax.broadcasted_iota(jnp.int32, (POOL_BUF + 1, POOL_GDIM), 0)
    d_rows = []
    for j in range(sb):
        ext = jnp.concatenate([buf_in[j], u[j:j + 1, :]], axis=0)
        buf_out[j] = ext[1:POOL_BUF + 1, :]
        parts = []
        for g, w in enumerate(POOL_WINDOWS):
            gs = slice(g * POOL_GDIM, (g + 1) * POOL_GDIM)
            win = jnp.sum(jnp.where(rowi >= POOL_BUF + 1 - w, ext[:, gs], 0.0), axis=0, keepdims=True)
            parts.append(win / float(w) - u[j:j + 1, gs])
        d_rows.append(jnp.concatenate(parts, axis=1))
    d = jnp.concatenate(d_rows, axis=0)
    for g in range(len(POOL_WINDOWS)):
        gs = slice(g * POOL_GDIM, (g + 1) * POOL_GDIM)
        y = jnp.dot(d[:, gs].astype(bf16), pw_ref[g], preferred_element_type=f32) * ps_ref[:, gs]
        br_ref[:, ML_W + GLA_VW + g * POOL_GDIM:ML_W + GLA_VW + (g + 1) * POOL_GDIM] = y.astype(br_ref.dtype)


def _sample_mix(z, zs, st, prev, layer, lw, row0, dec, sb):
    depth = st[0].shape[0]
    base = row0 // sb
    tail_rows = z.shape[0] - row0

    def zspec(col, width):
        blk = col // width
        return pl.BlockSpec((sb, width), lambda i: (base + i, blk))

    def full(shape):
        nd = len(shape)
        return pl.BlockSpec(shape, lambda i: (0,) * nd)

    st_shapes = [(ML_HEADS, ML_DH, ML_DH), (ML_HEADS, ML_DH), (LANES,),
                 (GLA_HEADS, GLA_DK, GLA_DV), (POOL_BUF, POOL_W)]
    st_specs = [pl.BlockSpec((1, sb) + s, functools.partial(lambda i, nd: (layer, i) + (0,) * nd, nd=len(s)))
                for s in st_shapes]
    n_in = N_ZBLOCKS + len(st_shapes) + 7
    prev = () if prev is None else tuple(prev)
    return pl.pallas_call(
        functools.partial(_sample_mix_kernel, sb=sb, dec=dec),
        grid=(dec // sb,),
        in_specs=[zspec(C_MQ, 512), zspec(C_MK, 512), zspec(C_MV, 512), zspec(C_MO, 512),
                  zspec(C_GQ, 512), zspec(C_GV, 512), zspec(C_GO, 512), zspec(C_PU, 512),
                  pl.BlockSpec((sb, LANES), lambda i: (base + i, 0))]
                 + st_specs
                 + [full((1, LANES)), full((1, ML_W)), full((LANES, GLA_KW)), full((1, GLA_KW)),
                    full((1, GLA_VW)), full((len(POOL_WINDOWS), POOL_GDIM, POOL_GDIM)), full((1, POOL_W))]
                 + [pl.BlockSpec(memory_space=pl.ANY)] * len(prev),
        out_specs=[pl.BlockSpec((tail_rows, N_BRANCH * ML_W), lambda i: (0, 0))] + st_specs,
        out_shape=[jax.ShapeDtypeStruct((tail_rows, N_BRANCH * ML_W), bf16)]
                  + [jax.ShapeDtypeStruct((depth, dec) + s, f32) for s in st_shapes],
        input_output_aliases={n_in + j: 1 + j for j in range(len(prev))},
        compiler_params=_cparams(("arbitrary",)),
        name="sample_mix",
    )(z, z, z, z, z, z, z, z, zs, *st,
      lw["ifb"], lw["ml_g"], lw["gla_gw"], lw["gla_gb"], lw["gla_g"], lw["pool_w"], lw["pool_scale"], *prev)


def _merge_kernel(*refs, tm, n_valid, n_prompt_tiles, x_starts):
    x_refs = refs[:len(x_starts)]
    brp_ref, brt_ref, g0_ref, g1_ref, g2_ref, wb_ref, wo_ref, o_ref = refs[len(x_starts):]
    in_tail = pl.program_id(0) >= n_prompt_tiles
    mixed = None
    for n, g_ref in enumerate((g0_ref, g1_ref, g2_ref)):
        cols = slice(n * ML_W, (n + 1) * ML_W)
        br = jnp.where(in_tail, brt_ref[:, cols], brp_ref[:, cols])
        proj = jnp.dot(br, wb_ref[n], preferred_element_type=f32)
        term = _sigmoid(g_ref[...].astype(f32)) * proj
        mixed = term if mixed is None else mixed + term
    out = (_pick_rows(x_refs, x_starts)
           + jnp.dot(mixed.astype(bf16), wo_ref[...], preferred_element_type=f32))
    row = pl.program_id(0) * tm + lax.broadcasted_iota(jnp.int32, (tm, 1), 0)
    o_ref[...] = jnp.where(row < n_valid, out, 0.0)


def _merge(x_parts, br_prompt, br_tail, z, w_branch, w_out, tm, n_valid):
    mp = sum(p.shape[0] for p in x_parts)
    x_starts = tuple(sum(p.shape[0] for p in x_parts[:j]) // tm for j in range(len(x_parts)))
    n_p = br_prompt.shape[0] // tm
    gate_specs = [pl.BlockSpec((tm, D_MODEL), functools.partial(lambda i, n: (i, C_GATES // D_MODEL + n), n=n))
                  for n in range(N_BRANCH)]
    return pl.pallas_call(
        functools.partial(_merge_kernel, tm=tm, n_valid=n_valid, n_prompt_tiles=n_p, x_starts=x_starts),
        grid=(mp // tm,),
        in_specs=[
            *_split_rows_specs(x_parts, tm, D_MODEL),
            pl.BlockSpec((tm, N_BRANCH * ML_W), lambda i: (jnp.minimum(i, n_p - 1), 0)),
            pl.BlockSpec((tm, N_BRANCH * ML_W), lambda i: (jnp.maximum(i - n_p, 0), 0)),
            *gate_specs,
            _resident((N_BRANCH, ML_W, D_MODEL)),
            _resident((D_MODEL, D_MODEL)),
        ],
        out_specs=pl.BlockSpec((tm, D_MODEL), lambda i: (i, 0)),
        out_shape=jax.ShapeDtypeStruct((mp, D_MODEL), f32),
        compiler_params=_cparams(("parallel",)),
        name="merge",
    )(*x_parts, br_prompt, br_tail, z, z, z, w_branch, w_out)


def _router_kernel(x_ref, g_ref, wr_ref, hb_ref, comb_ref, rnk_ref, rnkt_ref, cnt_ref, *, tm, n_valid):
    h = _rms(x_ref[...], g_ref[...])
    h_hi = h.astype(bf16)
    hb_ref[...] = h_hi
    h_lo = (h - h_hi.astype(f32)).astype(bf16)
    p_hi = jnp.dot(h_hi, wr_ref[...], preferred_element_type=f32)
    p_lo = jnp.dot(h_lo, wr_ref[...], preferred_element_type=f32)
    logits = p_hi + p_lo + pltpu.roll(p_hi, LANES - N_EXPERTS, 1)
    lane = lax.broadcasted_iota(jnp.int32, logits.shape, 1)
    valid = lane < N_EXPERTS
    logits = jnp.where(valid, logits, NEG)
    mx = jnp.max(logits, axis=1, keepdims=True)
    e = jnp.where(valid, jnp.exp(logits - mx), 0.0)
    probs = e / jnp.sum(e, axis=1, keepdims=True)
    p1 = jnp.max(probs, axis=1, keepdims=True)
    i1 = jnp.min(jnp.where(probs == p1, lane, LANES), axis=1, keepdims=True)
    rest = jnp.where((lane == i1) | ~valid, -1.0, probs)
    p2 = jnp.max(rest, axis=1, keepdims=True)
    i2 = jnp.min(jnp.where(rest == p2, lane, LANES), axis=1, keepdims=True)
    tot = p1 + p2
    comb_ref[...] = jnp.where(lane == i1, p1 / tot, 0.0) + jnp.where(lane == i2, p2 / tot, 0.0)
    row = pl.program_id(0) * tm + lax.broadcasted_iota(jnp.int32, logits.shape, 0)
    sel = ((lane == i1) | (lane == i2)) & (row < n_valid)
    r = lax.broadcasted_iota(jnp.int32, (tm, tm), 0)
    c = lax.broadcasted_iota(jnp.int32, (tm, tm), 1)
    rank = jnp.dot((c < r).astype(bf16), sel.astype(bf16), preferred_element_type=f32)
    rnk = jnp.where(sel, rank, -1.0)
    rnk_ref[...] = rnk
    rnkt_ref[0] = rnk.T[0:N_EXPERTS, :]
    cnt_ref[0] = jnp.sum(sel.astype(f32), axis=0, keepdims=True)


def _router(x, g, wr, tm, n_valid):
    mp = x.shape[0]
    nt = mp // tm
    return pl.pallas_call(
        functools.partial(_router_kernel, tm=tm, n_valid=n_valid),
        grid=(nt,),
        in_specs=[pl.BlockSpec((tm, D_MODEL), lambda i: (i, 0)),
                  pl.BlockSpec((1, D_MODEL), lambda i: (0, 0)),
                  pl.BlockSpec((D_MODEL, LANES), lambda i: (0, 0))],
        out_specs=[pl.BlockSpec((tm, D_MODEL), lambda i: (i, 0)),
                   pl.BlockSpec((tm, LANES), lambda i: (i, 0)),
                   pl.BlockSpec((tm, LANES), lambda i: (i, 0)),
                   pl.BlockSpec((1, N_EXPERTS, tm), lambda i: (i, 0, 0)),
                   pl.BlockSpec((1, 1, LANES), lambda i: (i, 0, 0))],
        out_shape=[jax.ShapeDtypeStruct((mp, D_MODEL), bf16),
                   jax.ShapeDtypeStruct((mp, LANES), f32),
                   jax.ShapeDtypeStruct((mp, LANES), f32),
                   jax.ShapeDtypeStruct((nt, N_EXPERTS, tm), f32),
                   jax.ShapeDtypeStruct((nt, 1, LANES), f32)],
        compiler_params=_cparams(("parallel",)),
        name="router",
    )(x, g, wr)


def _moe_kernel(rounds_ref, hb_ref, rnk_ref, rnkt_ref, comb_ref, wg_ref, wu_ref, wd_ref, o_ref,
                *, cap, n_sub, ts):
    i, e = pl.program_id(0), pl.program_id(1)

    @pl.when(e == 0)
    def _():
        o_ref[...] = jnp.zeros_like(o_ref)

    lane = lax.broadcasted_iota(jnp.int32, (ts, LANES), 1)
    slot_rows = lax.broadcasted_iota(jnp.int32, (cap, ts), 0).astype(f32)
    slot_cols = lax.broadcasted_iota(jnp.int32, (ts, cap), 1).astype(f32)

    def round_body(r, carry):
        base = (r * cap).astype(f32)
        xs = []
        for j in range(n_sub):
            rt = rnkt_ref[j, pl.ds(e, 1), :]
            p = (rt - base == slot_rows).astype(bf16)
            xs.append(jnp.dot(p, hb_ref[j * ts:(j + 1) * ts, :],
                              preferred_element_type=f32).astype(bf16))
        xs = jnp.concatenate(xs, axis=0)
        a = jnp.dot(xs, wg_ref[0], preferred_element_type=f32)
        a = (a * _sigmoid(a)) * jnp.dot(xs, wu_ref[0], preferred_element_type=f32)
        y = jnp.dot(a.astype(bf16), wd_ref[0], preferred_element_type=f32).astype(bf16)
        for j in range(n_sub):
            rows = slice(j * ts, (j + 1) * ts)
            col = jnp.sum(jnp.where(lane == e, rnk_ref[rows, :], 0.0), axis=1, keepdims=True)
            w = jnp.sum(jnp.where(lane == e, comb_ref[rows, :], 0.0), axis=1, keepdims=True)
            pt = (col - base == slot_cols).astype(bf16)
            o_ref[rows, :] += w * jnp.dot(pt, y[j * cap:(j + 1) * cap, :], preferred_element_type=f32)
        return carry

    lax.fori_loop(0, rounds_ref[i * N_EXPERTS + e], round_body, 0)


def _moe(hb, rnk, rnkt, comb, rounds, wg, wu, wd, ts, n_sub, cap):
    mp = hb.shape[0]
    n_e, _, d_ff = wg.shape
    tsup = ts * n_sub
    grid_spec = pltpu.PrefetchScalarGridSpec(
        num_scalar_prefetch=1,
        grid=(mp // tsup, n_e),
        in_specs=[pl.BlockSpec((tsup, D_MODEL), lambda i, e, r: (i, 0)),
                  pl.BlockSpec((tsup, LANES), lambda i, e, r: (i, 0)),
                  pl.BlockSpec((n_sub, N_EXPERTS, ts), lambda i, e, r: (i, 0, 0)),
                  pl.BlockSpec((tsup, LANES), lambda i, e, r: (i, 0)),
                  pl.BlockSpec((1, D_MODEL, d_ff), lambda i, e, r: (e, 0, 0)),
                  pl.BlockSpec((1, D_MODEL, d_ff), lambda i, e, r: (e, 0, 0)),
                  pl.BlockSpec((1, d_ff, D_MODEL), lambda i, e, r: (e, 0, 0))],
        out_specs=pl.BlockSpec((tsup, D_MODEL), lambda i, e, r: (i, 0)),
    )
    return pl.pallas_call(
        functools.partial(_moe_kernel, cap=cap, n_sub=n_sub, ts=ts),
        grid_spec=grid_spec,
        out_shape=jax.ShapeDtypeStruct((mp, D_MODEL), f32),
        compiler_params=_cparams(("parallel", "arbitrary")),
        name="moe",
    )(rounds, hb, rnk, rnkt, comb, wg, wu, wd)


def _ffn_kernel(x_ref, g_ref, wg_ref, wu_ref, wd_ref, o_ref, *, tf):
    x = x_ref[...]
    h = _rms(x, g_ref[...]).astype(bf16)
    acc = x
    for c in range(wg_ref.shape[1] // tf):
        cols = slice(c * tf, (c + 1) * tf)
        a = jnp.dot(h, wg_ref[:, cols], preferred_element_type=f32)
        a = (a * _sigmoid(a)) * jnp.dot(h, wu_ref[:, cols], preferred_element_type=f32)
        acc = acc + jnp.dot(a.astype(bf16), wd_ref[cols, :], preferred_element_type=f32)
    o_ref[...] = acc


def _ffn(x, g, wg, wu, wd, tm, tf):
    mp = x.shape[0]
    d_ff = wg.shape[1]
    return pl.pallas_call(
        functools.partial(_ffn_kernel, tf=tf),
        grid=(mp // tm,),
        in_specs=[pl.BlockSpec((tm, D_MODEL), lambda i: (i, 0)),
                  _resident((1, D_MODEL)),
                  _resident((D_MODEL, d_ff)),
                  _resident((D_MODEL, d_ff)),
                  _resident((d_ff, D_MODEL))],
        out_specs=pl.BlockSpec((tm, D_MODEL), lambda i: (i, 0)),
        out_shape=jax.ShapeDtypeStruct((mp, D_MODEL), f32),
        compiler_params=_cparams(("parallel",)),
        name="ffn",
    )(x, g, wg, wu, wd)


def _final_norm_kernel(x_ref, y_ref, g_ref, op_ref, os_ref, *, n_prompt_tiles, dec):
    i = pl.program_id(0)
    out = _rms(x_ref[...] + y_ref[...], g_ref[...])

    @pl.when(i < n_prompt_tiles)
    def _():
        op_ref[...] = out

    @pl.when(i == n_prompt_tiles)
    def _():
        os_ref[...] = out[0:dec]


def _final_norm(x, y, g, tm, m_prompt, dec):
    n_p = m_prompt // tm
    assert n_p * tm == m_prompt and dec <= tm
    return pl.pallas_call(
        functools.partial(_final_norm_kernel, n_prompt_tiles=n_p, dec=dec),
        grid=(n_p + 1,),
        in_specs=[pl.BlockSpec((tm, D_MODEL), lambda i: (i, 0)),
                  pl.BlockSpec((tm, D_MODEL), lambda i: (i, 0)),
                  pl.BlockSpec((1, D_MODEL), lambda i: (0, 0))],
        out_specs=[pl.BlockSpec((tm, D_MODEL), lambda i: (jnp.minimum(i, n_p - 1), 0)),
                   pl.BlockSpec((dec, D_MODEL), lambda i: (0, 0))],
        out_shape=[jax.ShapeDtypeStruct((m_prompt, D_MODEL), f32),
                   jax.ShapeDtypeStruct((dec, D_MODEL), f32)],
        compiler_params=_cparams(("arbitrary",)),
        name="final_norm",
    )(x, y, g)


O_MI = 3 * ML_W
O_MO = O_MI + 2 * ML_HEADS
O_GLR = O_MO + ML_W + 2 * GLA_KW + GLA_VW
O_GO = O_GLR + GLA_RANK
D_IN = O_GO + GLA_VW + POOL_W + N_BRANCH * D_MODEL


def _regroup_kernel(w_ref, main_ref, small_ref):
    w = w_ref[0]
    main_ref[0, :, 0:O_MI] = w[:, 0:O_MI].astype(bf16)
    main_ref[0, :, O_MI:O_MI + O_GLR - O_MO] = w[:, O_MO:O_GLR].astype(bf16)
    main_ref[0, :, O_MI + O_GLR - O_MO:] = w[:, O_GO:].astype(bf16)
    small_ref[0] = jnp.concatenate(
        [w[:, O_MI:O_MO], w[:, O_GLR:O_GO],
         jnp.zeros((w.shape[0], LANES - 2 * ML_HEADS - GLA_RANK), f32)], axis=1).astype(bf16)


def _regroup_w_in(w_in, tr=256):
    depth, d, n = w_in.shape
    assert n == D_IN and Z_MAIN == D_IN - 2 * ML_HEADS - GLA_RANK
    return pl.pallas_call(
        _regroup_kernel,
        grid=(depth, d // tr),
        in_specs=[pl.BlockSpec((1, tr, n), lambda l, i: (l, i, 0))],
        out_specs=[pl.BlockSpec((1, tr, Z_MAIN), lambda l, i: (l, i, 0)),
                   pl.BlockSpec((1, tr, LANES), lambda l, i: (l, i, 0))],
        out_shape=[jax.ShapeDtypeStruct((depth, d, Z_MAIN), bf16),
                   jax.ShapeDtypeStruct((depth, d, LANES), bf16)],
        compiler_params=_cparams(("parallel", "parallel")),
        name="regroup_w_in",
    )(w_in)


def _layer_weights(l, norm1_g, if_bias, ml_g, gla_gw, gla_gb, gla_g, pool_w, pool_scale,
                   w_branch, w_out):
    ifb = jnp.zeros((1, LANES), f32)
    ifb = ifb.at[0, S_MI:S_MI + ML_HEADS].set(if_bias[l, 0]).at[0, S_MF:S_MF + ML_HEADS].set(if_bias[l, 1])
    gw = jnp.zeros((LANES, GLA_KW), f32).at[S_GLR:S_GLR + GLA_RANK].set(gla_gw[l]).astype(bf16)
    return dict(
        norm1_g=norm1_g[l][None], ifb=ifb, ml_g=ml_g[l][None],
        gla_gw=gw, gla_gb=gla_gb[l][None], gla_g=gla_g[l][None], pool_w=pool_w[l].astype(bf16),
        pool_scale=pool_scale[l][None], w_branch=w_branch[l].astype(bf16), w_out=w_out[l].astype(bf16))


def _forward(x_prompt, x_sample, state_mlstm_C, state_mlstm_n, state_mlstm_m, state_gla_S,
             state_pool_buf, norm1_g, w_in, mlstm_if_bias, mlstm_norm_g, gla_gate_w, gla_gate_b,
             gla_norm_g, pool_w, pool_scale, w_branch, w_out, norm2_g, ffn_wg, ffn_wu, ffn_wd,
             router_w, moe_wg, moe_wu, moe_wd, final_norm_g, *, tm, tn, tt, chunk, sub, nb, sb, tf_dense, moe_sub, moe_cap):
    batch, seq, _ = x_prompt.shape
    dec = x_sample.shape[0]
    depth = w_in.shape[0]
    m_prompt = batch * seq
    m_all = m_prompt + dec
    mp = -(-m_all // tm) * tm
    assert m_prompt % tm == 0
    x_parts = (x_prompt.reshape(m_prompt, D_MODEL),
               jnp.concatenate([x_sample.reshape(dec, D_MODEL), jnp.zeros((mp - m_all, D_MODEL), f32)], axis=0))
    outs = [[] for _ in range(5)]
    moe_out = None
    st = (state_mlstm_C, state_mlstm_n, jnp.pad(state_mlstm_m, ((0, 0), (0, 0), (0, LANES - ML_HEADS))),
          state_gla_S, state_pool_buf)
    st_new = None
    w_main, w_small = _regroup_w_in(w_in)
    for l in range(depth):
        lw = _layer_weights(l, norm1_g, mlstm_if_bias, mlstm_norm_g, gla_gate_w, gla_gate_b,
                            gla_norm_g, pool_w, pool_scale, w_branch, w_out)
        z, zs = _norm_matmul(x_parts, lw["norm1_g"], w_main, w_small, l, tm, tn)
        br_p, c_p, n_p, m_p, s_p, buf_p = _prompt_mix(z, zs, lw, batch, seq, tt, chunk, sub, nb)
        br_t, *st_new = _sample_mix(z, zs, st, st_new, l, lw, m_prompt, dec, sb)
        x = _merge(x_parts, br_p, br_t, z, lw["w_branch"], lw["w_out"], tm, m_all)
        j = l // 2
        if l % 2 == 0:
            x = _ffn(x, norm2_g[l][None], ffn_wg[j].astype(bf16), ffn_wu[j].astype(bf16),
                     ffn_wd[j].astype(bf16), tm, tf_dense)
        else:
            wr_hi = router_w[j].astype(bf16)
            wr_lo = (router_w[j] - wr_hi.astype(f32)).astype(bf16)
            wr = jnp.pad(jnp.concatenate([wr_hi, wr_lo], axis=1), ((0, 0), (0, LANES - 2 * N_EXPERTS)))
            hb, comb, rnk, rnkt, cnt = _router(x, norm2_g[l][None], wr, tm, m_all)
            n_sup = mp // (tm * moe_sub)
            cnt = cnt[:, 0, :N_EXPERTS].reshape(n_sup, moe_sub, N_EXPERTS).max(axis=1)
            rounds = jnp.ceil(cnt / moe_cap).astype(jnp.int32).reshape(-1)
            moe_out = _moe(hb, rnk, rnkt, comb, rounds, moe_wg[j].astype(bf16), moe_wu[j].astype(bf16),
                           moe_wd[j].astype(bf16), tm, moe_sub, moe_cap)
            if l + 1 < depth:
                x = x + moe_out
                moe_out = None
        x_parts = (x,)
        for lst, val in zip(outs, (c_p, n_p, m_p[:, :ML_HEADS, 0], s_p, buf_p)):
            lst.append(val)
    if moe_out is None:
        moe_out = jnp.zeros_like(x)
    y_prompt, y_sample = _final_norm(x, moe_out, final_norm_g[None], tm, m_prompt, dec)
    c_p, n_p, m_p, s_p, buf_p = (jnp.stack(o) for o in outs)
    c_s, n_s, m_s, s_s, buf_s = st_new
    return (y_prompt.reshape(batch, seq, D_MODEL), y_sample.reshape(dec, 1, D_MODEL),
            c_p, c_s, n_p, n_s, m_p, m_s[:, :, :ML_HEADS], s_p, s_s, buf_p, buf_s)


def kernel(x_prompt, x_sample, state_mlstm_C, state_mlstm_n, state_mlstm_m, state_gla_S, state_pool_buf, norm1_g, w_in, mlstm_if_bias, mlstm_norm_g, gla_gate_w, gla_gate_b, gla_norm_g, pool_w, pool_scale, w_branch, w_out, norm2_g, ffn_wg, ffn_wu, ffn_wd, router_w, moe_wg, moe_wu, moe_wd, final_norm_g):
    return _forward(x_prompt, x_sample, state_mlstm_C, state_mlstm_n, state_mlstm_m, state_gla_S,
                    state_pool_buf, norm1_g, w_in, mlstm_if_bias, mlstm_norm_g, gla_gate_w, gla_gate_b,
                    gla_norm_g, pool_w, pool_scale, w_branch, w_out, norm2_g, ffn_wg, ffn_wu, ffn_wd,
                    router_w, moe_wg, moe_wu, moe_wd, final_norm_g,
                    tm=512, tn=1024, tt=512, chunk=128, sub=64, nb=2, sb=16, tf_dense=1408, moe_sub=3, moe_cap=160)
```

```python
import functools

import jax
import jax.numpy as jnp
from jax import lax
from jax.experimental import pallas as pl
from jax.experimental.pallas import tpu as pltpu

f32 = jnp.float32
bf16 = jnp.bfloat16

D_MODEL = 1024
ML_HEADS, ML_DH = 4, 128
ML_W = ML_HEADS * ML_DH
GLA_HEADS, GLA_DK, GLA_DV = 4, 64, 128
GLA_KW, GLA_VW = GLA_HEADS * GLA_DK, GLA_HEADS * GLA_DV
GLA_RANK = 16
GLA_TAU = 16.0
POOL_GDIM = 128
POOL_WINDOWS = (2, 4, 8, 16)
POOL_W = POOL_GDIM * len(POOL_WINDOWS)
POOL_BUF = 15
POOL_BASE = 24
N_BRANCH = 3
N_EXPERTS = 8
EPS = 1e-6
NEG = -1e30
LANES = 128

C_MQ, C_MK, C_MV, C_MO = 0, 512, 1024, 1536
C_GQ, C_GK, C_GV, C_GO, C_PU, C_GATES = 2048, 2304, 2560, 3072, 3584, 4096
Z_MAIN = C_GATES + N_BRANCH * D_MODEL
S_MI, S_MF, S_GLR = 0, 4, 8

VMEM_LIMIT = 56 * 1024 * 1024

_NT = (((1,), (1,)), ((), ()))


def _cparams(sem):
    return pltpu.CompilerParams(dimension_semantics=sem, vmem_limit_bytes=VMEM_LIMIT)


def _log_sigmoid(x):
    return jnp.minimum(x, 0.0) - jnp.log(1.0 + jnp.exp(-jnp.abs(x)))


def _sigmoid(x):
    return 0.5 * jnp.tanh(0.5 * x) + 0.5


def _rms(x, g):
    ms = jnp.mean(x * x, axis=-1, keepdims=True)
    return x * lax.rsqrt(ms + EPS) * g


def _lower_tri(n):
    r = lax.broadcasted_iota(jnp.int32, (n, n), 0)
    c = lax.broadcasted_iota(jnp.int32, (n, n), 1)
    return c <= r


def _cumsum_rows(tri_bf16, a):
    a1 = a.astype(bf16)
    r = a - a1.astype(f32)
    a2 = r.astype(bf16)
    a3 = (r - a2.astype(f32)).astype(bf16)
    d = lambda y: jnp.dot(tri_bf16, y, preferred_element_type=f32)
    return d(a1) + d(a2) + d(a3)


def _cumsum_lanes(triu_bf16, a):
    a1 = a.astype(bf16)
    r = a - a1.astype(f32)
    a2 = r.astype(bf16)
    a3 = (r - a2.astype(f32)).astype(bf16)
    d = lambda y: jnp.dot(y, triu_bf16, preferred_element_type=f32)
    return d(a1) + d(a2) + d(a3)


def _resident(shape):
    nd = len(shape)
    return pl.BlockSpec(shape, lambda *_: (0,) * nd, pipeline_mode=pl.Buffered(1))


def _split_rows_specs(parts, tm, width):
    specs, start = [], 0
    for p in parts:
        n_t = p.shape[0] // tm
        assert n_t * tm == p.shape[0]
        specs.append(pl.BlockSpec(
            (tm, width), functools.partial(lambda i, s, n: (jnp.clip(i - s, 0, n - 1), 0), s=start, n=n_t)))
        start += n_t
    return specs


def _pick_rows(refs, starts):
    i = pl.program_id(0)
    x = refs[0][...]
    for r, s in zip(refs[1:], starts[1:]):
        x = jnp.where(i >= s, r[...], x)
    return x


def _norm_matmul_kernel(*refs, tn, starts):
    x_refs = refs[:len(starts)]
    g_ref, w_ref, ws_ref, z_ref, zs_ref = refs[len(starts):]
    h = _rms(_pick_rows(x_refs, starts), g_ref[...]).astype(bf16)
    zs_ref[...] = jnp.dot(h, ws_ref[0], preferred_element_type=f32)
    for c in range(w_ref.shape[2] // tn):
        cols = slice(c * tn, (c + 1) * tn)
        z_ref[:, cols] = jnp.dot(h, w_ref[0, :, cols], preferred_element_type=f32).astype(z_ref.dtype)


def _norm_matmul(x_parts, g, w_main, w_small, layer, tm, tn):
    mp = sum(p.shape[0] for p in x_parts)
    n = w_main.shape[2]
    starts = tuple(sum(p.shape[0] for p in x_parts[:j]) // tm for j in range(len(x_parts)))
    layer_block = lambda width: pl.BlockSpec((1, D_MODEL, width), lambda i: (layer, 0, 0),
                                             pipeline_mode=pl.Buffered(1))
    return pl.pallas_call(
        functools.partial(_norm_matmul_kernel, tn=tn, starts=starts),
        grid=(mp // tm,),
        in_specs=[
            *_split_rows_specs(x_parts, tm, D_MODEL),
            _resident((1, D_MODEL)),
            layer_block(n),
            layer_block(LANES),
        ],
        out_specs=[
            pl.BlockSpec((tm, n), lambda i: (i, 0)),
            pl.BlockSpec((tm, LANES), lambda i: (i, 0)),
        ],
        out_shape=[jax.ShapeDtypeStruct((mp, n), bf16), jax.ShapeDtypeStruct((mp, LANES), f32)],
        compiler_params=_cparams(("parallel",)),
        name="norm_matmul",
    )(*x_parts, g, w_main, w_small)


def _head_norm(h, g):
    outs = []
    for j in range(h.shape[1] // LANES):
        hj = h[:, j * LANES:(j + 1) * LANES]
        outs.append(hj * lax.rsqrt(jnp.mean(hj * hj, axis=-1, keepdims=True) + EPS))
    return jnp.concatenate(outs, axis=1) * g


def _gla_log_decay(sm, gw_ref, gb_ref):
    xg = jnp.dot(sm.astype(bf16), gw_ref[...], preferred_element_type=f32) + gb_ref[...]
    return _log_sigmoid(xg) * (1.0 / GLA_TAU)


N_ZBLOCKS = 9


def _prompt_mix_kernel(*refs, nb, tt, chunk, sub, n_t):
    z_refs = [refs[s * N_ZBLOCKS:(s + 1) * N_ZBLOCKS] for s in range(nb)]
    ifb_ref, mlg_ref, gw_ref, gb_ref, glag_ref, pw_ref, ps_ref = refs[nb * N_ZBLOCKS:nb * N_ZBLOCKS + 7]
    br_all, c_out, n_out, m_out, s_out, buf_out = refs[nb * N_ZBLOCKS + 7:nb * N_ZBLOCKS + 13]
    scratch = refs[nb * N_ZBLOCKS + 13:]
    per = ML_HEADS + GLA_HEADS + 9
    c_refs = [scratch[s * per:s * per + ML_HEADS] for s in range(nb)]
    s_refs = [scratch[s * per + ML_HEADS:s * per + ML_HEADS + GLA_HEADS] for s in range(nb)]
    n_refs = [scratch[s * per + ML_HEADS + GLA_HEADS] for s in range(nb)]
    m_refs = [scratch[s * per + ML_HEADS + GLA_HEADS + 1] for s in range(nb)]
    ext_refs = [scratch[s * per + ML_HEADS + GLA_HEADS + 2] for s in range(nb)]
    prep_refs = [scratch[s * per + ML_HEADS + GLA_HEADS + 3:s * per + ML_HEADS + GLA_HEADS + 7] for s in range(nb)]
    stage_refs = [scratch[s * per + ML_HEADS + GLA_HEADS + 7:(s + 1) * per] for s in range(nb)]
    t_idx = pl.program_id(1)
    L = chunk

    @pl.when(t_idx == 0)
    def _():
        for s in range(nb):
            for r in (*c_refs[s], *s_refs[s], n_refs[s], m_refs[s]):
                r[...] = jnp.zeros_like(r)
            ext_refs[s][0:POOL_BASE, :] = jnp.zeros((POOL_BASE, POOL_W), f32)
            for r in stage_refs[s]:
                r[0:8, :] = jnp.zeros((8, POOL_GDIM), f32)

    tri = _lower_tri(L)
    tri_b = tri.astype(bf16)
    causal_sub = _lower_tri(sub)
    assert L == LANES
    ones_b = jnp.ones((L, LANES), bf16)
    triu_b = (lax.broadcasted_iota(jnp.int32, (L, L), 0)
              <= lax.broadcasted_iota(jnp.int32, (L, L), 1)).astype(bf16)
    ifb = ifb_ref[...]
    k_scale = ML_DH ** -0.5
    q_scale = GLA_DK ** -0.5

    def gate_prep(c, seq):
        sm_ref = z_refs[seq][8]
        y0_p, bc_p, rows_p, b_p = prep_refs[seq]
        sm = sm_ref[pl.ds(pl.multiple_of(c * L, L), L), :]
        y0 = sm + ifb
        y0_p[...] = y0
        bc_p[...] = _cumsum_rows(tri_b, _log_sigmoid(y0))
        y0t = y0.T[0:8, :]
        rows_p[0:8, :] = y0t
        rows_p[8:16, :] = _cumsum_lanes(triu_b, _log_sigmoid(y0t))
        b_p[...] = _cumsum_rows(tri_b, _gla_log_decay(sm, gw_ref, gb_ref))

    def one_seq(c, seq):
        q_ref, k_ref, v_ref, mo_ref, gqk_ref, gv_ref, go_ref, _, sm_ref = z_refs[seq]
        br_ref, c_s, s_s = br_all.at[seq], c_refs[seq], s_refs[seq]
        n_old, m_old = n_refs[seq][...], m_refs[seq][...]
        n_rows, m_rows = [], []
        r0 = pl.multiple_of(c * L, L)
        rows = pl.ds(r0, L)
        y0_p, bc_p, rows_p, b_p = prep_refs[seq]

        y0, bc = y0_p[...], bc_p[...]
        y0t, bct = rows_p[0:8, :], rows_p[8:16, :]
        b = b_p[...]
        hm = []
        for h in range(ML_HEADS):
            hs = slice(h * ML_DH, (h + 1) * ML_DH)
            v = v_ref[rows, hs].astype(f32)
            kb = k_ref[rows, hs].astype(bf16)
            qb, vb = q_ref[rows, hs].astype(bf16), v.astype(bf16)
            bcol = bc[:, S_MF + h:S_MF + h + 1]
            icol = y0[:, S_MI + h:S_MI + h + 1]
            brow = bct[S_MF + h:S_MF + h + 1, :]
            irow = y0t[S_MI + h:S_MI + h + 1, :]
            m_row = m_old[h:h + 1, :]
            cmat = c_s[h][...]
            nrow = n_old[h:h + 1, :]
            bcol_r = jnp.broadcast_to(bcol, (L, LANES))
            dm = jnp.where(tri, bcol_r - brow + irow, NEG)
            inter_r = bcol_r + m_row
            m_t_r = jnp.maximum(inter_r, jnp.broadcast_to(jnp.max(dm, axis=1, keepdims=True), (L, LANES)))
            w_intra = jnp.exp(dm - m_t_r) * k_scale
            w_inter_r = jnp.exp(inter_r - m_t_r)
            s = lax.dot_general(qb, kb, _NT, preferred_element_type=f32) * w_intra
            s_hi = s.astype(bf16)
            s_lo = (s - s_hi.astype(f32)).astype(bf16)
            r_intra = jnp.dot(s_hi, jnp.concatenate([vb, ones_b], axis=1), preferred_element_type=f32)
            c_aug = jnp.concatenate([cmat, jnp.broadcast_to(nrow, (LANES, LANES))], axis=0).astype(bf16)
            r_inter = lax.dot_general(qb, c_aug, _NT, preferred_element_type=f32)
            rs_lo = jnp.dot(s_lo, ones_b, preferred_element_type=f32)
            num = r_intra[:, 0:LANES] + w_inter_r * r_inter[:, 0:LANES]
            den_r = r_intra[:, LANES:] + rs_lo + w_inter_r * r_inter[:, LANES:]
            hm.append(num / jnp.maximum(jnp.abs(den_r), jnp.exp(-m_t_r)))
            b_last = bcol[L - 1:L, :]
            m = m_row[:, 0:1]
            g = b_last - bcol + icol
            m_new = jnp.maximum(b_last + m, jnp.max(g, axis=0, keepdims=True))
            w_s = jnp.broadcast_to(jnp.exp(g - m_new) * k_scale, (L, LANES))
            w_c = jnp.exp(b_last + m - m_new)
            vwt = (v * w_s).T.astype(bf16)
            c_s[h][...] = w_c * cmat + jnp.dot(vwt, kb, preferred_element_type=f32)
            n_rows.append(w_c * nrow + jnp.sum(w_s * k_ref[rows, hs].astype(f32), axis=0, keepdims=True))
            m_rows.append(jnp.broadcast_to(m_new, (1, LANES)))
        pad_rows = [jnp.zeros((8 - ML_HEADS, LANES), f32)]
        n_refs[seq][...] = jnp.concatenate(n_rows + pad_rows, axis=0)
        m_refs[seq][...] = jnp.concatenate(m_rows + pad_rows, axis=0)
        y_ml = (_head_norm(jnp.concatenate(hm, axis=1), mlg_ref[...])
                * _sigmoid(mo_ref[rows, :].astype(f32)))
        br_ref[rows, 0:ML_W] = y_ml.astype(br_ref.dtype)

        q2 = gqk_ref[rows, 0:GLA_KW].astype(f32) * q_scale
        k2 = gqk_ref[rows, GLA_KW:2 * GLA_KW].astype(f32)
        gv = gv_ref[rows, :].astype(bf16)
        s_old = [s_s[h][...] for h in range(GLA_HEADS)]
        qe_chunk = (q2 * jnp.exp(b)).astype(bf16)
        b_last = b[L - 1:L, :]
        kdt = (k2 * jnp.exp(b_last - b)).T.astype(bf16)
        decay_col = jnp.exp(b.T[:, L - 1:L])
        o_blocks = []
        for blk in range(L // sub):
            s0, s1 = blk * sub, (blk + 1) * sub
            mid = s0 + sub // 2
            b_blk = b[s0:s1]
            b_mid = b[mid - 1:mid, :]
            qe_d = (q2[s0:s1] * jnp.exp(b_blk - b_mid)).astype(bf16)
            ke_d = (k2[s0:s1] * jnp.exp(b_mid - b_blk)).astype(bf16)
            if blk > 0:
                b_start = b[s0 - 1:s0, :]
                qe_o = (q2[s0:s1] * jnp.exp(b_blk - b_start)).astype(bf16)
                ke_o = (k2[0:s0] * jnp.exp(b_start - b[0:s0])).astype(bf16)
            o_heads = []
            for h in range(GLA_HEADS):
                ks = slice(h * GLA_DK, (h + 1) * GLA_DK)
                vs = slice(h * GLA_DV, (h + 1) * GLA_DV)
                a = lax.dot_general(qe_d[:, ks], ke_d[:, ks], _NT, preferred_element_type=f32)
                a = jnp.where(causal_sub, a, 0.0)
                o = (jnp.dot(a.astype(bf16), gv[s0:s1, vs], preferred_element_type=f32)
                     + jnp.dot(qe_chunk[s0:s1, ks], s_old[h].astype(bf16), preferred_element_type=f32))
                if blk > 0:
                    a = lax.dot_general(qe_o[:, ks], ke_o[:, ks], _NT, preferred_element_type=f32)
                    o = o + jnp.dot(a.astype(bf16), gv[0:s0, vs], preferred_element_type=f32)
                o_heads.append(o)
            o_blocks.append(jnp.concatenate(o_heads, axis=1))
        og = jnp.concatenate(o_blocks, axis=0)
        gate_prep(jnp.minimum(c + 1, tt // L - 1), seq)
        for h in range(GLA_HEADS):
            ks = slice(h * GLA_DK, (h + 1) * GLA_DK)
            vs = slice(h * GLA_DV, (h + 1) * GLA_DV)
            s_s[h][...] = (decay_col[ks, :] * s_old[h]
                           + jnp.dot(kdt[ks, :], gv[:, vs], preferred_element_type=f32))
        go = go_ref[rows, :].astype(f32)
        y_gla = _head_norm(og, glag_ref[...]) * (go * _sigmoid(go))
        br_ref[rows, ML_W:ML_W + GLA_VW] = y_gla.astype(br_ref.dtype)

    def chunk_body(c, carry):
        for s in range(nb):
            one_seq(c, s)
        return carry

    for s in range(nb):
        gate_prep(0, s)
    lax.fori_loop(0, tt // L, chunk_body, 0)

    pos = t_idx * tt + lax.broadcasted_iota(jnp.int32, (tt, 1), 0)
    for s in range(nb):
        pu_ref, br_ref, ext_s = z_refs[s][7], br_all.at[s], ext_refs[s]
        stage = stage_refs[s]
        n_ext = tt + POOL_BASE
        ext_s[POOL_BASE:n_ext, :] = pu_ref[...].astype(f32)
        for g, w in enumerate(POOL_WINDOWS):
            gs = slice(g * POOL_GDIM, (g + 1) * POOL_GDIM)
            u = ext_s[POOL_BASE:n_ext, gs]
            acc = u
            for j in range(1, w):
                acc = acc + ext_s[POOL_BASE - j:n_ext - j, gs]
            cnt = jnp.minimum(pos + 1, w).astype(f32)
            d = acc / cnt - u
            y = jnp.dot(d.astype(bf16), pw_ref[g], preferred_element_type=f32) * ps_ref[:, gs]
            br_ref[:, ML_W + GLA_VW + g * POOL_GDIM:ML_W + GLA_VW + (g + 1) * POOL_GDIM] = y.astype(br_ref.dtype)
        ext_s[8:POOL_BASE, :] = ext_s[tt + 8:n_ext, :]

    @pl.when(t_idx == n_t - 1)
    def _():
        for s in range(nb):
            for h in range(ML_HEADS):
                c_out[s, h] = c_refs[s][h][...]
            for h in range(GLA_HEADS):
                s_out[s, h] = s_refs[s][h][...]
            n_out[s] = n_refs[s][0:ML_HEADS, :]
            m_out[s] = m_refs[s][...]
            buf_out[s] = ext_refs[s][POOL_BASE - POOL_BUF:POOL_BASE, :]


def _prompt_mix(z, zs, lw, batch, seq, tt, chunk, sub, nb):
    n_t = seq // tt
    assert batch % nb == 0

    def zspecs(s):
        row = lambda b, t: (b * nb + s) * n_t + t
        spec = lambda col, width: pl.BlockSpec((tt, width), lambda b, t: (row(b, t), col // width))
        return [spec(C_MQ, 512), spec(C_MK, 512), spec(C_MV, 512), spec(C_MO, 512),
                spec(C_GQ, 512), spec(C_GV, 512), spec(C_GO, 512), spec(C_PU, 512),
                pl.BlockSpec((tt, LANES), lambda b, t: (row(b, t), 0))]

    def full(shape):
        nd = len(shape)
        return pl.BlockSpec(shape, lambda b, t: (0,) * nd)

    def per_seq(shape):
        nd = len(shape)
        return pl.BlockSpec((nb,) + shape, lambda b, t: (b,) + (0,) * nd)

    kern = functools.partial(_prompt_mix_kernel, nb=nb, tt=tt, chunk=chunk, sub=sub, n_t=n_t)
    outs = pl.pallas_call(
        kern,
        grid=(batch // nb, n_t),
        in_specs=[sp for s in range(nb) for sp in zspecs(s)]
                 + [full((1, LANES)), full((1, ML_W)), full((LANES, GLA_KW)), full((1, GLA_KW)),
                    full((1, GLA_VW)), full((len(POOL_WINDOWS), POOL_GDIM, POOL_GDIM)), full((1, POOL_W))],
        out_specs=[
            pl.BlockSpec((nb, tt, N_BRANCH * ML_W), lambda b, t: (b, t, 0)),
            per_seq((ML_HEADS, ML_DH, ML_DH)), per_seq((ML_HEADS, ML_DH)), per_seq((8, LANES)),
            per_seq((GLA_HEADS, GLA_DK, GLA_DV)), per_seq((POOL_BUF, POOL_W)),
        ],
        out_shape=[
            jax.ShapeDtypeStruct((batch, seq, N_BRANCH * ML_W), bf16),
            jax.ShapeDtypeStruct((batch, ML_HEADS, ML_DH, ML_DH), f32),
            jax.ShapeDtypeStruct((batch, ML_HEADS, ML_DH), f32),
            jax.ShapeDtypeStruct((batch, 8, LANES), f32),
            jax.ShapeDtypeStruct((batch, GLA_HEADS, GLA_DK, GLA_DV), f32),
            jax.ShapeDtypeStruct((batch, POOL_BUF, POOL_W), f32),
        ],
        scratch_shapes=([pltpu.VMEM((ML_DH, ML_DH), f32)] * ML_HEADS
                        + [pltpu.VMEM((GLA_DK, GLA_DV), f32)] * GLA_HEADS
                        + [pltpu.VMEM((8, LANES), f32), pltpu.VMEM((8, LANES), f32),
                           pltpu.VMEM((tt + POOL_BASE, POOL_W), f32),
                           pltpu.VMEM((chunk, LANES), f32), pltpu.VMEM((chunk, LANES), f32),
                           pltpu.VMEM((16, chunk), f32), pltpu.VMEM((chunk, GLA_KW), f32),
                           pltpu.VMEM((tt + POOL_BASE, POOL_GDIM), f32),
                           pltpu.VMEM((tt + POOL_BASE, POOL_GDIM), f32)]) * nb,
        compiler_params=_cparams(("parallel", "arbitrary")),
        name="prompt_mix",
    )(*([z] * 8 + [zs]) * nb,
      lw["ifb"], lw["ml_g"], lw["gla_gw"], lw["gla_gb"], lw["gla_g"], lw["pool_w"], lw["pool_scale"])
    return (outs[0].reshape(batch * seq, N_BRANCH * ML_W),) + tuple(outs[1:])


def _sample_mix_kernel(q_ref, k_ref, v_ref, mo_ref, gqk_ref, gv_ref, go_ref, pu_ref, sm_ref,
                       c_in, n_in, m_in, s_in, buf_in,
                       ifb_ref, mlg_ref, gw_ref, gb_ref, glag_ref, pw_ref, ps_ref, *tail, sb, dec):
    br_all = tail[-6]
    c_out, n_out, m_out, s_out, buf_out = (r.at[0] for r in tail[-5:])
    step = pl.program_id(0)
    c_in, n_in, m_in, s_in, buf_in = (r.at[0] for r in (c_in, n_in, m_in, s_in, buf_in))
    br_ref = br_all.at[pl.ds(pl.multiple_of(step * sb, sb), sb), :]

    @pl.when(step == 0)
    def _():
        br_all[dec:, :] = jnp.zeros((br_all.shape[0] - dec, br_all.shape[1]), br_all.dtype)

    sm = sm_ref[...]
    eye = (lax.broadcasted_iota(jnp.int32, (LANES, LANES), 0)
           == lax.broadcasted_iota(jnp.int32, (LANES, LANES), 1)).astype(f32)

    def to_cols(x):
        parts = [lax.dot_general(eye, x[:, j * LANES:(j + 1) * LANES], _NT,
                                 preferred_element_type=f32, precision=lax.Precision.HIGHEST)
                 for j in range(x.shape[1] // LANES)]
        return jnp.concatenate(parts, axis=0)

    y0 = sm + ifb_ref[...]
    logf_all = _log_sigmoid(y0)
    k_scale = ML_DH ** -0.5
    m_all = m_in[...]
    hm, m_new_cols = [], []
    for h in range(ML_HEADS):
        hs = slice(h * ML_DH, (h + 1) * ML_DH)
        q = q_ref[:, hs].astype(f32)
        k = k_ref[:, hs].astype(f32) * k_scale
        v = v_ref[:, hs].astype(f32)
        i_pre = y0[:, S_MI + h:S_MI + h + 1]
        logf = logf_all[:, S_MF + h:S_MF + h + 1]
        m = m_all[:, h:h + 1]
        inter = logf + m
        m_t = jnp.maximum(inter, i_pre)
        w_intra = jnp.exp(i_pre - m_t)
        w_inter = jnp.exp(inter - m_t)
        s = jnp.sum(q * k, axis=1, keepdims=True) * w_intra
        qb = q.astype(bf16)
        n_h = n_in[:, h, :]
        cq = jnp.concatenate(
            [lax.dot_general(qb, c_in[j, h].astype(bf16), _NT, preferred_element_type=f32)[j:j + 1, :]
             for j in range(sb)], axis=0)
        num = s * v + w_inter * cq
        den = s + w_inter * jnp.sum(n_h * q, axis=1, keepdims=True)
        hm.append(num / jnp.maximum(jnp.abs(den), jnp.exp(-m_t)))
        m_new = m_t
        w_s = w_intra
        w_c = w_inter
        n_out[:, h, :] = w_c * n_h + w_s * k
        m_new_cols.append(m_new)
        vw_cols = to_cols(v * w_s)
        for j in range(sb):
            c_out[j, h] = w_c[j:j + 1, :] * c_in[j, h] + vw_cols[:, j:j + 1] * k[j:j + 1, :]
    lane = lax.broadcasted_iota(jnp.int32, (sb, LANES), 1)
    m_pack = jnp.zeros((sb, LANES), f32)
    for h in range(ML_HEADS):
        m_pack = jnp.where(lane == h, m_new_cols[h], m_pack)
    m_out[...] = m_pack
    y_ml = _head_norm(jnp.concatenate(hm, axis=1), mlg_ref[...]) * _sigmoid(mo_ref[...].astype(f32))
    br_ref[:, 0:ML_W] = y_ml.astype(br_ref.dtype)

    log_a = _gla_log_decay(sm, gw_ref, gb_ref)
    decay = jnp.exp(log_a)
    q2 = gqk_ref[:, 0:GLA_KW].astype(f32) * (GLA_DK ** -0.5)
    k2 = gqk_ref[:, GLA_KW:2 * GLA_KW].astype(f32)
    gv = gv_ref[...].astype(f32)
    qe = (q2 * decay).astype(bf16)
    qk = q2 * k2
    k_cols = to_cols(k2)
    decay_cols = to_cols(decay)
    og = []
    for h in range(GLA_HEADS):
        ks = slice(h * GLA_DK, (h + 1) * GLA_DK)
        vs = slice(h * GLA_DV, (h + 1) * GLA_DV)
        a = jnp.sum(qk[:, ks], axis=1, keepdims=True)
        inter = jnp.concatenate(
            [jnp.dot(qe[:, ks], s_in[j, h].astype(bf16), preferred_element_type=f32)[j:j + 1, :]
             for j in range(sb)], axis=0)
        og.append(a * gv[:, vs] + inter)
        for j in range(sb):
            s_out[j, h] = (decay_cols[ks, j:j + 1] * s_in[j, h]
                           + k_cols[ks, j:j + 1] * gv[j:j + 1, vs])
    go = go_ref[...].astype(f32)
    y_gla = _head_norm(jnp.concatenate(og, axis=1), glag_ref[...]) * (go * _sigmoid(go))
    br_ref[:, ML_W:ML_W + GLA_VW] = y_gla.astype(br_ref.dtype)

    u = pu_ref[...].astype(f32)
    rowi = lax.broadcasted_iota(jnp.int32, (POOL_BUF + 1, POOL_GDIM), 0)
    d_rows = []
    for j in range(sb):
        ext = jnp.concatenate([buf_in[j], u[j:j + 1, :]], axis=0)
        buf_out[j] = ext[1:POOL_BUF + 1, :]
        parts = []
        for g, w in enumerate(POOL_WINDOWS):
            gs = slice(g * POOL_GDIM, (g + 1) * POOL_GDIM)
            win = jnp.sum(jnp.where(rowi >= POOL_BUF + 1 - w, ext[:, gs], 0.0), axis=0, keepdims=True)
            parts.append(win / float(w) - u[j:j + 1, gs])
        d_rows.append(jnp.concatenate(parts, axis=1))
    d = jnp.concatenate(d_rows, axis=0)
    for g in range(len(POOL_WINDOWS)):
        gs = slice(g * POOL_GDIM, (g + 1) * POOL_GDIM)
        y = jnp.dot(d[:, gs].astype(bf16), pw_ref[g], preferred_element_type=f32) * ps_ref[:, gs]
        br_ref[:, ML_W + GLA_VW + g * POOL_GDIM:ML_W + GLA_VW + (g + 1) * POOL_GDIM] = y.astype(br_ref.dtype)


def _sample_mix(z, zs, st, prev, layer, lw, row0, dec, sb):
    depth = st[0].shape[0]
    base = row0 // sb
    tail_rows = z.shape[0] - row0

    def zspec(col, width):
        blk = col // width
        return pl.BlockSpec((sb, width), lambda i: (base + i, blk))

    def full(shape):
        nd = len(shape)
        return pl.BlockSpec(shape, lambda i: (0,) * nd)

    st_shapes = [(ML_HEADS, ML_DH, ML_DH), (ML_HEADS, ML_DH), (LANES,),
                 (GLA_HEADS, GLA_DK, GLA_DV), (POOL_BUF, POOL_W)]
    st_specs = [pl.BlockSpec((1, sb) + s, functools.partial(lambda i, nd: (layer, i) + (0,) * nd, nd=len(s)))
                for s in st_shapes]
    n_in = N_ZBLOCKS + len(st_shapes) + 7
    prev = () if prev is None else tuple(prev)
    return pl.pallas_call(
        functools.partial(_sample_mix_kernel, sb=sb, dec=dec),
        grid=(dec // sb,),
        in_specs=[zspec(C_MQ, 512), zspec(C_MK, 512), zspec(C_MV, 512), zspec(C_MO, 512),
                  zspec(C_GQ, 512), zspec(C_GV, 512), zspec(C_GO, 512), zspec(C_PU, 512),
                  pl.BlockSpec((sb, LANES), lambda i: (base + i, 0))]
                 + st_specs
                 + [full((1, LANES)), full((1, ML_W)), full((LANES, GLA_KW)), full((1, GLA_KW)),
                    full((1, GLA_VW)), full((len(POOL_WINDOWS), POOL_GDIM, POOL_GDIM)), full((1, POOL_W))]
                 + [pl.BlockSpec(memory_space=pl.ANY)] * len(prev),
        out_specs=[pl.BlockSpec((tail_rows, N_BRANCH * ML_W), lambda i: (0, 0))] + st_specs,
        out_shape=[jax.ShapeDtypeStruct((tail_rows, N_BRANCH * ML_W), bf16)]
                  + [jax.ShapeDtypeStruct((depth, dec) + s, f32) for s in st_shapes],
        input_output_aliases={n_in + j: 1 + j for j in range(len(prev))},
        compiler_params=_cparams(("arbitrary",)),
        name="sample_mix",
    )(z, z, z, z, z, z, z, z, zs, *st,
      lw["ifb"], lw["ml_g"], lw["gla_gw"], lw["gla_gb"], lw["gla_g"], lw["pool_w"], lw["pool_scale"], *prev)


def _merge_kernel(*refs, tm, n_valid, n_prompt_tiles, x_starts, route):
    x_refs = refs[:len(x_starts)]
    brp_ref, brt_ref, g0_ref, g1_ref, g2_ref, wb_ref, wo_ref = refs[len(x_starts):len(x_starts) + 7]
    route_in = refs[len(x_starts) + 7:len(x_starts) + 9] if route else ()
    o_ref = refs[len(x_starts) + 7 + len(route_in)]
    route_out = refs[len(x_starts) + 8 + len(route_in):]
    in_tail = pl.program_id(0) >= n_prompt_tiles
    mixed = None
    for n, g_ref in enumerate((g0_ref, g1_ref, g2_ref)):
        cols = slice(n * ML_W, (n + 1) * ML_W)
        br = jnp.where(in_tail, brt_ref[:, cols], brp_ref[:, cols])
        proj = jnp.dot(br, wb_ref[n], preferred_element_type=f32)
        term = _sigmoid(g_ref[...].astype(f32)) * proj
        mixed = term if mixed is None else mixed + term
    out = (_pick_rows(x_refs, x_starts)
           + jnp.dot(mixed.astype(bf16), wo_ref[...], preferred_element_type=f32))
    row = pl.program_id(0) * tm + lax.broadcasted_iota(jnp.int32, (tm, 1), 0)
    out = jnp.where(row < n_valid, out, 0.0)
    o_ref[...] = out
    if route:
        _route(out, *route_in, *route_out, tm=tm, n_valid=n_valid)


def _merge(x_parts, br_prompt, br_tail, z, w_branch, w_out, tm, n_valid, route_w=None):
    mp = sum(p.shape[0] for p in x_parts)
    nt = mp // tm
    route = route_w is not None
    route_specs = [_resident((1, D_MODEL)), _resident((D_MODEL, LANES))] if route else []
    out_specs = [pl.BlockSpec((tm, D_MODEL), lambda i: (i, 0))]
    out_shape = [jax.ShapeDtypeStruct((mp, D_MODEL), f32)]
    if route:
        out_specs += [pl.BlockSpec((tm, D_MODEL), lambda i: (i, 0)),
                      pl.BlockSpec((tm, LANES), lambda i: (i, 0)),
                      pl.BlockSpec((tm, LANES), lambda i: (i, 0)),
                      pl.BlockSpec((1, N_EXPERTS, tm), lambda i: (i, 0, 0)),
                      pl.BlockSpec((1, 1, LANES), lambda i: (i, 0, 0))]
        out_shape += [jax.ShapeDtypeStruct((mp, D_MODEL), bf16),
                      jax.ShapeDtypeStruct((mp, LANES), f32),
                      jax.ShapeDtypeStruct((mp, LANES), f32),
                      jax.ShapeDtypeStruct((nt, N_EXPERTS, tm), f32),
                      jax.ShapeDtypeStruct((nt, 1, LANES), f32)]
    x_starts = tuple(sum(p.shape[0] for p in x_parts[:j]) // tm for j in range(len(x_parts)))
    n_p = br_prompt.shape[0] // tm
    gate_specs = [pl.BlockSpec((tm, D_MODEL), functools.partial(lambda i, n: (i, C_GATES // D_MODEL + n), n=n))
                  for n in range(N_BRANCH)]
    outs = pl.pallas_call(
        functools.partial(_merge_kernel, tm=tm, n_valid=n_valid, n_prompt_tiles=n_p, x_starts=x_starts,
                          route=route),
        grid=(nt,),
        in_specs=[
            *_split_rows_specs(x_parts, tm, D_MODEL),
            pl.BlockSpec((tm, N_BRANCH * ML_W), lambda i: (jnp.minimum(i, n_p - 1), 0)),
            pl.BlockSpec((tm, N_BRANCH * ML_W), lambda i: (jnp.maximum(i - n_p, 0), 0)),
            *gate_specs,
            _resident((N_BRANCH, ML_W, D_MODEL)),
            _resident((D_MODEL, D_MODEL)),
            *route_specs,
        ],
        out_specs=out_specs,
        out_shape=out_shape,
        compiler_params=_cparams(("parallel",)),
        name="merge_route" if route else "merge",
    )(*x_parts, br_prompt, br_tail, z, z, z, w_branch, w_out, *(route_w or ()))
    return outs if route else outs[0]


def _route(x, g_ref, wr_ref, hb_ref, comb_ref, rnk_ref, rnkt_ref, cnt_ref, *, tm, n_valid):
    h = _rms(x, g_ref[...])
    h_hi = h.astype(bf16)
    hb_ref[...] = h_hi
    h_lo = (h - h_hi.astype(f32)).astype(bf16)
    p_hi = jnp.dot(h_hi, wr_ref[...], preferred_element_type=f32)
    p_lo = jnp.dot(h_lo, wr_ref[...], preferred_element_type=f32)
    logits = p_hi + p_lo + pltpu.roll(p_hi, LANES - N_EXPERTS, 1)
    lane = lax.broadcasted_iota(jnp.int32, logits.shape, 1)
    valid = lane < N_EXPERTS
    logits = jnp.where(valid, logits, NEG)
    mx = jnp.max(logits, axis=1, keepdims=True)
    e = jnp.where(valid, jnp.exp(logits - mx), 0.0)
    probs = e / jnp.sum(e, axis=1, keepdims=True)
    p1 = jnp.max(probs, axis=1, keepdims=True)
    i1 = jnp.min(jnp.where(probs == p1, lane, LANES), axis=1, keepdims=True)
    rest = jnp.where((lane == i1) | ~valid, -1.0, probs)
    p2 = jnp.max(rest, axis=1, keepdims=True)
    i2 = jnp.min(jnp.where(rest == p2, lane, LANES), axis=1, keepdims=True)
    tot = p1 + p2
    comb_ref[...] = jnp.where(lane == i1, p1 / tot, 0.0) + jnp.where(lane == i2, p2 / tot, 0.0)
    row = pl.program_id(0) * tm + lax.broadcasted_iota(jnp.int32, logits.shape, 0)
    sel = ((lane == i1) | (lane == i2)) & (row < n_valid)
    r = lax.broadcasted_iota(jnp.int32, (tm, tm), 0)
    c = lax.broadcasted_iota(jnp.int32, (tm, tm), 1)
    rank = jnp.dot((c < r).astype(bf16), sel.astype(bf16), preferred_element_type=f32)
    rnk = jnp.where(sel, rank, -1.0)
    rnk_ref[...] = rnk
    rnkt_ref[0] = rnk.T[0:N_EXPERTS, :]
    cnt_ref[0] = jnp.sum(sel.astype(f32), axis=0, keepdims=True)


def _moe_kernel(rounds_ref, hb_ref, rnk_ref, rnkt_ref, comb_ref, wg_ref, wu_ref, wd_ref, o_ref,
                *, cap, n_sub, ts):
    i, e = pl.program_id(0), pl.program_id(1)

    @pl.when(e == 0)
    def _():
        o_ref[...] = jnp.zeros_like(o_ref)

    lane = lax.broadcasted_iota(jnp.int32, (ts, LANES), 1)
    slot_rows = lax.broadcasted_iota(jnp.int32, (cap, ts), 0).astype(f32)
    slot_cols = lax.broadcasted_iota(jnp.int32, (ts, cap), 1).astype(f32)

    def round_body(r, carry):
        base = (r * cap).astype(f32)
        xs = []
        for j in range(n_sub):
            rt = rnkt_ref[j, pl.ds(e, 1), :]
            p = (rt - base == slot_rows).astype(bf16)
            xs.append(jnp.dot(p, hb_ref[j * ts:(j + 1) * ts, :],
                              preferred_element_type=f32).astype(bf16))
        xs = jnp.concatenate(xs, axis=0)
        a = jnp.dot(xs, wg_ref[0], preferred_element_type=f32)
        a = (a * _sigmoid(a)) * jnp.dot(xs, wu_ref[0], preferred_element_type=f32)
        y = jnp.dot(a.astype(bf16), wd_ref[0], preferred_element_type=f32).astype(bf16)
        for j in range(n_sub):
            rows = slice(j * ts, (j + 1) * ts)
            col = jnp.sum(jnp.where(lane == e, rnk_ref[rows, :], 0.0), axis=1, keepdims=True)
            w = jnp.sum(jnp.where(lane == e, comb_ref[rows, :], 0.0), axis=1, keepdims=True)
            pt = (col - base == slot_cols).astype(bf16)
            o_ref[rows, :] += w * jnp.dot(pt, y[j * cap:(j + 1) * cap, :], preferred_element_type=f32)
        return carry

    lax.fori_loop(0, rounds_ref[i * N_EXPERTS + e], round_body, 0)


def _moe(hb, rnk, rnkt, comb, rounds, wg, wu, wd, ts, n_sub, cap):
    mp = hb.shape[0]
    n_e, _, d_ff = wg.shape
    tsup = ts * n_sub
    grid_spec = pltpu.PrefetchScalarGridSpec(
        num_scalar_prefetch=1,
        grid=(mp // tsup, n_e),
        in_specs=[pl.BlockSpec((tsup, D_MODEL), lambda i, e, r: (i, 0)),
                  pl.BlockSpec((tsup, LANES), lambda i, e, r: (i, 0)),
                  pl.BlockSpec((n_sub, N_EXPERTS, ts), lambda i, e, r: (i, 0, 0)),
                  pl.BlockSpec((tsup, LANES), lambda i, e, r: (i, 0)),
                  pl.BlockSpec((1, D_MODEL, d_ff), lambda i, e, r: (e, 0, 0)),
                  pl.BlockSpec((1, D_MODEL, d_ff), lambda i, e, r: (e, 0, 0)),
                  pl.BlockSpec((1, d_ff, D_MODEL), lambda i, e, r: (e, 0, 0))],
        out_specs=pl.BlockSpec((tsup, D_MODEL), lambda i, e, r: (i, 0)),
    )
    return pl.pallas_call(
        functools.partial(_moe_kernel, cap=cap, n_sub=n_sub, ts=ts),
        grid_spec=grid_spec,
        out_shape=jax.ShapeDtypeStruct((mp, D_MODEL), f32),
        compiler_params=_cparams(("parallel", "arbitrary")),
        name="moe",
    )(rounds, hb, rnk, rnkt, comb, wg, wu, wd)


def _ffn_kernel(x_ref, g_ref, wg_ref, wu_ref, wd_ref, o_ref, *, tf):
    x = x_ref[...]
    h = _rms(x, g_ref[...]).astype(bf16)
    acc = x
    for c in range(wg_ref.shape[1] // tf):
        cols = slice(c * tf, (c + 1) * tf)
        a = jnp.dot(h, wg_ref[:, cols], preferred_element_type=f32)
        a = (a * _sigmoid(a)) * jnp.dot(h, wu_ref[:, cols], preferred_element_type=f32)
        acc = acc + jnp.dot(a.astype(bf16), wd_ref[cols, :], preferred_element_type=f32)
    o_ref[...] = acc


def _ffn(x, g, wg, wu, wd, tm, tf):
    mp = x.shape[0]
    d_ff = wg.shape[1]
    return pl.pallas_call(
        functools.partial(_ffn_kernel, tf=tf),
        grid=(mp // tm,),
        in_specs=[pl.BlockSpec((tm, D_MODEL), lambda i: (i, 0)),
                  _resident((1, D_MODEL)),
                  _resident((D_MODEL, d_ff)),
                  _resident((D_MODEL, d_ff)),
                  _resident((d_ff, D_MODEL))],
        out_specs=pl.BlockSpec((tm, D_MODEL), lambda i: (i, 0)),
        out_shape=jax.ShapeDtypeStruct((mp, D_MODEL), f32),
        compiler_params=_cparams(("parallel",)),
        name="ffn",
    )(x, g, wg, wu, wd)


def _final_norm_kernel(x_ref, y_ref, g_ref, op_ref, os_ref, *, n_prompt_tiles, dec):
    i = pl.program_id(0)
    out = _rms(x_ref[...] + y_ref[...], g_ref[...])

    @pl.when(i < n_prompt_tiles)
    def _():
        op_ref[...] = out

    @pl.when(i == n_prompt_tiles)
    def _():
        os_ref[...] = out[0:dec]


def _final_norm(x, y, g, tm, m_prompt, dec):
    n_p = m_prompt // tm
    assert n_p * tm == m_prompt and dec <= tm
    return pl.pallas_call(
        functools.partial(_final_norm_kernel, n_prompt_tiles=n_p, dec=dec),
        grid=(n_p + 1,),
        in_specs=[pl.BlockSpec((tm, D_MODEL), lambda i: (i, 0)),
                  pl.BlockSpec((tm, D_MODEL), lambda i: (i, 0)),
                  pl.BlockSpec((1, D_MODEL), lambda i: (0, 0))],
        out_specs=[pl.BlockSpec((tm, D_MODEL), lambda i: (jnp.minimum(i, n_p - 1), 0)),
                   pl.BlockSpec((dec, D_MODEL), lambda i: (0, 0))],
        out_shape=[jax.ShapeDtypeStruct((m_prompt, D_MODEL), f32),
                   jax.ShapeDtypeStruct((dec, D_MODEL), f32)],
        compiler_params=_cparams(("arbitrary",)),
        name="final_norm",
    )(x, y, g)


O_MI = 3 * ML_W
O_MO = O_MI + 2 * ML_HEADS
O_GLR = O_MO + ML_W + 2 * GLA_KW + GLA_VW
O_GO = O_GLR + GLA_RANK
D_IN = O_GO + GLA_VW + POOL_W + N_BRANCH * D_MODEL


def _regroup_kernel(w_ref, main_ref, small_ref):
    w = w_ref[0]
    main_ref[0, :, 0:O_MI] = w[:, 0:O_MI].astype(bf16)
    main_ref[0, :, O_MI:O_MI + O_GLR - O_MO] = w[:, O_MO:O_GLR].astype(bf16)
    main_ref[0, :, O_MI + O_GLR - O_MO:] = w[:, O_GO:].astype(bf16)
    small_ref[0] = jnp.concatenate(
        [w[:, O_MI:O_MO], w[:, O_GLR:O_GO],
         jnp.zeros((w.shape[0], LANES - 2 * ML_HEADS - GLA_RANK), f32)], axis=1).astype(bf16)


def _regroup_w_in(w_in, tr=256):
    depth, d, n = w_in.shape
    assert n == D_IN and Z_MAIN == D_IN - 2 * ML_HEADS - GLA_RANK
    return pl.pallas_call(
        _regroup_kernel,
        grid=(depth, d // tr),
        in_specs=[pl.BlockSpec((1, tr, n), lambda l, i: (l, i, 0))],
        out_specs=[pl.BlockSpec((1, tr, Z_MAIN), lambda l, i: (l, i, 0)),
                   pl.BlockSpec((1, tr, LANES), lambda l, i: (l, i, 0))],
        out_shape=[jax.ShapeDtypeStruct((depth, d, Z_MAIN), bf16),
                   jax.ShapeDtypeStruct((depth, d, LANES), bf16)],
        compiler_params=_cparams(("parallel", "parallel")),
        name="regroup_w_in",
    )(w_in)


def _layer_weights(l, norm1_g, if_bias, ml_g, gla_gw, gla_gb, gla_g, pool_w, pool_scale,
                   w_branch, w_out):
    ifb = jnp.zeros((1, LANES), f32)
    ifb = ifb.at[0, S_MI:S_MI + ML_HEADS].set(if_bias[l, 0]).at[0, S_MF:S_MF + ML_HEADS].set(if_bias[l, 1])
    gw = jnp.zeros((LANES, GLA_KW), f32).at[S_GLR:S_GLR + GLA_RANK].set(gla_gw[l]).astype(bf16)
    return dict(
        norm1_g=norm1_g[l][None], ifb=ifb, ml_g=ml_g[l][None],
        gla_gw=gw, gla_gb=gla_gb[l][None], gla_g=gla_g[l][None], pool_w=pool_w[l].astype(bf16),
        pool_scale=pool_scale[l][None], w_branch=w_branch[l].astype(bf16), w_out=w_out[l].astype(bf16))


def _forward(x_prompt, x_sample, state_mlstm_C, state_mlstm_n, state_mlstm_m, state_gla_S,
             state_pool_buf, norm1_g, w_in, mlstm_if_bias, mlstm_norm_g, gla_gate_w, gla_gate_b,
             gla_norm_g, pool_w, pool_scale, w_branch, w_out, norm2_g, ffn_wg, ffn_wu, ffn_wd,
             router_w, moe_wg, moe_wu, moe_wd, final_norm_g, *, tm, tn, tt, chunk, sub, nb, sb, tf_dense, moe_sub, moe_cap):
    batch, seq, _ = x_prompt.shape
    dec = x_sample.shape[0]
    depth = w_in.shape[0]
    m_prompt = batch * seq
    m_all = m_prompt + dec
    mp = -(-m_all // tm) * tm
    assert m_prompt % tm == 0
    x_parts = (x_prompt.reshape(m_prompt, D_MODEL),
               jnp.concatenate([x_sample.reshape(dec, D_MODEL), jnp.zeros((mp - m_all, D_MODEL), f32)], axis=0))
    outs = [[] for _ in range(5)]
    moe_out = None
    st = (state_mlstm_C, state_mlstm_n, jnp.pad(state_mlstm_m, ((0, 0), (0, 0), (0, LANES - ML_HEADS))),
          state_gla_S, state_pool_buf)
    st_new = None
    w_main, w_small = _regroup_w_in(w_in)
    for l in range(depth):
        lw = _layer_weights(l, norm1_g, mlstm_if_bias, mlstm_norm_g, gla_gate_w, gla_gate_b,
                            gla_norm_g, pool_w, pool_scale, w_branch, w_out)
        z, zs = _norm_matmul(x_parts, lw["norm1_g"], w_main, w_small, l, tm, tn)
        br_p, c_p, n_p, m_p, s_p, buf_p = _prompt_mix(z, zs, lw, batch, seq, tt, chunk, sub, nb)
        br_t, *st_new = _sample_mix(z, zs, st, st_new, l, lw, m_prompt, dec, sb)
        j = l // 2
        if l % 2 == 0:
            x = _merge(x_parts, br_p, br_t, z, lw["w_branch"], lw["w_out"], tm, m_all)
            x = _ffn(x, norm2_g[l][None], ffn_wg[j].astype(bf16), ffn_wu[j].astype(bf16),
                     ffn_wd[j].astype(bf16), tm, tf_dense)
        else:
            wr_hi = router_w[j].astype(bf16)
            wr_lo = (router_w[j] - wr_hi.astype(f32)).astype(bf16)
            wr = jnp.pad(jnp.concatenate([wr_hi, wr_lo], axis=1), ((0, 0), (0, LANES - 2 * N_EXPERTS)))
            x, hb, comb, rnk, rnkt, cnt = _merge(x_parts, br_p, br_t, z, lw["w_branch"], lw["w_out"], tm, m_all,
                                                 route_w=(norm2_g[l][None], wr))
            n_sup = mp // (tm * moe_sub)
            cnt = cnt[:, 0, :N_EXPERTS].reshape(n_sup, moe_sub, N_EXPERTS).max(axis=1)
            rounds = jnp.ceil(cnt / moe_cap).astype(jnp.int32).reshape(-1)
            moe_out = _moe(hb, rnk, rnkt, comb, rounds, moe_wg[j].astype(bf16), moe_wu[j].astype(bf16),
                           moe_wd[j].astype(bf16), tm, moe_sub, moe_cap)
            if l + 1 < depth:
                x = x + moe_out
                moe_out = None
        x_parts = (x,)
        for lst, val in zip(outs, (c_p, n_p, m_p[:, :ML_HEADS, 0], s_p, buf_p)):
            lst.append(val)
    if moe_out is None:
        moe_out = jnp.zeros_like(x)
    y_prompt, y_sample = _final_norm(x, moe_out, final_norm_g[None], tm, m_prompt, dec)
    c_p, n_p, m_p, s_p, buf_p = (jnp.stack(o) for o in outs)
    c_s, n_s, m_s, s_s, buf_s = st_new
    return (y_prompt.reshape(batch, seq, D_MODEL), y_sample.reshape(dec, 1, D_MODEL),
            c_p, c_s, n_p, n_s, m_p, m_s[:, :, :ML_HEADS], s_p, s_s, buf_p, buf_s)


def kernel(x_prompt, x_sample, state_mlstm_C, state_mlstm_n, state_mlstm_m, state_gla_S, state_pool_buf, norm1_g, w_in, mlstm_if_bias, mlstm_norm_g, gla_gate_w, gla_gate_b, gla_norm_g, pool_w, pool_scale, w_branch, w_out, norm2_g, ffn_wg, ffn_wu, ffn_wd, router_w, moe_wg, moe_wu, moe_wd, final_norm_g):
    return _forward(x_prompt, x_sample, state_mlstm_C, state_mlstm_n, state_mlstm_m, state_gla_S,
                    state_pool_buf, norm1_g, w_in, mlstm_if_bias, mlstm_norm_g, gla_gate_w, gla_gate_b,
                    gla_norm_g, pool_w, pool_scale, w_branch, w_out, norm2_g, ffn_wg, ffn_wu, ffn_wd,
                    router_w, moe_wg, moe_wu, moe_wd, final_norm_g,
                    tm=512, tn=1024, tt=512, chunk=128, sub=64, nb=2, sb=16, tf_dense=1408, moe_sub=3, moe_cap=160)
```

```python
import functools

import jax
import jax.numpy as jnp
from jax import lax
from jax.experimental import pallas as pl
from jax.experimental.pallas import tpu as pltpu

f32 = jnp.float32
bf16 = jnp.bfloat16

D_MODEL = 1024
ML_HEADS, ML_DH = 4, 128
ML_W = ML_HEADS * ML_DH
GLA_HEADS, GLA_DK, GLA_DV = 4, 64, 128
GLA_KW, GLA_VW = GLA_HEADS * GLA_DK, GLA_HEADS * GLA_DV
GLA_RANK = 16
GLA_TAU = 16.0
POOL_GDIM = 128
POOL_WINDOWS = (2, 4, 8, 16)
POOL_W = POOL_GDIM * len(POOL_WINDOWS)
POOL_BUF = 15
POOL_BASE = 24
N_BRANCH = 3
N_EXPERTS = 8
EPS = 1e-6
NEG = -1e30
LANES = 128
MXU_COLS = 256

C_MQ, C_MK, C_MV, C_MO = 0, 512, 1024, 1536
C_GQ, C_GK, C_GV, C_GO, C_PU, C_GATES = 2048, 2304, 2560, 3072, 3584, 4096
Z_MAIN = C_GATES + N_BRANCH * D_MODEL
S_MI, S_MF, S_GLR = 0, 4, 8

VMEM_LIMIT = 56 * 1024 * 1024

_NT = (((1,), (1,)), ((), ()))


def _cparams(sem):
    return pltpu.CompilerParams(dimension_semantics=sem, vmem_limit_bytes=VMEM_LIMIT)


def _log_sigmoid(x):
    return jnp.minimum(x, 0.0) - jnp.log(1.0 + jnp.exp(-jnp.abs(x)))


def _sigmoid(x):
    return 0.5 * jnp.tanh(0.5 * x) + 0.5


def _rms(x, g):
    ms = jnp.mean(x * x, axis=-1, keepdims=True)
    return x * lax.rsqrt(ms + EPS) * g


def _lower_tri(n):
    r = lax.broadcasted_iota(jnp.int32, (n, n), 0)
    c = lax.broadcasted_iota(jnp.int32, (n, n), 1)
    return c <= r


def _cumsum_rows(tri_bf16, a):
    a1 = a.astype(bf16)
    r = a - a1.astype(f32)
    a2 = r.astype(bf16)
    a3 = (r - a2.astype(f32)).astype(bf16)
    d = lambda y: jnp.dot(tri_bf16, y, preferred_element_type=f32)
    return d(a1) + d(a2) + d(a3)


def _cumsum_lanes(triu_bf16, a):
    a1 = a.astype(bf16)
    r = a - a1.astype(f32)
    a2 = r.astype(bf16)
    a3 = (r - a2.astype(f32)).astype(bf16)
    d = lambda y: jnp.dot(y, triu_bf16, preferred_element_type=f32)
    return d(a1) + d(a2) + d(a3)


def _resident(shape):
    nd = len(shape)
    return pl.BlockSpec(shape, lambda *_: (0,) * nd, pipeline_mode=pl.Buffered(1))


def _split_rows_specs(parts, tm, width):
    specs, start = [], 0
    for p in parts:
        n_t = p.shape[0] // tm
        assert n_t * tm == p.shape[0]
        specs.append(pl.BlockSpec(
            (tm, width), functools.partial(lambda i, s, n: (jnp.clip(i - s, 0, n - 1), 0), s=start, n=n_t)))
        start += n_t
    return specs


def _pick_rows(refs, starts):
    i = pl.program_id(0)
    x = refs[0][...]
    for r, s in zip(refs[1:], starts[1:]):
        x = jnp.where(i >= s, r[...], x)
    return x


def _norm_matmul_kernel(*refs, tn, starts):
    x_refs = refs[:len(starts)]
    g_ref, w_ref, ws_ref, z_ref, zs_ref = refs[len(starts):]
    h = _rms(_pick_rows(x_refs, starts), g_ref[...]).astype(bf16)
    zs_ref[...] = jnp.dot(h, ws_ref[0], preferred_element_type=f32)
    for c in range(w_ref.shape[2] // tn):
        cols = slice(c * tn, (c + 1) * tn)
        z_ref[:, cols] = jnp.dot(h, w_ref[0, :, cols], preferred_element_type=f32).astype(z_ref.dtype)


def _norm_matmul(x_parts, g, w_main, w_small, layer, tm, tn):
    mp = sum(p.shape[0] for p in x_parts)
    n = w_main.shape[2]
    starts = tuple(sum(p.shape[0] for p in x_parts[:j]) // tm for j in range(len(x_parts)))
    layer_block = lambda width: pl.BlockSpec((1, D_MODEL, width), lambda i: (layer, 0, 0),
                                             pipeline_mode=pl.Buffered(1))
    return pl.pallas_call(
        functools.partial(_norm_matmul_kernel, tn=tn, starts=starts),
        grid=(mp // tm,),
        in_specs=[
            *_split_rows_specs(x_parts, tm, D_MODEL),
            _resident((1, D_MODEL)),
            layer_block(n),
            layer_block(LANES),
        ],
        out_specs=[
            pl.BlockSpec((tm, n), lambda i: (i, 0)),
            pl.BlockSpec((tm, LANES), lambda i: (i, 0)),
        ],
        out_shape=[jax.ShapeDtypeStruct((mp, n), bf16), jax.ShapeDtypeStruct((mp, LANES), f32)],
        compiler_params=_cparams(("parallel",)),
        name="norm_matmul",
    )(*x_parts, g, w_main, w_small)


def _head_norm(h, g):
    outs = []
    for j in range(h.shape[1] // LANES):
        hj = h[:, j * LANES:(j + 1) * LANES]
        outs.append(hj * lax.rsqrt(jnp.mean(hj * hj, axis=-1, keepdims=True) + EPS))
    return jnp.concatenate(outs, axis=1) * g


def _gla_log_decay(sm, gw_ref, gb_ref):
    xg = jnp.dot(sm.astype(bf16), gw_ref[...], preferred_element_type=f32) + gb_ref[...]
    return _log_sigmoid(xg) * (1.0 / GLA_TAU)


N_ZBLOCKS = 9


def _prompt_mix_kernel(*refs, nb, tt, chunk, sub, n_t):
    z_refs = [refs[s * N_ZBLOCKS:(s + 1) * N_ZBLOCKS] for s in range(nb)]
    ifb_ref, mlg_ref, gw_ref, gb_ref, glag_ref, pw_ref, ps_ref = refs[nb * N_ZBLOCKS:nb * N_ZBLOCKS + 7]
    br_all, c_out, n_out, m_out, s_out, buf_out = refs[nb * N_ZBLOCKS + 7:nb * N_ZBLOCKS + 13]
    scratch = refs[nb * N_ZBLOCKS + 13:]
    per = ML_HEADS + GLA_HEADS + 7
    c_refs = [scratch[s * per:s * per + ML_HEADS] for s in range(nb)]
    s_refs = [scratch[s * per + ML_HEADS:s * per + ML_HEADS + GLA_HEADS] for s in range(nb)]
    n_refs = [scratch[s * per + ML_HEADS + GLA_HEADS] for s in range(nb)]
    m_refs = [scratch[s * per + ML_HEADS + GLA_HEADS + 1] for s in range(nb)]
    ext_refs = [scratch[s * per + ML_HEADS + GLA_HEADS + 2] for s in range(nb)]
    prep_refs = [scratch[s * per + ML_HEADS + GLA_HEADS + 3:(s + 1) * per] for s in range(nb)]
    t_idx = pl.program_id(1)
    L = chunk

    @pl.when(t_idx == 0)
    def _():
        for s in range(nb):
            for r in (*c_refs[s], *s_refs[s], n_refs[s], m_refs[s]):
                r[...] = jnp.zeros_like(r)
            ext_refs[s][0:POOL_BASE, :] = jnp.zeros((POOL_BASE, POOL_W), f32)

    tri = _lower_tri(L)
    tri_b = tri.astype(bf16)
    causal_sub = _lower_tri(sub)
    assert L == LANES
    ones_b = jnp.ones((L, LANES), bf16)
    triu_b = (lax.broadcasted_iota(jnp.int32, (L, L), 0)
              <= lax.broadcasted_iota(jnp.int32, (L, L), 1)).astype(bf16)
    ifb = ifb_ref[...]
    k_scale = ML_DH ** -0.5
    q_scale = GLA_DK ** -0.5

    def gate_prep(c, seq):
        sm_ref = z_refs[seq][8]
        y0_p, bc_p, rows_p, b_p = prep_refs[seq]
        sm = sm_ref[pl.ds(pl.multiple_of(c * L, L), L), :]
        y0 = sm + ifb
        y0_p[...] = y0
        bc_p[...] = _cumsum_rows(tri_b, _log_sigmoid(y0))
        y0t = y0.T[0:8, :]
        rows_p[0:8, :] = y0t
        rows_p[8:16, :] = _cumsum_lanes(triu_b, _log_sigmoid(y0t))
        b_p[...] = _cumsum_rows(tri_b, _gla_log_decay(sm, gw_ref, gb_ref))

    def one_seq(c, seq):
        q_ref, k_ref, v_ref, mo_ref, gqk_ref, gv_ref, go_ref, _, sm_ref = z_refs[seq]
        br_ref, c_s, s_s = br_all.at[seq], c_refs[seq], s_refs[seq]
        n_old, m_old = n_refs[seq][...], m_refs[seq][...]
        n_rows, m_rows = [], []
        r0 = pl.multiple_of(c * L, L)
        rows = pl.ds(r0, L)
        y0_p, bc_p, rows_p, b_p = prep_refs[seq]

        y0, bc = y0_p[...], bc_p[...]
        y0t, bct = rows_p[0:8, :], rows_p[8:16, :]
        b = b_p[...]
        hm = []
        for h in range(ML_HEADS):
            hs = slice(h * ML_DH, (h + 1) * ML_DH)
            v = v_ref[rows, hs].astype(f32)
            kb = k_ref[rows, hs].astype(bf16)
            qb, vb = q_ref[rows, hs].astype(bf16), v.astype(bf16)
            bcol = bc[:, S_MF + h:S_MF + h + 1]
            icol = y0[:, S_MI + h:S_MI + h + 1]
            brow = bct[S_MF + h:S_MF + h + 1, :]
            irow = y0t[S_MI + h:S_MI + h + 1, :]
            m_row = m_old[h:h + 1, :]
            cmat = c_s[h][...]
            nrow = n_old[h:h + 1, :]
            bcol_r = jnp.broadcast_to(bcol, (L, LANES))
            dm = jnp.where(tri, bcol_r - brow + irow, NEG)
            inter_r = bcol_r + m_row
            m_t_r = jnp.maximum(inter_r, jnp.broadcast_to(jnp.max(dm, axis=1, keepdims=True), (L, LANES)))
            w_intra = jnp.exp(dm - m_t_r) * k_scale
            w_inter_r = jnp.exp(inter_r - m_t_r)
            s = lax.dot_general(qb, kb, _NT, preferred_element_type=f32) * w_intra
            s_hi = s.astype(bf16)
            s_lo = (s - s_hi.astype(f32)).astype(bf16)
            r_intra = jnp.dot(s_hi, jnp.concatenate([vb, ones_b], axis=1), preferred_element_type=f32)
            c_aug = jnp.concatenate([cmat, jnp.broadcast_to(nrow, (LANES, LANES))], axis=0).astype(bf16)
            r_inter = lax.dot_general(qb, c_aug, _NT, preferred_element_type=f32)
            rs_lo = jnp.dot(s_lo, ones_b, preferred_element_type=f32)
            num = r_intra[:, 0:LANES] + w_inter_r * r_inter[:, 0:LANES]
            den_r = r_intra[:, LANES:] + rs_lo + w_inter_r * r_inter[:, LANES:]
            hm.append(num / jnp.maximum(jnp.abs(den_r), jnp.exp(-m_t_r)))
            b_last = bcol[L - 1:L, :]
            m = m_row[:, 0:1]
            g = b_last - bcol + icol
            m_new = jnp.maximum(b_last + m, jnp.max(g, axis=0, keepdims=True))
            w_s = jnp.broadcast_to(jnp.exp(g - m_new) * k_scale, (L, LANES))
            w_c = jnp.exp(b_last + m - m_new)
            vwt = (v * w_s).T.astype(bf16)
            c_s[h][...] = w_c * cmat + jnp.dot(vwt, kb, preferred_element_type=f32)
            n_rows.append(w_c * nrow + jnp.sum(w_s * k_ref[rows, hs].astype(f32), axis=0, keepdims=True))
            m_rows.append(jnp.broadcast_to(m_new, (1, LANES)))
        pad_rows = [jnp.zeros((8 - ML_HEADS, LANES), f32)]
        n_refs[seq][...] = jnp.concatenate(n_rows + pad_rows, axis=0)
        m_refs[seq][...] = jnp.concatenate(m_rows + pad_rows, axis=0)
        y_ml = (_head_norm(jnp.concatenate(hm, axis=1), mlg_ref[...])
                * _sigmoid(mo_ref[rows, :].astype(f32)))
        br_ref[rows, 0:ML_W] = y_ml.astype(br_ref.dtype)

        q2 = gqk_ref[rows, 0:GLA_KW].astype(f32) * q_scale
        k2 = gqk_ref[rows, GLA_KW:2 * GLA_KW].astype(f32)
        gv = gv_ref[rows, :].astype(bf16)
        s_old = [s_s[h][...] for h in range(GLA_HEADS)]
        qe_chunk = (q2 * jnp.exp(b)).astype(bf16)
        b_last = b[L - 1:L, :]
        kdt = (k2 * jnp.exp(b_last - b)).T.astype(bf16)
        decay_col = jnp.exp(b.T[:, L - 1:L])
        o_blocks = []
        for blk in range(L // sub):
            s0, s1 = blk * sub, (blk + 1) * sub
            mid = s0 + sub // 2
            b_blk = b[s0:s1]
            b_mid = b[mid - 1:mid, :]
            qe_d = (q2[s0:s1] * jnp.exp(b_blk - b_mid)).astype(bf16)
            ke_d = (k2[s0:s1] * jnp.exp(b_mid - b_blk)).astype(bf16)
            if blk > 0:
                b_start = b[s0 - 1:s0, :]
                qe_o = (q2[s0:s1] * jnp.exp(b_blk - b_start)).astype(bf16)
                ke_o = (k2[0:s0] * jnp.exp(b_start - b[0:s0])).astype(bf16)
            o_heads = []
            for h in range(GLA_HEADS):
                ks = slice(h * GLA_DK, (h + 1) * GLA_DK)
                vs = slice(h * GLA_DV, (h + 1) * GLA_DV)
                a = lax.dot_general(qe_d[:, ks], ke_d[:, ks], _NT, preferred_element_type=f32)
                a = jnp.where(causal_sub, a, 0.0)
                o = (jnp.dot(a.astype(bf16), gv[s0:s1, vs], preferred_element_type=f32)
                     + jnp.dot(qe_chunk[s0:s1, ks], s_old[h].astype(bf16), preferred_element_type=f32))
                if blk > 0:
                    a = lax.dot_general(qe_o[:, ks], ke_o[:, ks], _NT, preferred_element_type=f32)
                    o = o + jnp.dot(a.astype(bf16), gv[0:s0, vs], preferred_element_type=f32)
                o_heads.append(o)
            o_blocks.append(jnp.concatenate(o_heads, axis=1))
        og = jnp.concatenate(o_blocks, axis=0)
        gate_prep(jnp.minimum(c + 1, tt // L - 1), seq)
        for h in range(GLA_HEADS):
            ks = slice(h * GLA_DK, (h + 1) * GLA_DK)
            vs = slice(h * GLA_DV, (h + 1) * GLA_DV)
            s_s[h][...] = (decay_col[ks, :] * s_old[h]
                           + jnp.dot(kdt[ks, :], gv[:, vs], preferred_element_type=f32))
        go = go_ref[rows, :].astype(f32)
        y_gla = _head_norm(og, glag_ref[...]) * (go * _sigmoid(go))
        br_ref[rows, ML_W:ML_W + GLA_VW] = y_gla.astype(br_ref.dtype)

    def chunk_body(c, carry):
        for s in range(nb):
            one_seq(c, s)
        return carry

    for s in range(nb):
        gate_prep(0, s)
    lax.fori_loop(0, tt // L, chunk_body, 0)

    pos = t_idx * tt + lax.broadcasted_iota(jnp.int32, (tt, 1), 0)
    for s in range(nb):
        pu_ref, br_ref, ext_s = z_refs[s][7], br_all.at[s], ext_refs[s]
        n_ext = tt + POOL_BASE
        ext_s[POOL_BASE:n_ext, :] = pu_ref[...].astype(f32)
        for g, w in enumerate(POOL_WINDOWS):
            gs = slice(g * POOL_GDIM, (g + 1) * POOL_GDIM)
            u = ext_s[POOL_BASE:n_ext, gs]
            p, k = ext_s[:, gs], 1
            while k < w:
                p = p + pltpu.roll(p, k, 0)
                k *= 2
            acc = p[POOL_BASE:n_ext, :]
            cnt = jnp.minimum(pos + 1, w).astype(f32)
            d = acc / cnt - u
            y = jnp.dot(d.astype(bf16), pw_ref[g], preferred_element_type=f32) * ps_ref[:, gs]
            br_ref[:, ML_W + GLA_VW + g * POOL_GDIM:ML_W + GLA_VW + (g + 1) * POOL_GDIM] = y.astype(br_ref.dtype)
        ext_s[8:POOL_BASE, :] = ext_s[tt + 8:n_ext, :]

    @pl.when(t_idx == n_t - 1)
    def _():
        for s in range(nb):
            for h in range(ML_HEADS):
                c_out[s, h] = c_refs[s][h][...]
            for h in range(GLA_HEADS):
                s_out[s, h] = s_refs[s][h][...]
            n_out[s] = n_refs[s][0:ML_HEADS, :]
            m_out[s] = m_refs[s][...]
            buf_out[s] = ext_refs[s][POOL_BASE - POOL_BUF:POOL_BASE, :]


def _prompt_mix(z, zs, lw, batch, seq, tt, chunk, sub, nb):
    n_t = seq // tt
    assert batch % nb == 0

    def zspecs(s):
        row = lambda b, t: (b * nb + s) * n_t + t
        spec = lambda col, width: pl.BlockSpec((tt, width), lambda b, t: (row(b, t), col // width))
        return [spec(C_MQ, 512), spec(C_MK, 512), spec(C_MV, 512), spec(C_MO, 512),
                spec(C_GQ, 512), spec(C_GV, 512), spec(C_GO, 512), spec(C_PU, 512),
                pl.BlockSpec((tt, LANES), lambda b, t: (row(b, t), 0))]

    def full(shape):
        nd = len(shape)
        return pl.BlockSpec(shape, lambda b, t: (0,) * nd)

    def per_seq(shape):
        nd = len(shape)
        return pl.BlockSpec((nb,) + shape, lambda b, t: (b,) + (0,) * nd)

    kern = functools.partial(_prompt_mix_kernel, nb=nb, tt=tt, chunk=chunk, sub=sub, n_t=n_t)
    outs = pl.pallas_call(
        kern,
        grid=(batch // nb, n_t),
        in_specs=[sp for s in range(nb) for sp in zspecs(s)]
                 + [full((1, LANES)), full((1, ML_W)), full((LANES, GLA_KW)), full((1, GLA_KW)),
                    full((1, GLA_VW)), full((len(POOL_WINDOWS), POOL_GDIM, POOL_GDIM)), full((1, POOL_W))],
        out_specs=[
            pl.BlockSpec((nb, tt, N_BRANCH * ML_W), lambda b, t: (b, t, 0)),
            per_seq((ML_HEADS, ML_DH, ML_DH)), per_seq((ML_HEADS, ML_DH)), per_seq((8, LANES)),
            per_seq((GLA_HEADS, GLA_DK, GLA_DV)), per_seq((POOL_BUF, POOL_W)),
        ],
        out_shape=[
            jax.ShapeDtypeStruct((batch, seq, N_BRANCH * ML_W), bf16),
            jax.ShapeDtypeStruct((batch, ML_HEADS, ML_DH, ML_DH), f32),
            jax.ShapeDtypeStruct((batch, ML_HEADS, ML_DH), f32),
            jax.ShapeDtypeStruct((batch, 8, LANES), f32),
            jax.ShapeDtypeStruct((batch, GLA_HEADS, GLA_DK, GLA_DV), f32),
            jax.ShapeDtypeStruct((batch, POOL_BUF, POOL_W), f32),
        ],
        scratch_shapes=([pltpu.VMEM((ML_DH, ML_DH), f32)] * ML_HEADS
                        + [pltpu.VMEM((GLA_DK, GLA_DV), f32)] * GLA_HEADS
                        + [pltpu.VMEM((8, LANES), f32), pltpu.VMEM((8, LANES), f32),
                           pltpu.VMEM((tt + POOL_BASE, POOL_W), f32),
                           pltpu.VMEM((chunk, LANES), f32), pltpu.VMEM((chunk, LANES), f32),
                           pltpu.VMEM((16, chunk), f32), pltpu.VMEM((chunk, GLA_KW), f32)]) * nb,
        compiler_params=_cparams(("parallel", "arbitrary")),
        name="prompt_mix",
    )(*([z] * 8 + [zs]) * nb,
      lw["ifb"], lw["ml_g"], lw["gla_gw"], lw["gla_gb"], lw["gla_g"], lw["pool_w"], lw["pool_scale"])
    return (outs[0].reshape(batch * seq, N_BRANCH * ML_W),) + tuple(outs[1:])


def _sample_mix_kernel(q_ref, k_ref, v_ref, mo_ref, gqk_ref, gv_ref, go_ref, pu_ref, sm_ref,
                       c_in, n_in, m_in, s_in, buf_in,
                       ifb_ref, mlg_ref, gw_ref, gb_ref, glag_ref, pw_ref, ps_ref, *tail, sb, dec):
    br_all = tail[-6]
    c_out, n_out, m_out, s_out, buf_out = (r.at[0] for r in tail[-5:])
    step = pl.program_id(0)
    c_in, n_in, m_in, s_in, buf_in = (r.at[0] for r in (c_in, n_in, m_in, s_in, buf_in))
    br_ref = br_all.at[pl.ds(pl.multiple_of(step * sb, sb), sb), :]

    @pl.when(step == 0)
    def _():
        br_all[dec:, :] = jnp.zeros((br_all.shape[0] - dec, br_all.shape[1]), br_all.dtype)

    sm = sm_ref[...]
    eye = (lax.broadcasted_iota(jnp.int32, (LANES, LANES), 0)
           == lax.broadcasted_iota(jnp.int32, (LANES, LANES), 1)).astype(f32)

    def to_cols(x):
        parts = [lax.dot_general(eye, x[:, j * LANES:(j + 1) * LANES], _NT,
                                 preferred_element_type=f32, precision=lax.Precision.HIGHEST)
                 for j in range(x.shape[1] // LANES)]
        return jnp.concatenate(parts, axis=0)

    y0 = sm + ifb_ref[...]
    logf_all = _log_sigmoid(y0)
    k_scale = ML_DH ** -0.5
    m_all = m_in[...]
    hm, m_new_cols = [], []
    for h in range(ML_HEADS):
        hs = slice(h * ML_DH, (h + 1) * ML_DH)
        q = q_ref[:, hs].astype(f32)
        k = k_ref[:, hs].astype(f32) * k_scale
        v = v_ref[:, hs].astype(f32)
        i_pre = y0[:, S_MI + h:S_MI + h + 1]
        logf = logf_all[:, S_MF + h:S_MF + h + 1]
        m = m_all[:, h:h + 1]
        inter = logf + m
        m_t = jnp.maximum(inter, i_pre)
        w_intra = jnp.exp(i_pre - m_t)
        w_inter = jnp.exp(inter - m_t)
        s = jnp.sum(q * k, axis=1, keepdims=True) * w_intra
        qb = q.astype(bf16)
        n_h = n_in[:, h, :]
        cq = jnp.concatenate(
            [lax.dot_general(qb, c_in[j, h].astype(bf16), _NT, preferred_element_type=f32)[j:j + 1, :]
             for j in range(sb)], axis=0)
        num = s * v + w_inter * cq
        den = s + w_inter * jnp.sum(n_h * q, axis=1, keepdims=True)
        hm.append(num / jnp.maximum(jnp.abs(den), jnp.exp(-m_t)))
        m_new = m_t
        w_s = w_intra
        w_c = w_inter
        n_out[:, h, :] = w_c * n_h + w_s * k
        m_new_cols.append(m_new)
        vw_cols = to_cols(v * w_s)
        for j in range(sb):
            c_out[j, h] = w_c[j:j + 1, :] * c_in[j, h] + vw_cols[:, j:j + 1] * k[j:j + 1, :]
    lane = lax.broadcasted_iota(jnp.int32, (sb, LANES), 1)
    m_pack = jnp.zeros((sb, LANES), f32)
    for h in range(ML_HEADS):
        m_pack = jnp.where(lane == h, m_new_cols[h], m_pack)
    m_out[...] = m_pack
    y_ml = _head_norm(jnp.concatenate(hm, axis=1), mlg_ref[...]) * _sigmoid(mo_ref[...].astype(f32))
    br_ref[:, 0:ML_W] = y_ml.astype(br_ref.dtype)

    log_a = _gla_log_decay(sm, gw_ref, gb_ref)
    decay = jnp.exp(log_a)
    q2 = gqk_ref[:, 0:GLA_KW].astype(f32) * (GLA_DK ** -0.5)
    k2 = gqk_ref[:, GLA_KW:2 * GLA_KW].astype(f32)
    gv = gv_ref[...].astype(f32)
    qe = (q2 * decay).astype(bf16)
    qk = q2 * k2
    k_cols = to_cols(k2)
    decay_cols = to_cols(decay)
    og = []
    for h in range(GLA_HEADS):
        ks = slice(h * GLA_DK, (h + 1) * GLA_DK)
        vs = slice(h * GLA_DV, (h + 1) * GLA_DV)
        a = jnp.sum(qk[:, ks], axis=1, keepdims=True)
        inter = jnp.concatenate(
            [jnp.dot(qe[:, ks], s_in[j, h].astype(bf16), preferred_element_type=f32)[j:j + 1, :]
             for j in range(sb)], axis=0)
        og.append(a * gv[:, vs] + inter)
        for j in range(sb):
            s_out[j, h] = (decay_cols[ks, j:j + 1] * s_in[j, h]
                           + k_cols[ks, j:j + 1] * gv[j:j + 1, vs])
    go = go_ref[...].astype(f32)
    y_gla = _head_norm(jnp.concatenate(og, axis=1), glag_ref[...]) * (go * _sigmoid(go))
    br_ref[:, ML_W:ML_W + GLA_VW] = y_gla.astype(br_ref.dtype)

    u = pu_ref[...].astype(f32)
    rowi = lax.broadcasted_iota(jnp.int32, (POOL_BUF + 1, POOL_GDIM), 0)
    d_rows = []
    for j in range(sb):
        ext = jnp.concatenate([buf_in[j], u[j:j + 1, :]], axis=0)
        buf_out[j] = ext[1:POOL_BUF + 1, :]
        parts = []
        for g, w in enumerate(POOL_WINDOWS):
            gs = slice(g * POOL_GDIM, (g + 1) * POOL_GDIM)
            win = jnp.sum(jnp.where(rowi >= POOL_BUF + 1 - w, ext[:, gs], 0.0), axis=0, keepdims=True)
            parts.append(win / float(w) - u[j:j + 1, gs])
        d_rows.append(jnp.concatenate(parts, axis=1))
    d = jnp.concatenate(d_rows, axis=0)
    for g in range(len(POOL_WINDOWS)):
        gs = slice(g * POOL_GDIM, (g + 1) * POOL_GDIM)
        y = jnp.dot(d[:, gs].astype(bf16), pw_ref[g], preferred_element_type=f32) * ps_ref[:, gs]
        br_ref[:, ML_W + GLA_VW + g * POOL_GDIM:ML_W + GLA_VW + (g + 1) * POOL_GDIM] = y.astype(br_ref.dtype)


def _sample_mix(z, zs, st, prev, layer, lw, row0, dec, sb):
    depth = st[0].shape[0]
    base = row0 // sb
    tail_rows = z.shape[0] - row0

    def zspec(col, width):
        blk = col // width
        return pl.BlockSpec((sb, width), lambda i: (base + i, blk))

    def full(shape):
        nd = len(shape)
        return pl.BlockSpec(shape, lambda i: (0,) * nd)

    st_shapes = [(ML_HEADS, ML_DH, ML_DH), (ML_HEADS, ML_DH), (LANES,),
                 (GLA_HEADS, GLA_DK, GLA_DV), (POOL_BUF, POOL_W)]
    st_specs = [pl.BlockSpec((1, sb) + s, functools.partial(lambda i, nd: (layer, i) + (0,) * nd, nd=len(s)))
                for s in st_shapes]
    n_in = N_ZBLOCKS + len(st_shapes) + 7
    prev = () if prev is None else tuple(prev)
    return pl.pallas_call(
        functools.partial(_sample_mix_kernel, sb=sb, dec=dec),
        grid=(dec // sb,),
        in_specs=[zspec(C_MQ, 512), zspec(C_MK, 512), zspec(C_MV, 512), zspec(C_MO, 512),
                  zspec(C_GQ, 512), zspec(C_GV, 512), zspec(C_GO, 512), zspec(C_PU, 512),
                  pl.BlockSpec((sb, LANES), lambda i: (base + i, 0))]
                 + st_specs
                 + [full((1, LANES)), full((1, ML_W)), full((LANES, GLA_KW)), full((1, GLA_KW)),
                    full((1, GLA_VW)), full((len(POOL_WINDOWS), POOL_GDIM, POOL_GDIM)), full((1, POOL_W))]
                 + [pl.BlockSpec(memory_space=pl.ANY)] * len(prev),
        out_specs=[pl.BlockSpec((tail_rows, N_BRANCH * ML_W), lambda i: (0, 0))] + st_specs,
        out_shape=[jax.ShapeDtypeStruct((tail_rows, N_BRANCH * ML_W), bf16)]
                  + [jax.ShapeDtypeStruct((depth, dec) + s, f32) for s in st_shapes],
        input_output_aliases={n_in + j: 1 + j for j in range(len(prev))},
        compiler_params=_cparams(("arbitrary",)),
        name="sample_mix",
    )(z, z, z, z, z, z, z, z, zs, *st,
      lw["ifb"], lw["ml_g"], lw["gla_gw"], lw["gla_gb"], lw["gla_g"], lw["pool_w"], lw["pool_scale"], *prev)


def _merge_kernel(*refs, tm, n_valid, n_prompt_tiles, x_starts, route):
    x_refs = refs[:len(x_starts)]
    brp_ref, brt_ref, g0_ref, g1_ref, g2_ref, wb_ref, wo_ref = refs[len(x_starts):len(x_starts) + 7]
    route_in = refs[len(x_starts) + 7:len(x_starts) + 9] if route else ()
    o_ref = refs[len(x_starts) + 7 + len(route_in)]
    route_out = refs[len(x_starts) + 8 + len(route_in):]
    in_tail = pl.program_id(0) >= n_prompt_tiles
    mixed = None
    for n, g_ref in enumerate((g0_ref, g1_ref, g2_ref)):
        cols = slice(n * ML_W, (n + 1) * ML_W)
        br = jnp.where(in_tail, brt_ref[:, cols], brp_ref[:, cols])
        proj = jnp.dot(br, wb_ref[n], preferred_element_type=f32)
        term = _sigmoid(g_ref[...].astype(f32)) * proj
        mixed = term if mixed is None else mixed + term
    out = (_pick_rows(x_refs, x_starts)
           + jnp.dot(mixed.astype(bf16), wo_ref[...], preferred_element_type=f32))
    row = pl.program_id(0) * tm + lax.broadcasted_iota(jnp.int32, (tm, 1), 0)
    out = jnp.where(row < n_valid, out, 0.0)
    o_ref[...] = out
    if route:
        _route(out, *route_in, *route_out, tm=tm, n_valid=n_valid)


def _merge(x_parts, br_prompt, br_tail, z, w_branch, w_out, tm, n_valid, route_w=None):
    mp = sum(p.shape[0] for p in x_parts)
    nt = mp // tm
    route = route_w is not None
    route_specs = [_resident((1, D_MODEL)), _resident((D_MODEL, LANES))] if route else []
    out_specs = [pl.BlockSpec((tm, D_MODEL), lambda i: (i, 0))]
    out_shape = [jax.ShapeDtypeStruct((mp, D_MODEL), f32)]
    if route:
        out_specs += [pl.BlockSpec((tm, D_MODEL), lambda i: (i, 0)),
                      pl.BlockSpec((tm, LANES), lambda i: (i, 0)),
                      pl.BlockSpec((tm, LANES), lambda i: (i, 0)),
                      pl.BlockSpec((1, N_EXPERTS, tm), lambda i: (i, 0, 0)),
                      pl.BlockSpec((1, 1, LANES), lambda i: (i, 0, 0))]
        out_shape += [jax.ShapeDtypeStruct((mp, D_MODEL), bf16),
                      jax.ShapeDtypeStruct((mp, LANES), f32),
                      jax.ShapeDtypeStruct((mp, LANES), f32),
                      jax.ShapeDtypeStruct((nt, N_EXPERTS, tm), f32),
                      jax.ShapeDtypeStruct((nt, 1, LANES), f32)]
    x_starts = tuple(sum(p.shape[0] for p in x_parts[:j]) // tm for j in range(len(x_parts)))
    n_p = br_prompt.shape[0] // tm
    gate_specs = [pl.BlockSpec((tm, D_MODEL), functools.partial(lambda i, n: (i, C_GATES // D_MODEL + n), n=n))
                  for n in range(N_BRANCH)]
    outs = pl.pallas_call(
        functools.partial(_merge_kernel, tm=tm, n_valid=n_valid, n_prompt_tiles=n_p, x_starts=x_starts,
                          route=route),
        grid=(nt,),
        in_specs=[
            *_split_rows_specs(x_parts, tm, D_MODEL),
            pl.BlockSpec((tm, N_BRANCH * ML_W), lambda i: (jnp.minimum(i, n_p - 1), 0)),
            pl.BlockSpec((tm, N_BRANCH * ML_W), lambda i: (jnp.maximum(i - n_p, 0), 0)),
            *gate_specs,
            _resident((N_BRANCH, ML_W, D_MODEL)),
            _resident((D_MODEL, D_MODEL)),
            *route_specs,
        ],
        out_specs=out_specs,
        out_shape=out_shape,
        compiler_params=_cparams(("parallel",)),
        name="merge_route" if route else "merge",
    )(*x_parts, br_prompt, br_tail, z, z, z, w_branch, w_out, *(route_w or ()))
    return outs if route else outs[0]


def _route(x, g_ref, wr_ref, hb_ref, comb_ref, rnk_ref, rnkt_ref, cnt_ref, *, tm, n_valid):
    h = _rms(x, g_ref[...])
    h_hi = h.astype(bf16)
    hb_ref[...] = h_hi
    h_lo = (h - h_hi.astype(f32)).astype(bf16)
    p_hi = jnp.dot(h_hi, wr_ref[...], preferred_element_type=f32)
    p_lo = jnp.dot(h_lo, wr_ref[...], preferred_element_type=f32)
    logits = p_hi + p_lo + pltpu.roll(p_hi, LANES - N_EXPERTS, 1)
    lane = lax.broadcasted_iota(jnp.int32, logits.shape, 1)
    valid = lane < N_EXPERTS
    logits = jnp.where(valid, logits, NEG)
    mx = jnp.max(logits, axis=1, keepdims=True)
    e = jnp.where(valid, jnp.exp(logits - mx), 0.0)
    probs = e / jnp.sum(e, axis=1, keepdims=True)
    p1 = jnp.max(probs, axis=1, keepdims=True)
    i1 = jnp.min(jnp.where(probs == p1, lane, LANES), axis=1, keepdims=True)
    rest = jnp.where((lane == i1) | ~valid, -1.0, probs)
    p2 = jnp.max(rest, axis=1, keepdims=True)
    i2 = jnp.min(jnp.where(rest == p2, lane, LANES), axis=1, keepdims=True)
    tot = p1 + p2
    comb_ref[...] = jnp.where(lane == i1, p1 / tot, 0.0) + jnp.where(lane == i2, p2 / tot, 0.0)
    row = pl.program_id(0) * tm + lax.broadcasted_iota(jnp.int32, logits.shape, 0)
    sel = ((lane == i1) | (lane == i2)) & (row < n_valid)
    r = lax.broadcasted_iota(jnp.int32, (tm, tm), 0)
    c = lax.broadcasted_iota(jnp.int32, (tm, tm), 1)
    rank = jnp.dot((c < r).astype(bf16), sel.astype(bf16), preferred_element_type=f32)
    rnk = jnp.where(sel, rank, -1.0)
    rnk_ref[...] = rnk
    rnkt_ref[0] = rnk.T[0:N_EXPERTS, :]
    cnt_ref[0] = jnp.sum(sel.astype(f32), axis=0, keepdims=True)


def _moe_kernel(rounds_ref, hb_ref, rnk_ref, rnkt_ref, comb_ref, wg_ref, wu_ref, wd_ref, o_ref,
                *, cap, n_sub, ts):
    i, e = pl.program_id(0), pl.program_id(1)

    @pl.when(e == 0)
    def _():
        o_ref[...] = jnp.zeros_like(o_ref)

    lane = lax.broadcasted_iota(jnp.int32, (ts, LANES), 1)
    slot_rows = lax.broadcasted_iota(jnp.int32, (cap, ts), 0).astype(f32)
    slot_cols = lax.broadcasted_iota(jnp.int32, (ts, cap), 1).astype(f32)

    def round_body(r, carry):
        base = (r * cap).astype(f32)
        xs = []
        for j in range(n_sub):
            rt = rnkt_ref[j, pl.ds(e, 1), :]
            p = (rt - base == slot_rows).astype(bf16)
            xs.append(jnp.dot(p, hb_ref[j * ts:(j + 1) * ts, :],
                              preferred_element_type=f32).astype(bf16))
        xs = jnp.concatenate(xs, axis=0)
        a = jnp.dot(xs, wg_ref[0], preferred_element_type=f32)
        a = (a * _sigmoid(a)) * jnp.dot(xs, wu_ref[0], preferred_element_type=f32)
        y = jnp.dot(a.astype(bf16), wd_ref[0], preferred_element_type=f32).astype(bf16)
        for j in range(n_sub):
            rows = slice(j * ts, (j + 1) * ts)
            col = jnp.sum(jnp.where(lane == e, rnk_ref[rows, :], 0.0), axis=1, keepdims=True)
            w = jnp.sum(jnp.where(lane == e, comb_ref[rows, :], 0.0), axis=1, keepdims=True)
            pt = (col - base == slot_cols).astype(bf16)
            o_ref[rows, :] += w * jnp.dot(pt, y[j * cap:(j + 1) * cap, :], preferred_element_type=f32)
        return carry

    lax.fori_loop(0, rounds_ref[i * N_EXPERTS + e], round_body, 0)


def _moe(hb, rnk, rnkt, comb, rounds, wg, wu, wd, ts, n_sub, cap):
    mp = hb.shape[0]
    n_e, _, d_ff = wg.shape
    tsup = ts * n_sub
    grid_spec = pltpu.PrefetchScalarGridSpec(
        num_scalar_prefetch=1,
        grid=(mp // tsup, n_e),
        in_specs=[pl.BlockSpec((tsup, D_MODEL), lambda i, e, r: (i, 0)),
                  pl.BlockSpec((tsup, LANES), lambda i, e, r: (i, 0)),
                  pl.BlockSpec((n_sub, N_EXPERTS, ts), lambda i, e, r: (i, 0, 0)),
                  pl.BlockSpec((tsup, LANES), lambda i, e, r: (i, 0)),
                  pl.BlockSpec((1, D_MODEL, d_ff), lambda i, e, r: (e, 0, 0)),
                  pl.BlockSpec((1, D_MODEL, d_ff), lambda i, e, r: (e, 0, 0)),
                  pl.BlockSpec((1, d_ff, D_MODEL), lambda i, e, r: (e, 0, 0))],
        out_specs=pl.BlockSpec((tsup, D_MODEL), lambda i, e, r: (i, 0)),
    )
    return pl.pallas_call(
        functools.partial(_moe_kernel, cap=cap, n_sub=n_sub, ts=ts),
        grid_spec=grid_spec,
        out_shape=jax.ShapeDtypeStruct((mp, D_MODEL), f32),
        compiler_params=_cparams(("parallel", "arbitrary")),
        name="moe",
    )(rounds, hb, rnk, rnkt, comb, wg, wu, wd)


def _ffn_kernel(x_ref, g_ref, wg_ref, wu_ref, wd_ref, o_ref, *, tf):
    x = x_ref[...]
    h = _rms(x, g_ref[...]).astype(bf16)
    acc = x
    for c in range(wg_ref.shape[1] // tf):
        cols = slice(c * tf, (c + 1) * tf)
        a = jnp.dot(h, wg_ref[:, cols], preferred_element_type=f32)
        a = (a * _sigmoid(a)) * jnp.dot(h, wu_ref[:, cols], preferred_element_type=f32)
        acc = acc + jnp.dot(a.astype(bf16), wd_ref[cols, :], preferred_element_type=f32)
    o_ref[...] = acc


def _ffn(x, g, wg, wu, wd, tm, tf):
    mp = x.shape[0]
    d_ff = wg.shape[1]
    return pl.pallas_call(
        functools.partial(_ffn_kernel, tf=tf),
        grid=(mp // tm,),
        in_specs=[pl.BlockSpec((tm, D_MODEL), lambda i: (i, 0)),
                  _resident((1, D_MODEL)),
                  _resident((D_MODEL, d_ff)),
                  _resident((D_MODEL, d_ff)),
                  _resident((d_ff, D_MODEL))],
        out_specs=pl.BlockSpec((tm, D_MODEL), lambda i: (i, 0)),
        out_shape=jax.ShapeDtypeStruct((mp, D_MODEL), f32),
        compiler_params=_cparams(("parallel",)),
        name="ffn",
    )(x, g, wg, wu, wd)


def _final_norm_kernel(x_ref, y_ref, g_ref, op_ref, os_ref, *, n_prompt_tiles, dec):
    i = pl.program_id(0)
    out = _rms(x_ref[...] + y_ref[...], g_ref[...])

    @pl.when(i < n_prompt_tiles)
    def _():
        op_ref[...] = out

    @pl.when(i == n_prompt_tiles)
    def _():
        os_ref[...] = out[0:dec]


def _final_norm(x, y, g, tm, m_prompt, dec):
    n_p = m_prompt // tm
    assert n_p * tm == m_prompt and dec <= tm
    return pl.pallas_call(
        functools.partial(_final_norm_kernel, n_prompt_tiles=n_p, dec=dec),
        grid=(n_p + 1,),
        in_specs=[pl.BlockSpec((tm, D_MODEL), lambda i: (i, 0)),
                  pl.BlockSpec((tm, D_MODEL), lambda i: (i, 0)),
                  pl.BlockSpec((1, D_MODEL), lambda i: (0, 0))],
        out_specs=[pl.BlockSpec((tm, D_MODEL), lambda i: (jnp.minimum(i, n_p - 1), 0)),
                   pl.BlockSpec((dec, D_MODEL), lambda i: (0, 0))],
        out_shape=[jax.ShapeDtypeStruct((m_prompt, D_MODEL), f32),
                   jax.ShapeDtypeStruct((dec, D_MODEL), f32)],
        compiler_params=_cparams(("arbitrary",)),
        name="final_norm",
    )(x, y, g)


O_MI = 3 * ML_W
O_MO = O_MI + 2 * ML_HEADS
O_GLR = O_MO + ML_W + 2 * GLA_KW + GLA_VW
O_GO = O_GLR + GLA_RANK
D_IN = O_GO + GLA_VW + POOL_W + N_BRANCH * D_MODEL


def _regroup_kernel(w_ref, main_ref, small_ref):
    w = w_ref[0]
    main_ref[0, :, 0:O_MI] = w[:, 0:O_MI].astype(bf16)
    main_ref[0, :, O_MI:O_MI + O_GLR - O_MO] = w[:, O_MO:O_GLR].astype(bf16)
    main_ref[0, :, O_MI + O_GLR - O_MO:] = w[:, O_GO:].astype(bf16)
    small_ref[0] = jnp.concatenate(
        [w[:, O_MI:O_MO], w[:, O_GLR:O_GO],
         jnp.zeros((w.shape[0], LANES - 2 * ML_HEADS - GLA_RANK), f32)], axis=1).astype(bf16)


def _regroup_w_in(w_in, tr=256):
    depth, d, n = w_in.shape
    assert n == D_IN and Z_MAIN == D_IN - 2 * ML_HEADS - GLA_RANK
    return pl.pallas_call(
        _regroup_kernel,
        grid=(depth, d // tr),
        in_specs=[pl.BlockSpec((1, tr, n), lambda l, i: (l, i, 0))],
        out_specs=[pl.BlockSpec((1, tr, Z_MAIN), lambda l, i: (l, i, 0)),
                   pl.BlockSpec((1, tr, LANES), lambda l, i: (l, i, 0))],
        out_shape=[jax.ShapeDtypeStruct((depth, d, Z_MAIN), bf16),
                   jax.ShapeDtypeStruct((depth, d, LANES), bf16)],
        compiler_params=_cparams(("parallel", "parallel")),
        name="regroup_w_in",
    )(w_in)


def _layer_weights(l, norm1_g, if_bias, ml_g, gla_gw, gla_gb, gla_g, pool_w, pool_scale,
                   w_branch, w_out):
    ifb = jnp.zeros((1, LANES), f32)
    ifb = ifb.at[0, S_MI:S_MI + ML_HEADS].set(if_bias[l, 0]).at[0, S_MF:S_MF + ML_HEADS].set(if_bias[l, 1])
    gw = jnp.zeros((LANES, GLA_KW), f32).at[S_GLR:S_GLR + GLA_RANK].set(gla_gw[l]).astype(bf16)
    return dict(
        norm1_g=norm1_g[l][None], ifb=ifb, ml_g=ml_g[l][None],
        gla_gw=gw, gla_gb=gla_gb[l][None], gla_g=gla_g[l][None], pool_w=pool_w[l].astype(bf16),
        pool_scale=pool_scale[l][None], w_branch=w_branch[l].astype(bf16), w_out=w_out[l].astype(bf16))


def _forward(x_prompt, x_sample, state_mlstm_C, state_mlstm_n, state_mlstm_m, state_gla_S,
             state_pool_buf, norm1_g, w_in, mlstm_if_bias, mlstm_norm_g, gla_gate_w, gla_gate_b,
             gla_norm_g, pool_w, pool_scale, w_branch, w_out, norm2_g, ffn_wg, ffn_wu, ffn_wd,
             router_w, moe_wg, moe_wu, moe_wd, final_norm_g, *, tm, tn, tt, chunk, sub, nb, sb, tf_dense, moe_sub, moe_cap):
    batch, seq, _ = x_prompt.shape
    dec = x_sample.shape[0]
    depth = w_in.shape[0]
    m_prompt = batch * seq
    m_all = m_prompt + dec
    mp = -(-m_all // tm) * tm
    assert m_prompt % tm == 0
    x_parts = (x_prompt.reshape(m_prompt, D_MODEL),
               jnp.concatenate([x_sample.reshape(dec, D_MODEL), jnp.zeros((mp - m_all, D_MODEL), f32)], axis=0))
    outs = [[] for _ in range(5)]
    moe_out = None
    st = (state_mlstm_C, state_mlstm_n, jnp.pad(state_mlstm_m, ((0, 0), (0, 0), (0, LANES - ML_HEADS))),
          state_gla_S, state_pool_buf)
    st_new = None
    w_main, w_small = _regroup_w_in(w_in.astype(bf16))
    for l in range(depth):
        lw = _layer_weights(l, norm1_g, mlstm_if_bias, mlstm_norm_g, gla_gate_w, gla_gate_b,
                            gla_norm_g, pool_w, pool_scale, w_branch, w_out)
        z, zs = _norm_matmul(x_parts, lw["norm1_g"], w_main, w_small, l, tm, tn)
        br_p, c_p, n_p, m_p, s_p, buf_p = _prompt_mix(z, zs, lw, batch, seq, tt, chunk, sub, nb)
        br_t, *st_new = _sample_mix(z, zs, st, st_new, l, lw, m_prompt, dec, sb)
        j = l // 2
        if l % 2 == 0:
            x = _merge(x_parts, br_p, br_t, z, lw["w_branch"], lw["w_out"], tm, m_all)
            x = _ffn(x, norm2_g[l][None], ffn_wg[j].astype(bf16), ffn_wu[j].astype(bf16),
                     ffn_wd[j].astype(bf16), tm, tf_dense)
        else:
            wr_hi = router_w[j].astype(bf16)
            wr_lo = (router_w[j] - wr_hi.astype(f32)).astype(bf16)
            wr = jnp.pad(jnp.concatenate([wr_hi, wr_lo], axis=1), ((0, 0), (0, LANES - 2 * N_EXPERTS)))
            x, hb, comb, rnk, rnkt, cnt = _merge(x_parts, br_p, br_t, z, lw["w_branch"], lw["w_out"], tm, m_all,
                                                 route_w=(norm2_g[l][None], wr))
            n_sup = mp // (tm * moe_sub)
            cnt = cnt[:, 0, :N_EXPERTS].reshape(n_sup, moe_sub, N_EXPERTS).max(axis=1)
            rounds = jnp.ceil(cnt / moe_cap).astype(jnp.int32).reshape(-1)
            moe_out = _moe(hb, rnk, rnkt, comb, rounds, moe_wg[j].astype(bf16), moe_wu[j].astype(bf16),
                           moe_wd[j].astype(bf16), tm, moe_sub, moe_cap)
            if l + 1 < depth:
                x = x + moe_out
                moe_out = None
        x_parts = (x,)
        for lst, val in zip(outs, (c_p, n_p, m_p[:, :ML_HEADS, 0], s_p, buf_p)):
            lst.append(val)
    if moe_out is None:
        moe_out = jnp.zeros_like(x)
    y_prompt, y_sample = _final_norm(x, moe_out, final_norm_g[None], tm, m_prompt, dec)
    c_p, n_p, m_p, s_p, buf_p = (jnp.stack(o) for o in outs)
    c_s, n_s, m_s, s_s, buf_s = st_new
    return (y_prompt.reshape(batch, seq, D_MODEL), y_sample.reshape(dec, 1, D_MODEL),
            c_p, c_s, n_p, n_s, m_p, m_s[:, :, :ML_HEADS], s_p, s_s, buf_p, buf_s)


def kernel(x_prompt, x_sample, state_mlstm_C, state_mlstm_n, state_mlstm_m, state_gla_S, state_pool_buf, norm1_g, w_in, mlstm_if_bias, mlstm_norm_g, gla_gate_w, gla_gate_b, gla_norm_g, pool_w, pool_scale, w_branch, w_out, norm2_g, ffn_wg, ffn_wu, ffn_wd, router_w, moe_wg, moe_wu, moe_wd, final_norm_g):
    return _forward(x_prompt, x_sample, state_mlstm_C, state_mlstm_n, state_mlstm_m, state_gla_S,
                    state_pool_buf, norm1_g, w_in, mlstm_if_bias, mlstm_norm_g, gla_gate_w, gla_gate_b,
                    gla_norm_g, pool_w, pool_scale, w_branch, w_out, norm2_g, ffn_wg, ffn_wu, ffn_wd,
                    router_w, moe_wg, moe_wu, moe_wd, final_norm_g,
                    tm=512, tn=1024, tt=512, chunk=128, sub=64, nb=2, sb=16, tf_dense=MXU_COLS, moe_sub=3, moe_cap=160)
```

```python
import functools

import jax
import jax.numpy as jnp
from jax import lax
from jax.experimental import pallas as pl
from jax.experimental.pallas import tpu as pltpu

f32 = jnp.float32
bf16 = jnp.bfloat16

D_MODEL = 1024
ML_HEADS, ML_DH = 4, 128
ML_W = ML_HEADS * ML_DH
GLA_HEADS, GLA_DK, GLA_DV = 4, 64, 128
GLA_KW, GLA_VW = GLA_HEADS * GLA_DK, GLA_HEADS * GLA_DV
GLA_RANK = 16
GLA_TAU = 16.0
POOL_GDIM = 128
POOL_WINDOWS = (2, 4, 8, 16)
POOL_W = POOL_GDIM * len(POOL_WINDOWS)
POOL_BUF = 15
POOL_BASE = 24
N_BRANCH = 3
N_EXPERTS = 8
EPS = 1e-6
NEG = -1e30
LANES = 128
MXU_COLS = 256

C_MQ, C_MK, C_MV, C_MO = 0, 512, 1024, 1536
C_GQ, C_GK, C_GV, C_GO, C_PU, C_GATES = 2048, 2304, 2560, 3072, 3584, 4096
Z_MAIN = C_GATES + N_BRANCH * D_MODEL
S_MI, S_MF, S_GLR = 0, 4, 8

VMEM_LIMIT = 56 * 1024 * 1024

_NT = (((1,), (1,)), ((), ()))


def _cparams(sem):
    return pltpu.CompilerParams(dimension_semantics=sem, vmem_limit_bytes=VMEM_LIMIT)


def _log_sigmoid(x):
    return jnp.minimum(x, 0.0) - jnp.log(1.0 + jnp.exp(-jnp.abs(x)))


def _sigmoid(x):
    return 0.5 * jnp.tanh(0.5 * x) + 0.5


def _rms(x, g):
    ms = jnp.mean(x * x, axis=-1, keepdims=True)
    return x * lax.rsqrt(ms + EPS) * g


def _lower_tri(n):
    r = lax.broadcasted_iota(jnp.int32, (n, n), 0)
    c = lax.broadcasted_iota(jnp.int32, (n, n), 1)
    return c <= r


def _cumsum_rows(tri_bf16, a):
    a1 = a.astype(bf16)
    r = a - a1.astype(f32)
    a2 = r.astype(bf16)
    a3 = (r - a2.astype(f32)).astype(bf16)
    d = lambda y: jnp.dot(tri_bf16, y, preferred_element_type=f32)
    return d(a1) + d(a2) + d(a3)


def _cumsum_lanes(triu_bf16, a):
    a1 = a.astype(bf16)
    r = a - a1.astype(f32)
    a2 = r.astype(bf16)
    a3 = (r - a2.astype(f32)).astype(bf16)
    d = lambda y: jnp.dot(y, triu_bf16, preferred_element_type=f32)
    return d(a1) + d(a2) + d(a3)


def _resident(shape):
    nd = len(shape)
    return pl.BlockSpec(shape, lambda *_: (0,) * nd, pipeline_mode=pl.Buffered(1))


def _split_rows_specs(parts, tm, width):
    specs, start = [], 0
    for p in parts:
        n_t = p.shape[0] // tm
        assert n_t * tm == p.shape[0]
        specs.append(pl.BlockSpec(
            (tm, width), functools.partial(lambda i, s, n: (jnp.clip(i - s, 0, n - 1), 0), s=start, n=n_t)))
        start += n_t
    return specs


def _pick_rows(refs, starts):
    i = pl.program_id(0)
    x = refs[0][...]
    for r, s in zip(refs[1:], starts[1:]):
        x = jnp.where(i >= s, r[...], x)
    return x


def _norm_matmul_kernel(*refs, tn, starts):
    x_refs = refs[:len(starts)]
    g_ref, w_ref, ws_ref, z_ref, zs_ref = refs[len(starts):]
    h = _rms(_pick_rows(x_refs, starts), g_ref[...]).astype(bf16)
    zs_ref[...] = lax.dot_general(h, ws_ref[0], _NT, preferred_element_type=f32)
    for c in range(w_ref.shape[1] // tn):
        cols = slice(c * tn, (c + 1) * tn)
        z_ref[:, cols] = lax.dot_general(h, w_ref[0, cols, :], _NT,
                                         preferred_element_type=f32).astype(z_ref.dtype)


def _norm_matmul(x_parts, g, wt_main, wt_small, layer, tm, tn):
    mp = sum(p.shape[0] for p in x_parts)
    n = wt_main.shape[1]
    starts = tuple(sum(p.shape[0] for p in x_parts[:j]) // tm for j in range(len(x_parts)))
    layer_block = lambda width: pl.BlockSpec((1, width, D_MODEL), lambda i: (layer, 0, 0),
                                             pipeline_mode=pl.Buffered(1))
    return pl.pallas_call(
        functools.partial(_norm_matmul_kernel, tn=tn, starts=starts),
        grid=(mp // tm,),
        in_specs=[
            *_split_rows_specs(x_parts, tm, D_MODEL),
            _resident((1, D_MODEL)),
            layer_block(n),
            layer_block(LANES),
        ],
        out_specs=[
            pl.BlockSpec((tm, n), lambda i: (i, 0)),
            pl.BlockSpec((tm, LANES), lambda i: (i, 0)),
        ],
        out_shape=[jax.ShapeDtypeStruct((mp, n), bf16), jax.ShapeDtypeStruct((mp, LANES), f32)],
        compiler_params=_cparams(("parallel",)),
        name="norm_matmul",
    )(*x_parts, g, wt_main, wt_small)


def _head_norm(h, g):
    outs = []
    for j in range(h.shape[1] // LANES):
        hj = h[:, j * LANES:(j + 1) * LANES]
        outs.append(hj * lax.rsqrt(jnp.mean(hj * hj, axis=-1, keepdims=True) + EPS))
    return jnp.concatenate(outs, axis=1) * g


def _gla_log_decay(sm, gw_ref, gb_ref):
    xg = jnp.dot(sm.astype(bf16), gw_ref[...], preferred_element_type=f32) + gb_ref[...]
    return _log_sigmoid(xg) * (1.0 / GLA_TAU)


N_ZBLOCKS = 9


def _prompt_mix_kernel(*refs, nb, tt, chunk, sub, n_t):
    z_refs = [refs[s * N_ZBLOCKS:(s + 1) * N_ZBLOCKS] for s in range(nb)]
    ifb_ref, mlg_ref, gw_ref, gb_ref, glag_ref, pw_ref, ps_ref = refs[nb * N_ZBLOCKS:nb * N_ZBLOCKS + 7]
    br_all, c_out, n_out, m_out, s_out, buf_out = refs[nb * N_ZBLOCKS + 7:nb * N_ZBLOCKS + 13]
    scratch = refs[nb * N_ZBLOCKS + 13:]
    per = ML_HEADS + GLA_HEADS + 7
    c_refs = [scratch[s * per:s * per + ML_HEADS] for s in range(nb)]
    s_refs = [scratch[s * per + ML_HEADS:s * per + ML_HEADS + GLA_HEADS] for s in range(nb)]
    n_refs = [scratch[s * per + ML_HEADS + GLA_HEADS] for s in range(nb)]
    m_refs = [scratch[s * per + ML_HEADS + GLA_HEADS + 1] for s in range(nb)]
    ext_refs = [scratch[s * per + ML_HEADS + GLA_HEADS + 2] for s in range(nb)]
    prep_refs = [scratch[s * per + ML_HEADS + GLA_HEADS + 3:(s + 1) * per] for s in range(nb)]
    t_idx = pl.program_id(1)
    L = chunk

    @pl.when(t_idx == 0)
    def _():
        for s in range(nb):
            for r in (*c_refs[s], *s_refs[s], n_refs[s], m_refs[s]):
                r[...] = jnp.zeros_like(r)
            ext_refs[s][0:POOL_BASE, :] = jnp.zeros((POOL_BASE, POOL_W), f32)

    tri = _lower_tri(L)
    tri_b = tri.astype(bf16)
    causal_sub = _lower_tri(sub)
    assert L == LANES
    ones_b = jnp.ones((L, LANES), bf16)
    triu_b = (lax.broadcasted_iota(jnp.int32, (L, L), 0)
              <= lax.broadcasted_iota(jnp.int32, (L, L), 1)).astype(bf16)
    ifb = ifb_ref[...]
    k_scale = ML_DH ** -0.5
    q_scale = GLA_DK ** -0.5

    def gate_prep(c, seq):
        sm_ref = z_refs[seq][8]
        y0_p, bc_p, rows_p, b_p = prep_refs[seq]
        sm = sm_ref[pl.ds(pl.multiple_of(c * L, L), L), :]
        y0 = sm + ifb
        y0_p[...] = y0
        bc_p[...] = _cumsum_rows(tri_b, _log_sigmoid(y0))
        y0t = y0.T[0:8, :]
        rows_p[0:8, :] = y0t
        rows_p[8:16, :] = _cumsum_lanes(triu_b, _log_sigmoid(y0t))
        b_p[...] = _cumsum_rows(tri_b, _gla_log_decay(sm, gw_ref, gb_ref))

    def one_seq(c, seq):
        q_ref, k_ref, v_ref, mo_ref, gqk_ref, gv_ref, go_ref, _, sm_ref = z_refs[seq]
        br_ref, c_s, s_s = br_all.at[seq], c_refs[seq], s_refs[seq]
        n_old, m_old = n_refs[seq][...], m_refs[seq][...]
        n_rows, m_rows = [], []
        r0 = pl.multiple_of(c * L, L)
        rows = pl.ds(r0, L)
        y0_p, bc_p, rows_p, b_p = prep_refs[seq]

        y0, bc = y0_p[...], bc_p[...]
        y0t, bct = rows_p[0:8, :], rows_p[8:16, :]
        b = b_p[...]
        hm = []
        for h in range(ML_HEADS):
            hs = slice(h * ML_DH, (h + 1) * ML_DH)
            v = v_ref[rows, hs].astype(f32)
            kb = k_ref[rows, hs].astype(bf16)
            qb, vb = q_ref[rows, hs].astype(bf16), v.astype(bf16)
            bcol = bc[:, S_MF + h:S_MF + h + 1]
            icol = y0[:, S_MI + h:S_MI + h + 1]
            brow = bct[S_MF + h:S_MF + h + 1, :]
            irow = y0t[S_MI + h:S_MI + h + 1, :]
            m_row = m_old[h:h + 1, :]
            cmat = c_s[h][...]
            nrow = n_old[h:h + 1, :]
            bcol_r = jnp.broadcast_to(bcol, (L, LANES))
            dm = jnp.where(tri, bcol_r - brow + irow, NEG)
            inter_r = bcol_r + m_row
            m_t_r = jnp.maximum(inter_r, jnp.broadcast_to(jnp.max(dm, axis=1, keepdims=True), (L, LANES)))
            w_intra = jnp.exp(dm - m_t_r) * k_scale
            w_inter_r = jnp.exp(inter_r - m_t_r)
            s = lax.dot_general(qb, kb, _NT, preferred_element_type=f32) * w_intra
            s_hi = s.astype(bf16)
            s_lo = (s - s_hi.astype(f32)).astype(bf16)
            r_intra = jnp.dot(s_hi, jnp.concatenate([vb, ones_b], axis=1), preferred_element_type=f32)
            c_aug = jnp.concatenate([cmat, jnp.broadcast_to(nrow, (LANES, LANES))], axis=0).astype(bf16)
            r_inter = lax.dot_general(qb, c_aug, _NT, preferred_element_type=f32)
            rs_lo = jnp.dot(s_lo, ones_b, preferred_element_type=f32)
            num = r_intra[:, 0:LANES] + w_inter_r * r_inter[:, 0:LANES]
            den_r = r_intra[:, LANES:] + rs_lo + w_inter_r * r_inter[:, LANES:]
            hm.append(num / jnp.maximum(jnp.abs(den_r), jnp.exp(-m_t_r)))
            b_last = bcol[L - 1:L, :]
            m = m_row[:, 0:1]
            g = b_last - bcol + icol
            m_new = jnp.maximum(b_last + m, jnp.max(g, axis=0, keepdims=True))
            w_s = jnp.broadcast_to(jnp.exp(g - m_new) * k_scale, (L, LANES))
            w_c = jnp.exp(b_last + m - m_new)
            vwt = (v * w_s).T.astype(bf16)
            c_s[h][...] = w_c * cmat + jnp.dot(vwt, kb, preferred_element_type=f32)
            n_rows.append(w_c * nrow + jnp.sum(w_s * k_ref[rows, hs].astype(f32), axis=0, keepdims=True))
            m_rows.append(jnp.broadcast_to(m_new, (1, LANES)))
        pad_rows = [jnp.zeros((8 - ML_HEADS, LANES), f32)]
        n_refs[seq][...] = jnp.concatenate(n_rows + pad_rows, axis=0)
        m_refs[seq][...] = jnp.concatenate(m_rows + pad_rows, axis=0)
        y_ml = (_head_norm(jnp.concatenate(hm, axis=1), mlg_ref[...])
                * _sigmoid(mo_ref[rows, :].astype(f32)))
        br_ref[rows, 0:ML_W] = y_ml.astype(br_ref.dtype)

        q2 = gqk_ref[rows, 0:GLA_KW].astype(f32) * q_scale
        k2 = gqk_ref[rows, GLA_KW:2 * GLA_KW].astype(f32)
        gv = gv_ref[rows, :].astype(bf16)
        s_old = [s_s[h][...] for h in range(GLA_HEADS)]
        qe_chunk = (q2 * jnp.exp(b)).astype(bf16)
        b_last = b[L - 1:L, :]
        kdt = (k2 * jnp.exp(b_last - b)).T.astype(bf16)
        decay_col = jnp.exp(b.T[:, L - 1:L])
        o_blocks = []
        for blk in range(L // sub):
            s0, s1 = blk * sub, (blk + 1) * sub
            mid = s0 + sub // 2
            b_blk = b[s0:s1]
            b_mid = b[mid - 1:mid, :]
            qe_d = (q2[s0:s1] * jnp.exp(b_blk - b_mid)).astype(bf16)
            ke_d = (k2[s0:s1] * jnp.exp(b_mid - b_blk)).astype(bf16)
            if blk > 0:
                b_start = b[s0 - 1:s0, :]
                qe_o = (q2[s0:s1] * jnp.exp(b_blk - b_start)).astype(bf16)
                ke_o = (k2[0:s0] * jnp.exp(b_start - b[0:s0])).astype(bf16)
            o_heads = []
            for h in range(GLA_HEADS):
                ks = slice(h * GLA_DK, (h + 1) * GLA_DK)
                vs = slice(h * GLA_DV, (h + 1) * GLA_DV)
                a = lax.dot_general(qe_d[:, ks], ke_d[:, ks], _NT, preferred_element_type=f32)
                a = jnp.where(causal_sub, a, 0.0)
                o = (jnp.dot(a.astype(bf16), gv[s0:s1, vs], preferred_element_type=f32)
                     + jnp.dot(qe_chunk[s0:s1, ks], s_old[h].astype(bf16), preferred_element_type=f32))
                if blk > 0:
                    a = lax.dot_general(qe_o[:, ks], ke_o[:, ks], _NT, preferred_element_type=f32)
                    o = o + jnp.dot(a.astype(bf16), gv[0:s0, vs], preferred_element_type=f32)
                o_heads.append(o)
            o_blocks.append(jnp.concatenate(o_heads, axis=1))
        og = jnp.concatenate(o_blocks, axis=0)
        gate_prep(jnp.minimum(c + 1, tt // L - 1), seq)
        for h in range(GLA_HEADS):
            ks = slice(h * GLA_DK, (h + 1) * GLA_DK)
            vs = slice(h * GLA_DV, (h + 1) * GLA_DV)
            s_s[h][...] = (decay_col[ks, :] * s_old[h]
                           + jnp.dot(kdt[ks, :], gv[:, vs], preferred_element_type=f32))
        go = go_ref[rows, :].astype(f32)
        y_gla = _head_norm(og, glag_ref[...]) * (go * _sigmoid(go))
        br_ref[rows, ML_W:ML_W + GLA_VW] = y_gla.astype(br_ref.dtype)

    def chunk_body(c, carry):
        for s in range(nb):
            one_seq(c, s)
        return carry

    for s in range(nb):
        gate_prep(0, s)
    lax.fori_loop(0, tt // L, chunk_body, 0)

    pos = t_idx * tt + lax.broadcasted_iota(jnp.int32, (tt, 1), 0)
    for s in range(nb):
        pu_ref, br_ref, ext_s = z_refs[s][7], br_all.at[s], ext_refs[s]
        n_ext = tt + POOL_BASE
        ext_s[POOL_BASE:n_ext, :] = pu_ref[...].astype(f32)
        for g, w in enumerate(POOL_WINDOWS):
            gs = slice(g * POOL_GDIM, (g + 1) * POOL_GDIM)
            u = ext_s[POOL_BASE:n_ext, gs]
            p, k = ext_s[:, gs], 1
            while k < w:
                p = p + pltpu.roll(p, k, 0)
                k *= 2
            acc = p[POOL_BASE:n_ext, :]
            cnt = jnp.minimum(pos + 1, w).astype(f32)
            d = acc / cnt - u
            y = jnp.dot(d.astype(bf16), pw_ref[g], preferred_element_type=f32) * ps_ref[:, gs]
            br_ref[:, ML_W + GLA_VW + g * POOL_GDIM:ML_W + GLA_VW + (g + 1) * POOL_GDIM] = y.astype(br_ref.dtype)
        ext_s[8:POOL_BASE, :] = ext_s[tt + 8:n_ext, :]

    @pl.when(t_idx == n_t - 1)
    def _():
        for s in range(nb):
            for h in range(ML_HEADS):
                c_out[s, h] = c_refs[s][h][...]
            for h in range(GLA_HEADS):
                s_out[s, h] = s_refs[s][h][...]
            n_out[s] = n_refs[s][0:ML_HEADS, :]
            m_out[s] = m_refs[s][...]
            buf_out[s] = ext_refs[s][POOL_BASE - POOL_BUF:POOL_BASE, :]


def _prompt_mix(z, zs, lw, batch, seq, tt, chunk, sub, nb):
    n_t = seq // tt
    assert batch % nb == 0

    def zspecs(s):
        row = lambda b, t: (b * nb + s) * n_t + t
        spec = lambda col, width: pl.BlockSpec((tt, width), lambda b, t: (row(b, t), col // width))
        return [spec(C_MQ, 512), spec(C_MK, 512), spec(C_MV, 512), spec(C_MO, 512),
                spec(C_GQ, 512), spec(C_GV, 512), spec(C_GO, 512), spec(C_PU, 512),
                pl.BlockSpec((tt, LANES), lambda b, t: (row(b, t), 0))]

    def full(shape):
        nd = len(shape)
        return pl.BlockSpec(shape, lambda b, t: (0,) * nd)

    def per_seq(shape):
        nd = len(shape)
        return pl.BlockSpec((nb,) + shape, lambda b, t: (b,) + (0,) * nd)

    kern = functools.partial(_prompt_mix_kernel, nb=nb, tt=tt, chunk=chunk, sub=sub, n_t=n_t)
    outs = pl.pallas_call(
        kern,
        grid=(batch // nb, n_t),
        in_specs=[sp for s in range(nb) for sp in zspecs(s)]
                 + [full((1, LANES)), full((1, ML_W)), full((LANES, GLA_KW)), full((1, GLA_KW)),
                    full((1, GLA_VW)), full((len(POOL_WINDOWS), POOL_GDIM, POOL_GDIM)), full((1, POOL_W))],
        out_specs=[
            pl.BlockSpec((nb, tt, N_BRANCH * ML_W), lambda b, t: (b, t, 0)),
            per_seq((ML_HEADS, ML_DH, ML_DH)), per_seq((ML_HEADS, ML_DH)), per_seq((8, LANES)),
            per_seq((GLA_HEADS, GLA_DK, GLA_DV)), per_seq((POOL_BUF, POOL_W)),
        ],
        out_shape=[
            jax.ShapeDtypeStruct((batch, seq, N_BRANCH * ML_W), bf16),
            jax.ShapeDtypeStruct((batch, ML_HEADS, ML_DH, ML_DH), f32),
            jax.ShapeDtypeStruct((batch, ML_HEADS, ML_DH), f32),
            jax.ShapeDtypeStruct((batch, 8, LANES), f32),
            jax.ShapeDtypeStruct((batch, GLA_HEADS, GLA_DK, GLA_DV), f32),
            jax.ShapeDtypeStruct((batch, POOL_BUF, POOL_W), f32),
        ],
        scratch_shapes=([pltpu.VMEM((ML_DH, ML_DH), f32)] * ML_HEADS
                        + [pltpu.VMEM((GLA_DK, GLA_DV), f32)] * GLA_HEADS
                        + [pltpu.VMEM((8, LANES), f32), pltpu.VMEM((8, LANES), f32),
                           pltpu.VMEM((tt + POOL_BASE, POOL_W), f32),
                           pltpu.VMEM((chunk, LANES), f32), pltpu.VMEM((chunk, LANES), f32),
                           pltpu.VMEM((16, chunk), f32), pltpu.VMEM((chunk, GLA_KW), f32)]) * nb,
        compiler_params=_cparams(("parallel", "arbitrary")),
        name="prompt_mix",
    )(*([z] * 8 + [zs]) * nb,
      lw["ifb"], lw["ml_g"], lw["gla_gw"], lw["gla_gb"], lw["gla_g"], lw["pool_w"], lw["pool_scale"])
    return (outs[0].reshape(batch * seq, N_BRANCH * ML_W),) + tuple(outs[1:])


def _sample_mix_kernel(q_ref, k_ref, v_ref, mo_ref, gqk_ref, gv_ref, go_ref, pu_ref, sm_ref,
                       c_in, n_in, m_in, s_in, buf_in,
                       ifb_ref, mlg_ref, gw_ref, gb_ref, glag_ref, pw_ref, ps_ref, *tail, sb, dec):
    br_all = tail[-6]
    c_out, n_out, m_out, s_out, buf_out = (r.at[0] for r in tail[-5:])
    step = pl.program_id(0)
    c_in, n_in, m_in, s_in, buf_in = (r.at[0] for r in (c_in, n_in, m_in, s_in, buf_in))
    br_ref = br_all.at[pl.ds(pl.multiple_of(step * sb, sb), sb), :]

    @pl.when(step == 0)
    def _():
        br_all[dec:, :] = jnp.zeros((br_all.shape[0] - dec, br_all.shape[1]), br_all.dtype)

    sm = sm_ref[...]
    eye = (lax.broadcasted_iota(jnp.int32, (LANES, LANES), 0)
           == lax.broadcasted_iota(jnp.int32, (LANES, LANES), 1)).astype(f32)

    def to_cols(x):
        parts = [lax.dot_general(eye, x[:, j * LANES:(j + 1) * LANES], _NT,
                                 preferred_element_type=f32, precision=lax.Precision.HIGHEST)
                 for j in range(x.shape[1] // LANES)]
        return jnp.concatenate(parts, axis=0)

    y0 = sm + ifb_ref[...]
    logf_all = _log_sigmoid(y0)
    k_scale = ML_DH ** -0.5
    m_all = m_in[...]
    hm, m_new_cols = [], []
    for h in range(ML_HEADS):
        hs = slice(h * ML_DH, (h + 1) * ML_DH)
        q = q_ref[:, hs].astype(f32)
        k = k_ref[:, hs].astype(f32) * k_scale
        v = v_ref[:, hs].astype(f32)
        i_pre = y0[:, S_MI + h:S_MI + h + 1]
        logf = logf_all[:, S_MF + h:S_MF + h + 1]
        m = m_all[:, h:h + 1]
        inter = logf + m
        m_t = jnp.maximum(inter, i_pre)
        w_intra = jnp.exp(i_pre - m_t)
        w_inter = jnp.exp(inter - m_t)
        s = jnp.sum(q * k, axis=1, keepdims=True) * w_intra
        qb = q.astype(bf16)
        n_h = n_in[:, h, :]
        cq = jnp.concatenate(
            [lax.dot_general(qb, c_in[j, h].astype(bf16), _NT, preferred_element_type=f32)[j:j + 1, :]
             for j in range(sb)], axis=0)
        num = s * v + w_inter * cq
        den = s + w_inter * jnp.sum(n_h * q, axis=1, keepdims=True)
        hm.append(num / jnp.maximum(jnp.abs(den), jnp.exp(-m_t)))
        m_new = m_t
        w_s = w_intra
        w_c = w_inter
        n_out[:, h, :] = w_c * n_h + w_s * k
        m_new_cols.append(m_new)
        vw_cols = to_cols(v * w_s)
        for j in range(sb):
            c_out[j, h] = w_c[j:j + 1, :] * c_in[j, h] + vw_cols[:, j:j + 1] * k[j:j + 1, :]
    lane = lax.broadcasted_iota(jnp.int32, (sb, LANES), 1)
    m_pack = jnp.zeros((sb, LANES), f32)
    for h in range(ML_HEADS):
        m_pack = jnp.where(lane == h, m_new_cols[h], m_pack)
    m_out[...] = m_pack
    y_ml = _head_norm(jnp.concatenate(hm, axis=1), mlg_ref[...]) * _sigmoid(mo_ref[...].astype(f32))
    br_ref[:, 0:ML_W] = y_ml.astype(br_ref.dtype)

    log_a = _gla_log_decay(sm, gw_ref, gb_ref)
    decay = jnp.exp(log_a)
    q2 = gqk_ref[:, 0:GLA_KW].astype(f32) * (GLA_DK ** -0.5)
    k2 = gqk_ref[:, GLA_KW:2 * GLA_KW].astype(f32)
    gv = gv_ref[...].astype(f32)
    qe = (q2 * decay).astype(bf16)
    qk = q2 * k2
    k_cols = to_cols(k2)
    decay_cols = to_cols(decay)
    og = []
    for h in range(GLA_HEADS):
        ks = slice(h * GLA_DK, (h + 1) * GLA_DK)
        vs = slice(h * GLA_DV, (h + 1) * GLA_DV)
        a = jnp.sum(qk[:, ks], axis=1, keepdims=True)
        inter = jnp.concatenate(
            [jnp.dot(qe[:, ks], s_in[j, h].astype(bf16), preferred_element_type=f32)[j:j + 1, :]
             for j in range(sb)], axis=0)
        og.append(a * gv[:, vs] + inter)
        for j in range(sb):
            s_out[j, h] = (decay_cols[ks, j:j + 1] * s_in[j, h]
                           + k_cols[ks, j:j + 1] * gv[j:j + 1, vs])
    go = go_ref[...].astype(f32)
    y_gla = _head_norm(jnp.concatenate(og, axis=1), glag_ref[...]) * (go * _sigmoid(go))
    br_ref[:, ML_W:ML_W + GLA_VW] = y_gla.astype(br_ref.dtype)

    u = pu_ref[...].astype(f32)
    rowi = lax.broadcasted_iota(jnp.int32, (POOL_BUF + 1, POOL_GDIM), 0)
    d_rows = []
    for j in range(sb):
        ext = jnp.concatenate([buf_in[j], u[j:j + 1, :]], axis=0)
        buf_out[j] = ext[1:POOL_BUF + 1, :]
        parts = []
        for g, w in enumerate(POOL_WINDOWS):
            gs = slice(g * POOL_GDIM, (g + 1) * POOL_GDIM)
            win = jnp.sum(jnp.where(rowi >= POOL_BUF + 1 - w, ext[:, gs], 0.0), axis=0, keepdims=True)
            parts.append(win / float(w) - u[j:j + 1, gs])
        d_rows.append(jnp.concatenate(parts, axis=1))
    d = jnp.concatenate(d_rows, axis=0)
    for g in range(len(POOL_WINDOWS)):
        gs = slice(g * POOL_GDIM, (g + 1) * POOL_GDIM)
        y = jnp.dot(d[:, gs].astype(bf16), pw_ref[g], preferred_element_type=f32) * ps_ref[:, gs]
        br_ref[:, ML_W + GLA_VW + g * POOL_GDIM:ML_W + GLA_VW + (g + 1) * POOL_GDIM] = y.astype(br_ref.dtype)


def _sample_mix(z, zs, st, prev, layer, lw, row0, dec, sb):
    depth = st[0].shape[0]
    base = row0 // sb
    tail_rows = z.shape[0] - row0

    def zspec(col, width):
        blk = col // width
        return pl.BlockSpec((sb, width), lambda i: (base + i, blk))

    def full(shape):
        nd = len(shape)
        return pl.BlockSpec(shape, lambda i: (0,) * nd)

    st_shapes = [(ML_HEADS, ML_DH, ML_DH), (ML_HEADS, ML_DH), (LANES,),
                 (GLA_HEADS, GLA_DK, GLA_DV), (POOL_BUF, POOL_W)]
    st_specs = [pl.BlockSpec((1, sb) + s, functools.partial(lambda i, nd: (layer, i) + (0,) * nd, nd=len(s)))
                for s in st_shapes]
    n_in = N_ZBLOCKS + len(st_shapes) + 7
    prev = () if prev is None else tuple(prev)
    return pl.pallas_call(
        functools.partial(_sample_mix_kernel, sb=sb, dec=dec),
        grid=(dec // sb,),
        in_specs=[zspec(C_MQ, 512), zspec(C_MK, 512), zspec(C_MV, 512), zspec(C_MO, 512),
                  zspec(C_GQ, 512), zspec(C_GV, 512), zspec(C_GO, 512), zspec(C_PU, 512),
                  pl.BlockSpec((sb, LANES), lambda i: (base + i, 0))]
                 + st_specs
                 + [full((1, LANES)), full((1, ML_W)), full((LANES, GLA_KW)), full((1, GLA_KW)),
                    full((1, GLA_VW)), full((len(POOL_WINDOWS), POOL_GDIM, POOL_GDIM)), full((1, POOL_W))]
                 + [pl.BlockSpec(memory_space=pl.ANY)] * len(prev),
        out_specs=[pl.BlockSpec((tail_rows, N_BRANCH * ML_W), lambda i: (0, 0))] + st_specs,
        out_shape=[jax.ShapeDtypeStruct((tail_rows, N_BRANCH * ML_W), bf16)]
                  + [jax.ShapeDtypeStruct((depth, dec) + s, f32) for s in st_shapes],
        input_output_aliases={n_in + j: 1 + j for j in range(len(prev))},
        compiler_params=_cparams(("arbitrary",)),
        name="sample_mix",
    )(z, z, z, z, z, z, z, z, zs, *st,
      lw["ifb"], lw["ml_g"], lw["gla_gw"], lw["gla_gb"], lw["gla_g"], lw["pool_w"], lw["pool_scale"], *prev)


def _merge_kernel(*refs, tm, n_valid, n_prompt_tiles, x_starts, route):
    x_refs = refs[:len(x_starts)]
    brp_ref, brt_ref, g0_ref, g1_ref, g2_ref, wb_ref, wo_ref = refs[len(x_starts):len(x_starts) + 7]
    route_in = refs[len(x_starts) + 7:len(x_starts) + 9] if route else ()
    o_ref = refs[len(x_starts) + 7 + len(route_in)]
    route_out = refs[len(x_starts) + 8 + len(route_in):]
    in_tail = pl.program_id(0) >= n_prompt_tiles
    mixed = None
    for n, g_ref in enumerate((g0_ref, g1_ref, g2_ref)):
        cols = slice(n * ML_W, (n + 1) * ML_W)
        br = jnp.where(in_tail, brt_ref[:, cols], brp_ref[:, cols])
        proj = jnp.dot(br, wb_ref[n], preferred_element_type=f32)
        term = _sigmoid(g_ref[...].astype(f32)) * proj
        mixed = term if mixed is None else mixed + term
    out = (_pick_rows(x_refs, x_starts)
           + jnp.dot(mixed.astype(bf16), wo_ref[...], preferred_element_type=f32))
    row = pl.program_id(0) * tm + lax.broadcasted_iota(jnp.int32, (tm, 1), 0)
    out = jnp.where(row < n_valid, out, 0.0)
    o_ref[...] = out
    if route:
        _route(out, *route_in, *route_out, tm=tm, n_valid=n_valid)


def _merge(x_parts, br_prompt, br_tail, z, w_branch, w_out, tm, n_valid, route_w=None):
    mp = sum(p.shape[0] for p in x_parts)
    nt = mp // tm
    route = route_w is not None
    route_specs = [_resident((1, D_MODEL)), _resident((D_MODEL, LANES))] if route else []
    out_specs = [pl.BlockSpec((tm, D_MODEL), lambda i: (i, 0))]
    out_shape = [jax.ShapeDtypeStruct((mp, D_MODEL), f32)]
    if route:
        out_specs += [pl.BlockSpec((tm, D_MODEL), lambda i: (i, 0)),
                      pl.BlockSpec((tm, LANES), lambda i: (i, 0)),
                      pl.BlockSpec((tm, LANES), lambda i: (i, 0)),
                      pl.BlockSpec((1, N_EXPERTS, tm), lambda i: (i, 0, 0)),
                      pl.BlockSpec((1, 1, LANES), lambda i: (i, 0, 0))]
        out_shape += [jax.ShapeDtypeStruct((mp, D_MODEL), bf16),
                      jax.ShapeDtypeStruct((mp, LANES), f32),
                      jax.ShapeDtypeStruct((mp, LANES), f32),
                      jax.ShapeDtypeStruct((nt, N_EXPERTS, tm), f32),
                      jax.ShapeDtypeStruct((nt, 1, LANES), f32)]
    x_starts = tuple(sum(p.shape[0] for p in x_parts[:j]) // tm for j in range(len(x_parts)))
    n_p = br_prompt.shape[0] // tm
    gate_specs = [pl.BlockSpec((tm, D_MODEL), functools.partial(lambda i, n: (i, C_GATES // D_MODEL + n), n=n))
                  for n in range(N_BRANCH)]
    outs = pl.pallas_call(
        functools.partial(_merge_kernel, tm=tm, n_valid=n_valid, n_prompt_tiles=n_p, x_starts=x_starts,
                          route=route),
        grid=(nt,),
        in_specs=[
            *_split_rows_specs(x_parts, tm, D_MODEL),
            pl.BlockSpec((tm, N_BRANCH * ML_W), lambda i: (jnp.minimum(i, n_p - 1), 0)),
            pl.BlockSpec((tm, N_BRANCH * ML_W), lambda i: (jnp.maximum(i - n_p, 0), 0)),
            *gate_specs,
            _resident((N_BRANCH, ML_W, D_MODEL)),
            _resident((D_MODEL, D_MODEL)),
            *route_specs,
        ],
        out_specs=out_specs,
        out_shape=out_shape,
        compiler_params=_cparams(("parallel",)),
        name="merge_route" if route else "merge",
    )(*x_parts, br_prompt, br_tail, z, z, z, w_branch, w_out, *(route_w or ()))
    return outs if route else outs[0]


def _route(x, g_ref, wr_ref, hb_ref, comb_ref, rnk_ref, rnkt_ref, cnt_ref, *, tm, n_valid):
    h = _rms(x, g_ref[...])
    h_hi = h.astype(bf16)
    hb_ref[...] = h_hi
    h_lo = (h - h_hi.astype(f32)).astype(bf16)
    p_hi = jnp.dot(h_hi, wr_ref[...], preferred_element_type=f32)
    p_lo = jnp.dot(h_lo, wr_ref[...], preferred_element_type=f32)
    logits = p_hi + p_lo + pltpu.roll(p_hi, LANES - N_EXPERTS, 1)
    lane = lax.broadcasted_iota(jnp.int32, logits.shape, 1)
    valid = lane < N_EXPERTS
    logits = jnp.where(valid, logits, NEG)
    mx = jnp.max(logits, axis=1, keepdims=True)
    e = jnp.where(valid, jnp.exp(logits - mx), 0.0)
    probs = e / jnp.sum(e, axis=1, keepdims=True)
    p1 = jnp.max(probs, axis=1, keepdims=True)
    i1 = jnp.min(jnp.where(probs == p1, lane, LANES), axis=1, keepdims=True)
    rest = jnp.where((lane == i1) | ~valid, -1.0, probs)
    p2 = jnp.max(rest, axis=1, keepdims=True)
    i2 = jnp.min(jnp.where(rest == p2, lane, LANES), axis=1, keepdims=True)
    tot = p1 + p2
    comb_ref[...] = jnp.where(lane == i1, p1 / tot, 0.0) + jnp.where(lane == i2, p2 / tot, 0.0)
    row = pl.program_id(0) * tm + lax.broadcasted_iota(jnp.int32, logits.shape, 0)
    sel = ((lane == i1) | (lane == i2)) & (row < n_valid)
    r = lax.broadcasted_iota(jnp.int32, (tm, tm), 0)
    c = lax.broadcasted_iota(jnp.int32, (tm, tm), 1)
    rank = jnp.dot((c < r).astype(bf16), sel.astype(bf16), preferred_element_type=f32)
    rnk = jnp.where(sel, rank, -1.0)
    rnk_ref[...] = rnk
    rnkt_ref[0] = rnk.T[0:N_EXPERTS, :]
    cnt_ref[0] = jnp.sum(sel.astype(f32), axis=0, keepdims=True)


def _moe_kernel(rounds_ref, hb_ref, rnk_ref, rnkt_ref, comb_ref, wg_ref, wu_ref, wd_ref, o_ref,
                *, cap, n_sub, ts):
    i, e = pl.program_id(0), pl.program_id(1)

    @pl.when(e == 0)
    def _():
        o_ref[...] = jnp.zeros_like(o_ref)

    lane = lax.broadcasted_iota(jnp.int32, (ts, LANES), 1)
    slot_rows = lax.broadcasted_iota(jnp.int32, (cap, ts), 0).astype(f32)
    slot_cols = lax.broadcasted_iota(jnp.int32, (ts, cap), 1).astype(f32)

    def round_body(r, carry):
        base = (r * cap).astype(f32)
        xs = []
        for j in range(n_sub):
            rt = rnkt_ref[j, pl.ds(e, 1), :]
            p = (rt - base == slot_rows).astype(bf16)
            xs.append(jnp.dot(p, hb_ref[j * ts:(j + 1) * ts, :],
                              preferred_element_type=f32).astype(bf16))
        xs = jnp.concatenate(xs, axis=0)
        a = jnp.dot(xs, wg_ref[0], preferred_element_type=f32)
        a = (a * _sigmoid(a)) * jnp.dot(xs, wu_ref[0], preferred_element_type=f32)
        y = jnp.dot(a.astype(bf16), wd_ref[0], preferred_element_type=f32).astype(bf16)
        for j in range(n_sub):
            rows = slice(j * ts, (j + 1) * ts)
            col = jnp.sum(jnp.where(lane == e, rnk_ref[rows, :], 0.0), axis=1, keepdims=True)
            w = jnp.sum(jnp.where(lane == e, comb_ref[rows, :], 0.0), axis=1, keepdims=True)
            pt = (col - base == slot_cols).astype(bf16)
            o_ref[rows, :] += w * jnp.dot(pt, y[j * cap:(j + 1) * cap, :], preferred_element_type=f32)
        return carry

    lax.fori_loop(0, rounds_ref[i * N_EXPERTS + e], round_body, 0)


def _moe(hb, rnk, rnkt, comb, rounds, wg, wu, wd, ts, n_sub, cap):
    mp = hb.shape[0]
    n_e, _, d_ff = wg.shape
    tsup = ts * n_sub
    grid_spec = pltpu.PrefetchScalarGridSpec(
        num_scalar_prefetch=1,
        grid=(mp // tsup, n_e),
        in_specs=[pl.BlockSpec((tsup, D_MODEL), lambda i, e, r: (i, 0)),
                  pl.BlockSpec((tsup, LANES), lambda i, e, r: (i, 0)),
                  pl.BlockSpec((n_sub, N_EXPERTS, ts), lambda i, e, r: (i, 0, 0)),
                  pl.BlockSpec((tsup, LANES), lambda i, e, r: (i, 0)),
                  pl.BlockSpec((1, D_MODEL, d_ff), lambda i, e, r: (e, 0, 0)),
                  pl.BlockSpec((1, D_MODEL, d_ff), lambda i, e, r: (e, 0, 0)),
                  pl.BlockSpec((1, d_ff, D_MODEL), lambda i, e, r: (e, 0, 0))],
        out_specs=pl.BlockSpec((tsup, D_MODEL), lambda i, e, r: (i, 0)),
    )
    return pl.pallas_call(
        functools.partial(_moe_kernel, cap=cap, n_sub=n_sub, ts=ts),
        grid_spec=grid_spec,
        out_shape=jax.ShapeDtypeStruct((mp, D_MODEL), f32),
        compiler_params=_cparams(("parallel", "arbitrary")),
        name="moe",
    )(rounds, hb, rnk, rnkt, comb, wg, wu, wd)


def _ffn_kernel(x_ref, g_ref, wg_ref, wu_ref, wd_ref, o_ref, *, tf):
    x = x_ref[...]
    h = _rms(x, g_ref[...]).astype(bf16)
    acc = x
    for c in range(wg_ref.shape[1] // tf):
        cols = slice(c * tf, (c + 1) * tf)
        a = jnp.dot(h, wg_ref[:, cols], preferred_element_type=f32)
        a = (a * _sigmoid(a)) * jnp.dot(h, wu_ref[:, cols], preferred_element_type=f32)
        acc = acc + jnp.dot(a.astype(bf16), wd_ref[cols, :], preferred_element_type=f32)
    o_ref[...] = acc


def _ffn(x, g, wg, wu, wd, tm, tf):
    mp = x.shape[0]
    d_ff = wg.shape[1]
    return pl.pallas_call(
        functools.partial(_ffn_kernel, tf=tf),
        grid=(mp // tm,),
        in_specs=[pl.BlockSpec((tm, D_MODEL), lambda i: (i, 0)),
                  _resident((1, D_MODEL)),
                  _resident((D_MODEL, d_ff)),
                  _resident((D_MODEL, d_ff)),
                  _resident((d_ff, D_MODEL))],
        out_specs=pl.BlockSpec((tm, D_MODEL), lambda i: (i, 0)),
        out_shape=jax.ShapeDtypeStruct((mp, D_MODEL), f32),
        compiler_params=_cparams(("parallel",)),
        name="ffn",
    )(x, g, wg, wu, wd)


def _final_norm_kernel(x_ref, y_ref, g_ref, op_ref, os_ref, *, n_prompt_tiles, dec):
    i = pl.program_id(0)
    out = _rms(x_ref[...] + y_ref[...], g_ref[...])

    @pl.when(i < n_prompt_tiles)
    def _():
        op_ref[...] = out

    @pl.when(i == n_prompt_tiles)
    def _():
        os_ref[...] = out[0:dec]


def _final_norm(x, y, g, tm, m_prompt, dec):
    n_p = m_prompt // tm
    assert n_p * tm == m_prompt and dec <= tm
    return pl.pallas_call(
        functools.partial(_final_norm_kernel, n_prompt_tiles=n_p, dec=dec),
        grid=(n_p + 1,),
        in_specs=[pl.BlockSpec((tm, D_MODEL), lambda i: (i, 0)),
                  pl.BlockSpec((tm, D_MODEL), lambda i: (i, 0)),
                  pl.BlockSpec((1, D_MODEL), lambda i: (0, 0))],
        out_specs=[pl.BlockSpec((tm, D_MODEL), lambda i: (jnp.minimum(i, n_p - 1), 0)),
                   pl.BlockSpec((dec, D_MODEL), lambda i: (0, 0))],
        out_shape=[jax.ShapeDtypeStruct((m_prompt, D_MODEL), f32),
                   jax.ShapeDtypeStruct((dec, D_MODEL), f32)],
        compiler_params=_cparams(("arbitrary",)),
        name="final_norm",
    )(x, y, g)


O_MI = 3 * ML_W
O_MO = O_MI + 2 * ML_HEADS
O_GLR = O_MO + ML_W + 2 * GLA_KW + GLA_VW
O_GO = O_GLR + GLA_RANK
D_IN = O_GO + GLA_VW + POOL_W + N_BRANCH * D_MODEL


def _regroup_kernel(w_ref, if_ref, glr_ref, main_ref, small_ref):
    main_ref[0] = w_ref[0].astype(bf16)
    n_if, n_glr = if_ref.shape[1], glr_ref.shape[1]
    small_ref[0, 0:n_if, :] = if_ref[0].astype(bf16)
    small_ref[0, n_if:n_if + n_glr, :] = glr_ref[0].astype(bf16)
    small_ref[0, n_if + n_glr:, :] = jnp.zeros((LANES - n_if - n_glr, small_ref.shape[2]), bf16)


def _regroup_w_in(wt, tr=512):
    depth, n, d = wt.shape
    assert n == D_IN and Z_MAIN == D_IN - 2 * ML_HEADS - GLA_RANK
    assert O_MI % tr == 0 and (O_GLR - O_MO) % tr == 0 and (D_IN - O_GO) % tr == 0
    c1, c2 = O_MI // tr, (O_MI + O_GLR - O_MO) // tr

    def src_row(c):
        skip = jnp.where(c < c1, 0, jnp.where(c < c2, O_MO - O_MI, O_MO - O_MI + O_GO - O_GLR))
        return pl.multiple_of(c * tr + skip, 8)

    return pl.pallas_call(
        _regroup_kernel,
        grid=(depth, Z_MAIN // tr),
        in_specs=[pl.BlockSpec((pl.Element(1), pl.Element(rows), pl.Element(d)), index_map)
                  for rows, index_map in ((tr, lambda l, c: (l, src_row(c), 0)),
                                          (O_MO - O_MI, lambda l, c: (l, O_MI, 0)),
                                          (O_GO - O_GLR, lambda l, c: (l, O_GLR, 0)))],
        out_specs=[pl.BlockSpec((1, tr, d), lambda l, c: (l, c, 0)),
                   pl.BlockSpec((1, LANES, d), lambda l, c: (l, 0, 0))],
        out_shape=[jax.ShapeDtypeStruct((depth, Z_MAIN, d), bf16),
                   jax.ShapeDtypeStruct((depth, LANES, d), bf16)],
        compiler_params=_cparams(("parallel", "arbitrary")),
        name="regroup_w_in",
    )(wt, wt, wt)


def _layer_weights(l, norm1_g, if_bias, ml_g, gla_gw, gla_gb, gla_g, pool_w, pool_scale,
                   w_branch, w_out):
    ifb = jnp.zeros((1, LANES), f32)
    ifb = ifb.at[0, S_MI:S_MI + ML_HEADS].set(if_bias[l, 0]).at[0, S_MF:S_MF + ML_HEADS].set(if_bias[l, 1])
    gw = jnp.zeros((LANES, GLA_KW), f32).at[S_GLR:S_GLR + GLA_RANK].set(gla_gw[l]).astype(bf16)
    return dict(
        norm1_g=norm1_g[l][None], ifb=ifb, ml_g=ml_g[l][None],
        gla_gw=gw, gla_gb=gla_gb[l][None], gla_g=gla_g[l][None], pool_w=pool_w[l].astype(bf16),
        pool_scale=pool_scale[l][None], w_branch=w_branch[l].astype(bf16), w_out=w_out[l].astype(bf16))


def _forward(x_prompt, x_sample, state_mlstm_C, state_mlstm_n, state_mlstm_m, state_gla_S,
             state_pool_buf, norm1_g, w_in, mlstm_if_bias, mlstm_norm_g, gla_gate_w, gla_gate_b,
             gla_norm_g, pool_w, pool_scale, w_branch, w_out, norm2_g, ffn_wg, ffn_wu, ffn_wd,
             router_w, moe_wg, moe_wu, moe_wd, final_norm_g, *, tm, tn, tt, chunk, sub, nb, sb, tf_dense, moe_sub, moe_cap):
    batch, seq, _ = x_prompt.shape
    dec = x_sample.shape[0]
    depth = w_in.shape[0]
    m_prompt = batch * seq
    m_all = m_prompt + dec
    mp = -(-m_all // tm) * tm
    assert m_prompt % tm == 0
    x_parts = (x_prompt.reshape(m_prompt, D_MODEL),
               jnp.concatenate([x_sample.reshape(dec, D_MODEL), jnp.zeros((mp - m_all, D_MODEL), f32)], axis=0))
    outs = [[] for _ in range(5)]
    moe_out = None
    st = (state_mlstm_C, state_mlstm_n, jnp.pad(state_mlstm_m, ((0, 0), (0, 0), (0, LANES - ML_HEADS))),
          state_gla_S, state_pool_buf)
    st_new = None
    w_main, w_small = _regroup_w_in(jnp.swapaxes(w_in, 1, 2))
    for l in range(depth):
        lw = _layer_weights(l, norm1_g, mlstm_if_bias, mlstm_norm_g, gla_gate_w, gla_gate_b,
                            gla_norm_g, pool_w, pool_scale, w_branch, w_out)
        z, zs = _norm_matmul(x_parts, lw["norm1_g"], w_main, w_small, l, tm, tn)
        br_p, c_p, n_p, m_p, s_p, buf_p = _prompt_mix(z, zs, lw, batch, seq, tt, chunk, sub, nb)
        br_t, *st_new = _sample_mix(z, zs, st, st_new, l, lw, m_prompt, dec, sb)
        j = l // 2
        if l % 2 == 0:
            x = _merge(x_parts, br_p, br_t, z, lw["w_branch"], lw["w_out"], tm, m_all)
            x = _ffn(x, norm2_g[l][None], ffn_wg[j].astype(bf16), ffn_wu[j].astype(bf16),
                     ffn_wd[j].astype(bf16), tm, tf_dense)
        else:
            wr_hi = router_w[j].astype(bf16)
            wr_lo = (router_w[j] - wr_hi.astype(f32)).astype(bf16)
            wr = jnp.pad(jnp.concatenate([wr_hi, wr_lo], axis=1), ((0, 0), (0, LANES - 2 * N_EXPERTS)))
            x, hb, comb, rnk, rnkt, cnt = _merge(x_parts, br_p, br_t, z, lw["w_branch"], lw["w_out"], tm, m_all,
                                                 route_w=(norm2_g[l][None], wr))
            n_sup = mp // (tm * moe_sub)
            cnt = cnt[:, 0, :N_EXPERTS].reshape(n_sup, moe_sub, N_EXPERTS).max(axis=1)
            rounds = jnp.ceil(cnt / moe_cap).astype(jnp.int32).reshape(-1)
            moe_out = _moe(hb, rnk, rnkt, comb, rounds, moe_wg[j].astype(bf16), moe_wu[j].astype(bf16),
                           moe_wd[j].astype(bf16), tm, moe_sub, moe_cap)
            if l + 1 < depth:
                x = x + moe_out
                moe_out = None
        x_parts = (x,)
        for lst, val in zip(outs, (c_p, n_p, m_p[:, :ML_HEADS, 0], s_p, buf_p)):
            lst.append(val)
    if moe_out is None:
        moe_out = jnp.zeros_like(x)
    y_prompt, y_sample = _final_norm(x, moe_out, final_norm_g[None], tm, m_prompt, dec)
    c_p, n_p, m_p, s_p, buf_p = (jnp.stack(o) for o in outs)
    c_s, n_s, m_s, s_s, buf_s = st_new
    return (y_prompt.reshape(batch, seq, D_MODEL), y_sample.reshape(dec, 1, D_MODEL),
            c_p, c_s, n_p, n_s, m_p, m_s[:, :, :ML_HEADS], s_p, s_s, buf_p, buf_s)


def kernel(x_prompt, x_sample, state_mlstm_C, state_mlstm_n, state_mlstm_m, state_gla_S, state_pool_buf, norm1_g, w_in, mlstm_if_bias, mlstm_norm_g, gla_gate_w, gla_gate_b, gla_norm_g, pool_w, pool_scale, w_branch, w_out, norm2_g, ffn_wg, ffn_wu, ffn_wd, router_w, moe_wg, moe_wu, moe_wd, final_norm_g):
    return _forward(x_prompt, x_sample, state_mlstm_C, state_mlstm_n, state_mlstm_m, state_gla_S,
                    state_pool_buf, norm1_g, w_in, mlstm_if_bias, mlstm_norm_g, gla_gate_w, gla_gate_b,
                    gla_norm_g, pool_w, pool_scale, w_branch, w_out, norm2_g, ffn_wg, ffn_wu, ffn_wd,
                    router_w, moe_wg, moe_wu, moe_wd, final_norm_g,
                    tm=512, tn=1024, tt=512, chunk=128, sub=64, nb=2, sb=16, tf_dense=MXU_COLS, moe_sub=3, moe_cap=160)
```

```python
import functools

import jax
import jax.numpy as jnp
from jax import lax
from jax.experimental import pallas as pl
from jax.experimental.pallas import tpu as pltpu

f32 = jnp.float32
bf16 = jnp.bfloat16

D_MODEL = 1024
ML_HEADS, ML_DH = 4, 128
ML_W = ML_HEADS * ML_DH
GLA_HEADS, GLA_DK, GLA_DV = 4, 64, 128
GLA_KW, GLA_VW = GLA_HEADS * GLA_DK, GLA_HEADS * GLA_DV
GLA_RANK = 16
GLA_TAU = 16.0
POOL_GDIM = 128
POOL_WINDOWS = (2, 4, 8, 16)
POOL_W = POOL_GDIM * len(POOL_WINDOWS)
POOL_BUF = 15
POOL_BASE = 24
N_BRANCH = 3
N_EXPERTS = 8
EPS = 1e-6
NEG = -1e30
LANES = 128
MXU_COLS = 256

C_MQ, C_MK, C_MV, C_MO = 0, 512, 1024, 1536
C_GQ, C_GK, C_GV, C_GO, C_PU, C_GATES = 2048, 2304, 2560, 3072, 3584, 4096
Z_MAIN = C_GATES + N_BRANCH * D_MODEL
S_MI, S_MF, S_GLR = 0, 4, 8

VMEM_LIMIT = 56 * 1024 * 1024

_NT = (((1,), (1,)), ((), ()))


def _cparams(sem):
    return pltpu.CompilerParams(dimension_semantics=sem, vmem_limit_bytes=VMEM_LIMIT)


def _log_sigmoid(x):
    return jnp.minimum(x, 0.0) - jnp.log(1.0 + jnp.exp(-jnp.abs(x)))


def _sigmoid(x):
    return 0.5 * jnp.tanh(0.5 * x) + 0.5


def _rms(x, g):
    ms = jnp.mean(x * x, axis=-1, keepdims=True)
    return x * lax.rsqrt(ms + EPS) * g


def _lower_tri(n):
    r = lax.broadcasted_iota(jnp.int32, (n, n), 0)
    c = lax.broadcasted_iota(jnp.int32, (n, n), 1)
    return c <= r


def _cumsum_rows(tri_bf16, a):
    a1 = a.astype(bf16)
    r = a - a1.astype(f32)
    a2 = r.astype(bf16)
    a3 = (r - a2.astype(f32)).astype(bf16)
    d = lambda y: jnp.dot(tri_bf16, y, preferred_element_type=f32)
    return d(a1) + d(a2) + d(a3)


def _cumsum_lanes(triu_bf16, a):
    a1 = a.astype(bf16)
    r = a - a1.astype(f32)
    a2 = r.astype(bf16)
    a3 = (r - a2.astype(f32)).astype(bf16)
    d = lambda y: jnp.dot(y, triu_bf16, preferred_element_type=f32)
    return d(a1) + d(a2) + d(a3)


def _resident(shape):
    nd = len(shape)
    return pl.BlockSpec(shape, lambda *_: (0,) * nd, pipeline_mode=pl.Buffered(1))


def _split_rows_specs(parts, tm, width):
    specs, start = [], 0
    for p in parts:
        n_t = p.shape[0] // tm
        assert n_t * tm == p.shape[0]
        specs.append(pl.BlockSpec(
            (tm, width), functools.partial(lambda i, s, n: (jnp.clip(i - s, 0, n - 1), 0), s=start, n=n_t)))
        start += n_t
    return specs


def _pick_rows(refs, starts):
    i = pl.program_id(0)
    x = refs[0][...]
    for r, s in zip(refs[1:], starts[1:]):
        x = jnp.where(i >= s, r[...], x)
    return x


def _norm_matmul_kernel(*refs, tn, starts):
    x_refs = refs[:len(starts)]
    g_ref, w_ref, ws_ref, z_ref, zs_ref = refs[len(starts):]
    h = _rms(_pick_rows(x_refs, starts), g_ref[...]).astype(bf16)
    zs_ref[...] = lax.dot_general(h, ws_ref[0], _NT, preferred_element_type=f32)
    for c in range(w_ref.shape[1] // tn):
        cols = slice(c * tn, (c + 1) * tn)
        z_ref[:, cols] = lax.dot_general(h, w_ref[0, cols, :], _NT,
                                         preferred_element_type=f32).astype(z_ref.dtype)


def _norm_matmul(x_parts, g, wt_main, wt_small, layer, tm, tn):
    mp = sum(p.shape[0] for p in x_parts)
    n = wt_main.shape[1]
    starts = tuple(sum(p.shape[0] for p in x_parts[:j]) // tm for j in range(len(x_parts)))
    layer_block = lambda width: pl.BlockSpec((1, width, D_MODEL), lambda i: (layer, 0, 0),
                                             pipeline_mode=pl.Buffered(1))
    return pl.pallas_call(
        functools.partial(_norm_matmul_kernel, tn=tn, starts=starts),
        grid=(mp // tm,),
        in_specs=[
            *_split_rows_specs(x_parts, tm, D_MODEL),
            _resident((1, D_MODEL)),
            layer_block(n),
            layer_block(LANES),
        ],
        out_specs=[
            pl.BlockSpec((tm, n), lambda i: (i, 0)),
            pl.BlockSpec((tm, LANES), lambda i: (i, 0)),
        ],
        out_shape=[jax.ShapeDtypeStruct((mp, n), bf16), jax.ShapeDtypeStruct((mp, LANES), f32)],
        compiler_params=_cparams(("parallel",)),
        name="norm_matmul",
    )(*x_parts, g, wt_main, wt_small)


def _head_norm(h, g):
    outs = []
    for j in range(h.shape[1] // LANES):
        hj = h[:, j * LANES:(j + 1) * LANES]
        outs.append(hj * lax.rsqrt(jnp.mean(hj * hj, axis=-1, keepdims=True) + EPS))
    return jnp.concatenate(outs, axis=1) * g


def _gla_log_decay(sm, gw_ref, gb_ref):
    xg = jnp.dot(sm.astype(bf16), gw_ref[...], preferred_element_type=f32) + gb_ref[...]
    return _log_sigmoid(xg) * (1.0 / GLA_TAU)


N_ZBLOCKS = 9


def _prompt_mix_kernel(*refs, nb, tt, chunk, sub, n_t):
    z_refs = [refs[s * N_ZBLOCKS:(s + 1) * N_ZBLOCKS] for s in range(nb)]
    ifb_ref, mlg_ref, gw_ref, gb_ref, glag_ref, pw_ref, ps_ref = refs[nb * N_ZBLOCKS:nb * N_ZBLOCKS + 7]
    br_all, c_out, n_out, m_out, s_out, buf_out = refs[nb * N_ZBLOCKS + 7:nb * N_ZBLOCKS + 13]
    scratch = refs[nb * N_ZBLOCKS + 13:]
    per = ML_HEADS + GLA_HEADS + 7
    c_refs = [scratch[s * per:s * per + ML_HEADS] for s in range(nb)]
    s_refs = [scratch[s * per + ML_HEADS:s * per + ML_HEADS + GLA_HEADS] for s in range(nb)]
    n_refs = [scratch[s * per + ML_HEADS + GLA_HEADS] for s in range(nb)]
    m_refs = [scratch[s * per + ML_HEADS + GLA_HEADS + 1] for s in range(nb)]
    ext_refs = [scratch[s * per + ML_HEADS + GLA_HEADS + 2] for s in range(nb)]
    prep_refs = [scratch[s * per + ML_HEADS + GLA_HEADS + 3:(s + 1) * per] for s in range(nb)]
    t_idx = pl.program_id(1)
    L = chunk

    @pl.when(t_idx == 0)
    def _():
        for s in range(nb):
            for r in (*c_refs[s], *s_refs[s], n_refs[s], m_refs[s]):
                r[...] = jnp.zeros_like(r)
            ext_refs[s][0:POOL_BASE, :] = jnp.zeros((POOL_BASE, POOL_W), f32)

    tri = _lower_tri(L)
    tri_b = tri.astype(bf16)
    causal_sub = _lower_tri(sub)
    assert L == LANES
    ones_b = jnp.ones((L, LANES), bf16)
    triu_b = (lax.broadcasted_iota(jnp.int32, (L, L), 0)
              <= lax.broadcasted_iota(jnp.int32, (L, L), 1)).astype(bf16)
    ifb = ifb_ref[...]
    k_scale = ML_DH ** -0.5
    q_scale = GLA_DK ** -0.5

    def gate_prep(c, seq):
        sm_ref = z_refs[seq][8]
        y0_p, bc_p, rows_p, b_p = prep_refs[seq]
        sm = sm_ref[pl.ds(pl.multiple_of(c * L, L), L), :]
        y0 = sm + ifb
        y0_p[...] = y0
        bc_p[...] = _cumsum_rows(tri_b, _log_sigmoid(y0))
        y0t = y0.T[0:8, :]
        rows_p[0:8, :] = y0t
        rows_p[8:16, :] = _cumsum_lanes(triu_b, _log_sigmoid(y0t))
        b_p[...] = _cumsum_rows(tri_b, _gla_log_decay(sm, gw_ref, gb_ref))

    def one_seq(c, seq):
        q_ref, k_ref, v_ref, mo_ref, gqk_ref, gv_ref, go_ref, _, sm_ref = z_refs[seq]
        br_ref, c_s, s_s = br_all.at[seq], c_refs[seq], s_refs[seq]
        n_old, m_old = n_refs[seq][...], m_refs[seq][...]
        n_rows, m_rows = [], []
        r0 = pl.multiple_of(c * L, L)
        rows = pl.ds(r0, L)
        y0_p, bc_p, rows_p, b_p = prep_refs[seq]

        y0, bc = y0_p[...], bc_p[...]
        y0t, bct = rows_p[0:8, :], rows_p[8:16, :]
        b = b_p[...]
        hm = []
        for h in range(ML_HEADS):
            hs = slice(h * ML_DH, (h + 1) * ML_DH)
            v = v_ref[rows, hs].astype(f32)
            kb = k_ref[rows, hs].astype(bf16)
            qb, vb = q_ref[rows, hs].astype(bf16), v.astype(bf16)
            bcol = bc[:, S_MF + h:S_MF + h + 1]
            icol = y0[:, S_MI + h:S_MI + h + 1]
            brow = bct[S_MF + h:S_MF + h + 1, :]
            irow = y0t[S_MI + h:S_MI + h + 1, :]
            m_row = m_old[h:h + 1, :]
            cmat = c_s[h][...]
            nrow = n_old[h:h + 1, :]
            bcol_r = jnp.broadcast_to(bcol, (L, LANES))
            dm = jnp.where(tri, bcol_r - brow + irow, NEG)
            inter_r = bcol_r + m_row
            m_t_r = jnp.maximum(inter_r, jnp.broadcast_to(jnp.max(dm, axis=1, keepdims=True), (L, LANES)))
            w_intra = jnp.exp(dm - m_t_r) * k_scale
            w_inter_r = jnp.exp(inter_r - m_t_r)
            s = lax.dot_general(qb, kb, _NT, preferred_element_type=f32) * w_intra
            s_hi = s.astype(bf16)
            s_lo = (s - s_hi.astype(f32)).astype(bf16)
            r_intra = jnp.dot(s_hi, jnp.concatenate([vb, ones_b], axis=1), preferred_element_type=f32)
            c_aug = jnp.concatenate([cmat, jnp.broadcast_to(nrow, (LANES, LANES))], axis=0).astype(bf16)
            r_inter = lax.dot_general(qb, c_aug, _NT, preferred_element_type=f32)
            rs_lo = jnp.dot(s_lo, ones_b, preferred_element_type=f32)
            num = r_intra[:, 0:LANES] + w_inter_r * r_inter[:, 0:LANES]
            den_r = r_intra[:, LANES:] + rs_lo + w_inter_r * r_inter[:, LANES:]
            hm.append(num / jnp.maximum(jnp.abs(den_r), jnp.exp(-m_t_r)))
            b_last = bcol[L - 1:L, :]
            m = m_row[:, 0:1]
            g = b_last - bcol + icol
            m_new = jnp.maximum(b_last + m, jnp.max(g, axis=0, keepdims=True))
            w_s = jnp.broadcast_to(jnp.exp(g - m_new) * k_scale, (L, LANES))
            w_c = jnp.exp(b_last + m - m_new)
            vwt = (v * w_s).T.astype(bf16)
            c_s[h][...] = w_c * cmat + jnp.dot(vwt, kb, preferred_element_type=f32)
            n_rows.append(w_c * nrow + jnp.sum(w_s * k_ref[rows, hs].astype(f32), axis=0, keepdims=True))
            m_rows.append(jnp.broadcast_to(m_new, (1, LANES)))
        pad_rows = [jnp.zeros((8 - ML_HEADS, LANES), f32)]
        n_refs[seq][...] = jnp.concatenate(n_rows + pad_rows, axis=0)
        m_refs[seq][...] = jnp.concatenate(m_rows + pad_rows, axis=0)
        y_ml = (_head_norm(jnp.concatenate(hm, axis=1), mlg_ref[...])
                * _sigmoid(mo_ref[rows, :].astype(f32)))
        br_ref[rows, 0:ML_W] = y_ml.astype(br_ref.dtype)

        q2 = gqk_ref[rows, 0:GLA_KW].astype(f32) * q_scale
        k2 = gqk_ref[rows, GLA_KW:2 * GLA_KW].astype(f32)
        gv = gv_ref[rows, :].astype(bf16)
        s_old = [s_s[h][...] for h in range(GLA_HEADS)]
        qe_chunk = (q2 * jnp.exp(b)).astype(bf16)
        b_last = b[L - 1:L, :]
        kdt = (k2 * jnp.exp(b_last - b)).T.astype(bf16)
        decay_col = jnp.exp(b.T[:, L - 1:L])
        o_blocks = []
        for blk in range(L // sub):
            s0, s1 = blk * sub, (blk + 1) * sub
            mid = s0 + sub // 2
            b_blk = b[s0:s1]
            b_mid = b[mid - 1:mid, :]
            qe_d = (q2[s0:s1] * jnp.exp(b_blk - b_mid)).astype(bf16)
            ke_d = (k2[s0:s1] * jnp.exp(b_mid - b_blk)).astype(bf16)
            if blk > 0:
                b_start = b[s0 - 1:s0, :]
                qe_o = (q2[s0:s1] * jnp.exp(b_blk - b_start)).astype(bf16)
                ke_o = (k2[0:s0] * jnp.exp(b_start - b[0:s0])).astype(bf16)
            o_heads = []
            for h in range(GLA_HEADS):
                ks = slice(h * GLA_DK, (h + 1) * GLA_DK)
                vs = slice(h * GLA_DV, (h + 1) * GLA_DV)
                a = lax.dot_general(qe_d[:, ks], ke_d[:, ks], _NT, preferred_element_type=f32)
                a = jnp.where(causal_sub, a, 0.0)
                o = (jnp.dot(a.astype(bf16), gv[s0:s1, vs], preferred_element_type=f32)
                     + jnp.dot(qe_chunk[s0:s1, ks], s_old[h].astype(bf16), preferred_element_type=f32))
                if blk > 0:
                    a = lax.dot_general(qe_o[:, ks], ke_o[:, ks], _NT, preferred_element_type=f32)
                    o = o + jnp.dot(a.astype(bf16), gv[0:s0, vs], preferred_element_type=f32)
                o_heads.append(o)
            o_blocks.append(jnp.concatenate(o_heads, axis=1))
        og = jnp.concatenate(o_blocks, axis=0)
        gate_prep(jnp.minimum(c + 1, tt // L - 1), seq)
        for h in range(GLA_HEADS):
            ks = slice(h * GLA_DK, (h + 1) * GLA_DK)
            vs = slice(h * GLA_DV, (h + 1) * GLA_DV)
            s_s[h][...] = (decay_col[ks, :] * s_old[h]
                           + jnp.dot(kdt[ks, :], gv[:, vs], preferred_element_type=f32))
        go = go_ref[rows, :].astype(f32)
        y_gla = _head_norm(og, glag_ref[...]) * (go * _sigmoid(go))
        br_ref[rows, ML_W:ML_W + GLA_VW] = y_gla.astype(br_ref.dtype)

    def chunk_body(c, carry):
        for s in range(nb):
            one_seq(c, s)
        return carry

    for s in range(nb):
        gate_prep(0, s)
    lax.fori_loop(0, tt // L, chunk_body, 0)

    pos = t_idx * tt + lax.broadcasted_iota(jnp.int32, (tt, 1), 0)
    for s in range(nb):
        pu_ref, br_ref, ext_s = z_refs[s][7], br_all.at[s], ext_refs[s]
        n_ext = tt + POOL_BASE
        ext_s[POOL_BASE:n_ext, :] = pu_ref[...].astype(f32)
        for g, w in enumerate(POOL_WINDOWS):
            gs = slice(g * POOL_GDIM, (g + 1) * POOL_GDIM)
            u = ext_s[POOL_BASE:n_ext, gs]
            p, k = ext_s[:, gs], 1
            while k < w:
                p = p + pltpu.roll(p, k, 0)
                k *= 2
            acc = p[POOL_BASE:n_ext, :]
            cnt = jnp.minimum(pos + 1, w).astype(f32)
            d = acc / cnt - u
            y = jnp.dot(d.astype(bf16), pw_ref[g], preferred_element_type=f32) * ps_ref[:, gs]
            br_ref[:, ML_W + GLA_VW + g * POOL_GDIM:ML_W + GLA_VW + (g + 1) * POOL_GDIM] = y.astype(br_ref.dtype)
        ext_s[8:POOL_BASE, :] = ext_s[tt + 8:n_ext, :]

    @pl.when(t_idx == n_t - 1)
    def _():
        for s in range(nb):
            for h in range(ML_HEADS):
                c_out[s, h] = c_refs[s][h][...]
            for h in range(GLA_HEADS):
                s_out[s, h] = s_refs[s][h][...]
            n_out[s] = n_refs[s][0:ML_HEADS, :]
            m_out[s] = m_refs[s][...]
            buf_out[s] = ext_refs[s][POOL_BASE - POOL_BUF:POOL_BASE, :]


def _prompt_mix(z, zs, lw, batch, seq, tt, chunk, sub, nb):
    n_t = seq // tt
    assert batch % nb == 0

    def zspecs(s):
        row = lambda b, t: (b * nb + s) * n_t + t
        spec = lambda col, width: pl.BlockSpec((tt, width), lambda b, t: (row(b, t), col // width))
        return [spec(C_MQ, 512), spec(C_MK, 512), spec(C_MV, 512), spec(C_MO, 512),
                spec(C_GQ, 512), spec(C_GV, 512), spec(C_GO, 512), spec(C_PU, 512),
                pl.BlockSpec((tt, LANES), lambda b, t: (row(b, t), 0))]

    def full(shape):
        nd = len(shape)
        return pl.BlockSpec(shape, lambda b, t: (0,) * nd)

    def per_seq(shape):
        nd = len(shape)
        return pl.BlockSpec((nb,) + shape, lambda b, t: (b,) + (0,) * nd)

    kern = functools.partial(_prompt_mix_kernel, nb=nb, tt=tt, chunk=chunk, sub=sub, n_t=n_t)
    outs = pl.pallas_call(
        kern,
        grid=(batch // nb, n_t),
        in_specs=[sp for s in range(nb) for sp in zspecs(s)]
                 + [full((1, LANES)), full((1, ML_W)), full((LANES, GLA_KW)), full((1, GLA_KW)),
                    full((1, GLA_VW)), full((len(POOL_WINDOWS), POOL_GDIM, POOL_GDIM)), full((1, POOL_W))],
        out_specs=[
            pl.BlockSpec((nb, tt, N_BRANCH * ML_W), lambda b, t: (b, t, 0)),
            per_seq((ML_HEADS, ML_DH, ML_DH)), per_seq((ML_HEADS, ML_DH)), per_seq((8, LANES)),
            per_seq((GLA_HEADS, GLA_DK, GLA_DV)), per_seq((POOL_BUF, POOL_W)),
        ],
        out_shape=[
            jax.ShapeDtypeStruct((batch, seq, N_BRANCH * ML_W), bf16),
            jax.ShapeDtypeStruct((batch, ML_HEADS, ML_DH, ML_DH), f32),
            jax.ShapeDtypeStruct((batch, ML_HEADS, ML_DH), f32),
            jax.ShapeDtypeStruct((batch, 8, LANES), f32),
            jax.ShapeDtypeStruct((batch, GLA_HEADS, GLA_DK, GLA_DV), f32),
            jax.ShapeDtypeStruct((batch, POOL_BUF, POOL_W), f32),
        ],
        scratch_shapes=([pltpu.VMEM((ML_DH, ML_DH), f32)] * ML_HEADS
                        + [pltpu.VMEM((GLA_DK, GLA_DV), f32)] * GLA_HEADS
                        + [pltpu.VMEM((8, LANES), f32), pltpu.VMEM((8, LANES), f32),
                           pltpu.VMEM((tt + POOL_BASE, POOL_W), f32),
                           pltpu.VMEM((chunk, LANES), f32), pltpu.VMEM((chunk, LANES), f32),
                           pltpu.VMEM((16, chunk), f32), pltpu.VMEM((chunk, GLA_KW), f32)]) * nb,
        compiler_params=_cparams(("parallel", "arbitrary")),
        name="prompt_mix",
    )(*([z] * 8 + [zs]) * nb,
      lw["ifb"], lw["ml_g"], lw["gla_gw"], lw["gla_gb"], lw["gla_g"], lw["pool_w"], lw["pool_scale"])
    return (outs[0].reshape(batch * seq, N_BRANCH * ML_W),) + tuple(outs[1:])


def _sample_mix_kernel(q_ref, k_ref, v_ref, mo_ref, gqk_ref, gv_ref, go_ref, pu_ref, sm_ref,
                       c_in, n_in, m_in, s_in, buf_in,
                       ifb_ref, mlg_ref, gw_ref, gb_ref, glag_ref, pw_ref, ps_ref, *tail, sb, dec):
    br_all = tail[-6]
    c_out, n_out, m_out, s_out, buf_out = (r.at[0] for r in tail[-5:])
    step = pl.program_id(0)
    c_in, n_in, m_in, s_in, buf_in = (r.at[0] for r in (c_in, n_in, m_in, s_in, buf_in))
    br_ref = br_all.at[pl.ds(pl.multiple_of(step * sb, sb), sb), :]

    @pl.when(step == 0)
    def _():
        br_all[dec:, :] = jnp.zeros((br_all.shape[0] - dec, br_all.shape[1]), br_all.dtype)

    sm = sm_ref[...]
    eye = (lax.broadcasted_iota(jnp.int32, (LANES, LANES), 0)
           == lax.broadcasted_iota(jnp.int32, (LANES, LANES), 1)).astype(f32)

    def to_cols(x):
        parts = [lax.dot_general(eye, x[:, j * LANES:(j + 1) * LANES], _NT,
                                 preferred_element_type=f32, precision=lax.Precision.HIGHEST)
                 for j in range(x.shape[1] // LANES)]
        return jnp.concatenate(parts, axis=0)

    y0 = sm + ifb_ref[...]
    logf_all = _log_sigmoid(y0)
    k_scale = ML_DH ** -0.5
    m_all = m_in[...]
    hm, m_new_cols = [], []
    for h in range(ML_HEADS):
        hs = slice(h * ML_DH, (h + 1) * ML_DH)
        q = q_ref[:, hs].astype(f32)
        k = k_ref[:, hs].astype(f32) * k_scale
        v = v_ref[:, hs].astype(f32)
        i_pre = y0[:, S_MI + h:S_MI + h + 1]
        logf = logf_all[:, S_MF + h:S_MF + h + 1]
        m = m_all[:, h:h + 1]
        inter = logf + m
        m_t = jnp.maximum(inter, i_pre)
        w_intra = jnp.exp(i_pre - m_t)
        w_inter = jnp.exp(inter - m_t)
        s = jnp.sum(q * k, axis=1, keepdims=True) * w_intra
        qb = q.astype(bf16)
        n_h = n_in[:, h, :]
        cq = jnp.concatenate(
            [lax.dot_general(qb, c_in[j, h].astype(bf16), _NT, preferred_element_type=f32)[j:j + 1, :]
             for j in range(sb)], axis=0)
        num = s * v + w_inter * cq
        den = s + w_inter * jnp.sum(n_h * q, axis=1, keepdims=True)
        hm.append(num / jnp.maximum(jnp.abs(den), jnp.exp(-m_t)))
        m_new = m_t
        w_s = w_intra
        w_c = w_inter
        n_out[:, h, :] = w_c * n_h + w_s * k
        m_new_cols.append(m_new)
        vw_cols = to_cols(v * w_s)
        for j in range(sb):
            c_out[j, h] = w_c[j:j + 1, :] * c_in[j, h] + vw_cols[:, j:j + 1] * k[j:j + 1, :]
    lane = lax.broadcasted_iota(jnp.int32, (sb, LANES), 1)
    m_pack = jnp.zeros((sb, LANES), f32)
    for h in range(ML_HEADS):
        m_pack = jnp.where(lane == h, m_new_cols[h], m_pack)
    m_out[...] = m_pack
    y_ml = _head_norm(jnp.concatenate(hm, axis=1), mlg_ref[...]) * _sigmoid(mo_ref[...].astype(f32))
    br_ref[:, 0:ML_W] = y_ml.astype(br_ref.dtype)

    log_a = _gla_log_decay(sm, gw_ref, gb_ref)
    decay = jnp.exp(log_a)
    q2 = gqk_ref[:, 0:GLA_KW].astype(f32) * (GLA_DK ** -0.5)
    k2 = gqk_ref[:, GLA_KW:2 * GLA_KW].astype(f32)
    gv = gv_ref[...].astype(f32)
    qe = (q2 * decay).astype(bf16)
    qk = q2 * k2
    k_cols = to_cols(k2)
    decay_cols = to_cols(decay)
    og = []
    for h in range(GLA_HEADS):
        ks = slice(h * GLA_DK, (h + 1) * GLA_DK)
        vs = slice(h * GLA_DV, (h + 1) * GLA_DV)
        a = jnp.sum(qk[:, ks], axis=1, keepdims=True)
        inter = jnp.concatenate(
            [jnp.dot(qe[:, ks], s_in[j, h].astype(bf16), preferred_element_type=f32)[j:j + 1, :]
             for j in range(sb)], axis=0)
        og.append(a * gv[:, vs] + inter)
        for j in range(sb):
            s_out[j, h] = (decay_cols[ks, j:j + 1] * s_in[j, h]
                           + k_cols[ks, j:j + 1] * gv[j:j + 1, vs])
    go = go_ref[...].astype(f32)
    y_gla = _head_norm(jnp.concatenate(og, axis=1), glag_ref[...]) * (go * _sigmoid(go))
    br_ref[:, ML_W:ML_W + GLA_VW] = y_gla.astype(br_ref.dtype)

    u = pu_ref[...].astype(f32)
    rowi = lax.broadcasted_iota(jnp.int32, (POOL_BUF + 1, POOL_GDIM), 0)
    d_rows = []
    for j in range(sb):
        ext = jnp.concatenate([buf_in[j], u[j:j + 1, :]], axis=0)
        buf_out[j] = ext[1:POOL_BUF + 1, :]
        parts = []
        for g, w in enumerate(POOL_WINDOWS):
            gs = slice(g * POOL_GDIM, (g + 1) * POOL_GDIM)
            win = jnp.sum(jnp.where(rowi >= POOL_BUF + 1 - w, ext[:, gs], 0.0), axis=0, keepdims=True)
            parts.append(win / float(w) - u[j:j + 1, gs])
        d_rows.append(jnp.concatenate(parts, axis=1))
    d = jnp.concatenate(d_rows, axis=0)
    for g in range(len(POOL_WINDOWS)):
        gs = slice(g * POOL_GDIM, (g + 1) * POOL_GDIM)
        y = jnp.dot(d[:, gs].astype(bf16), pw_ref[g], preferred_element_type=f32) * ps_ref[:, gs]
        br_ref[:, ML_W + GLA_VW + g * POOL_GDIM:ML_W + GLA_VW + (g + 1) * POOL_GDIM] = y.astype(br_ref.dtype)


def _sample_mix(z, zs, st, prev, layer, lw, row0, dec, sb):
    depth = st[0].shape[0]
    base = row0 // sb
    tail_rows = z.shape[0] - row0

    def zspec(col, width):
        blk = col // width
        return pl.BlockSpec((sb, width), lambda i: (base + i, blk))

    def full(shape):
        nd = len(shape)
        return pl.BlockSpec(shape, lambda i: (0,) * nd)

    st_shapes = [(ML_HEADS, ML_DH, ML_DH), (ML_HEADS, ML_DH), (LANES,),
                 (GLA_HEADS, GLA_DK, GLA_DV), (POOL_BUF, POOL_W)]
    st_specs = [pl.BlockSpec((1, sb) + s, functools.partial(lambda i, nd: (layer, i) + (0,) * nd, nd=len(s)))
                for s in st_shapes]
    n_in = N_ZBLOCKS + len(st_shapes) + 7
    prev = () if prev is None else tuple(prev)
    return pl.pallas_call(
        functools.partial(_sample_mix_kernel, sb=sb, dec=dec),
        grid=(dec // sb,),
        in_specs=[zspec(C_MQ, 512), zspec(C_MK, 512), zspec(C_MV, 512), zspec(C_MO, 512),
                  zspec(C_GQ, 512), zspec(C_GV, 512), zspec(C_GO, 512), zspec(C_PU, 512),
                  pl.BlockSpec((sb, LANES), lambda i: (base + i, 0))]
                 + st_specs
                 + [full((1, LANES)), full((1, ML_W)), full((LANES, GLA_KW)), full((1, GLA_KW)),
                    full((1, GLA_VW)), full((len(POOL_WINDOWS), POOL_GDIM, POOL_GDIM)), full((1, POOL_W))]
                 + [pl.BlockSpec(memory_space=pl.ANY)] * len(prev),
        out_specs=[pl.BlockSpec((tail_rows, N_BRANCH * ML_W), lambda i: (0, 0))] + st_specs,
        out_shape=[jax.ShapeDtypeStruct((tail_rows, N_BRANCH * ML_W), bf16)]
                  + [jax.ShapeDtypeStruct((depth, dec) + s, f32) for s in st_shapes],
        input_output_aliases={n_in + j: 1 + j for j in range(len(prev))},
        compiler_params=_cparams(("arbitrary",)),
        name="sample_mix",
    )(z, z, z, z, z, z, z, z, zs, *st,
      lw["ifb"], lw["ml_g"], lw["gla_gw"], lw["gla_gb"], lw["gla_g"], lw["pool_w"], lw["pool_scale"], *prev)


def _merge_kernel(*refs, tm, n_valid, n_prompt_tiles, x_starts, n_extra, tf):
    x_refs = refs[:len(x_starts)]
    brp_ref, brt_ref, g0_ref, g1_ref, g2_ref, wb_ref, wo_ref = refs[len(x_starts):len(x_starts) + 7]
    extra_in = refs[len(x_starts) + 7:len(x_starts) + 7 + n_extra]
    o_ref = refs[len(x_starts) + 7 + n_extra]
    route_out = refs[len(x_starts) + 8 + n_extra:]
    in_tail = pl.program_id(0) >= n_prompt_tiles
    mixed = None
    for n, g_ref in enumerate((g0_ref, g1_ref, g2_ref)):
        cols = slice(n * ML_W, (n + 1) * ML_W)
        br = jnp.where(in_tail, brt_ref[:, cols], brp_ref[:, cols])
        proj = jnp.dot(br, wb_ref[n], preferred_element_type=f32)
        term = _sigmoid(g_ref[...].astype(f32)) * proj
        mixed = term if mixed is None else mixed + term
    out = (_pick_rows(x_refs, x_starts)
           + jnp.dot(mixed.astype(bf16), wo_ref[...], preferred_element_type=f32))
    row = pl.program_id(0) * tm + lax.broadcasted_iota(jnp.int32, (tm, 1), 0)
    out = jnp.where(row < n_valid, out, 0.0)
    if n_extra == 4:
        out = _swiglu_residual(out, *extra_in, tf=tf)
    o_ref[...] = out
    if n_extra == 2:
        _route(out, *extra_in, *route_out, tm=tm, n_valid=n_valid)


def _merge(x_parts, br_prompt, br_tail, z, w_branch, w_out, tm, n_valid, route_w=None, ffn_w=None, tf=None):
    mp = sum(p.shape[0] for p in x_parts)
    nt = mp // tm
    route = route_w is not None
    assert not (route and ffn_w is not None)
    extra = tuple(route_w or ffn_w or ())
    route_specs = [_resident(a.shape) for a in extra]
    out_specs = [pl.BlockSpec((tm, D_MODEL), lambda i: (i, 0))]
    out_shape = [jax.ShapeDtypeStruct((mp, D_MODEL), f32)]
    if route:
        out_specs += [pl.BlockSpec((tm, D_MODEL), lambda i: (i, 0)),
                      pl.BlockSpec((tm, LANES), lambda i: (i, 0)),
                      pl.BlockSpec((tm, LANES), lambda i: (i, 0)),
                      pl.BlockSpec((1, N_EXPERTS, tm), lambda i: (i, 0, 0)),
                      pl.BlockSpec((1, 1, LANES), lambda i: (i, 0, 0))]
        out_shape += [jax.ShapeDtypeStruct((mp, D_MODEL), bf16),
                      jax.ShapeDtypeStruct((mp, LANES), f32),
                      jax.ShapeDtypeStruct((mp, LANES), f32),
                      jax.ShapeDtypeStruct((nt, N_EXPERTS, tm), f32),
                      jax.ShapeDtypeStruct((nt, 1, LANES), f32)]
    x_starts = tuple(sum(p.shape[0] for p in x_parts[:j]) // tm for j in range(len(x_parts)))
    n_p = br_prompt.shape[0] // tm
    gate_specs = [pl.BlockSpec((tm, D_MODEL), functools.partial(lambda i, n: (i, C_GATES // D_MODEL + n), n=n))
                  for n in range(N_BRANCH)]
    outs = pl.pallas_call(
        functools.partial(_merge_kernel, tm=tm, n_valid=n_valid, n_prompt_tiles=n_p, x_starts=x_starts,
                          n_extra=len(extra), tf=tf),
        grid=(nt,),
        in_specs=[
            *_split_rows_specs(x_parts, tm, D_MODEL),
            pl.BlockSpec((tm, N_BRANCH * ML_W), lambda i: (jnp.minimum(i, n_p - 1), 0)),
            pl.BlockSpec((tm, N_BRANCH * ML_W), lambda i: (jnp.maximum(i - n_p, 0), 0)),
            *gate_specs,
            _resident((N_BRANCH, ML_W, D_MODEL)),
            _resident((D_MODEL, D_MODEL)),
            *route_specs,
        ],
        out_specs=out_specs,
        out_shape=out_shape,
        compiler_params=_cparams(("parallel",)),
        name="merge_route" if route else ("merge_ffn" if extra else "merge"),
    )(*x_parts, br_prompt, br_tail, z, z, z, w_branch, w_out, *extra)
    return outs if route else outs[0]


def _route(x, g_ref, wr_ref, hb_ref, comb_ref, rnk_ref, rnkt_ref, cnt_ref, *, tm, n_valid):
    h = _rms(x, g_ref[...])
    h_hi = h.astype(bf16)
    hb_ref[...] = h_hi
    h_lo = (h - h_hi.astype(f32)).astype(bf16)
    p_hi = jnp.dot(h_hi, wr_ref[...], preferred_element_type=f32)
    p_lo = jnp.dot(h_lo, wr_ref[...], preferred_element_type=f32)
    logits = p_hi + p_lo + pltpu.roll(p_hi, LANES - N_EXPERTS, 1)
    lane = lax.broadcasted_iota(jnp.int32, logits.shape, 1)
    valid = lane < N_EXPERTS
    logits = jnp.where(valid, logits, NEG)
    mx = jnp.max(logits, axis=1, keepdims=True)
    e = jnp.where(valid, jnp.exp(logits - mx), 0.0)
    probs = e / jnp.sum(e, axis=1, keepdims=True)
    p1 = jnp.max(probs, axis=1, keepdims=True)
    i1 = jnp.min(jnp.where(probs == p1, lane, LANES), axis=1, keepdims=True)
    rest = jnp.where((lane == i1) | ~valid, -1.0, probs)
    p2 = jnp.max(rest, axis=1, keepdims=True)
    i2 = jnp.min(jnp.where(rest == p2, lane, LANES), axis=1, keepdims=True)
    tot = p1 + p2
    comb_ref[...] = jnp.where(lane == i1, p1 / tot, 0.0) + jnp.where(lane == i2, p2 / tot, 0.0)
    row = pl.program_id(0) * tm + lax.broadcasted_iota(jnp.int32, logits.shape, 0)
    sel = ((lane == i1) | (lane == i2)) & (row < n_valid)
    r = lax.broadcasted_iota(jnp.int32, (tm, tm), 0)
    c = lax.broadcasted_iota(jnp.int32, (tm, tm), 1)
    rank = jnp.dot((c < r).astype(bf16), sel.astype(bf16), preferred_element_type=f32)
    rnk = jnp.where(sel, rank, -1.0)
    rnk_ref[...] = rnk
    rnkt_ref[0] = rnk.T[0:N_EXPERTS, :]
    cnt_ref[0] = jnp.sum(sel.astype(f32), axis=0, keepdims=True)


def _moe_kernel(rounds_ref, hb_ref, rnk_ref, rnkt_ref, comb_ref, wg_ref, wu_ref, wd_ref, o_ref,
                *, cap, n_sub, ts):
    i, e = pl.program_id(0), pl.program_id(1)

    @pl.when(e == 0)
    def _():
        o_ref[...] = jnp.zeros_like(o_ref)

    lane = lax.broadcasted_iota(jnp.int32, (ts, LANES), 1)
    slot_rows = lax.broadcasted_iota(jnp.int32, (cap, ts), 0).astype(f32)
    slot_cols = lax.broadcasted_iota(jnp.int32, (ts, cap), 1).astype(f32)

    def round_body(r, carry):
        base = (r * cap).astype(f32)
        xs = []
        for j in range(n_sub):
            rt = rnkt_ref[j, pl.ds(e, 1), :]
            p = (rt - base == slot_rows).astype(bf16)
            xs.append(jnp.dot(p, hb_ref[j * ts:(j + 1) * ts, :],
                              preferred_element_type=f32).astype(bf16))
        xs = jnp.concatenate(xs, axis=0)
        a = jnp.dot(xs, wg_ref[0], preferred_element_type=f32)
        a = (a * _sigmoid(a)) * jnp.dot(xs, wu_ref[0], preferred_element_type=f32)
        y = jnp.dot(a.astype(bf16), wd_ref[0], preferred_element_type=f32).astype(bf16)
        for j in range(n_sub):
            rows = slice(j * ts, (j + 1) * ts)
            col = jnp.sum(jnp.where(lane == e, rnk_ref[rows, :], 0.0), axis=1, keepdims=True)
            w = jnp.sum(jnp.where(lane == e, comb_ref[rows, :], 0.0), axis=1, keepdims=True)
            pt = (col - base == slot_cols).astype(bf16)
            o_ref[rows, :] += w * jnp.dot(pt, y[j * cap:(j + 1) * cap, :], preferred_element_type=f32)
        return carry

    lax.fori_loop(0, rounds_ref[i * N_EXPERTS + e], round_body, 0)


def _moe(hb, rnk, rnkt, comb, rounds, wg, wu, wd, ts, n_sub, cap):
    mp = hb.shape[0]
    n_e, _, d_ff = wg.shape
    tsup = ts * n_sub
    grid_spec = pltpu.PrefetchScalarGridSpec(
        num_scalar_prefetch=1,
        grid=(mp // tsup, n_e),
        in_specs=[pl.BlockSpec((tsup, D_MODEL), lambda i, e, r: (i, 0)),
                  pl.BlockSpec((tsup, LANES), lambda i, e, r: (i, 0)),
                  pl.BlockSpec((n_sub, N_EXPERTS, ts), lambda i, e, r: (i, 0, 0)),
                  pl.BlockSpec((tsup, LANES), lambda i, e, r: (i, 0)),
                  pl.BlockSpec((1, D_MODEL, d_ff), lambda i, e, r: (e, 0, 0)),
                  pl.BlockSpec((1, D_MODEL, d_ff), lambda i, e, r: (e, 0, 0)),
                  pl.BlockSpec((1, d_ff, D_MODEL), lambda i, e, r: (e, 0, 0))],
        out_specs=pl.BlockSpec((tsup, D_MODEL), lambda i, e, r: (i, 0)),
    )
    return pl.pallas_call(
        functools.partial(_moe_kernel, cap=cap, n_sub=n_sub, ts=ts),
        grid_spec=grid_spec,
        out_shape=jax.ShapeDtypeStruct((mp, D_MODEL), f32),
        compiler_params=_cparams(("parallel", "arbitrary")),
        name="moe",
    )(rounds, hb, rnk, rnkt, comb, wg, wu, wd)


def _swiglu_residual(x, g_ref, wg_ref, wu_ref, wd_ref, *, tf):
    h = _rms(x, g_ref[...]).astype(bf16)
    acc = x
    assert wg_ref.shape[1] % tf == 0
    for c in range(wg_ref.shape[1] // tf):
        cols = slice(c * tf, (c + 1) * tf)
        a = jnp.dot(h, wg_ref[:, cols], preferred_element_type=f32)
        a = (a * _sigmoid(a)) * jnp.dot(h, wu_ref[:, cols], preferred_element_type=f32)
        acc = acc + jnp.dot(a.astype(bf16), wd_ref[cols, :], preferred_element_type=f32)
    return acc


def _final_norm_kernel(x_ref, y_ref, g_ref, op_ref, os_ref, *, n_prompt_tiles, dec):
    i = pl.program_id(0)
    out = _rms(x_ref[...] + y_ref[...], g_ref[...])

    @pl.when(i < n_prompt_tiles)
    def _():
        op_ref[...] = out

    @pl.when(i == n_prompt_tiles)
    def _():
        os_ref[...] = out[0:dec]


def _final_norm(x, y, g, tm, m_prompt, dec):
    n_p = m_prompt // tm
    assert n_p * tm == m_prompt and dec <= tm
    return pl.pallas_call(
        functools.partial(_final_norm_kernel, n_prompt_tiles=n_p, dec=dec),
        grid=(n_p + 1,),
        in_specs=[pl.BlockSpec((tm, D_MODEL), lambda i: (i, 0)),
                  pl.BlockSpec((tm, D_MODEL), lambda i: (i, 0)),
                  pl.BlockSpec((1, D_MODEL), lambda i: (0, 0))],
        out_specs=[pl.BlockSpec((tm, D_MODEL), lambda i: (jnp.minimum(i, n_p - 1), 0)),
                   pl.BlockSpec((dec, D_MODEL), lambda i: (0, 0))],
        out_shape=[jax.ShapeDtypeStruct((m_prompt, D_MODEL), f32),
                   jax.ShapeDtypeStruct((dec, D_MODEL), f32)],
        compiler_params=_cparams(("arbitrary",)),
        name="final_norm",
    )(x, y, g)


O_MI = 3 * ML_W
O_MO = O_MI + 2 * ML_HEADS
O_GLR = O_MO + ML_W + 2 * GLA_KW + GLA_VW
O_GO = O_GLR + GLA_RANK
D_IN = O_GO + GLA_VW + POOL_W + N_BRANCH * D_MODEL


def _regroup_kernel(w_ref, if_ref, glr_ref, main_ref, small_ref):
    main_ref[0] = w_ref[0].astype(bf16)
    n_if, n_glr = if_ref.shape[1], glr_ref.shape[1]
    small_ref[0, 0:n_if, :] = if_ref[0].astype(bf16)
    small_ref[0, n_if:n_if + n_glr, :] = glr_ref[0].astype(bf16)
    small_ref[0, n_if + n_glr:, :] = jnp.zeros((LANES - n_if - n_glr, small_ref.shape[2]), bf16)


def _regroup_w_in(wt, tr=512):
    depth, n, d = wt.shape
    assert n == D_IN and Z_MAIN == D_IN - 2 * ML_HEADS - GLA_RANK
    assert O_MI % tr == 0 and (O_GLR - O_MO) % tr == 0 and (D_IN - O_GO) % tr == 0
    c1, c2 = O_MI // tr, (O_MI + O_GLR - O_MO) // tr

    def src_row(c):
        skip = jnp.where(c < c1, 0, jnp.where(c < c2, O_MO - O_MI, O_MO - O_MI + O_GO - O_GLR))
        return pl.multiple_of(c * tr + skip, 8)

    return pl.pallas_call(
        _regroup_kernel,
        grid=(depth, Z_MAIN // tr),
        in_specs=[pl.BlockSpec((pl.Element(1), pl.Element(rows), pl.Element(d)), index_map)
                  for rows, index_map in ((tr, lambda l, c: (l, src_row(c), 0)),
                                          (O_MO - O_MI, lambda l, c: (l, O_MI, 0)),
                                          (O_GO - O_GLR, lambda l, c: (l, O_GLR, 0)))],
        out_specs=[pl.BlockSpec((1, tr, d), lambda l, c: (l, c, 0)),
                   pl.BlockSpec((1, LANES, d), lambda l, c: (l, 0, 0))],
        out_shape=[jax.ShapeDtypeStruct((depth, Z_MAIN, d), bf16),
                   jax.ShapeDtypeStruct((depth, LANES, d), bf16)],
        compiler_params=_cparams(("parallel", "arbitrary")),
        name="regroup_w_in",
    )(wt, wt, wt)


def _layer_weights(l, norm1_g, if_bias, ml_g, gla_gw, gla_gb, gla_g, pool_w, pool_scale,
                   w_branch, w_out):
    ifb = jnp.zeros((1, LANES), f32)
    ifb = ifb.at[0, S_MI:S_MI + ML_HEADS].set(if_bias[l, 0]).at[0, S_MF:S_MF + ML_HEADS].set(if_bias[l, 1])
    gw = jnp.zeros((LANES, GLA_KW), f32).at[S_GLR:S_GLR + GLA_RANK].set(gla_gw[l]).astype(bf16)
    return dict(
        norm1_g=norm1_g[l][None], ifb=ifb, ml_g=ml_g[l][None],
        gla_gw=gw, gla_gb=gla_gb[l][None], gla_g=gla_g[l][None], pool_w=pool_w[l].astype(bf16),
        pool_scale=pool_scale[l][None], w_branch=w_branch[l].astype(bf16), w_out=w_out[l].astype(bf16))


def _forward(x_prompt, x_sample, state_mlstm_C, state_mlstm_n, state_mlstm_m, state_gla_S,
             state_pool_buf, norm1_g, w_in, mlstm_if_bias, mlstm_norm_g, gla_gate_w, gla_gate_b,
             gla_norm_g, pool_w, pool_scale, w_branch, w_out, norm2_g, ffn_wg, ffn_wu, ffn_wd,
             router_w, moe_wg, moe_wu, moe_wd, final_norm_g, *, tm, tn, tt, chunk, sub, nb, sb, tf_dense, moe_sub, moe_cap):
    batch, seq, _ = x_prompt.shape
    dec = x_sample.shape[0]
    depth = w_in.shape[0]
    m_prompt = batch * seq
    m_all = m_prompt + dec
    mp = -(-m_all // tm) * tm
    assert m_prompt % tm == 0
    x_parts = (x_prompt.reshape(m_prompt, D_MODEL),
               jnp.concatenate([x_sample.reshape(dec, D_MODEL), jnp.zeros((mp - m_all, D_MODEL), f32)], axis=0))
    outs = [[] for _ in range(5)]
    moe_out = None
    st = (state_mlstm_C, state_mlstm_n, jnp.pad(state_mlstm_m, ((0, 0), (0, 0), (0, LANES - ML_HEADS))),
          state_gla_S, state_pool_buf)
    st_new = None
    w_main, w_small = _regroup_w_in(jnp.swapaxes(w_in, 1, 2))
    for l in range(depth):
        lw = _layer_weights(l, norm1_g, mlstm_if_bias, mlstm_norm_g, gla_gate_w, gla_gate_b,
                            gla_norm_g, pool_w, pool_scale, w_branch, w_out)
        z, zs = _norm_matmul(x_parts, lw["norm1_g"], w_main, w_small, l, tm, tn)
        br_p, c_p, n_p, m_p, s_p, buf_p = _prompt_mix(z, zs, lw, batch, seq, tt, chunk, sub, nb)
        br_t, *st_new = _sample_mix(z, zs, st, st_new, l, lw, m_prompt, dec, sb)
        j = l // 2
        if l % 2 == 0:
            x = _merge(x_parts, br_p, br_t, z, lw["w_branch"], lw["w_out"], tm, m_all,
                       ffn_w=(norm2_g[l][None], ffn_wg[j].astype(bf16), ffn_wu[j].astype(bf16),
                              ffn_wd[j].astype(bf16)), tf=tf_dense)
        else:
            wr_hi = router_w[j].astype(bf16)
            wr_lo = (router_w[j] - wr_hi.astype(f32)).astype(bf16)
            wr = jnp.pad(jnp.concatenate([wr_hi, wr_lo], axis=1), ((0, 0), (0, LANES - 2 * N_EXPERTS)))
            x, hb, comb, rnk, rnkt, cnt = _merge(x_parts, br_p, br_t, z, lw["w_branch"], lw["w_out"], tm, m_all,
                                                 route_w=(norm2_g[l][None], wr))
            n_sup = mp // (tm * moe_sub)
            cnt = cnt[:, 0, :N_EXPERTS].reshape(n_sup, moe_sub, N_EXPERTS).max(axis=1)
            rounds = jnp.ceil(cnt / moe_cap).astype(jnp.int32).reshape(-1)
            moe_out = _moe(hb, rnk, rnkt, comb, rounds, moe_wg[j].astype(bf16), moe_wu[j].astype(bf16),
                           moe_wd[j].astype(bf16), tm, moe_sub, moe_cap)
            if l + 1 < depth:
                x = x + moe_out
                moe_out = None
        x_parts = (x,)
        for lst, val in zip(outs, (c_p, n_p, m_p[:, :ML_HEADS, 0], s_p, buf_p)):
            lst.append(val)
    if moe_out is None:
        moe_out = jnp.zeros_like(x)
    y_prompt, y_sample = _final_norm(x, moe_out, final_norm_g[None], tm, m_prompt, dec)
    c_p, n_p, m_p, s_p, buf_p = (jnp.stack(o) for o in outs)
    c_s, n_s, m_s, s_s, buf_s = st_new
    return (y_prompt.reshape(batch, seq, D_MODEL), y_sample.reshape(dec, 1, D_MODEL),
            c_p, c_s, n_p, n_s, m_p, m_s[:, :, :ML_HEADS], s_p, s_s, buf_p, buf_s)


def kernel(x_prompt, x_sample, state_mlstm_C, state_mlstm_n, state_mlstm_m, state_gla_S, state_pool_buf, norm1_g, w_in, mlstm_if_bias, mlstm_norm_g, gla_gate_w, gla_gate_b, gla_norm_g, pool_w, pool_scale, w_branch, w_out, norm2_g, ffn_wg, ffn_wu, ffn_wd, router_w, moe_wg, moe_wu, moe_wd, final_norm_g):
    return _forward(x_prompt, x_sample, state_mlstm_C, state_mlstm_n, state_mlstm_m, state_gla_S,
                    state_pool_buf, norm1_g, w_in, mlstm_if_bias, mlstm_norm_g, gla_gate_w, gla_gate_b,
                    gla_norm_g, pool_w, pool_scale, w_branch, w_out, norm2_g, ffn_wg, ffn_wu, ffn_wd,
                    router_w, moe_wg, moe_wu, moe_wd, final_norm_g,
                    tm=512, tn=1024, tt=512, chunk=128, sub=64, nb=2, sb=16, tf_dense=MXU_COLS, moe_sub=3, moe_cap=160)
```

```python
import functools

import jax
import jax.numpy as jnp
from jax import lax
from jax.experimental import pallas as pl
from jax.experimental.pallas import tpu as pltpu

f32 = jnp.float32
bf16 = jnp.bfloat16

D_MODEL = 1024
ML_HEADS, ML_DH = 4, 128
ML_W = ML_HEADS * ML_DH
GLA_HEADS, GLA_DK, GLA_DV = 4, 64, 128
GLA_KW, GLA_VW = GLA_HEADS * GLA_DK, GLA_HEADS * GLA_DV
GLA_RANK = 16
GLA_TAU = 16.0
POOL_GDIM = 128
POOL_WINDOWS = (2, 4, 8, 16)
POOL_W = POOL_GDIM * len(POOL_WINDOWS)
POOL_BUF = 15
POOL_BASE = 24
N_BRANCH = 3
N_EXPERTS = 8
EPS = 1e-6
NEG = -1e30
LANES = 128
MXU_COLS = 256

C_MQ, C_MK, C_MV, C_MO = 0, 512, 1024, 1536
C_GQ, C_GK, C_GV, C_GO, C_PU, C_GATES = 2048, 2304, 2560, 3072, 3584, 4096
Z_MAIN = C_GATES + N_BRANCH * D_MODEL
S_MI, S_MF, S_GLR = 0, 4, 8

VMEM_LIMIT = 56 * 1024 * 1024

_NT = (((1,), (1,)), ((), ()))


def _cparams(sem):
    return pltpu.CompilerParams(dimension_semantics=sem, vmem_limit_bytes=VMEM_LIMIT)


def _log_sigmoid(x):
    return jnp.minimum(x, 0.0) - jnp.log(1.0 + jnp.exp(-jnp.abs(x)))


def _sigmoid(x):
    return 0.5 * jnp.tanh(0.5 * x) + 0.5


def _rms(x, g):
    ms = jnp.mean(x * x, axis=-1, keepdims=True)
    return x * lax.rsqrt(ms + EPS) * g


def _lower_tri(n):
    r = lax.broadcasted_iota(jnp.int32, (n, n), 0)
    c = lax.broadcasted_iota(jnp.int32, (n, n), 1)
    return c <= r


def _cumsum_rows(tri_bf16, a):
    a1 = a.astype(bf16)
    r = a - a1.astype(f32)
    a2 = r.astype(bf16)
    a3 = (r - a2.astype(f32)).astype(bf16)
    d = lambda y: jnp.dot(tri_bf16, y, preferred_element_type=f32)
    return d(a1) + d(a2) + d(a3)


def _cumsum_lanes(triu_bf16, a):
    a1 = a.astype(bf16)
    r = a - a1.astype(f32)
    a2 = r.astype(bf16)
    a3 = (r - a2.astype(f32)).astype(bf16)
    d = lambda y: jnp.dot(y, triu_bf16, preferred_element_type=f32)
    return d(a1) + d(a2) + d(a3)


def _resident(shape):
    nd = len(shape)
    return pl.BlockSpec(shape, lambda *_: (0,) * nd, pipeline_mode=pl.Buffered(1))


def _split_rows_specs(parts, tm, width):
    specs, start = [], 0
    for p in parts:
        n_t = p.shape[0] // tm
        assert n_t * tm == p.shape[0]
        specs.append(pl.BlockSpec(
            (tm, width), functools.partial(lambda i, s, n: (jnp.clip(i - s, 0, n - 1), 0), s=start, n=n_t)))
        start += n_t
    return specs


def _pick_rows(refs, starts):
    i = pl.program_id(0)
    x = refs[0][...]
    for r, s in zip(refs[1:], starts[1:]):
        x = jnp.where(i >= s, r[...], x)
    return x


def _norm_matmul_kernel(*refs, tn, starts):
    x_refs = refs[:len(starts)]
    g_ref, w_ref, ws_ref, z_ref, zs_ref = refs[len(starts):]
    h = _rms(_pick_rows(x_refs, starts), g_ref[...]).astype(bf16)
    zs_ref[...] = lax.dot_general(h, ws_ref[0], _NT, preferred_element_type=f32)
    for c in range(w_ref.shape[1] // tn):
        cols = slice(c * tn, (c + 1) * tn)
        z_ref[:, cols] = lax.dot_general(h, w_ref[0, cols, :], _NT,
                                         preferred_element_type=f32).astype(z_ref.dtype)


def _norm_matmul(x_parts, g, wt_main, wt_small, layer, tm, tn):
    mp = sum(p.shape[0] for p in x_parts)
    n = wt_main.shape[1]
    starts = tuple(sum(p.shape[0] for p in x_parts[:j]) // tm for j in range(len(x_parts)))
    layer_block = lambda width: pl.BlockSpec((1, width, D_MODEL), lambda i: (layer, 0, 0),
                                             pipeline_mode=pl.Buffered(1))
    return pl.pallas_call(
        functools.partial(_norm_matmul_kernel, tn=tn, starts=starts),
        grid=(mp // tm,),
        in_specs=[
            *_split_rows_specs(x_parts, tm, D_MODEL),
            _resident((1, D_MODEL)),
            layer_block(n),
            layer_block(LANES),
        ],
        out_specs=[
            pl.BlockSpec((tm, n), lambda i: (i, 0)),
            pl.BlockSpec((tm, LANES), lambda i: (i, 0)),
        ],
        out_shape=[jax.ShapeDtypeStruct((mp, n), bf16), jax.ShapeDtypeStruct((mp, LANES), f32)],
        compiler_params=_cparams(("parallel",)),
        name="norm_matmul",
    )(*x_parts, g, wt_main, wt_small)


def _head_norm(h, g):
    outs = []
    for j in range(h.shape[1] // LANES):
        hj = h[:, j * LANES:(j + 1) * LANES]
        outs.append(hj * lax.rsqrt(jnp.mean(hj * hj, axis=-1, keepdims=True) + EPS))
    return jnp.concatenate(outs, axis=1) * g


def _gla_log_decay(sm, gw_ref, gb_ref):
    xg = jnp.dot(sm.astype(bf16), gw_ref[...], preferred_element_type=f32) + gb_ref[...]
    return _log_sigmoid(xg) * (1.0 / GLA_TAU)


N_ZBLOCKS = 9


def _prompt_mix_kernel(*refs, nb, tt, chunk, sub, n_t):
    z_refs = [refs[s * N_ZBLOCKS:(s + 1) * N_ZBLOCKS] for s in range(nb)]
    ifb_ref, mlg_ref, gw_ref, gb_ref, glag_ref, pw_ref, ps_ref = refs[nb * N_ZBLOCKS:nb * N_ZBLOCKS + 7]
    br_all, c_out, n_out, m_out, s_out, buf_out = refs[nb * N_ZBLOCKS + 7:nb * N_ZBLOCKS + 13]
    scratch = refs[nb * N_ZBLOCKS + 13:]
    per = ML_HEADS + GLA_HEADS + 7
    c_refs = [scratch[s * per:s * per + ML_HEADS] for s in range(nb)]
    s_refs = [scratch[s * per + ML_HEADS:s * per + ML_HEADS + GLA_HEADS] for s in range(nb)]
    n_refs = [scratch[s * per + ML_HEADS + GLA_HEADS] for s in range(nb)]
    m_refs = [scratch[s * per + ML_HEADS + GLA_HEADS + 1] for s in range(nb)]
    ext_refs = [scratch[s * per + ML_HEADS + GLA_HEADS + 2] for s in range(nb)]
    prep_refs = [scratch[s * per + ML_HEADS + GLA_HEADS + 3:(s + 1) * per] for s in range(nb)]
    t_idx = pl.program_id(1)
    L = chunk

    @pl.when(t_idx == 0)
    def _():
        for s in range(nb):
            for r in (*c_refs[s], *s_refs[s], n_refs[s], m_refs[s]):
                r[...] = jnp.zeros_like(r)
            ext_refs[s][0:POOL_BASE, :] = jnp.zeros((POOL_BASE, POOL_W), f32)

    tri = _lower_tri(L)
    tri_b = tri.astype(bf16)
    causal_sub = _lower_tri(sub)
    assert L == LANES
    ones_b = jnp.ones((L, LANES), bf16)
    triu_b = (lax.broadcasted_iota(jnp.int32, (L, L), 0)
              <= lax.broadcasted_iota(jnp.int32, (L, L), 1)).astype(bf16)
    ifb = ifb_ref[...]
    k_scale = ML_DH ** -0.5
    q_scale = GLA_DK ** -0.5

    def gate_prep(c, seq):
        sm_ref = z_refs[seq][8]
        y0_p, bc_p, rows_p, b_p = prep_refs[seq]
        sm = sm_ref[c * L:(c + 1) * L, :]
        y0 = sm + ifb
        y0_p[...] = y0
        bc_p[...] = _cumsum_rows(tri_b, _log_sigmoid(y0))
        y0t = y0.T[0:8, :]
        rows_p[0:8, :] = y0t
        rows_p[8:16, :] = _cumsum_lanes(triu_b, _log_sigmoid(y0t))
        b_p[...] = _cumsum_rows(tri_b, _gla_log_decay(sm, gw_ref, gb_ref))

    def one_seq(c, seq):
        q_ref, k_ref, v_ref, mo_ref, gqk_ref, gv_ref, go_ref, _, sm_ref = z_refs[seq]
        br_ref, c_s, s_s = br_all.at[seq], c_refs[seq], s_refs[seq]
        n_old, m_old = n_refs[seq][...], m_refs[seq][...]
        n_rows, m_rows = [], []
        rows = slice(c * L, (c + 1) * L)
        y0_p, bc_p, rows_p, b_p = prep_refs[seq]

        y0, bc = y0_p[...], bc_p[...]
        y0t, bct = rows_p[0:8, :], rows_p[8:16, :]
        b = b_p[...]
        hm = []
        for h in range(ML_HEADS):
            hs = slice(h * ML_DH, (h + 1) * ML_DH)
            v = v_ref[rows, hs].astype(f32)
            kb = k_ref[rows, hs].astype(bf16)
            qb, vb = q_ref[rows, hs].astype(bf16), v.astype(bf16)
            bcol = bc[:, S_MF + h:S_MF + h + 1]
            icol = y0[:, S_MI + h:S_MI + h + 1]
            brow = bct[S_MF + h:S_MF + h + 1, :]
            irow = y0t[S_MI + h:S_MI + h + 1, :]
            m_row = m_old[h:h + 1, :]
            cmat = c_s[h][...]
            nrow = n_old[h:h + 1, :]
            bcol_r = jnp.broadcast_to(bcol, (L, LANES))
            dm = jnp.where(tri, bcol_r - brow + irow, NEG)
            inter_r = bcol_r + m_row
            m_t_r = jnp.maximum(inter_r, jnp.broadcast_to(jnp.max(dm, axis=1, keepdims=True), (L, LANES)))
            w_intra = jnp.exp(dm - m_t_r) * k_scale
            w_inter_r = jnp.exp(inter_r - m_t_r)
            s = lax.dot_general(qb, kb, _NT, preferred_element_type=f32) * w_intra
            s_hi = s.astype(bf16)
            s_lo = (s - s_hi.astype(f32)).astype(bf16)
            r_intra = jnp.dot(s_hi, jnp.concatenate([vb, ones_b], axis=1), preferred_element_type=f32)
            c_aug = jnp.concatenate([cmat, jnp.broadcast_to(nrow, (LANES, LANES))], axis=0).astype(bf16)
            r_inter = lax.dot_general(qb, c_aug, _NT, preferred_element_type=f32)
            rs_lo = jnp.dot(s_lo, ones_b, preferred_element_type=f32)
            num = r_intra[:, 0:LANES] + w_inter_r * r_inter[:, 0:LANES]
            den_r = r_intra[:, LANES:] + rs_lo + w_inter_r * r_inter[:, LANES:]
            hm.append(num / jnp.maximum(jnp.abs(den_r), jnp.exp(-m_t_r)))
            b_last = bcol[L - 1:L, :]
            m = m_row[:, 0:1]
            g = b_last - bcol + icol
            m_new = jnp.maximum(b_last + m, jnp.max(g, axis=0, keepdims=True))
            w_s = jnp.broadcast_to(jnp.exp(g - m_new) * k_scale, (L, LANES))
            w_c = jnp.exp(b_last + m - m_new)
            vwt = (v * w_s).T.astype(bf16)
            c_s[h][...] = w_c * cmat + jnp.dot(vwt, kb, preferred_element_type=f32)
            n_rows.append(w_c * nrow + jnp.sum(w_s * k_ref[rows, hs].astype(f32), axis=0, keepdims=True))
            m_rows.append(jnp.broadcast_to(m_new, (1, LANES)))
        pad_rows = [jnp.zeros((8 - ML_HEADS, LANES), f32)]
        n_refs[seq][...] = jnp.concatenate(n_rows + pad_rows, axis=0)
        m_refs[seq][...] = jnp.concatenate(m_rows + pad_rows, axis=0)
        y_ml = (_head_norm(jnp.concatenate(hm, axis=1), mlg_ref[...])
                * _sigmoid(mo_ref[rows, :].astype(f32)))
        br_ref[rows, 0:ML_W] = y_ml.astype(br_ref.dtype)

        q2 = gqk_ref[rows, 0:GLA_KW].astype(f32) * q_scale
        k2 = gqk_ref[rows, GLA_KW:2 * GLA_KW].astype(f32)
        gv = gv_ref[rows, :].astype(bf16)
        s_old = [s_s[h][...] for h in range(GLA_HEADS)]
        qe_chunk = (q2 * jnp.exp(b)).astype(bf16)
        b_last = b[L - 1:L, :]
        kdt = (k2 * jnp.exp(b_last - b)).T.astype(bf16)
        decay_col = jnp.exp(b.T[:, L - 1:L])
        o_blocks = []
        for blk in range(L // sub):
            s0, s1 = blk * sub, (blk + 1) * sub
            mid = s0 + sub // 2
            b_blk = b[s0:s1]
            b_mid = b[mid - 1:mid, :]
            qe_d = (q2[s0:s1] * jnp.exp(b_blk - b_mid)).astype(bf16)
            ke_d = (k2[s0:s1] * jnp.exp(b_mid - b_blk)).astype(bf16)
            if blk > 0:
                b_start = b[s0 - 1:s0, :]
                qe_o = (q2[s0:s1] * jnp.exp(b_blk - b_start)).astype(bf16)
                ke_o = (k2[0:s0] * jnp.exp(b_start - b[0:s0])).astype(bf16)
            o_heads = []
            for h in range(GLA_HEADS):
                ks = slice(h * GLA_DK, (h + 1) * GLA_DK)
                vs = slice(h * GLA_DV, (h + 1) * GLA_DV)
                a = lax.dot_general(qe_d[:, ks], ke_d[:, ks], _NT, preferred_element_type=f32)
                a = jnp.where(causal_sub, a, 0.0)
                o = (jnp.dot(a.astype(bf16), gv[s0:s1, vs], preferred_element_type=f32)
                     + jnp.dot(qe_chunk[s0:s1, ks], s_old[h].astype(bf16), preferred_element_type=f32))
                if blk > 0:
                    a = lax.dot_general(qe_o[:, ks], ke_o[:, ks], _NT, preferred_element_type=f32)
                    o = o + jnp.dot(a.astype(bf16), gv[0:s0, vs], preferred_element_type=f32)
                o_heads.append(o)
            o_blocks.append(jnp.concatenate(o_heads, axis=1))
        og = jnp.concatenate(o_blocks, axis=0)
        if c + 1 < tt // L:
            gate_prep(c + 1, seq)
        for h in range(GLA_HEADS):
            ks = slice(h * GLA_DK, (h + 1) * GLA_DK)
            vs = slice(h * GLA_DV, (h + 1) * GLA_DV)
            s_s[h][...] = (decay_col[ks, :] * s_old[h]
                           + jnp.dot(kdt[ks, :], gv[:, vs], preferred_element_type=f32))
        go = go_ref[rows, :].astype(f32)
        y_gla = _head_norm(og, glag_ref[...]) * (go * _sigmoid(go))
        br_ref[rows, ML_W:ML_W + GLA_VW] = y_gla.astype(br_ref.dtype)

    for s in range(nb):
        gate_prep(0, s)
    for c in range(tt // L):
        for s in range(nb):
            one_seq(c, s)

    pos = t_idx * tt + lax.broadcasted_iota(jnp.int32, (tt, 1), 0)
    for s in range(nb):
        pu_ref, br_ref, ext_s = z_refs[s][7], br_all.at[s], ext_refs[s]
        n_ext = tt + POOL_BASE
        ext_s[POOL_BASE:n_ext, :] = pu_ref[...].astype(f32)
        for g, w in enumerate(POOL_WINDOWS):
            gs = slice(g * POOL_GDIM, (g + 1) * POOL_GDIM)
            u = ext_s[POOL_BASE:n_ext, gs]
            p, k = ext_s[:, gs], 1
            while k < w:
                p = p + pltpu.roll(p, k, 0)
                k *= 2
            acc = p[POOL_BASE:n_ext, :]
            cnt = jnp.minimum(pos + 1, w).astype(f32)
            d = acc / cnt - u
            y = jnp.dot(d.astype(bf16), pw_ref[g], preferred_element_type=f32) * ps_ref[:, gs]
            br_ref[:, ML_W + GLA_VW + g * POOL_GDIM:ML_W + GLA_VW + (g + 1) * POOL_GDIM] = y.astype(br_ref.dtype)
        ext_s[8:POOL_BASE, :] = ext_s[tt + 8:n_ext, :]

    @pl.when(t_idx == n_t - 1)
    def _():
        for s in range(nb):
            for h in range(ML_HEADS):
                c_out[s, h] = c_refs[s][h][...]
            for h in range(GLA_HEADS):
                s_out[s, h] = s_refs[s][h][...]
            n_out[s] = n_refs[s][0:ML_HEADS, :]
            m_out[s] = m_refs[s][...]
            buf_out[s] = ext_refs[s][POOL_BASE - POOL_BUF:POOL_BASE, :]


def _prompt_mix(z, zs, lw, batch, seq, tt, chunk, sub, nb):
    n_t = seq // tt
    assert batch % nb == 0

    def zspecs(s):
        row = lambda b, t: (b * nb + s) * n_t + t
        spec = lambda col, width: pl.BlockSpec((tt, width), lambda b, t: (row(b, t), col // width))
        return [spec(C_MQ, 512), spec(C_MK, 512), spec(C_MV, 512), spec(C_MO, 512),
                spec(C_GQ, 512), spec(C_GV, 512), spec(C_GO, 512), spec(C_PU, 512),
                pl.BlockSpec((tt, LANES), lambda b, t: (row(b, t), 0))]

    def full(shape):
        nd = len(shape)
        return pl.BlockSpec(shape, lambda b, t: (0,) * nd)

    def per_seq(shape):
        nd = len(shape)
        return pl.BlockSpec((nb,) + shape, lambda b, t: (b,) + (0,) * nd)

    kern = functools.partial(_prompt_mix_kernel, nb=nb, tt=tt, chunk=chunk, sub=sub, n_t=n_t)
    outs = pl.pallas_call(
        kern,
        grid=(batch // nb, n_t),
        in_specs=[sp for s in range(nb) for sp in zspecs(s)]
                 + [full((1, LANES)), full((1, ML_W)), full((LANES, GLA_KW)), full((1, GLA_KW)),
                    full((1, GLA_VW)), full((len(POOL_WINDOWS), POOL_GDIM, POOL_GDIM)), full((1, POOL_W))],
        out_specs=[
            pl.BlockSpec((nb, tt, N_BRANCH * ML_W), lambda b, t: (b, t, 0)),
            per_seq((ML_HEADS, ML_DH, ML_DH)), per_seq((ML_HEADS, ML_DH)), per_seq((8, LANES)),
            per_seq((GLA_HEADS, GLA_DK, GLA_DV)), per_seq((POOL_BUF, POOL_W)),
        ],
        out_shape=[
            jax.ShapeDtypeStruct((batch, seq, N_BRANCH * ML_W), bf16),
            jax.ShapeDtypeStruct((batch, ML_HEADS, ML_DH, ML_DH), f32),
            jax.ShapeDtypeStruct((batch, ML_HEADS, ML_DH), f32),
            jax.ShapeDtypeStruct((batch, 8, LANES), f32),
            jax.ShapeDtypeStruct((batch, GLA_HEADS, GLA_DK, GLA_DV), f32),
            jax.ShapeDtypeStruct((batch, POOL_BUF, POOL_W), f32),
        ],
        scratch_shapes=([pltpu.VMEM((ML_DH, ML_DH), f32)] * ML_HEADS
                        + [pltpu.VMEM((GLA_DK, GLA_DV), f32)] * GLA_HEADS
                        + [pltpu.VMEM((8, LANES), f32), pltpu.VMEM((8, LANES), f32),
                           pltpu.VMEM((tt + POOL_BASE, POOL_W), f32),
                           pltpu.VMEM((chunk, LANES), f32), pltpu.VMEM((chunk, LANES), f32),
                           pltpu.VMEM((16, chunk), f32), pltpu.VMEM((chunk, GLA_KW), f32)]) * nb,
        compiler_params=_cparams(("parallel", "arbitrary")),
        name="prompt_mix",
    )(*([z] * 8 + [zs]) * nb,
      lw["ifb"], lw["ml_g"], lw["gla_gw"], lw["gla_gb"], lw["gla_g"], lw["pool_w"], lw["pool_scale"])
    return (outs[0].reshape(batch * seq, N_BRANCH * ML_W),) + tuple(outs[1:])


def _sample_mix_kernel(q_ref, k_ref, v_ref, mo_ref, gqk_ref, gv_ref, go_ref, pu_ref, sm_ref,
                       c_in, n_in, m_in, s_in, buf_in,
                       ifb_ref, mlg_ref, gw_ref, gb_ref, glag_ref, pw_ref, ps_ref, *tail, sb, dec):
    br_all = tail[-6]
    c_out, n_out, m_out, s_out, buf_out = (r.at[0] for r in tail[-5:])
    step = pl.program_id(0)
    c_in, n_in, m_in, s_in, buf_in = (r.at[0] for r in (c_in, n_in, m_in, s_in, buf_in))
    br_ref = br_all.at[pl.ds(pl.multiple_of(step * sb, sb), sb), :]

    @pl.when(step == 0)
    def _():
        br_all[dec:, :] = jnp.zeros((br_all.shape[0] - dec, br_all.shape[1]), br_all.dtype)

    sm = sm_ref[...]
    eye = (lax.broadcasted_iota(jnp.int32, (LANES, LANES), 0)
           == lax.broadcasted_iota(jnp.int32, (LANES, LANES), 1)).astype(f32)

    def to_cols(x):
        parts = [lax.dot_general(eye, x[:, j * LANES:(j + 1) * LANES], _NT,
                                 preferred_element_type=f32, precision=lax.Precision.HIGHEST)
                 for j in range(x.shape[1] // LANES)]
        return jnp.concatenate(parts, axis=0)

    y0 = sm + ifb_ref[...]
    logf_all = _log_sigmoid(y0)
    k_scale = ML_DH ** -0.5
    m_all = m_in[...]
    hm, m_new_cols = [], []
    for h in range(ML_HEADS):
        hs = slice(h * ML_DH, (h + 1) * ML_DH)
        q = q_ref[:, hs].astype(f32)
        k = k_ref[:, hs].astype(f32) * k_scale
        v = v_ref[:, hs].astype(f32)
        i_pre = y0[:, S_MI + h:S_MI + h + 1]
        logf = logf_all[:, S_MF + h:S_MF + h + 1]
        m = m_all[:, h:h + 1]
        inter = logf + m
        m_t = jnp.maximum(inter, i_pre)
        w_intra = jnp.exp(i_pre - m_t)
        w_inter = jnp.exp(inter - m_t)
        s = jnp.sum(q * k, axis=1, keepdims=True) * w_intra
        qb = q.astype(bf16)
        n_h = n_in[:, h, :]
        cq = jnp.concatenate(
            [lax.dot_general(qb, c_in[j, h].astype(bf16), _NT, preferred_element_type=f32)[j:j + 1, :]
             for j in range(sb)], axis=0)
        num = s * v + w_inter * cq
        den = s + w_inter * jnp.sum(n_h * q, axis=1, keepdims=True)
        hm.append(num / jnp.maximum(jnp.abs(den), jnp.exp(-m_t)))
        m_new = m_t
        w_s = w_intra
        w_c = w_inter
        n_out[:, h, :] = w_c * n_h + w_s * k
        m_new_cols.append(m_new)
        vw_cols = to_cols(v * w_s)
        for j in range(sb):
            c_out[j, h] = w_c[j:j + 1, :] * c_in[j, h] + vw_cols[:, j:j + 1] * k[j:j + 1, :]
    lane = lax.broadcasted_iota(jnp.int32, (sb, LANES), 1)
    m_pack = jnp.zeros((sb, LANES), f32)
    for h in range(ML_HEADS):
        m_pack = jnp.where(lane == h, m_new_cols[h], m_pack)
    m_out[...] = m_pack
    y_ml = _head_norm(jnp.concatenate(hm, axis=1), mlg_ref[...]) * _sigmoid(mo_ref[...].astype(f32))
    br_ref[:, 0:ML_W] = y_ml.astype(br_ref.dtype)

    log_a = _gla_log_decay(sm, gw_ref, gb_ref)
    decay = jnp.exp(log_a)
    q2 = gqk_ref[:, 0:GLA_KW].astype(f32) * (GLA_DK ** -0.5)
    k2 = gqk_ref[:, GLA_KW:2 * GLA_KW].astype(f32)
    gv = gv_ref[...].astype(f32)
    qe = (q2 * decay).astype(bf16)
    qk = q2 * k2
    k_cols = to_cols(k2)
    decay_cols = to_cols(decay)
    og = []
    for h in range(GLA_HEADS):
        ks = slice(h * GLA_DK, (h + 1) * GLA_DK)
        vs = slice(h * GLA_DV, (h + 1) * GLA_DV)
        a = jnp.sum(qk[:, ks], axis=1, keepdims=True)
        inter = jnp.concatenate(
            [jnp.dot(qe[:, ks], s_in[j, h].astype(bf16), preferred_element_type=f32)[j:j + 1, :]
             for j in range(sb)], axis=0)
        og.append(a * gv[:, vs] + inter)
        for j in range(sb):
            s_out[j, h] = (decay_cols[ks, j:j + 1] * s_in[j, h]
                           + k_cols[ks, j:j + 1] * gv[j:j + 1, vs])
    go = go_ref[...].astype(f32)
    y_gla = _head_norm(jnp.concatenate(og, axis=1), glag_ref[...]) * (go * _sigmoid(go))
    br_ref[:, ML_W:ML_W + GLA_VW] = y_gla.astype(br_ref.dtype)

    u = pu_ref[...].astype(f32)
    rowi = lax.broadcasted_iota(jnp.int32, (POOL_BUF + 1, POOL_GDIM), 0)
    d_rows = []
    for j in range(sb):
        ext = jnp.concatenate([buf_in[j], u[j:j + 1, :]], axis=0)
        buf_out[j] = ext[1:POOL_BUF + 1, :]
        parts = []
        for g, w in enumerate(POOL_WINDOWS):
            gs = slice(g * POOL_GDIM, (g + 1) * POOL_GDIM)
            win = jnp.sum(jnp.where(rowi >= POOL_BUF + 1 - w, ext[:, gs], 0.0), axis=0, keepdims=True)
            parts.append(win / float(w) - u[j:j + 1, gs])
        d_rows.append(jnp.concatenate(parts, axis=1))
    d = jnp.concatenate(d_rows, axis=0)
    for g in range(len(POOL_WINDOWS)):
        gs = slice(g * POOL_GDIM, (g + 1) * POOL_GDIM)
        y = jnp.dot(d[:, gs].astype(bf16), pw_ref[g], preferred_element_type=f32) * ps_ref[:, gs]
        br_ref[:, ML_W + GLA_VW + g * POOL_GDIM:ML_W + GLA_VW + (g + 1) * POOL_GDIM] = y.astype(br_ref.dtype)


def _sample_mix(z, zs, st, prev, layer, lw, row0, dec, sb):
    depth = st[0].shape[0]
    base = row0 // sb
    tail_rows = z.shape[0] - row0

    def zspec(col, width):
        blk = col // width
        return pl.BlockSpec((sb, width), lambda i: (base + i, blk))

    def full(shape):
        nd = len(shape)
        return pl.BlockSpec(shape, lambda i: (0,) * nd)

    st_shapes = [(ML_HEADS, ML_DH, ML_DH), (ML_HEADS, ML_DH), (LANES,),
                 (GLA_HEADS, GLA_DK, GLA_DV), (POOL_BUF, POOL_W)]
    st_specs = [pl.BlockSpec((1, sb) + s, functools.partial(lambda i, nd: (layer, i) + (0,) * nd, nd=len(s)))
                for s in st_shapes]
    n_in = N_ZBLOCKS + len(st_shapes) + 7
    prev = () if prev is None else tuple(prev)
    return pl.pallas_call(
        functools.partial(_sample_mix_kernel, sb=sb, dec=dec),
        grid=(dec // sb,),
        in_specs=[zspec(C_MQ, 512), zspec(C_MK, 512), zspec(C_MV, 512), zspec(C_MO, 512),
                  zspec(C_GQ, 512), zspec(C_GV, 512), zspec(C_GO, 512), zspec(C_PU, 512),
                  pl.BlockSpec((sb, LANES), lambda i: (base + i, 0))]
                 + st_specs
                 + [full((1, LANES)), full((1, ML_W)), full((LANES, GLA_KW)), full((1, GLA_KW)),
                    full((1, GLA_VW)), full((len(POOL_WINDOWS), POOL_GDIM, POOL_GDIM)), full((1, POOL_W))]
                 + [pl.BlockSpec(memory_space=pl.ANY)] * len(prev),
        out_specs=[pl.BlockSpec((tail_rows, N_BRANCH * ML_W), lambda i: (0, 0))] + st_specs,
        out_shape=[jax.ShapeDtypeStruct((tail_rows, N_BRANCH * ML_W), bf16)]
                  + [jax.ShapeDtypeStruct((depth, dec) + s, f32) for s in st_shapes],
        input_output_aliases={n_in + j: 1 + j for j in range(len(prev))},
        compiler_params=_cparams(("arbitrary",)),
        name="sample_mix",
    )(z, z, z, z, z, z, z, z, zs, *st,
      lw["ifb"], lw["ml_g"], lw["gla_gw"], lw["gla_gb"], lw["gla_g"], lw["pool_w"], lw["pool_scale"], *prev)


def _merge_kernel(*refs, tm, n_valid, n_prompt_tiles, x_starts, n_extra, tf):
    x_refs = refs[:len(x_starts)]
    brp_ref, brt_ref, g0_ref, g1_ref, g2_ref, wb_ref, wo_ref = refs[len(x_starts):len(x_starts) + 7]
    extra_in = refs[len(x_starts) + 7:len(x_starts) + 7 + n_extra]
    o_ref = refs[len(x_starts) + 7 + n_extra]
    route_out = refs[len(x_starts) + 8 + n_extra:]
    in_tail = pl.program_id(0) >= n_prompt_tiles
    mixed = None
    for n, g_ref in enumerate((g0_ref, g1_ref, g2_ref)):
        cols = slice(n * ML_W, (n + 1) * ML_W)
        br = jnp.where(in_tail, brt_ref[:, cols], brp_ref[:, cols])
        proj = jnp.dot(br, wb_ref[n], preferred_element_type=f32)
        term = _sigmoid(g_ref[...].astype(f32)) * proj
        mixed = term if mixed is None else mixed + term
    out = (_pick_rows(x_refs, x_starts)
           + jnp.dot(mixed.astype(bf16), wo_ref[...], preferred_element_type=f32))
    row = pl.program_id(0) * tm + lax.broadcasted_iota(jnp.int32, (tm, 1), 0)
    out = jnp.where(row < n_valid, out, 0.0)
    if n_extra == 4:
        out = _swiglu_residual(out, *extra_in, tf=tf)
    o_ref[...] = out
    if n_extra == 2:
        _route(out, *extra_in, *route_out, tm=tm, n_valid=n_valid)


def _merge(x_parts, br_prompt, br_tail, z, w_branch, w_out, tm, n_valid, route_w=None, ffn_w=None, tf=None):
    mp = sum(p.shape[0] for p in x_parts)
    nt = mp // tm
    route = route_w is not None
    assert not (route and ffn_w is not None)
    extra = tuple(route_w or ffn_w or ())
    route_specs = [_resident(a.shape) for a in extra]
    out_specs = [pl.BlockSpec((tm, D_MODEL), lambda i: (i, 0))]
    out_shape = [jax.ShapeDtypeStruct((mp, D_MODEL), f32)]
    if route:
        out_specs += [pl.BlockSpec((tm, D_MODEL), lambda i: (i, 0)),
                      pl.BlockSpec((tm, LANES), lambda i: (i, 0)),
                      pl.BlockSpec((tm, LANES), lambda i: (i, 0)),
                      pl.BlockSpec((1, N_EXPERTS, tm), lambda i: (i, 0, 0)),
                      pl.BlockSpec((1, 1, LANES), lambda i: (i, 0, 0))]
        out_shape += [jax.ShapeDtypeStruct((mp, D_MODEL), bf16),
                      jax.ShapeDtypeStruct((mp, LANES), f32),
                      jax.ShapeDtypeStruct((mp, LANES), f32),
                      jax.ShapeDtypeStruct((nt, N_EXPERTS, tm), f32),
                      jax.ShapeDtypeStruct((nt, 1, LANES), f32)]
    x_starts = tuple(sum(p.shape[0] for p in x_parts[:j]) // tm for j in range(len(x_parts)))
    n_p = br_prompt.shape[0] // tm
    gate_specs = [pl.BlockSpec((tm, D_MODEL), functools.partial(lambda i, n: (i, C_GATES // D_MODEL + n), n=n))
                  for n in range(N_BRANCH)]
    outs = pl.pallas_call(
        functools.partial(_merge_kernel, tm=tm, n_valid=n_valid, n_prompt_tiles=n_p, x_starts=x_starts,
                          n_extra=len(extra), tf=tf),
        grid=(nt,),
        in_specs=[
            *_split_rows_specs(x_parts, tm, D_MODEL),
            pl.BlockSpec((tm, N_BRANCH * ML_W), lambda i: (jnp.minimum(i, n_p - 1), 0)),
            pl.BlockSpec((tm, N_BRANCH * ML_W), lambda i: (jnp.maximum(i - n_p, 0), 0)),
            *gate_specs,
            _resident((N_BRANCH, ML_W, D_MODEL)),
            _resident((D_MODEL, D_MODEL)),
            *route_specs,
        ],
        out_specs=out_specs,
        out_shape=out_shape,
        compiler_params=_cparams(("parallel",)),
        name="merge_route" if route else ("merge_ffn" if extra else "merge"),
    )(*x_parts, br_prompt, br_tail, z, z, z, w_branch, w_out, *extra)
    return outs if route else outs[0]


def _route(x, g_ref, wr_ref, hb_ref, comb_ref, rnk_ref, rnkt_ref, cnt_ref, *, tm, n_valid):
    h = _rms(x, g_ref[...])
    h_hi = h.astype(bf16)
    hb_ref[...] = h_hi
    h_lo = (h - h_hi.astype(f32)).astype(bf16)
    p_hi = jnp.dot(h_hi, wr_ref[...], preferred_element_type=f32)
    p_lo = jnp.dot(h_lo, wr_ref[...], preferred_element_type=f32)
    logits = p_hi + p_lo + pltpu.roll(p_hi, LANES - N_EXPERTS, 1)
    lane = lax.broadcasted_iota(jnp.int32, logits.shape, 1)
    valid = lane < N_EXPERTS
    logits = jnp.where(valid, logits, NEG)
    mx = jnp.max(logits, axis=1, keepdims=True)
    e = jnp.where(valid, jnp.exp(logits - mx), 0.0)
    probs = e / jnp.sum(e, axis=1, keepdims=True)
    p1 = jnp.max(probs, axis=1, keepdims=True)
    i1 = jnp.min(jnp.where(probs == p1, lane, LANES), axis=1, keepdims=True)
    rest = jnp.where((lane == i1) | ~valid, -1.0, probs)
    p2 = jnp.max(rest, axis=1, keepdims=True)
    i2 = jnp.min(jnp.where(rest == p2, lane, LANES), axis=1, keepdims=True)
    tot = p1 + p2
    comb_ref[...] = jnp.where(lane == i1, p1 / tot, 0.0) + jnp.where(lane == i2, p2 / tot, 0.0)
    row = pl.program_id(0) * tm + lax.broadcasted_iota(jnp.int32, logits.shape, 0)
    sel = ((lane == i1) | (lane == i2)) & (row < n_valid)
    r = lax.broadcasted_iota(jnp.int32, (tm, tm), 0)
    c = lax.broadcasted_iota(jnp.int32, (tm, tm), 1)
    rank = jnp.dot((c < r).astype(bf16), sel.astype(bf16), preferred_element_type=f32)
    rnk = jnp.where(sel, rank, -1.0)
    rnk_ref[...] = rnk
    rnkt_ref[0] = rnk.T[0:N_EXPERTS, :]
    cnt_ref[0] = jnp.sum(sel.astype(f32), axis=0, keepdims=True)


def _moe_kernel(rounds_ref, hb_ref, rnk_ref, rnkt_ref, comb_ref, wg_ref, wu_ref, wd_ref, o_ref,
                *, cap, n_sub, ts):
    i, e = pl.program_id(0), pl.program_id(1)

    @pl.when(e == 0)
    def _():
        o_ref[...] = jnp.zeros_like(o_ref)

    lane = lax.broadcasted_iota(jnp.int32, (ts, LANES), 1)
    slot_rows = lax.broadcasted_iota(jnp.int32, (cap, ts), 0).astype(f32)
    slot_cols = lax.broadcasted_iota(jnp.int32, (ts, cap), 1).astype(f32)

    def round_body(r, carry):
        base = (r * cap).astype(f32)
        xs = []
        for j in range(n_sub):
            rt = rnkt_ref[j, pl.ds(e, 1), :]
            p = (rt - base == slot_rows).astype(bf16)
            xs.append(jnp.dot(p, hb_ref[j * ts:(j + 1) * ts, :],
                              preferred_element_type=f32).astype(bf16))
        xs = jnp.concatenate(xs, axis=0)
        a = jnp.dot(xs, wg_ref[0], preferred_element_type=f32)
        a = (a * _sigmoid(a)) * jnp.dot(xs, wu_ref[0], preferred_element_type=f32)
        y = jnp.dot(a.astype(bf16), wd_ref[0], preferred_element_type=f32).astype(bf16)
        for j in range(n_sub):
            rows = slice(j * ts, (j + 1) * ts)
            col = jnp.sum(jnp.where(lane == e, rnk_ref[rows, :], 0.0), axis=1, keepdims=True)
            w = jnp.sum(jnp.where(lane == e, comb_ref[rows, :], 0.0), axis=1, keepdims=True)
            pt = (col - base == slot_cols).astype(bf16)
            o_ref[rows, :] += w * jnp.dot(pt, y[j * cap:(j + 1) * cap, :], preferred_element_type=f32)
        return carry

    lax.fori_loop(0, rounds_ref[i * N_EXPERTS + e], round_body, 0)


def _moe(hb, rnk, rnkt, comb, rounds, wg, wu, wd, ts, n_sub, cap):
    mp = hb.shape[0]
    n_e, _, d_ff = wg.shape
    tsup = ts * n_sub
    grid_spec = pltpu.PrefetchScalarGridSpec(
        num_scalar_prefetch=1,
        grid=(mp // tsup, n_e),
        in_specs=[pl.BlockSpec((tsup, D_MODEL), lambda i, e, r: (i, 0)),
                  pl.BlockSpec((tsup, LANES), lambda i, e, r: (i, 0)),
                  pl.BlockSpec((n_sub, N_EXPERTS, ts), lambda i, e, r: (i, 0, 0)),
                  pl.BlockSpec((tsup, LANES), lambda i, e, r: (i, 0)),
                  pl.BlockSpec((1, D_MODEL, d_ff), lambda i, e, r: (e, 0, 0)),
                  pl.BlockSpec((1, D_MODEL, d_ff), lambda i, e, r: (e, 0, 0)),
                  pl.BlockSpec((1, d_ff, D_MODEL), lambda i, e, r: (e, 0, 0))],
        out_specs=pl.BlockSpec((tsup, D_MODEL), lambda i, e, r: (i, 0)),
    )
    return pl.pallas_call(
        functools.partial(_moe_kernel, cap=cap, n_sub=n_sub, ts=ts),
        grid_spec=grid_spec,
        out_shape=jax.ShapeDtypeStruct((mp, D_MODEL), f32),
        compiler_params=_cparams(("parallel", "arbitrary")),
        name="moe",
    )(rounds, hb, rnk, rnkt, comb, wg, wu, wd)


def _swiglu_residual(x, g_ref, wg_ref, wu_ref, wd_ref, *, tf):
    h = _rms(x, g_ref[...]).astype(bf16)
    acc = x
    assert wg_ref.shape[1] % tf == 0
    for c in range(wg_ref.shape[1] // tf):
        cols = slice(c * tf, (c + 1) * tf)
        a = jnp.dot(h, wg_ref[:, cols], preferred_element_type=f32)
        a = (a * _sigmoid(a)) * jnp.dot(h, wu_ref[:, cols], preferred_element_type=f32)
        acc = acc + jnp.dot(a.astype(bf16), wd_ref[cols, :], preferred_element_type=f32)
    return acc


def _final_norm_kernel(x_ref, y_ref, g_ref, op_ref, os_ref, *, n_prompt_tiles, dec):
    i = pl.program_id(0)
    out = _rms(x_ref[...] + y_ref[...], g_ref[...])

    @pl.when(i < n_prompt_tiles)
    def _():
        op_ref[...] = out

    @pl.when(i == n_prompt_tiles)
    def _():
        os_ref[...] = out[0:dec]


def _final_norm(x, y, g, tm, m_prompt, dec):
    n_p = m_prompt // tm
    assert n_p * tm == m_prompt and dec <= tm
    return pl.pallas_call(
        functools.partial(_final_norm_kernel, n_prompt_tiles=n_p, dec=dec),
        grid=(n_p + 1,),
        in_specs=[pl.BlockSpec((tm, D_MODEL), lambda i: (i, 0)),
                  pl.BlockSpec((tm, D_MODEL), lambda i: (i, 0)),
                  pl.BlockSpec((1, D_MODEL), lambda i: (0, 0))],
        out_specs=[pl.BlockSpec((tm, D_MODEL), lambda i: (jnp.minimum(i, n_p - 1), 0)),
                   pl.BlockSpec((dec, D_MODEL), lambda i: (0, 0))],
        out_shape=[jax.ShapeDtypeStruct((m_prompt, D_MODEL), f32),
                   jax.ShapeDtypeStruct((dec, D_MODEL), f32)],
        compiler_params=_cparams(("arbitrary",)),
        name="final_norm",
    )(x, y, g)


O_MI = 3 * ML_W
O_MO = O_MI + 2 * ML_HEADS
O_GLR = O_MO + ML_W + 2 * GLA_KW + GLA_VW
O_GO = O_GLR + GLA_RANK
D_IN = O_GO + GLA_VW + POOL_W + N_BRANCH * D_MODEL


def _regroup_kernel(w_ref, if_ref, glr_ref, main_ref, small_ref):
    main_ref[0] = w_ref[0].astype(bf16)
    n_if, n_glr = if_ref.shape[1], glr_ref.shape[1]
    small_ref[0, 0:n_if, :] = if_ref[0].astype(bf16)
    small_ref[0, n_if:n_if + n_glr, :] = glr_ref[0].astype(bf16)
    small_ref[0, n_if + n_glr:, :] = jnp.zeros((LANES - n_if - n_glr, small_ref.shape[2]), bf16)


def _regroup_w_in(wt, tr=512):
    depth, n, d = wt.shape
    assert n == D_IN and Z_MAIN == D_IN - 2 * ML_HEADS - GLA_RANK
    assert O_MI % tr == 0 and (O_GLR - O_MO) % tr == 0 and (D_IN - O_GO) % tr == 0
    c1, c2 = O_MI // tr, (O_MI + O_GLR - O_MO) // tr

    def src_row(c):
        skip = jnp.where(c < c1, 0, jnp.where(c < c2, O_MO - O_MI, O_MO - O_MI + O_GO - O_GLR))
        return pl.multiple_of(c * tr + skip, 8)

    return pl.pallas_call(
        _regroup_kernel,
        grid=(depth, Z_MAIN // tr),
        in_specs=[pl.BlockSpec((pl.Element(1), pl.Element(rows), pl.Element(d)), index_map)
                  for rows, index_map in ((tr, lambda l, c: (l, src_row(c), 0)),
                                          (O_MO - O_MI, lambda l, c: (l, O_MI, 0)),
                                          (O_GO - O_GLR, lambda l, c: (l, O_GLR, 0)))],
        out_specs=[pl.BlockSpec((1, tr, d), lambda l, c: (l, c, 0)),
                   pl.BlockSpec((1, LANES, d), lambda l, c: (l, 0, 0))],
        out_shape=[jax.ShapeDtypeStruct((depth, Z_MAIN, d), bf16),
                   jax.ShapeDtypeStruct((depth, LANES, d), bf16)],
        compiler_params=_cparams(("parallel", "arbitrary")),
        name="regroup_w_in",
    )(wt, wt, wt)


def _layer_weights(l, norm1_g, if_bias, ml_g, gla_gw, gla_gb, gla_g, pool_w, pool_scale,
                   w_branch, w_out):
    ifb = jnp.zeros((1, LANES), f32)
    ifb = ifb.at[0, S_MI:S_MI + ML_HEADS].set(if_bias[l, 0]).at[0, S_MF:S_MF + ML_HEADS].set(if_bias[l, 1])
    gw = jnp.zeros((LANES, GLA_KW), f32).at[S_GLR:S_GLR + GLA_RANK].set(gla_gw[l]).astype(bf16)
    return dict(
        norm1_g=norm1_g[l][None], ifb=ifb, ml_g=ml_g[l][None],
        gla_gw=gw, gla_gb=gla_gb[l][None], gla_g=gla_g[l][None], pool_w=pool_w[l].astype(bf16),
        pool_scale=pool_scale[l][None], w_branch=w_branch[l].astype(bf16), w_out=w_out[l].astype(bf16))


def _forward(x_prompt, x_sample, state_mlstm_C, state_mlstm_n, state_mlstm_m, state_gla_S,
             state_pool_buf, norm1_g, w_in, mlstm_if_bias, mlstm_norm_g, gla_gate_w, gla_gate_b,
             gla_norm_g, pool_w, pool_scale, w_branch, w_out, norm2_g, ffn_wg, ffn_wu, ffn_wd,
             router_w, moe_wg, moe_wu, moe_wd, final_norm_g, *, tm, tn, tt, chunk, sub, nb, sb, tf_dense, moe_sub, moe_cap):
    batch, seq, _ = x_prompt.shape
    dec = x_sample.shape[0]
    depth = w_in.shape[0]
    m_prompt = batch * seq
    m_all = m_prompt + dec
    mp = -(-m_all // tm) * tm
    assert m_prompt % tm == 0
    x_parts = (x_prompt.reshape(m_prompt, D_MODEL),
               jnp.concatenate([x_sample.reshape(dec, D_MODEL), jnp.zeros((mp - m_all, D_MODEL), f32)], axis=0))
    outs = [[] for _ in range(5)]
    moe_out = None
    st = (state_mlstm_C, state_mlstm_n, jnp.pad(state_mlstm_m, ((0, 0), (0, 0), (0, LANES - ML_HEADS))),
          state_gla_S, state_pool_buf)
    st_new = None
    w_main, w_small = _regroup_w_in(jnp.swapaxes(w_in, 1, 2))
    for l in range(depth):
        lw = _layer_weights(l, norm1_g, mlstm_if_bias, mlstm_norm_g, gla_gate_w, gla_gate_b,
                            gla_norm_g, pool_w, pool_scale, w_branch, w_out)
        z, zs = _norm_matmul(x_parts, lw["norm1_g"], w_main, w_small, l, tm, tn)
        br_p, c_p, n_p, m_p, s_p, buf_p = _prompt_mix(z, zs, lw, batch, seq, tt, chunk, sub, nb)
        br_t, *st_new = _sample_mix(z, zs, st, st_new, l, lw, m_prompt, dec, sb)
        j = l // 2
        if l % 2 == 0:
            x = _merge(x_parts, br_p, br_t, z, lw["w_branch"], lw["w_out"], tm, m_all,
                       ffn_w=(norm2_g[l][None], ffn_wg[j].astype(bf16), ffn_wu[j].astype(bf16),
                              ffn_wd[j].astype(bf16)), tf=tf_dense)
        else:
            wr_hi = router_w[j].astype(bf16)
            wr_lo = (router_w[j] - wr_hi.astype(f32)).astype(bf16)
            wr = jnp.pad(jnp.concatenate([wr_hi, wr_lo], axis=1), ((0, 0), (0, LANES - 2 * N_EXPERTS)))
            x, hb, comb, rnk, rnkt, cnt = _merge(x_parts, br_p, br_t, z, lw["w_branch"], lw["w_out"], tm, m_all,
                                                 route_w=(norm2_g[l][None], wr))
            n_sup = mp // (tm * moe_sub)
            cnt = cnt[:, 0, :N_EXPERTS].reshape(n_sup, moe_sub, N_EXPERTS).max(axis=1)
            rounds = jnp.ceil(cnt / moe_cap).astype(jnp.int32).reshape(-1)
            moe_out = _moe(hb, rnk, rnkt, comb, rounds, moe_wg[j].astype(bf16), moe_wu[j].astype(bf16),
                           moe_wd[j].astype(bf16), tm, moe_sub, moe_cap)
            if l + 1 < depth:
                x = x + moe_out
                moe_out = None
        x_parts = (x,)
        for lst, val in zip(outs, (c_p, n_p, m_p[:, :ML_HEADS, 0], s_p, buf_p)):
            lst.append(val)
    if moe_out is None:
        moe_out = jnp.zeros_like(x)
    y_prompt, y_sample = _final_norm(x, moe_out, final_norm_g[None], tm, m_prompt, dec)
    c_p, n_p, m_p, s_p, buf_p = (jnp.stack(o) for o in outs)
    c_s, n_s, m_s, s_s, buf_s = st_new
    return (y_prompt.reshape(batch, seq, D_MODEL), y_sample.reshape(dec, 1, D_MODEL),
            c_p, c_s, n_p, n_s, m_p, m_s[:, :, :ML_HEADS], s_p, s_s, buf_p, buf_s)


def kernel(x_prompt, x_sample, state_mlstm_C, state_mlstm_n, state_mlstm_m, state_gla_S, state_pool_buf, norm1_g, w_in, mlstm_if_bias, mlstm_norm_g, gla_gate_w, gla_gate_b, gla_norm_g, pool_w, pool_scale, w_branch, w_out, norm2_g, ffn_wg, ffn_wu, ffn_wd, router_w, moe_wg, moe_wu, moe_wd, final_norm_g):
    return _forward(x_prompt, x_sample, state_mlstm_C, state_mlstm_n, state_mlstm_m, state_gla_S,
                    state_pool_buf, norm1_g, w_in, mlstm_if_bias, mlstm_norm_g, gla_gate_w, gla_gate_b,
                    gla_norm_g, pool_w, pool_scale, w_branch, w_out, norm2_g, ffn_wg, ffn_wu, ffn_wd,
                    router_w, moe_wg, moe_wu, moe_wd, final_norm_g,
                    tm=512, tn=1024, tt=512, chunk=128, sub=64, nb=2, sb=16, tf_dense=MXU_COLS, moe_sub=3, moe_cap=160)
```

```python
import functools

import jax
import jax.numpy as jnp
from jax import lax
from jax.experimental import pallas as pl
from jax.experimental.pallas import tpu as pltpu

f32 = jnp.float32
bf16 = jnp.bfloat16

D_MODEL = 1024
ML_HEADS, ML_DH = 4, 128
ML_W = ML_HEADS * ML_DH
GLA_HEADS, GLA_DK, GLA_DV = 4, 64, 128
GLA_KW, GLA_VW = GLA_HEADS * GLA_DK, GLA_HEADS * GLA_DV
GLA_RANK = 16
GLA_TAU = 16.0
POOL_GDIM = 128
POOL_WINDOWS = (2, 4, 8, 16)
POOL_W = POOL_GDIM * len(POOL_WINDOWS)
POOL_BUF = 15
POOL_BASE = 24
N_BRANCH = 3
N_EXPERTS = 8
EPS = 1e-6
NEG = -1e30
LANES = 128
MXU_COLS = 256

C_MQ, C_MK, C_MV, C_MO = 0, 512, 1024, 1536
C_GQ, C_GK, C_GV, C_GO, C_PU, C_GATES = 2048, 2304, 2560, 3072, 3584, 4096
Z_MAIN = C_GATES + N_BRANCH * D_MODEL
S_MI, S_MF, S_GLR = 0, 4, 8

VMEM_LIMIT = 56 * 1024 * 1024

_NT = (((1,), (1,)), ((), ()))


def _cparams(sem):
    return pltpu.CompilerParams(dimension_semantics=sem, vmem_limit_bytes=VMEM_LIMIT)


def _log_sigmoid(x):
    return jnp.minimum(x, 0.0) - jnp.log(1.0 + jnp.exp(-jnp.abs(x)))


def _sigmoid(x):
    return 0.5 * jnp.tanh(0.5 * x) + 0.5


def _rms(x, g):
    ms = jnp.mean(x * x, axis=-1, keepdims=True)
    return x * lax.rsqrt(ms + EPS) * g


def _lower_tri(n):
    r = lax.broadcasted_iota(jnp.int32, (n, n), 0)
    c = lax.broadcasted_iota(jnp.int32, (n, n), 1)
    return c <= r


def _cumsum_rows(tri_bf16, a):
    a1 = a.astype(bf16)
    r = a - a1.astype(f32)
    a2 = r.astype(bf16)
    a3 = (r - a2.astype(f32)).astype(bf16)
    d = lambda y: jnp.dot(tri_bf16, y, preferred_element_type=f32)
    return d(a1) + d(a2) + d(a3)


def _cumsum_lanes(triu_bf16, a):
    a1 = a.astype(bf16)
    r = a - a1.astype(f32)
    a2 = r.astype(bf16)
    a3 = (r - a2.astype(f32)).astype(bf16)
    d = lambda y: jnp.dot(y, triu_bf16, preferred_element_type=f32)
    return d(a1) + d(a2) + d(a3)


def _resident(shape):
    nd = len(shape)
    return pl.BlockSpec(shape, lambda *_: (0,) * nd, pipeline_mode=pl.Buffered(1))


def _split_rows_specs(parts, tm, width):
    specs, start = [], 0
    for p in parts:
        n_t = p.shape[0] // tm
        assert n_t * tm == p.shape[0]
        specs.append(pl.BlockSpec(
            (tm, width), functools.partial(lambda i, s, n: (jnp.clip(i - s, 0, n - 1), 0), s=start, n=n_t)))
        start += n_t
    return specs


def _pick_rows(refs, starts):
    i = pl.program_id(0)
    x = refs[0][...]
    for r, s in zip(refs[1:], starts[1:]):
        x = jnp.where(i >= s, r[...], x)
    return x


def _norm_matmul_kernel(*refs, tn, starts):
    x_refs = refs[:len(starts)]
    g_ref, w_ref, ws_ref, z_ref, zs_ref = refs[len(starts):]
    h = _rms(_pick_rows(x_refs, starts), g_ref[...]).astype(bf16)
    zs_ref[...] = lax.dot_general(h, ws_ref[0], _NT, preferred_element_type=f32)
    for c in range(w_ref.shape[1] // tn):
        cols = slice(c * tn, (c + 1) * tn)
        z_ref[:, cols] = lax.dot_general(h, w_ref[0, cols, :], _NT,
                                         preferred_element_type=f32).astype(z_ref.dtype)


def _norm_matmul(x_parts, g, wt_main, wt_small, layer, tm, tn):
    mp = sum(p.shape[0] for p in x_parts)
    n = wt_main.shape[1]
    starts = tuple(sum(p.shape[0] for p in x_parts[:j]) // tm for j in range(len(x_parts)))
    layer_block = lambda width: pl.BlockSpec((1, width, D_MODEL), lambda i: (layer, 0, 0),
                                             pipeline_mode=pl.Buffered(1))
    return pl.pallas_call(
        functools.partial(_norm_matmul_kernel, tn=tn, starts=starts),
        grid=(mp // tm,),
        in_specs=[
            *_split_rows_specs(x_parts, tm, D_MODEL),
            _resident((1, D_MODEL)),
            layer_block(n),
            layer_block(LANES),
        ],
        out_specs=[
            pl.BlockSpec((tm, n), lambda i: (i, 0)),
            pl.BlockSpec((tm, LANES), lambda i: (i, 0)),
        ],
        out_shape=[jax.ShapeDtypeStruct((mp, n), bf16), jax.ShapeDtypeStruct((mp, LANES), f32)],
        compiler_params=_cparams(("parallel",)),
        name="norm_matmul",
    )(*x_parts, g, wt_main, wt_small)


def _head_norm(h, g):
    outs = []
    for j in range(h.shape[1] // LANES):
        hj = h[:, j * LANES:(j + 1) * LANES]
        outs.append(hj * lax.rsqrt(jnp.mean(hj * hj, axis=-1, keepdims=True) + EPS))
    return jnp.concatenate(outs, axis=1) * g


def _gla_log_decay(sm, gw_ref, gb_ref):
    xg = jnp.dot(sm.astype(bf16), gw_ref[...], preferred_element_type=f32) + gb_ref[...]
    return _log_sigmoid(xg) * (1.0 / GLA_TAU)


N_ZBLOCKS = 9


def _prompt_mix_kernel(*refs, nb, tt, chunk, sub, n_t):
    z_refs = [refs[s * N_ZBLOCKS:(s + 1) * N_ZBLOCKS] for s in range(nb)]
    ifb_ref, mlg_ref, gw_ref, gb_ref, glag_ref, pw_ref, ps_ref = refs[nb * N_ZBLOCKS:nb * N_ZBLOCKS + 7]
    br_all, c_out, n_out, m_out, s_out, buf_out = refs[nb * N_ZBLOCKS + 7:nb * N_ZBLOCKS + 13]
    scratch = refs[nb * N_ZBLOCKS + 13:]
    per = ML_HEADS + GLA_HEADS + 7
    c_refs = [scratch[s * per:s * per + ML_HEADS] for s in range(nb)]
    s_refs = [scratch[s * per + ML_HEADS:s * per + ML_HEADS + GLA_HEADS] for s in range(nb)]
    n_refs = [scratch[s * per + ML_HEADS + GLA_HEADS] for s in range(nb)]
    m_refs = [scratch[s * per + ML_HEADS + GLA_HEADS + 1] for s in range(nb)]
    ext_refs = [scratch[s * per + ML_HEADS + GLA_HEADS + 2] for s in range(nb)]
    prep_refs = [scratch[s * per + ML_HEADS + GLA_HEADS + 3:(s + 1) * per] for s in range(nb)]
    t_idx = pl.program_id(1)
    L = chunk

    @pl.when(t_idx == 0)
    def _():
        for s in range(nb):
            for r in (*c_refs[s], *s_refs[s], n_refs[s], m_refs[s]):
                r[...] = jnp.zeros_like(r)
            ext_refs[s][0:POOL_BASE, :] = jnp.zeros((POOL_BASE, POOL_W), f32)

    tri = _lower_tri(L)
    tri_b = tri.astype(bf16)
    causal_sub = _lower_tri(sub)
    assert L == LANES
    ones_b = jnp.ones((L, LANES), bf16)
    triu_b = (lax.broadcasted_iota(jnp.int32, (L, L), 0)
              <= lax.broadcasted_iota(jnp.int32, (L, L), 1)).astype(bf16)
    ifb = ifb_ref[...]
    k_scale = ML_DH ** -0.5
    q_scale = GLA_DK ** -0.5

    def gate_prep(c, seq):
        sm_ref = z_refs[seq][8]
        y0_p, bc_p, rows_p, b_p = prep_refs[seq]
        sm = sm_ref[c * L:(c + 1) * L, :]
        y0 = sm + ifb
        y0_p[...] = y0
        bc_p[...] = _cumsum_rows(tri_b, _log_sigmoid(y0))
        y0t = y0.T[0:8, :]
        rows_p[0:8, :] = y0t
        rows_p[8:16, :] = _cumsum_lanes(triu_b, _log_sigmoid(y0t))
        b_p[...] = _cumsum_rows(tri_b, _gla_log_decay(sm, gw_ref, gb_ref))

    def one_seq(c, seq):
        q_ref, k_ref, v_ref, mo_ref, gqk_ref, gv_ref, go_ref, _, sm_ref = z_refs[seq]
        br_ref, c_s, s_s = br_all.at[seq], c_refs[seq], s_refs[seq]
        n_old, m_old = n_refs[seq][...], m_refs[seq][...]
        n_rows, m_rows = [], []
        rows = slice(c * L, (c + 1) * L)
        y0_p, bc_p, rows_p, b_p = prep_refs[seq]

        y0, bc = y0_p[...], bc_p[...]
        y0t, bct = rows_p[0:8, :], rows_p[8:16, :]
        b = b_p[...]
        hm = []
        for h in range(ML_HEADS):
            hs = slice(h * ML_DH, (h + 1) * ML_DH)
            v = v_ref[rows, hs].astype(f32)
            kb = k_ref[rows, hs].astype(bf16)
            qb, vb = q_ref[rows, hs].astype(bf16), v.astype(bf16)
            bcol = bc[:, S_MF + h:S_MF + h + 1]
            icol = y0[:, S_MI + h:S_MI + h + 1]
            brow = bct[S_MF + h:S_MF + h + 1, :]
            irow = y0t[S_MI + h:S_MI + h + 1, :]
            m_row = m_old[h:h + 1, :]
            cmat = c_s[h][...]
            nrow = n_old[h:h + 1, :]
            bcol_r = jnp.broadcast_to(bcol, (L, LANES))
            dm = jnp.where(tri, bcol_r - brow + irow, NEG)
            inter_r = bcol_r + m_row
            m_t_r = jnp.maximum(inter_r, jnp.broadcast_to(jnp.max(dm, axis=1, keepdims=True), (L, LANES)))
            w_intra = jnp.exp(dm - m_t_r) * k_scale
            w_inter_r = jnp.exp(inter_r - m_t_r)
            s = lax.dot_general(qb, kb, _NT, preferred_element_type=f32) * w_intra
            s_hi = s.astype(bf16)
            s_lo = (s - s_hi.astype(f32)).astype(bf16)
            r_intra = jnp.dot(s_hi, jnp.concatenate([vb, ones_b], axis=1), preferred_element_type=f32)
            c_aug = jnp.concatenate([cmat, jnp.broadcast_to(nrow, (LANES, LANES))], axis=0).astype(bf16)
            r_inter = lax.dot_general(qb, c_aug, _NT, preferred_element_type=f32)
            rs_lo = jnp.dot(s_lo, ones_b, preferred_element_type=f32)
            num = r_intra[:, 0:LANES] + w_inter_r * r_inter[:, 0:LANES]
            den_r = r_intra[:, LANES:] + rs_lo + w_inter_r * r_inter[:, LANES:]
            hm.append(num / jnp.maximum(jnp.abs(den_r), jnp.exp(-m_t_r)))
            b_last = bcol[L - 1:L, :]
            m = m_row[:, 0:1]
            g = b_last - bcol + icol
            m_new = jnp.maximum(b_last + m, jnp.max(g, axis=0, keepdims=True))
            w_s = jnp.broadcast_to(jnp.exp(g - m_new) * k_scale, (L, LANES))
            w_c = jnp.exp(b_last + m - m_new)
            vwt = (v * w_s).T.astype(bf16)
            c_s[h][...] = w_c * cmat + jnp.dot(vwt, kb, preferred_element_type=f32)
            n_rows.append(w_c * nrow + jnp.sum(w_s * k_ref[rows, hs].astype(f32), axis=0, keepdims=True))
            m_rows.append(jnp.broadcast_to(m_new, (1, LANES)))
        pad_rows = [jnp.zeros((8 - ML_HEADS, LANES), f32)]
        n_refs[seq][...] = jnp.concatenate(n_rows + pad_rows, axis=0)
        m_refs[seq][...] = jnp.concatenate(m_rows + pad_rows, axis=0)
        y_ml = (_head_norm(jnp.concatenate(hm, axis=1), mlg_ref[...])
                * _sigmoid(mo_ref[rows, :].astype(f32)))
        br_ref[rows, 0:ML_W] = y_ml.astype(br_ref.dtype)

        q2 = gqk_ref[rows, 0:GLA_KW].astype(f32) * q_scale
        k2 = gqk_ref[rows, GLA_KW:2 * GLA_KW].astype(f32)
        gv = gv_ref[rows, :].astype(bf16)
        s_old = [s_s[h][...] for h in range(GLA_HEADS)]
        qe_chunk = (q2 * jnp.exp(b)).astype(bf16)
        b_last = b[L - 1:L, :]
        kdt = (k2 * jnp.exp(b_last - b)).T.astype(bf16)
        decay_col = jnp.exp(b.T[:, L - 1:L])
        o_blocks = []
        for blk in range(L // sub):
            s0, s1 = blk * sub, (blk + 1) * sub
            mid = s0 + sub // 2
            b_blk = b[s0:s1]
            b_mid = b[mid - 1:mid, :]
            qe_d = (q2[s0:s1] * jnp.exp(b_blk - b_mid)).astype(bf16)
            ke_d = (k2[s0:s1] * jnp.exp(b_mid - b_blk)).astype(bf16)
            if blk > 0:
                b_start = b[s0 - 1:s0, :]
                qe_o = (q2[s0:s1] * jnp.exp(b_blk - b_start)).astype(bf16)
                ke_o = (k2[0:s0] * jnp.exp(b_start - b[0:s0])).astype(bf16)
            o_heads = []
            for h in range(GLA_HEADS):
                ks = slice(h * GLA_DK, (h + 1) * GLA_DK)
                vs = slice(h * GLA_DV, (h + 1) * GLA_DV)
                a = lax.dot_general(qe_d[:, ks], ke_d[:, ks], _NT, preferred_element_type=f32)
                a = jnp.where(causal_sub, a, 0.0)
                o = (jnp.dot(a.astype(bf16), gv[s0:s1, vs], preferred_element_type=f32)
                     + jnp.dot(qe_chunk[s0:s1, ks], s_old[h].astype(bf16), preferred_element_type=f32))
                if blk > 0:
                    a = lax.dot_general(qe_o[:, ks], ke_o[:, ks], _NT, preferred_element_type=f32)
                    o = o + jnp.dot(a.astype(bf16), gv[0:s0, vs], preferred_element_type=f32)
                o_heads.append(o)
            o_blocks.append(jnp.concatenate(o_heads, axis=1))
        og = jnp.concatenate(o_blocks, axis=0)
        if c + 1 < tt // L:
            gate_prep(c + 1, seq)
        for h in range(GLA_HEADS):
            ks = slice(h * GLA_DK, (h + 1) * GLA_DK)
            vs = slice(h * GLA_DV, (h + 1) * GLA_DV)
            s_s[h][...] = (decay_col[ks, :] * s_old[h]
                           + jnp.dot(kdt[ks, :], gv[:, vs], preferred_element_type=f32))
        go = go_ref[rows, :].astype(f32)
        y_gla = _head_norm(og, glag_ref[...]) * (go * _sigmoid(go))
        br_ref[rows, ML_W:ML_W + GLA_VW] = y_gla.astype(br_ref.dtype)

    for s in range(nb):
        gate_prep(0, s)
    for c in range(tt // L):
        for s in range(nb):
            one_seq(c, s)

    pos = t_idx * tt + lax.broadcasted_iota(jnp.int32, (tt, 1), 0)
    for s in range(nb):
        pu_ref, br_ref, ext_s = z_refs[s][7], br_all.at[s], ext_refs[s]
        n_ext = tt + POOL_BASE
        ext_s[POOL_BASE:n_ext, :] = pu_ref[...].astype(f32)
        for g, w in enumerate(POOL_WINDOWS):
            gs = slice(g * POOL_GDIM, (g + 1) * POOL_GDIM)
            u = ext_s[POOL_BASE:n_ext, gs]
            p, k = ext_s[:, gs], 1
            while k < w:
                p = p + pltpu.roll(p, k, 0)
                k *= 2
            acc = p[POOL_BASE:n_ext, :]
            cnt = jnp.minimum(pos + 1, w).astype(f32)
            d = acc / cnt - u
            y = jnp.dot(d.astype(bf16), pw_ref[g], preferred_element_type=f32) * ps_ref[:, gs]
            br_ref[:, ML_W + GLA_VW + g * POOL_GDIM:ML_W + GLA_VW + (g + 1) * POOL_GDIM] = y.astype(br_ref.dtype)
        ext_s[8:POOL_BASE, :] = ext_s[tt + 8:n_ext, :]

    @pl.when(t_idx == n_t - 1)
    def _():
        for s in range(nb):
            for h in range(ML_HEADS):
                c_out[s, h] = c_refs[s][h][...]
            for h in range(GLA_HEADS):
                s_out[s, h] = s_refs[s][h][...]
            n_out[s] = n_refs[s][0:ML_HEADS, :]
            m_out[s] = m_refs[s][...]
            buf_out[s] = ext_refs[s][POOL_BASE - POOL_BUF:POOL_BASE, :]


def _prompt_mix(z, zs, lw, batch, seq, tt, chunk, sub, nb):
    n_t = seq // tt
    assert batch % nb == 0

    def zspecs(s):
        row = lambda b, t: (b * nb + s) * n_t + t
        spec = lambda col, width: pl.BlockSpec((tt, width), lambda b, t: (row(b, t), col // width))
        return [spec(C_MQ, 512), spec(C_MK, 512), spec(C_MV, 512), spec(C_MO, 512),
                spec(C_GQ, 512), spec(C_GV, 512), spec(C_GO, 512), spec(C_PU, 512),
                pl.BlockSpec((tt, LANES), lambda b, t: (row(b, t), 0))]

    def full(shape):
        nd = len(shape)
        return pl.BlockSpec(shape, lambda b, t: (0,) * nd)

    def per_seq(shape):
        nd = len(shape)
        return pl.BlockSpec((nb,) + shape, lambda b, t: (b,) + (0,) * nd)

    kern = functools.partial(_prompt_mix_kernel, nb=nb, tt=tt, chunk=chunk, sub=sub, n_t=n_t)
    outs = pl.pallas_call(
        kern,
        grid=(batch // nb, n_t),
        in_specs=[sp for s in range(nb) for sp in zspecs(s)]
                 + [full((1, LANES)), full((1, ML_W)), full((LANES, GLA_KW)), full((1, GLA_KW)),
                    full((1, GLA_VW)), full((len(POOL_WINDOWS), POOL_GDIM, POOL_GDIM)), full((1, POOL_W))],
        out_specs=[
            pl.BlockSpec((nb, tt, N_BRANCH * ML_W), lambda b, t: (b, t, 0)),
            per_seq((ML_HEADS, ML_DH, ML_DH)), per_seq((ML_HEADS, ML_DH)), per_seq((8, LANES)),
            per_seq((GLA_HEADS, GLA_DK, GLA_DV)), per_seq((POOL_BUF, POOL_W)),
        ],
        out_shape=[
            jax.ShapeDtypeStruct((batch, seq, N_BRANCH * ML_W), bf16),
            jax.ShapeDtypeStruct((batch, ML_HEADS, ML_DH, ML_DH), f32),
            jax.ShapeDtypeStruct((batch, ML_HEADS, ML_DH), f32),
            jax.ShapeDtypeStruct((batch, 8, LANES), f32),
            jax.ShapeDtypeStruct((batch, GLA_HEADS, GLA_DK, GLA_DV), f32),
            jax.ShapeDtypeStruct((batch, POOL_BUF, POOL_W), f32),
        ],
        scratch_shapes=([pltpu.VMEM((ML_DH, ML_DH), f32)] * ML_HEADS
                        + [pltpu.VMEM((GLA_DK, GLA_DV), f32)] * GLA_HEADS
                        + [pltpu.VMEM((8, LANES), f32), pltpu.VMEM((8, LANES), f32),
                           pltpu.VMEM((tt + POOL_BASE, POOL_W), f32),
                           pltpu.VMEM((chunk, LANES), f32), pltpu.VMEM((chunk, LANES), f32),
                           pltpu.VMEM((16, chunk), f32), pltpu.VMEM((chunk, GLA_KW), f32)]) * nb,
        compiler_params=_cparams(("parallel", "arbitrary")),
        name="prompt_mix",
    )(*([z] * 8 + [zs]) * nb,
      lw["ifb"], lw["ml_g"], lw["gla_gw"], lw["gla_gb"], lw["gla_g"], lw["pool_w"], lw["pool_scale"])
    return (outs[0].reshape(batch * seq, N_BRANCH * ML_W),) + tuple(outs[1:])


def _sample_mix_kernel(q_ref, k_ref, v_ref, mo_ref, gqk_ref, gv_ref, go_ref, pu_ref, sm_ref,
                       c_in, n_in, m_in, s_in, buf_in,
                       ifb_ref, mlg_ref, gw_ref, gb_ref, glag_ref, pw_ref, ps_ref, *tail, sb, dec):
    br_all = tail[-6]
    c_out, n_out, m_out, s_out, buf_out = (r.at[0] for r in tail[-5:])
    step = pl.program_id(0)
    c_in, n_in, m_in, s_in, buf_in = (r.at[0] for r in (c_in, n_in, m_in, s_in, buf_in))
    br_ref = br_all.at[pl.ds(pl.multiple_of(step * sb, sb), sb), :]

    @pl.when(step == 0)
    def _():
        br_all[dec:, :] = jnp.zeros((br_all.shape[0] - dec, br_all.shape[1]), br_all.dtype)

    sm = sm_ref[...]
    def to_cols(x):
        parts = [x[:, j * LANES:(j + 1) * LANES].T for j in range(x.shape[1] // LANES)]
        return jnp.concatenate(parts, axis=0)

    y0 = sm + ifb_ref[...]
    logf_all = _log_sigmoid(y0)
    k_scale = ML_DH ** -0.5
    m_all = m_in[...]
    hm, m_new_cols = [], []
    for h in range(ML_HEADS):
        hs = slice(h * ML_DH, (h + 1) * ML_DH)
        q = q_ref[:, hs].astype(f32)
        k = k_ref[:, hs].astype(f32) * k_scale
        v = v_ref[:, hs].astype(f32)
        i_pre = y0[:, S_MI + h:S_MI + h + 1]
        logf = logf_all[:, S_MF + h:S_MF + h + 1]
        m = m_all[:, h:h + 1]
        inter = logf + m
        m_t = jnp.maximum(inter, i_pre)
        w_intra = jnp.exp(i_pre - m_t)
        w_inter = jnp.exp(inter - m_t)
        s = jnp.sum(q * k, axis=1, keepdims=True) * w_intra
        qb = q.astype(bf16)
        n_h = n_in[:, h, :]
        cq = jnp.concatenate(
            [lax.dot_general(qb, c_in[j, h].astype(bf16), _NT, preferred_element_type=f32)[j:j + 1, :]
             for j in range(sb)], axis=0)
        num = s * v + w_inter * cq
        den = s + w_inter * jnp.sum(n_h * q, axis=1, keepdims=True)
        hm.append(num / jnp.maximum(jnp.abs(den), jnp.exp(-m_t)))
        m_new = m_t
        w_s = w_intra
        w_c = w_inter
        n_out[:, h, :] = w_c * n_h + w_s * k
        m_new_cols.append(m_new)
        vw_cols = to_cols(v * w_s)
        for j in range(sb):
            c_out[j, h] = w_c[j:j + 1, :] * c_in[j, h] + vw_cols[:, j:j + 1] * k[j:j + 1, :]
    lane = lax.broadcasted_iota(jnp.int32, (sb, LANES), 1)
    m_pack = jnp.zeros((sb, LANES), f32)
    for h in range(ML_HEADS):
        m_pack = jnp.where(lane == h, m_new_cols[h], m_pack)
    m_out[...] = m_pack
    y_ml = _head_norm(jnp.concatenate(hm, axis=1), mlg_ref[...]) * _sigmoid(mo_ref[...].astype(f32))
    br_ref[:, 0:ML_W] = y_ml.astype(br_ref.dtype)

    log_a = _gla_log_decay(sm, gw_ref, gb_ref)
    decay = jnp.exp(log_a)
    q2 = gqk_ref[:, 0:GLA_KW].astype(f32) * (GLA_DK ** -0.5)
    k2 = gqk_ref[:, GLA_KW:2 * GLA_KW].astype(f32)
    gv = gv_ref[...].astype(f32)
    qe = (q2 * decay).astype(bf16)
    qk = q2 * k2
    k_cols = to_cols(k2)
    decay_cols = to_cols(decay)
    og = []
    for h in range(GLA_HEADS):
        ks = slice(h * GLA_DK, (h + 1) * GLA_DK)
        vs = slice(h * GLA_DV, (h + 1) * GLA_DV)
        a = jnp.sum(qk[:, ks], axis=1, keepdims=True)
        inter = jnp.concatenate(
            [jnp.dot(qe[:, ks], s_in[j, h].astype(bf16), preferred_element_type=f32)[j:j + 1, :]
             for j in range(sb)], axis=0)
        og.append(a * gv[:, vs] + inter)
        for j in range(sb):
            s_out[j, h] = (decay_cols[ks, j:j + 1] * s_in[j, h]
                           + k_cols[ks, j:j + 1] * gv[j:j + 1, vs])
    go = go_ref[...].astype(f32)
    y_gla = _head_norm(jnp.concatenate(og, axis=1), glag_ref[...]) * (go * _sigmoid(go))
    br_ref[:, ML_W:ML_W + GLA_VW] = y_gla.astype(br_ref.dtype)

    u = pu_ref[...].astype(f32)
    rowi = lax.broadcasted_iota(jnp.int32, (POOL_BUF + 1, POOL_GDIM), 0)
    d_rows = []
    for j in range(sb):
        ext = jnp.concatenate([buf_in[j], u[j:j + 1, :]], axis=0)
        buf_out[j] = ext[1:POOL_BUF + 1, :]
        parts = []
        for g, w in enumerate(POOL_WINDOWS):
            gs = slice(g * POOL_GDIM, (g + 1) * POOL_GDIM)
            win = jnp.sum(jnp.where(rowi >= POOL_BUF + 1 - w, ext[:, gs], 0.0), axis=0, keepdims=True)
            parts.append(win / float(w) - u[j:j + 1, gs])
        d_rows.append(jnp.concatenate(parts, axis=1))
    d = jnp.concatenate(d_rows, axis=0)
    for g in range(len(POOL_WINDOWS)):
        gs = slice(g * POOL_GDIM, (g + 1) * POOL_GDIM)
        y = jnp.dot(d[:, gs].astype(bf16), pw_ref[g], preferred_element_type=f32) * ps_ref[:, gs]
        br_ref[:, ML_W + GLA_VW + g * POOL_GDIM:ML_W + GLA_VW + (g + 1) * POOL_GDIM] = y.astype(br_ref.dtype)


def _sample_mix(z, zs, st, prev, layer, lw, row0, dec, sb):
    depth = st[0].shape[0]
    base = row0 // sb
    tail_rows = z.shape[0] - row0

    def zspec(col, width):
        blk = col // width
        return pl.BlockSpec((sb, width), lambda i: (base + i, blk))

    def full(shape):
        nd = len(shape)
        return pl.BlockSpec(shape, lambda i: (0,) * nd)

    st_shapes = [(ML_HEADS, ML_DH, ML_DH), (ML_HEADS, ML_DH), (LANES,),
                 (GLA_HEADS, GLA_DK, GLA_DV), (POOL_BUF, POOL_W)]
    st_specs = [pl.BlockSpec((1, sb) + s, functools.partial(lambda i, nd: (layer, i) + (0,) * nd, nd=len(s)))
                for s in st_shapes]
    n_in = N_ZBLOCKS + len(st_shapes) + 7
    prev = () if prev is None else tuple(prev)
    return pl.pallas_call(
        functools.partial(_sample_mix_kernel, sb=sb, dec=dec),
        grid=(dec // sb,),
        in_specs=[zspec(C_MQ, 512), zspec(C_MK, 512), zspec(C_MV, 512), zspec(C_MO, 512),
                  zspec(C_GQ, 512), zspec(C_GV, 512), zspec(C_GO, 512), zspec(C_PU, 512),
                  pl.BlockSpec((sb, LANES), lambda i: (base + i, 0))]
                 + st_specs
                 + [full((1, LANES)), full((1, ML_W)), full((LANES, GLA_KW)), full((1, GLA_KW)),
                    full((1, GLA_VW)), full((len(POOL_WINDOWS), POOL_GDIM, POOL_GDIM)), full((1, POOL_W))]
                 + [pl.BlockSpec(memory_space=pl.ANY)] * len(prev),
        out_specs=[pl.BlockSpec((tail_rows, N_BRANCH * ML_W), lambda i: (0, 0))] + st_specs,
        out_shape=[jax.ShapeDtypeStruct((tail_rows, N_BRANCH * ML_W), bf16)]
                  + [jax.ShapeDtypeStruct((depth, dec) + s, f32) for s in st_shapes],
        input_output_aliases={n_in + j: 1 + j for j in range(len(prev))},
        compiler_params=_cparams(("arbitrary",)),
        name="sample_mix",
    )(z, z, z, z, z, z, z, z, zs, *st,
      lw["ifb"], lw["ml_g"], lw["gla_gw"], lw["gla_gb"], lw["gla_g"], lw["pool_w"], lw["pool_scale"], *prev)


def _merge_kernel(*refs, tm, n_valid, n_prompt_tiles, x_starts, n_extra, tf):
    x_refs = refs[:len(x_starts)]
    brp_ref, brt_ref, g0_ref, g1_ref, g2_ref, wb_ref, wo_ref = refs[len(x_starts):len(x_starts) + 7]
    extra_in = refs[len(x_starts) + 7:len(x_starts) + 7 + n_extra]
    o_ref = refs[len(x_starts) + 7 + n_extra]
    route_out = refs[len(x_starts) + 8 + n_extra:]
    in_tail = pl.program_id(0) >= n_prompt_tiles
    mixed = None
    for n, g_ref in enumerate((g0_ref, g1_ref, g2_ref)):
        cols = slice(n * ML_W, (n + 1) * ML_W)
        br = jnp.where(in_tail, brt_ref[:, cols], brp_ref[:, cols])
        proj = jnp.dot(br, wb_ref[n], preferred_element_type=f32)
        term = _sigmoid(g_ref[...].astype(f32)) * proj
        mixed = term if mixed is None else mixed + term
    out = (_pick_rows(x_refs, x_starts)
           + jnp.dot(mixed.astype(bf16), wo_ref[...], preferred_element_type=f32))
    row = pl.program_id(0) * tm + lax.broadcasted_iota(jnp.int32, (tm, 1), 0)
    out = jnp.where(row < n_valid, out, 0.0)
    if n_extra == 4:
        out = _swiglu_residual(out, *extra_in, tf=tf)
    o_ref[...] = out
    if n_extra == 3:
        _route(out, *extra_in, *route_out, tm=tm, n_valid=n_valid)


def _merge(x_parts, br_prompt, br_tail, z, w_branch, w_out, tm, n_valid, route_w=None, ffn_w=None, tf=None):
    mp = sum(p.shape[0] for p in x_parts)
    nt = mp // tm
    route = route_w is not None
    assert not (route and ffn_w is not None)
    extra = tuple(route_w or ffn_w or ())
    route_specs = [_resident(a.shape) for a in extra]
    out_specs = [pl.BlockSpec((tm, D_MODEL), lambda i: (i, 0))]
    out_shape = [jax.ShapeDtypeStruct((mp, D_MODEL), f32)]
    if route:
        out_specs += [pl.BlockSpec((tm, D_MODEL), lambda i: (i, 0)),
                      pl.BlockSpec((tm, LANES), lambda i: (i, 0)),
                      pl.BlockSpec((tm, LANES), lambda i: (i, 0)),
                      pl.BlockSpec((1, N_EXPERTS, tm), lambda i: (i, 0, 0)),
                      pl.BlockSpec((1, 1, LANES), lambda i: (i, 0, 0))]
        out_shape += [jax.ShapeDtypeStruct((mp, D_MODEL), bf16),
                      jax.ShapeDtypeStruct((mp, LANES), f32),
                      jax.ShapeDtypeStruct((mp, LANES), f32),
                      jax.ShapeDtypeStruct((nt, N_EXPERTS, tm), f32),
                      jax.ShapeDtypeStruct((nt, 1, LANES), f32)]
    x_starts = tuple(sum(p.shape[0] for p in x_parts[:j]) // tm for j in range(len(x_parts)))
    n_p = br_prompt.shape[0] // tm
    gate_specs = [pl.BlockSpec((tm, D_MODEL), functools.partial(lambda i, n: (i, C_GATES // D_MODEL + n), n=n))
                  for n in range(N_BRANCH)]
    outs = pl.pallas_call(
        functools.partial(_merge_kernel, tm=tm, n_valid=n_valid, n_prompt_tiles=n_p, x_starts=x_starts,
                          n_extra=len(extra), tf=tf),
        grid=(nt,),
        in_specs=[
            *_split_rows_specs(x_parts, tm, D_MODEL),
            pl.BlockSpec((tm, N_BRANCH * ML_W), lambda i: (jnp.minimum(i, n_p - 1), 0)),
            pl.BlockSpec((tm, N_BRANCH * ML_W), lambda i: (jnp.maximum(i - n_p, 0), 0)),
            *gate_specs,
            _resident((N_BRANCH, ML_W, D_MODEL)),
            _resident((D_MODEL, D_MODEL)),
            *route_specs,
        ],
        out_specs=out_specs,
        out_shape=out_shape,
        compiler_params=_cparams(("parallel",)),
        name="merge_route" if route else ("merge_ffn" if extra else "merge"),
    )(*x_parts, br_prompt, br_tail, z, z, z, w_branch, w_out, *extra)
    return outs if route else outs[0]


def _route(x, g_ref, wrt_ref, before_ref, hb_ref, comb_ref, rnk_ref, rnkt_ref, cnt_ref, *, tm, n_valid):
    h = _rms(x, g_ref[...])
    h_hi = h.astype(bf16)
    hb_ref[...] = h_hi
    h_lo = (h - h_hi.astype(f32)).astype(bf16)
    p_hi = lax.dot_general(wrt_ref[...], h_hi, _NT, preferred_element_type=f32)
    p_lo = lax.dot_general(wrt_ref[...], h_lo, _NT, preferred_element_type=f32)
    logits = p_hi[0:N_EXPERTS] + p_hi[N_EXPERTS:] + p_lo[0:N_EXPERTS]
    eidx = lax.broadcasted_iota(jnp.int32, logits.shape, 0)
    mx = jnp.max(logits, axis=0, keepdims=True)
    e = jnp.exp(logits - mx)
    probs = e / jnp.sum(e, axis=0, keepdims=True)
    p1 = jnp.max(probs, axis=0, keepdims=True)
    i1 = jnp.min(jnp.where(probs == p1, eidx, N_EXPERTS), axis=0, keepdims=True)
    rest = jnp.where(eidx == i1, -1.0, probs)
    p2 = jnp.max(rest, axis=0, keepdims=True)
    i2 = jnp.min(jnp.where(rest == p2, eidx, N_EXPERTS), axis=0, keepdims=True)
    tot = p1 + p2
    comb_t = jnp.where(eidx == i1, p1 / tot, 0.0) + jnp.where(eidx == i2, p2 / tot, 0.0)
    tok = pl.program_id(0) * tm + lax.broadcasted_iota(jnp.int32, logits.shape, 1)
    sel = ((eidx == i1) | (eidx == i2)) & (tok < n_valid)
    rank = jnp.dot(sel.astype(bf16), before_ref[...], preferred_element_type=f32)
    rnk_t = jnp.where(sel, rank, -1.0)
    rnkt_ref[0] = rnk_t
    both = jnp.concatenate([comb_t, rnk_t, jnp.zeros((LANES - 2 * N_EXPERTS, tm), f32)], axis=0).T
    lane = lax.broadcasted_iota(jnp.int32, (tm, LANES), 1)
    comb_ref[...] = jnp.where(lane < N_EXPERTS, both, 0.0)
    rnk = jnp.where(lane < N_EXPERTS, pltpu.roll(both, LANES - N_EXPERTS, 1), -1.0)
    rnk_ref[...] = rnk
    cnt_ref[0] = jnp.sum((rnk >= 0.0).astype(f32), axis=0, keepdims=True)


def _moe_kernel(rounds_ref, hb_ref, rnk_ref, rnkt_ref, comb_ref, wg_ref, wu_ref, wd_ref, o_ref,
                *, cap, n_sub, ts):
    i, e = pl.program_id(0), pl.program_id(1)

    @pl.when(e == 0)
    def _():
        o_ref[...] = jnp.zeros_like(o_ref)

    lane = lax.broadcasted_iota(jnp.int32, (ts, LANES), 1)
    slot_rows = lax.broadcasted_iota(jnp.int32, (cap, ts), 0).astype(f32)
    slot_cols = lax.broadcasted_iota(jnp.int32, (ts, cap), 1).astype(f32)

    def round_body(r, carry):
        base = (r * cap).astype(f32)
        xs = []
        for j in range(n_sub):
            rt = rnkt_ref[j, pl.ds(e, 1), :]
            p = (rt - base == slot_rows).astype(bf16)
            xs.append(jnp.dot(p, hb_ref[j * ts:(j + 1) * ts, :],
                              preferred_element_type=f32).astype(bf16))
        xs = jnp.concatenate(xs, axis=0)
        a = jnp.dot(xs, wg_ref[0], preferred_element_type=f32)
        a = (a * _sigmoid(a)) * jnp.dot(xs, wu_ref[0], preferred_element_type=f32)
        y = jnp.dot(a.astype(bf16), wd_ref[0], preferred_element_type=f32).astype(bf16)
        for j in range(n_sub):
            rows = slice(j * ts, (j + 1) * ts)
            col = jnp.sum(jnp.where(lane == e, rnk_ref[rows, :], 0.0), axis=1, keepdims=True)
            w = jnp.sum(jnp.where(lane == e, comb_ref[rows, :], 0.0), axis=1, keepdims=True)
            pt = (col - base == slot_cols).astype(bf16)
            o_ref[rows, :] += w * jnp.dot(pt, y[j * cap:(j + 1) * cap, :], preferred_element_type=f32)
        return carry

    lax.fori_loop(0, rounds_ref[i * N_EXPERTS + e], round_body, 0)


def _moe(hb, rnk, rnkt, comb, rounds, wg, wu, wd, ts, n_sub, cap):
    mp = hb.shape[0]
    n_e, _, d_ff = wg.shape
    tsup = ts * n_sub
    grid_spec = pltpu.PrefetchScalarGridSpec(
        num_scalar_prefetch=1,
        grid=(mp // tsup, n_e),
        in_specs=[pl.BlockSpec((tsup, D_MODEL), lambda i, e, r: (i, 0)),
                  pl.BlockSpec((tsup, LANES), lambda i, e, r: (i, 0)),
                  pl.BlockSpec((n_sub, N_EXPERTS, ts), lambda i, e, r: (i, 0, 0)),
                  pl.BlockSpec((tsup, LANES), lambda i, e, r: (i, 0)),
                  pl.BlockSpec((1, D_MODEL, d_ff), lambda i, e, r: (e, 0, 0)),
                  pl.BlockSpec((1, D_MODEL, d_ff), lambda i, e, r: (e, 0, 0)),
                  pl.BlockSpec((1, d_ff, D_MODEL), lambda i, e, r: (e, 0, 0))],
        out_specs=pl.BlockSpec((tsup, D_MODEL), lambda i, e, r: (i, 0)),
    )
    return pl.pallas_call(
        functools.partial(_moe_kernel, cap=cap, n_sub=n_sub, ts=ts),
        grid_spec=grid_spec,
        out_shape=jax.ShapeDtypeStruct((mp, D_MODEL), f32),
        compiler_params=_cparams(("parallel", "arbitrary")),
        name="moe",
    )(rounds, hb, rnk, rnkt, comb, wg, wu, wd)


def _swiglu_residual(x, g_ref, wg_ref, wu_ref, wd_ref, *, tf):
    h = _rms(x, g_ref[...]).astype(bf16)
    acc = x
    assert wg_ref.shape[1] % tf == 0
    for c in range(wg_ref.shape[1] // tf):
        cols = slice(c * tf, (c + 1) * tf)
        a = jnp.dot(h, wg_ref[:, cols], preferred_element_type=f32)
        a = (a * _sigmoid(a)) * jnp.dot(h, wu_ref[:, cols], preferred_element_type=f32)
        acc = acc + jnp.dot(a.astype(bf16), wd_ref[cols, :], preferred_element_type=f32)
    return acc


def _final_norm_kernel(x_ref, y_ref, g_ref, op_ref, os_ref, *, n_prompt_tiles, dec):
    i = pl.program_id(0)
    out = _rms(x_ref[...] + y_ref[...], g_ref[...])

    @pl.when(i < n_prompt_tiles)
    def _():
        op_ref[...] = out

    @pl.when(i == n_prompt_tiles)
    def _():
        os_ref[...] = out[0:dec]


def _final_norm(x, y, g, tm, m_prompt, dec):
    n_p = m_prompt // tm
    assert n_p * tm == m_prompt and dec <= tm
    return pl.pallas_call(
        functools.partial(_final_norm_kernel, n_prompt_tiles=n_p, dec=dec),
        grid=(n_p + 1,),
        in_specs=[pl.BlockSpec((tm, D_MODEL), lambda i: (i, 0)),
                  pl.BlockSpec((tm, D_MODEL), lambda i: (i, 0)),
                  pl.BlockSpec((1, D_MODEL), lambda i: (0, 0))],
        out_specs=[pl.BlockSpec((tm, D_MODEL), lambda i: (jnp.minimum(i, n_p - 1), 0)),
                   pl.BlockSpec((dec, D_MODEL), lambda i: (0, 0))],
        out_shape=[jax.ShapeDtypeStruct((m_prompt, D_MODEL), f32),
                   jax.ShapeDtypeStruct((dec, D_MODEL), f32)],
        compiler_params=_cparams(("arbitrary",)),
        name="final_norm",
    )(x, y, g)


O_MI = 3 * ML_W
O_MO = O_MI + 2 * ML_HEADS
O_GLR = O_MO + ML_W + 2 * GLA_KW + GLA_VW
O_GO = O_GLR + GLA_RANK
D_IN = O_GO + GLA_VW + POOL_W + N_BRANCH * D_MODEL


def _regroup_kernel(w_ref, if_ref, glr_ref, main_ref, small_ref):
    main_ref[0] = w_ref[0].astype(bf16)
    n_if, n_glr = if_ref.shape[1], glr_ref.shape[1]
    small_ref[0, 0:n_if, :] = if_ref[0].astype(bf16)
    small_ref[0, n_if:n_if + n_glr, :] = glr_ref[0].astype(bf16)
    small_ref[0, n_if + n_glr:, :] = jnp.zeros((LANES - n_if - n_glr, small_ref.shape[2]), bf16)


def _regroup_w_in(wt, tr=512):
    depth, n, d = wt.shape
    assert n == D_IN and Z_MAIN == D_IN - 2 * ML_HEADS - GLA_RANK
    assert O_MI % tr == 0 and (O_GLR - O_MO) % tr == 0 and (D_IN - O_GO) % tr == 0
    c1, c2 = O_MI // tr, (O_MI + O_GLR - O_MO) // tr

    def src_row(c):
        skip = jnp.where(c < c1, 0, jnp.where(c < c2, O_MO - O_MI, O_MO - O_MI + O_GO - O_GLR))
        return pl.multiple_of(c * tr + skip, 8)

    return pl.pallas_call(
        _regroup_kernel,
        grid=(depth, Z_MAIN // tr),
        in_specs=[pl.BlockSpec((pl.Element(1), pl.Element(rows), pl.Element(d)), index_map)
                  for rows, index_map in ((tr, lambda l, c: (l, src_row(c), 0)),
                                          (O_MO - O_MI, lambda l, c: (l, O_MI, 0)),
                                          (O_GO - O_GLR, lambda l, c: (l, O_GLR, 0)))],
        out_specs=[pl.BlockSpec((1, tr, d), lambda l, c: (l, c, 0)),
                   pl.BlockSpec((1, LANES, d), lambda l, c: (l, 0, 0))],
        out_shape=[jax.ShapeDtypeStruct((depth, Z_MAIN, d), bf16),
                   jax.ShapeDtypeStruct((depth, LANES, d), bf16)],
        compiler_params=_cparams(("parallel", "arbitrary")),
        name="regroup_w_in",
    )(wt, wt, wt)


def _layer_weights(l, norm1_g, if_bias, ml_g, gla_gw, gla_gb, gla_g, pool_w, pool_scale,
                   w_branch, w_out):
    ifb = jnp.zeros((1, LANES), f32)
    ifb = ifb.at[0, S_MI:S_MI + ML_HEADS].set(if_bias[l, 0]).at[0, S_MF:S_MF + ML_HEADS].set(if_bias[l, 1])
    gw = jnp.zeros((LANES, GLA_KW), f32).at[S_GLR:S_GLR + GLA_RANK].set(gla_gw[l]).astype(bf16)
    return dict(
        norm1_g=norm1_g[l][None], ifb=ifb, ml_g=ml_g[l][None],
        gla_gw=gw, gla_gb=gla_gb[l][None], gla_g=gla_g[l][None], pool_w=pool_w[l].astype(bf16),
        pool_scale=pool_scale[l][None], w_branch=w_branch[l].astype(bf16), w_out=w_out[l].astype(bf16))


def _forward(x_prompt, x_sample, state_mlstm_C, state_mlstm_n, state_mlstm_m, state_gla_S,
             state_pool_buf, norm1_g, w_in, mlstm_if_bias, mlstm_norm_g, gla_gate_w, gla_gate_b,
             gla_norm_g, pool_w, pool_scale, w_branch, w_out, norm2_g, ffn_wg, ffn_wu, ffn_wd,
             router_w, moe_wg, moe_wu, moe_wd, final_norm_g, *, tm, tn, tt, chunk, sub, nb, sb, tf_dense, moe_sub, moe_cap):
    batch, seq, _ = x_prompt.shape
    dec = x_sample.shape[0]
    depth = w_in.shape[0]
    m_prompt = batch * seq
    m_all = m_prompt + dec
    mp = -(-m_all // tm) * tm
    assert m_prompt % tm == 0
    x_parts = (x_prompt.reshape(m_prompt, D_MODEL),
               jnp.concatenate([x_sample.reshape(dec, D_MODEL), jnp.zeros((mp - m_all, D_MODEL), f32)], axis=0))
    outs = [[] for _ in range(5)]
    moe_out = None
    st = (state_mlstm_C, state_mlstm_n, jnp.pad(state_mlstm_m, ((0, 0), (0, 0), (0, LANES - ML_HEADS))),
          state_gla_S, state_pool_buf)
    st_new = None
    w_main, w_small = _regroup_w_in(jnp.swapaxes(w_in, 1, 2))
    for l in range(depth):
        lw = _layer_weights(l, norm1_g, mlstm_if_bias, mlstm_norm_g, gla_gate_w, gla_gate_b,
                            gla_norm_g, pool_w, pool_scale, w_branch, w_out)
        z, zs = _norm_matmul(x_parts, lw["norm1_g"], w_main, w_small, l, tm, tn)
        br_p, c_p, n_p, m_p, s_p, buf_p = _prompt_mix(z, zs, lw, batch, seq, tt, chunk, sub, nb)
        br_t, *st_new = _sample_mix(z, zs, st, st_new, l, lw, m_prompt, dec, sb)
        j = l // 2
        if l % 2 == 0:
            x = _merge(x_parts, br_p, br_t, z, lw["w_branch"], lw["w_out"], tm, m_all,
                       ffn_w=(norm2_g[l][None], ffn_wg[j].astype(bf16), ffn_wu[j].astype(bf16),
                              ffn_wd[j].astype(bf16)), tf=tf_dense)
        else:
            wr_hi = router_w[j].astype(bf16)
            wr_lo = (router_w[j] - wr_hi.astype(f32)).astype(bf16)
            wrt = jnp.concatenate([wr_hi, wr_lo], axis=1).T
            before = (jnp.arange(tm)[:, None] < jnp.arange(tm)[None, :]).astype(bf16)
            x, hb, comb, rnk, rnkt, cnt = _merge(x_parts, br_p, br_t, z, lw["w_branch"], lw["w_out"], tm, m_all,
                                                 route_w=(norm2_g[l][None], wrt, before))
            n_sup = mp // (tm * moe_sub)
            cnt = cnt[:, 0, :N_EXPERTS].reshape(n_sup, moe_sub, N_EXPERTS).max(axis=1)
            rounds = jnp.ceil(cnt / moe_cap).astype(jnp.int32).reshape(-1)
            moe_out = _moe(hb, rnk, rnkt, comb, rounds, moe_wg[j].astype(bf16), moe_wu[j].astype(bf16),
                           moe_wd[j].astype(bf16), tm, moe_sub, moe_cap)
            if l + 1 < depth:
                x = x + moe_out
                moe_out = None
        x_parts = (x,)
        for lst, val in zip(outs, (c_p, n_p, m_p[:, :ML_HEADS, 0], s_p, buf_p)):
            lst.append(val)
    if moe_out is None:
        moe_out = jnp.zeros_like(x)
    y_prompt, y_sample = _final_norm(x, moe_out, final_norm_g[None], tm, m_prompt, dec)
    c_p, n_p, m_p, s_p, buf_p = (jnp.stack(o) for o in outs)
    c_s, n_s, m_s, s_s, buf_s = st_new
    return (y_prompt.reshape(batch, seq, D_MODEL), y_sample.reshape(dec, 1, D_MODEL),
            c_p, c_s, n_p, n_s, m_p, m_s[:, :, :ML_HEADS], s_p, s_s, buf_p, buf_s)


def kernel(x_prompt, x_sample, state_mlstm_C, state_mlstm_n, state_mlstm_m, state_gla_S, state_pool_buf, norm1_g, w_in, mlstm_if_bias, mlstm_norm_g, gla_gate_w, gla_gate_b, gla_norm_g, pool_w, pool_scale, w_branch, w_out, norm2_g, ffn_wg, ffn_wu, ffn_wd, router_w, moe_wg, moe_wu, moe_wd, final_norm_g):
    return _forward(x_prompt, x_sample, state_mlstm_C, state_mlstm_n, state_mlstm_m, state_gla_S,
                    state_pool_buf, norm1_g, w_in, mlstm_if_bias, mlstm_norm_g, gla_gate_w, gla_gate_b,
                    gla_norm_g, pool_w, pool_scale, w_branch, w_out, norm2_g, ffn_wg, ffn_wu, ffn_wd,
                    router_w, moe_wg, moe_wu, moe_wd, final_norm_g,
                    tm=512, tn=1024, tt=512, chunk=128, sub=64, nb=2, sb=16, tf_dense=MXU_COLS, moe_sub=3, moe_cap=160)
```

```python
import functools

import jax
import jax.numpy as jnp
from jax import lax
from jax.experimental import pallas as pl
from jax.experimental.pallas import tpu as pltpu

f32 = jnp.float32
bf16 = jnp.bfloat16

D_MODEL = 1024
ML_HEADS, ML_DH = 4, 128
ML_W = ML_HEADS * ML_DH
GLA_HEADS, GLA_DK, GLA_DV = 4, 64, 128
GLA_KW, GLA_VW = GLA_HEADS * GLA_DK, GLA_HEADS * GLA_DV
GLA_RANK = 16
GLA_TAU = 16.0
POOL_GDIM = 128
POOL_WINDOWS = (2, 4, 8, 16)
POOL_W = POOL_GDIM * len(POOL_WINDOWS)
POOL_BUF = 15
POOL_BASE = 24
N_BRANCH = 3
N_EXPERTS = 8
EPS = 1e-6
NEG = -1e30
LANES = 128
MXU_COLS = 256

C_MQ, C_MK, C_MV, C_MO = 0, 512, 1024, 1536
C_GQ, C_GK, C_GV, C_GO, C_PU, C_GATES = 2048, 2304, 2560, 3072, 3584, 4096
Z_MAIN = C_GATES + N_BRANCH * D_MODEL
S_MI, S_MF, S_GLR = 0, 4, 8

VMEM_LIMIT = 56 * 1024 * 1024

_NT = (((1,), (1,)), ((), ()))


def _cparams(sem):
    return pltpu.CompilerParams(dimension_semantics=sem, vmem_limit_bytes=VMEM_LIMIT)


def _log_sigmoid(x):
    return jnp.minimum(x, 0.0) - jnp.log(1.0 + jnp.exp(-jnp.abs(x)))


def _sigmoid(x):
    return 0.5 * jnp.tanh(0.5 * x) + 0.5


def _rms(x, g):
    ms = jnp.mean(x * x, axis=-1, keepdims=True)
    return x * lax.rsqrt(ms + EPS) * g


def _lower_tri(n):
    r = lax.broadcasted_iota(jnp.int32, (n, n), 0)
    c = lax.broadcasted_iota(jnp.int32, (n, n), 1)
    return c <= r


def _cumsum_rows(tri_bf16, a):
    a1 = a.astype(bf16)
    r = a - a1.astype(f32)
    a2 = r.astype(bf16)
    a3 = (r - a2.astype(f32)).astype(bf16)
    d = lambda y: jnp.dot(tri_bf16, y, preferred_element_type=f32)
    return d(a1) + d(a2) + d(a3)


def _cumsum_lanes(triu_bf16, a):
    a1 = a.astype(bf16)
    r = a - a1.astype(f32)
    a2 = r.astype(bf16)
    a3 = (r - a2.astype(f32)).astype(bf16)
    d = lambda y: jnp.dot(y, triu_bf16, preferred_element_type=f32)
    return d(a1) + d(a2) + d(a3)


def _resident(shape):
    nd = len(shape)
    return pl.BlockSpec(shape, lambda *_: (0,) * nd, pipeline_mode=pl.Buffered(1))


def _split_rows_specs(parts, tm, width):
    specs, start = [], 0
    for p in parts:
        n_t = p.shape[0] // tm
        assert n_t * tm == p.shape[0]
        specs.append(pl.BlockSpec(
            (tm, width), functools.partial(lambda i, s, n: (jnp.clip(i - s, 0, n - 1), 0), s=start, n=n_t)))
        start += n_t
    return specs


def _pick_rows(refs, starts):
    i = pl.program_id(0)
    x = refs[0][...]
    for r, s in zip(refs[1:], starts[1:]):
        x = jnp.where(i >= s, r[...], x)
    return x


def _norm_matmul_kernel(*refs, tn, starts):
    x_refs = refs[:len(starts)]
    g_ref, w_ref, ws_ref, z_ref, zs_ref = refs[len(starts):]
    h = _rms(_pick_rows(x_refs, starts), g_ref[...]).astype(bf16)
    zs_ref[...] = lax.dot_general(h, ws_ref[0], _NT, preferred_element_type=f32)
    for c in range(w_ref.shape[1] // tn):
        cols = slice(c * tn, (c + 1) * tn)
        z_ref[:, cols] = lax.dot_general(h, w_ref[0, cols, :], _NT,
                                         preferred_element_type=f32).astype(z_ref.dtype)


def _norm_matmul(x_parts, g, wt_main, wt_small, layer, tm, tn):
    mp = sum(p.shape[0] for p in x_parts)
    n = wt_main.shape[1]
    starts = tuple(sum(p.shape[0] for p in x_parts[:j]) // tm for j in range(len(x_parts)))
    layer_block = lambda width: pl.BlockSpec((1, width, D_MODEL), lambda i: (layer, 0, 0),
                                             pipeline_mode=pl.Buffered(1))
    return pl.pallas_call(
        functools.partial(_norm_matmul_kernel, tn=tn, starts=starts),
        grid=(mp // tm,),
        in_specs=[
            *_split_rows_specs(x_parts, tm, D_MODEL),
            _resident((1, D_MODEL)),
            layer_block(n),
            layer_block(LANES),
        ],
        out_specs=[
            pl.BlockSpec((tm, n), lambda i: (i, 0)),
            pl.BlockSpec((tm, LANES), lambda i: (i, 0)),
        ],
        out_shape=[jax.ShapeDtypeStruct((mp, n), bf16), jax.ShapeDtypeStruct((mp, LANES), f32)],
        compiler_params=_cparams(("parallel",)),
        name="norm_matmul",
    )(*x_parts, g, wt_main, wt_small)


def _head_norm(h, g):
    outs = []
    for j in range(h.shape[1] // LANES):
        hj = h[:, j * LANES:(j + 1) * LANES]
        outs.append(hj * lax.rsqrt(jnp.mean(hj * hj, axis=-1, keepdims=True) + EPS))
    return jnp.concatenate(outs, axis=1) * g


def _gla_log_decay(sm, gw_ref, gb_ref):
    xg = jnp.dot(sm.astype(bf16), gw_ref[...], preferred_element_type=f32) + gb_ref[...]
    return _log_sigmoid(xg) * (1.0 / GLA_TAU)


N_ZBLOCKS = 9


def _prompt_mix_kernel(*refs, nb, tt, chunk, sub, n_t):
    z_refs = [refs[s * N_ZBLOCKS:(s + 1) * N_ZBLOCKS] for s in range(nb)]
    ifb_ref, mlg_ref, gw_ref, gb_ref, glag_ref, pw_ref, ps_ref = refs[nb * N_ZBLOCKS:nb * N_ZBLOCKS + 7]
    br_all, c_out, n_out, m_out, s_out, buf_out = refs[nb * N_ZBLOCKS + 7:nb * N_ZBLOCKS + 13]
    scratch = refs[nb * N_ZBLOCKS + 13:]
    per = ML_HEADS + GLA_HEADS + 7
    c_refs = [scratch[s * per:s * per + ML_HEADS] for s in range(nb)]
    s_refs = [scratch[s * per + ML_HEADS:s * per + ML_HEADS + GLA_HEADS] for s in range(nb)]
    n_refs = [scratch[s * per + ML_HEADS + GLA_HEADS] for s in range(nb)]
    m_refs = [scratch[s * per + ML_HEADS + GLA_HEADS + 1] for s in range(nb)]
    ext_refs = [scratch[s * per + ML_HEADS + GLA_HEADS + 2] for s in range(nb)]
    prep_refs = [scratch[s * per + ML_HEADS + GLA_HEADS + 3:(s + 1) * per] for s in range(nb)]
    t_idx = pl.program_id(1)
    L = chunk

    @pl.when(t_idx == 0)
    def _():
        for s in range(nb):
            for r in (*c_refs[s], *s_refs[s], n_refs[s], m_refs[s]):
                r[...] = jnp.zeros_like(r)
            ext_refs[s][0:POOL_BASE, :] = jnp.zeros((POOL_BASE, POOL_W), f32)

    tri = _lower_tri(L)
    tri_b = tri.astype(bf16)
    causal_sub = _lower_tri(sub)
    assert L == LANES
    ones_b = jnp.ones((L, LANES), bf16)
    triu_b = (lax.broadcasted_iota(jnp.int32, (L, L), 0)
              <= lax.broadcasted_iota(jnp.int32, (L, L), 1)).astype(bf16)
    ifb = ifb_ref[...]
    k_scale = ML_DH ** -0.5
    q_scale = GLA_DK ** -0.5

    def gate_prep(c, seq):
        sm_ref = z_refs[seq][8]
        y0_p, bc_p, rows_p, b_p = prep_refs[seq]
        sm = sm_ref[c * L:(c + 1) * L, :]
        y0 = sm + ifb
        y0_p[...] = y0
        y0t = y0.T[0:8, :]
        bct = _cumsum_lanes(triu_b, _log_sigmoid(y0t))
        rows_p[0:8, :] = y0t
        rows_p[8:16, :] = bct
        bc_p[...] = jnp.concatenate([bct, jnp.zeros((LANES - 8, L), f32)], axis=0).T
        b_p[...] = _cumsum_rows(tri_b, _gla_log_decay(sm, gw_ref, gb_ref))

    def one_seq(c, seq):
        q_ref, k_ref, v_ref, mo_ref, gqk_ref, gv_ref, go_ref, _, sm_ref = z_refs[seq]
        br_ref, c_s, s_s = br_all.at[seq], c_refs[seq], s_refs[seq]
        n_old, m_old = n_refs[seq][...], m_refs[seq][...]
        n_rows, m_rows = [], []
        rows = slice(c * L, (c + 1) * L)
        y0_p, bc_p, rows_p, b_p = prep_refs[seq]

        y0, bc = y0_p[...], bc_p[...]
        y0t, bct = rows_p[0:8, :], rows_p[8:16, :]
        b = b_p[...]
        hm = []
        for h in range(ML_HEADS):
            hs = slice(h * ML_DH, (h + 1) * ML_DH)
            v = v_ref[rows, hs].astype(f32)
            kb = k_ref[rows, hs].astype(bf16)
            qb, vb = q_ref[rows, hs].astype(bf16), v.astype(bf16)
            bcol = bc[:, S_MF + h:S_MF + h + 1]
            icol = y0[:, S_MI + h:S_MI + h + 1]
            brow = bct[S_MF + h:S_MF + h + 1, :]
            irow = y0t[S_MI + h:S_MI + h + 1, :]
            m_row = m_old[h:h + 1, :]
            cmat = c_s[h][...]
            nrow = n_old[h:h + 1, :]
            bcol_r = jnp.broadcast_to(bcol, (L, LANES))
            dm = jnp.where(tri, bcol_r - brow + irow, NEG)
            inter_r = bcol_r + m_row
            m_t_r = jnp.maximum(inter_r, jnp.broadcast_to(jnp.max(dm, axis=1, keepdims=True), (L, LANES)))
            w_intra = jnp.exp(dm - m_t_r) * k_scale
            w_inter_r = jnp.exp(inter_r - m_t_r)
            s = lax.dot_general(qb, kb, _NT, preferred_element_type=f32) * w_intra
            s_hi = s.astype(bf16)
            s_lo = (s - s_hi.astype(f32)).astype(bf16)
            r_intra = jnp.dot(s_hi, jnp.concatenate([vb, ones_b], axis=1), preferred_element_type=f32)
            c_aug = jnp.concatenate([cmat, jnp.broadcast_to(nrow, (LANES, LANES))], axis=0).astype(bf16)
            r_inter = lax.dot_general(qb, c_aug, _NT, preferred_element_type=f32)
            rs_lo = jnp.dot(s_lo, ones_b, preferred_element_type=f32)
            num = r_intra[:, 0:LANES] + w_inter_r * r_inter[:, 0:LANES]
            den_r = r_intra[:, LANES:] + rs_lo + w_inter_r * r_inter[:, LANES:]
            hm.append(num / jnp.maximum(jnp.abs(den_r), jnp.exp(-m_t_r)))
            b_last = bcol[L - 1:L, :]
            m = m_row[:, 0:1]
            g = b_last - bcol + icol
            m_new = jnp.maximum(b_last + m, jnp.max(g, axis=0, keepdims=True))
            w_s = jnp.broadcast_to(jnp.exp(g - m_new) * k_scale, (L, LANES))
            w_c = jnp.exp(b_last + m - m_new)
            vwt = (v * w_s).T.astype(bf16)
            c_s[h][...] = w_c * cmat + jnp.dot(vwt, kb, preferred_element_type=f32)
            n_rows.append(w_c * nrow + jnp.sum(w_s * k_ref[rows, hs].astype(f32), axis=0, keepdims=True))
            m_rows.append(jnp.broadcast_to(m_new, (1, LANES)))
        pad_rows = [jnp.zeros((8 - ML_HEADS, LANES), f32)]
        n_refs[seq][...] = jnp.concatenate(n_rows + pad_rows, axis=0)
        m_refs[seq][...] = jnp.concatenate(m_rows + pad_rows, axis=0)
        y_ml = (_head_norm(jnp.concatenate(hm, axis=1), mlg_ref[...])
                * _sigmoid(mo_ref[rows, :].astype(f32)))
        br_ref[rows, 0:ML_W] = y_ml.astype(br_ref.dtype)

        q2 = gqk_ref[rows, 0:GLA_KW].astype(f32) * q_scale
        k2 = gqk_ref[rows, GLA_KW:2 * GLA_KW].astype(f32)
        gv = gv_ref[rows, :].astype(bf16)
        s_old = [s_s[h][...] for h in range(GLA_HEADS)]
        qe_chunk = (q2 * jnp.exp(b)).astype(bf16)
        b_last = b[L - 1:L, :]
        kdt = (k2 * jnp.exp(b_last - b)).T.astype(bf16)
        decay_col = jnp.exp(b.T[:, L - 1:L])
        o_blocks = []
        for blk in range(L // sub):
            s0, s1 = blk * sub, (blk + 1) * sub
            mid = s0 + sub // 2
            b_blk = b[s0:s1]
            b_mid = b[mid - 1:mid, :]
            qe_d = (q2[s0:s1] * jnp.exp(b_blk - b_mid)).astype(bf16)
            ke_d = (k2[s0:s1] * jnp.exp(b_mid - b_blk)).astype(bf16)
            if blk > 0:
                b_start = b[s0 - 1:s0, :]
                qe_o = (q2[s0:s1] * jnp.exp(b_blk - b_start)).astype(bf16)
                ke_o = (k2[0:s0] * jnp.exp(b_start - b[0:s0])).astype(bf16)
            o_heads = []
            for h in range(GLA_HEADS):
                ks = slice(h * GLA_DK, (h + 1) * GLA_DK)
                vs = slice(h * GLA_DV, (h + 1) * GLA_DV)
                a = lax.dot_general(qe_d[:, ks], ke_d[:, ks], _NT, preferred_element_type=f32)
                a = jnp.where(causal_sub, a, 0.0)
                o = (jnp.dot(a.astype(bf16), gv[s0:s1, vs], preferred_element_type=f32)
                     + jnp.dot(qe_chunk[s0:s1, ks], s_old[h].astype(bf16), preferred_element_type=f32))
                if blk > 0:
                    a = lax.dot_general(qe_o[:, ks], ke_o[:, ks], _NT, preferred_element_type=f32)
                    o = o + jnp.dot(a.astype(bf16), gv[0:s0, vs], preferred_element_type=f32)
                o_heads.append(o)
            o_blocks.append(jnp.concatenate(o_heads, axis=1))
        og = jnp.concatenate(o_blocks, axis=0)
        if c + 1 < tt // L:
            gate_prep(c + 1, seq)
        for h in range(GLA_HEADS):
            ks = slice(h * GLA_DK, (h + 1) * GLA_DK)
            vs = slice(h * GLA_DV, (h + 1) * GLA_DV)
            s_s[h][...] = (decay_col[ks, :] * s_old[h]
                           + jnp.dot(kdt[ks, :], gv[:, vs], preferred_element_type=f32))
        go = go_ref[rows, :].astype(f32)
        y_gla = _head_norm(og, glag_ref[...]) * (go * _sigmoid(go))
        br_ref[rows, ML_W:ML_W + GLA_VW] = y_gla.astype(br_ref.dtype)

    for s in range(nb):
        gate_prep(0, s)
    for c in range(tt // L):
        for s in range(nb):
            one_seq(c, s)

    pos = t_idx * tt + lax.broadcasted_iota(jnp.int32, (tt, 1), 0)
    for s in range(nb):
        pu_ref, br_ref, ext_s = z_refs[s][7], br_all.at[s], ext_refs[s]
        n_ext = tt + POOL_BASE
        ext_s[POOL_BASE:n_ext, :] = pu_ref[...].astype(f32)
        for g, w in enumerate(POOL_WINDOWS):
            gs = slice(g * POOL_GDIM, (g + 1) * POOL_GDIM)
            u = ext_s[POOL_BASE:n_ext, gs]
            p, k = ext_s[:, gs], 1
            while k < w:
                p = p + pltpu.roll(p, k, 0)
                k *= 2
            acc = p[POOL_BASE:n_ext, :]
            cnt = jnp.minimum(pos + 1, w).astype(f32)
            d = acc / cnt - u
            y = jnp.dot(d.astype(bf16), pw_ref[g], preferred_element_type=f32) * ps_ref[:, gs]
            br_ref[:, ML_W + GLA_VW + g * POOL_GDIM:ML_W + GLA_VW + (g + 1) * POOL_GDIM] = y.astype(br_ref.dtype)
        ext_s[8:POOL_BASE, :] = ext_s[tt + 8:n_ext, :]

    @pl.when(t_idx == n_t - 1)
    def _():
        for s in range(nb):
            for h in range(ML_HEADS):
                c_out[s, h] = c_refs[s][h][...]
            for h in range(GLA_HEADS):
                s_out[s, h] = s_refs[s][h][...]
            n_out[s] = n_refs[s][0:ML_HEADS, :]
            m_out[s] = m_refs[s][...]
            buf_out[s] = ext_refs[s][POOL_BASE - POOL_BUF:POOL_BASE, :]


def _prompt_mix(z, zs, lw, batch, seq, tt, chunk, sub, nb):
    n_t = seq // tt
    assert batch % nb == 0

    def zspecs(s):
        row = lambda b, t: (b * nb + s) * n_t + t
        spec = lambda col, width: pl.BlockSpec((tt, width), lambda b, t: (row(b, t), col // width))
        return [spec(C_MQ, 512), spec(C_MK, 512), spec(C_MV, 512), spec(C_MO, 512),
                spec(C_GQ, 512), spec(C_GV, 512), spec(C_GO, 512), spec(C_PU, 512),
                pl.BlockSpec((tt, LANES), lambda b, t: (row(b, t), 0))]

    def full(shape):
        nd = len(shape)
        return pl.BlockSpec(shape, lambda b, t: (0,) * nd)

    def per_seq(shape):
        nd = len(shape)
        return pl.BlockSpec((nb,) + shape, lambda b, t: (b,) + (0,) * nd)

    kern = functools.partial(_prompt_mix_kernel, nb=nb, tt=tt, chunk=chunk, sub=sub, n_t=n_t)
    outs = pl.pallas_call(
        kern,
        grid=(batch // nb, n_t),
        in_specs=[sp for s in range(nb) for sp in zspecs(s)]
                 + [full((1, LANES)), full((1, ML_W)), full((LANES, GLA_KW)), full((1, GLA_KW)),
                    full((1, GLA_VW)), full((len(POOL_WINDOWS), POOL_GDIM, POOL_GDIM)), full((1, POOL_W))],
        out_specs=[
            pl.BlockSpec((nb, tt, N_BRANCH * ML_W), lambda b, t: (b, t, 0)),
            per_seq((ML_HEADS, ML_DH, ML_DH)), per_seq((ML_HEADS, ML_DH)), per_seq((8, LANES)),
            per_seq((GLA_HEADS, GLA_DK, GLA_DV)), per_seq((POOL_BUF, POOL_W)),
        ],
        out_shape=[
            jax.ShapeDtypeStruct((batch, seq, N_BRANCH * ML_W), bf16),
            jax.ShapeDtypeStruct((batch, ML_HEADS, ML_DH, ML_DH), f32),
            jax.ShapeDtypeStruct((batch, ML_HEADS, ML_DH), f32),
            jax.ShapeDtypeStruct((batch, 8, LANES), f32),
            jax.ShapeDtypeStruct((batch, GLA_HEADS, GLA_DK, GLA_DV), f32),
            jax.ShapeDtypeStruct((batch, POOL_BUF, POOL_W), f32),
        ],
        scratch_shapes=([pltpu.VMEM((ML_DH, ML_DH), f32)] * ML_HEADS
                        + [pltpu.VMEM((GLA_DK, GLA_DV), f32)] * GLA_HEADS
                        + [pltpu.VMEM((8, LANES), f32), pltpu.VMEM((8, LANES), f32),
                           pltpu.VMEM((tt + POOL_BASE, POOL_W), f32),
                           pltpu.VMEM((chunk, LANES), f32), pltpu.VMEM((chunk, LANES), f32),
                           pltpu.VMEM((16, chunk), f32), pltpu.VMEM((chunk, GLA_KW), f32)]) * nb,
        compiler_params=_cparams(("parallel", "arbitrary")),
        name="prompt_mix",
    )(*([z] * 8 + [zs]) * nb,
      lw["ifb"], lw["ml_g"], lw["gla_gw"], lw["gla_gb"], lw["gla_g"], lw["pool_w"], lw["pool_scale"])
    return (outs[0].reshape(batch * seq, N_BRANCH * ML_W),) + tuple(outs[1:])


def _sample_mix_kernel(q_ref, k_ref, v_ref, mo_ref, gqk_ref, gv_ref, go_ref, pu_ref, sm_ref,
                       c_in, n_in, m_in, s_in, buf_in,
                       ifb_ref, mlg_ref, gw_ref, gb_ref, glag_ref, pw_ref, ps_ref, *tail, sb, dec):
    br_all = tail[-6]
    c_out, n_out, m_out, s_out, buf_out = (r.at[0] for r in tail[-5:])
    step = pl.program_id(0)
    c_in, n_in, m_in, s_in, buf_in = (r.at[0] for r in (c_in, n_in, m_in, s_in, buf_in))
    br_ref = br_all.at[pl.ds(pl.multiple_of(step * sb, sb), sb), :]

    @pl.when(step == 0)
    def _():
        br_all[dec:, :] = jnp.zeros((br_all.shape[0] - dec, br_all.shape[1]), br_all.dtype)

    sm = sm_ref[...]
    def to_cols(x):
        parts = [x[:, j * LANES:(j + 1) * LANES].T for j in range(x.shape[1] // LANES)]
        return jnp.concatenate(parts, axis=0)

    y0 = sm + ifb_ref[...]
    logf_all = _log_sigmoid(y0)
    k_scale = ML_DH ** -0.5
    m_all = m_in[...]
    hm, m_new_cols = [], []
    for h in range(ML_HEADS):
        hs = slice(h * ML_DH, (h + 1) * ML_DH)
        q = q_ref[:, hs].astype(f32)
        k = k_ref[:, hs].astype(f32) * k_scale
        v = v_ref[:, hs].astype(f32)
        i_pre = y0[:, S_MI + h:S_MI + h + 1]
        logf = logf_all[:, S_MF + h:S_MF + h + 1]
        m = m_all[:, h:h + 1]
        inter = logf + m
        m_t = jnp.maximum(inter, i_pre)
        w_intra = jnp.exp(i_pre - m_t)
        w_inter = jnp.exp(inter - m_t)
        s = jnp.sum(q * k, axis=1, keepdims=True) * w_intra
        qb = q.astype(bf16)
        n_h = n_in[:, h, :]
        cq = jnp.concatenate(
            [lax.dot_general(qb, c_in[j, h].astype(bf16), _NT, preferred_element_type=f32)[j:j + 1, :]
             for j in range(sb)], axis=0)
        num = s * v + w_inter * cq
        den = s + w_inter * jnp.sum(n_h * q, axis=1, keepdims=True)
        hm.append(num / jnp.maximum(jnp.abs(den), jnp.exp(-m_t)))
        m_new = m_t
        w_s = w_intra
        w_c = w_inter
        n_out[:, h, :] = w_c * n_h + w_s * k
        m_new_cols.append(m_new)
        vw_cols = to_cols(v * w_s)
        for j in range(sb):
            c_out[j, h] = w_c[j:j + 1, :] * c_in[j, h] + vw_cols[:, j:j + 1] * k[j:j + 1, :]
    lane = lax.broadcasted_iota(jnp.int32, (sb, LANES), 1)
    m_pack = jnp.zeros((sb, LANES), f32)
    for h in range(ML_HEADS):
        m_pack = jnp.where(lane == h, m_new_cols[h], m_pack)
    m_out[...] = m_pack
    y_ml = _head_norm(jnp.concatenate(hm, axis=1), mlg_ref[...]) * _sigmoid(mo_ref[...].astype(f32))
    br_ref[:, 0:ML_W] = y_ml.astype(br_ref.dtype)

    log_a = _gla_log_decay(sm, gw_ref, gb_ref)
    decay = jnp.exp(log_a)
    q2 = gqk_ref[:, 0:GLA_KW].astype(f32) * (GLA_DK ** -0.5)
    k2 = gqk_ref[:, GLA_KW:2 * GLA_KW].astype(f32)
    gv = gv_ref[...].astype(f32)
    qe = (q2 * decay).astype(bf16)
    qk = q2 * k2
    k_cols = to_cols(k2)
    decay_cols = to_cols(decay)
    og = []
    for h in range(GLA_HEADS):
        ks = slice(h * GLA_DK, (h + 1) * GLA_DK)
        vs = slice(h * GLA_DV, (h + 1) * GLA_DV)
        a = jnp.sum(qk[:, ks], axis=1, keepdims=True)
        inter = jnp.concatenate(
            [jnp.dot(qe[:, ks], s_in[j, h].astype(bf16), preferred_element_type=f32)[j:j + 1, :]
             for j in range(sb)], axis=0)
        og.append(a * gv[:, vs] + inter)
        for j in range(sb):
            s_out[j, h] = (decay_cols[ks, j:j + 1] * s_in[j, h]
                           + k_cols[ks, j:j + 1] * gv[j:j + 1, vs])
    go = go_ref[...].astype(f32)
    y_gla = _head_norm(jnp.concatenate(og, axis=1), glag_ref[...]) * (go * _sigmoid(go))
    br_ref[:, ML_W:ML_W + GLA_VW] = y_gla.astype(br_ref.dtype)

    u = pu_ref[...].astype(f32)
    rowi = lax.broadcasted_iota(jnp.int32, (POOL_BUF + 1, POOL_GDIM), 0)
    d_rows = []
    for j in range(sb):
        ext = jnp.concatenate([buf_in[j], u[j:j + 1, :]], axis=0)
        buf_out[j] = ext[1:POOL_BUF + 1, :]
        parts = []
        for g, w in enumerate(POOL_WINDOWS):
            gs = slice(g * POOL_GDIM, (g + 1) * POOL_GDIM)
            win = jnp.sum(jnp.where(rowi >= POOL_BUF + 1 - w, ext[:, gs], 0.0), axis=0, keepdims=True)
            parts.append(win / float(w) - u[j:j + 1, gs])
        d_rows.append(jnp.concatenate(parts, axis=1))
    d = jnp.concatenate(d_rows, axis=0)
    for g in range(len(POOL_WINDOWS)):
        gs = slice(g * POOL_GDIM, (g + 1) * POOL_GDIM)
        y = jnp.dot(d[:, gs].astype(bf16), pw_ref[g], preferred_element_type=f32) * ps_ref[:, gs]
        br_ref[:, ML_W + GLA_VW + g * POOL_GDIM:ML_W + GLA_VW + (g + 1) * POOL_GDIM] = y.astype(br_ref.dtype)


def _sample_mix(z, zs, st, prev, layer, lw, row0, dec, sb):
    depth = st[0].shape[0]
    base = row0 // sb
    tail_rows = z.shape[0] - row0

    def zspec(col, width):
        blk = col // width
        return pl.BlockSpec((sb, width), lambda i: (base + i, blk))

    def full(shape):
        nd = len(shape)
        return pl.BlockSpec(shape, lambda i: (0,) * nd)

    st_shapes = [(ML_HEADS, ML_DH, ML_DH), (ML_HEADS, ML_DH), (LANES,),
                 (GLA_HEADS, GLA_DK, GLA_DV), (POOL_BUF, POOL_W)]
    st_specs = [pl.BlockSpec((1, sb) + s, functools.partial(lambda i, nd: (layer, i) + (0,) * nd, nd=len(s)))
                for s in st_shapes]
    n_in = N_ZBLOCKS + len(st_shapes) + 7
    prev = () if prev is None else tuple(prev)
    return pl.pallas_call(
        functools.partial(_sample_mix_kernel, sb=sb, dec=dec),
        grid=(dec // sb,),
        in_specs=[zspec(C_MQ, 512), zspec(C_MK, 512), zspec(C_MV, 512), zspec(C_MO, 512),
                  zspec(C_GQ, 512), zspec(C_GV, 512), zspec(C_GO, 512), zspec(C_PU, 512),
                  pl.BlockSpec((sb, LANES), lambda i: (base + i, 0))]
                 + st_specs
                 + [full((1, LANES)), full((1, ML_W)), full((LANES, GLA_KW)), full((1, GLA_KW)),
                    full((1, GLA_VW)), full((len(POOL_WINDOWS), POOL_GDIM, POOL_GDIM)), full((1, POOL_W))]
                 + [pl.BlockSpec(memory_space=pl.ANY)] * len(prev),
        out_specs=[pl.BlockSpec((tail_rows, N_BRANCH * ML_W), lambda i: (0, 0))] + st_specs,
        out_shape=[jax.ShapeDtypeStruct((tail_rows, N_BRANCH * ML_W), bf16)]
                  + [jax.ShapeDtypeStruct((depth, dec) + s, f32) for s in st_shapes],
        input_output_aliases={n_in + j: 1 + j for j in range(len(prev))},
        compiler_params=_cparams(("arbitrary",)),
        name="sample_mix",
    )(z, z, z, z, z, z, z, z, zs, *st,
      lw["ifb"], lw["ml_g"], lw["gla_gw"], lw["gla_gb"], lw["gla_g"], lw["pool_w"], lw["pool_scale"], *prev)


def _merge_kernel(*refs, tm, n_valid, n_prompt_tiles, x_starts, n_extra, tf):
    x_refs = refs[:len(x_starts)]
    brp_ref, brt_ref, g0_ref, g1_ref, g2_ref, wb_ref, wo_ref = refs[len(x_starts):len(x_starts) + 7]
    extra_in = refs[len(x_starts) + 7:len(x_starts) + 7 + n_extra]
    o_ref = refs[len(x_starts) + 7 + n_extra]
    route_out = refs[len(x_starts) + 8 + n_extra:]
    in_tail = pl.program_id(0) >= n_prompt_tiles
    mixed = None
    for n, g_ref in enumerate((g0_ref, g1_ref, g2_ref)):
        cols = slice(n * ML_W, (n + 1) * ML_W)
        br = jnp.where(in_tail, brt_ref[:, cols], brp_ref[:, cols])
        proj = jnp.dot(br, wb_ref[n], preferred_element_type=f32)
        term = _sigmoid(g_ref[...].astype(f32)) * proj
        mixed = term if mixed is None else mixed + term
    out = (_pick_rows(x_refs, x_starts)
           + jnp.dot(mixed.astype(bf16), wo_ref[...], preferred_element_type=f32))
    row = pl.program_id(0) * tm + lax.broadcasted_iota(jnp.int32, (tm, 1), 0)
    out = jnp.where(row < n_valid, out, 0.0)
    if n_extra == 4:
        out = _swiglu_residual(out, *extra_in, tf=tf)
    o_ref[...] = out
    if n_extra == 3:
        _route(out, *extra_in, *route_out, tm=tm, n_valid=n_valid)


def _merge(x_parts, br_prompt, br_tail, z, w_branch, w_out, tm, n_valid, route_w=None, ffn_w=None, tf=None):
    mp = sum(p.shape[0] for p in x_parts)
    nt = mp // tm
    route = route_w is not None
    assert not (route and ffn_w is not None)
    extra = tuple(route_w or ffn_w or ())
    route_specs = [_resident(a.shape) for a in extra]
    out_specs = [pl.BlockSpec((tm, D_MODEL), lambda i: (i, 0))]
    out_shape = [jax.ShapeDtypeStruct((mp, D_MODEL), f32)]
    if route:
        out_specs += [pl.BlockSpec((tm, D_MODEL), lambda i: (i, 0)),
                      pl.BlockSpec((tm, LANES), lambda i: (i, 0)),
                      pl.BlockSpec((tm, LANES), lambda i: (i, 0)),
                      pl.BlockSpec((1, N_EXPERTS, tm), lambda i: (i, 0, 0)),
                      pl.BlockSpec((1, 1, LANES), lambda i: (i, 0, 0))]
        out_shape += [jax.ShapeDtypeStruct((mp, D_MODEL), bf16),
                      jax.ShapeDtypeStruct((mp, LANES), f32),
                      jax.ShapeDtypeStruct((mp, LANES), f32),
                      jax.ShapeDtypeStruct((nt, N_EXPERTS, tm), f32),
                      jax.ShapeDtypeStruct((nt, 1, LANES), f32)]
    x_starts = tuple(sum(p.shape[0] for p in x_parts[:j]) // tm for j in range(len(x_parts)))
    n_p = br_prompt.shape[0] // tm
    gate_specs = [pl.BlockSpec((tm, D_MODEL), functools.partial(lambda i, n: (i, C_GATES // D_MODEL + n), n=n))
                  for n in range(N_BRANCH)]
    outs = pl.pallas_call(
        functools.partial(_merge_kernel, tm=tm, n_valid=n_valid, n_prompt_tiles=n_p, x_starts=x_starts,
                          n_extra=len(extra), tf=tf),
        grid=(nt,),
        in_specs=[
            *_split_rows_specs(x_parts, tm, D_MODEL),
            pl.BlockSpec((tm, N_BRANCH * ML_W), lambda i: (jnp.minimum(i, n_p - 1), 0)),
            pl.BlockSpec((tm, N_BRANCH * ML_W), lambda i: (jnp.maximum(i - n_p, 0), 0)),
            *gate_specs,
            _resident((N_BRANCH, ML_W, D_MODEL)),
            _resident((D_MODEL, D_MODEL)),
            *route_specs,
        ],
        out_specs=out_specs,
        out_shape=out_shape,
        compiler_params=_cparams(("parallel",)),
        name="merge_route" if route else ("merge_ffn" if extra else "merge"),
    )(*x_parts, br_prompt, br_tail, z, z, z, w_branch, w_out, *extra)
    return outs if route else outs[0]


def _route(x, g_ref, wrt_ref, before_ref, hb_ref, comb_ref, rnk_ref, rnkt_ref, cnt_ref, *, tm, n_valid):
    h = _rms(x, g_ref[...])
    h_hi = h.astype(bf16)
    hb_ref[...] = h_hi
    h_lo = (h - h_hi.astype(f32)).astype(bf16)
    p_hi = lax.dot_general(wrt_ref[...], h_hi, _NT, preferred_element_type=f32)
    p_lo = lax.dot_general(wrt_ref[...], h_lo, _NT, preferred_element_type=f32)
    logits = p_hi[0:N_EXPERTS] + p_hi[N_EXPERTS:] + p_lo[0:N_EXPERTS]
    eidx = lax.broadcasted_iota(jnp.int32, logits.shape, 0)
    mx = jnp.max(logits, axis=0, keepdims=True)
    e = jnp.exp(logits - mx)
    probs = e / jnp.sum(e, axis=0, keepdims=True)
    p1 = jnp.max(probs, axis=0, keepdims=True)
    i1 = jnp.min(jnp.where(probs == p1, eidx, N_EXPERTS), axis=0, keepdims=True)
    rest = jnp.where(eidx == i1, -1.0, probs)
    p2 = jnp.max(rest, axis=0, keepdims=True)
    i2 = jnp.min(jnp.where(rest == p2, eidx, N_EXPERTS), axis=0, keepdims=True)
    tot = p1 + p2
    comb_t = jnp.where(eidx == i1, p1 / tot, 0.0) + jnp.where(eidx == i2, p2 / tot, 0.0)
    tok = pl.program_id(0) * tm + lax.broadcasted_iota(jnp.int32, logits.shape, 1)
    sel = ((eidx == i1) | (eidx == i2)) & (tok < n_valid)
    rank = jnp.dot(sel.astype(bf16), before_ref[...], preferred_element_type=f32)
    rnk_t = jnp.where(sel, rank, -1.0)
    rnkt_ref[0] = rnk_t
    both = jnp.concatenate([comb_t, rnk_t, jnp.zeros((LANES - 2 * N_EXPERTS, tm), f32)], axis=0).T
    lane = lax.broadcasted_iota(jnp.int32, (tm, LANES), 1)
    comb_ref[...] = jnp.where(lane < N_EXPERTS, both, 0.0)
    rnk = jnp.where(lane < N_EXPERTS, pltpu.roll(both, LANES - N_EXPERTS, 1), -1.0)
    rnk_ref[...] = rnk
    cnt_ref[0] = jnp.sum((rnk >= 0.0).astype(f32), axis=0, keepdims=True)


def _moe_kernel(rounds_ref, hb_ref, rnk_ref, rnkt_ref, comb_ref, wg_ref, wu_ref, wd_ref, o_ref,
                *, cap, n_sub, ts):
    i, e = pl.program_id(0), pl.program_id(1)

    @pl.when(e == 0)
    def _():
        o_ref[...] = jnp.zeros_like(o_ref)

    lane = lax.broadcasted_iota(jnp.int32, (ts, LANES), 1)
    slot_rows = lax.broadcasted_iota(jnp.int32, (cap, ts), 0).astype(f32)
    slot_cols = lax.broadcasted_iota(jnp.int32, (ts, cap), 1).astype(f32)

    def round_body(r, carry):
        base = (r * cap).astype(f32)
        xs = []
        for j in range(n_sub):
            rt = rnkt_ref[j, pl.ds(e, 1), :]
            p = (rt - base == slot_rows).astype(bf16)
            xs.append(jnp.dot(p, hb_ref[j * ts:(j + 1) * ts, :],
                              preferred_element_type=f32).astype(bf16))
        xs = jnp.concatenate(xs, axis=0)
        a = jnp.dot(xs, wg_ref[0], preferred_element_type=f32)
        a = (a * _sigmoid(a)) * jnp.dot(xs, wu_ref[0], preferred_element_type=f32)
        y = jnp.dot(a.astype(bf16), wd_ref[0], preferred_element_type=f32).astype(bf16)
        for j in range(n_sub):
            rows = slice(j * ts, (j + 1) * ts)
            col = jnp.sum(jnp.where(lane == e, rnk_ref[rows, :], 0.0), axis=1, keepdims=True)
            w = jnp.sum(jnp.where(lane == e, comb_ref[rows, :], 0.0), axis=1, keepdims=True)
            pt = (col - base == slot_cols).astype(bf16)
            o_ref[rows, :] += w * jnp.dot(pt, y[j * cap:(j + 1) * cap, :], preferred_element_type=f32)
        return carry

    lax.fori_loop(0, rounds_ref[i * N_EXPERTS + e], round_body, 0)


def _moe(hb, rnk, rnkt, comb, rounds, wg, wu, wd, ts, n_sub, cap):
    mp = hb.shape[0]
    n_e, _, d_ff = wg.shape
    tsup = ts * n_sub
    grid_spec = pltpu.PrefetchScalarGridSpec(
        num_scalar_prefetch=1,
        grid=(mp // tsup, n_e),
        in_specs=[pl.BlockSpec((tsup, D_MODEL), lambda i, e, r: (i, 0)),
                  pl.BlockSpec((tsup, LANES), lambda i, e, r: (i, 0)),
                  pl.BlockSpec((n_sub, N_EXPERTS, ts), lambda i, e, r: (i, 0, 0)),
                  pl.BlockSpec((tsup, LANES), lambda i, e, r: (i, 0)),
                  pl.BlockSpec((1, D_MODEL, d_ff), lambda i, e, r: (e, 0, 0)),
                  pl.BlockSpec((1, D_MODEL, d_ff), lambda i, e, r: (e, 0, 0)),
                  pl.BlockSpec((1, d_ff, D_MODEL), lambda i, e, r: (e, 0, 0))],
        out_specs=pl.BlockSpec((tsup, D_MODEL), lambda i, e, r: (i, 0)),
    )
    return pl.pallas_call(
        functools.partial(_moe_kernel, cap=cap, n_sub=n_sub, ts=ts),
        grid_spec=grid_spec,
        out_shape=jax.ShapeDtypeStruct((mp, D_MODEL), f32),
        compiler_params=_cparams(("parallel", "arbitrary")),
        name="moe",
    )(rounds, hb, rnk, rnkt, comb, wg, wu, wd)


def _swiglu_residual(x, g_ref, wg_ref, wu_ref, wd_ref, *, tf):
    h = _rms(x, g_ref[...]).astype(bf16)
    acc = x
    assert wg_ref.shape[1] % tf == 0
    for c in range(wg_ref.shape[1] // tf):
        cols = slice(c * tf, (c + 1) * tf)
        a = jnp.dot(h, wg_ref[:, cols], preferred_element_type=f32)
        a = (a * _sigmoid(a)) * jnp.dot(h, wu_ref[:, cols], preferred_element_type=f32)
        acc = acc + jnp.dot(a.astype(bf16), wd_ref[cols, :], preferred_element_type=f32)
    return acc


def _final_norm_kernel(x_ref, y_ref, g_ref, op_ref, os_ref, *, n_prompt_tiles, dec):
    i = pl.program_id(0)
    out = _rms(x_ref[...] + y_ref[...], g_ref[...])

    @pl.when(i < n_prompt_tiles)
    def _():
        op_ref[...] = out

    @pl.when(i == n_prompt_tiles)
    def _():
        os_ref[...] = out[0:dec]


def _final_norm(x, y, g, tm, m_prompt, dec):
    n_p = m_prompt // tm
    assert n_p * tm == m_prompt and dec <= tm
    return pl.pallas_call(
        functools.partial(_final_norm_kernel, n_prompt_tiles=n_p, dec=dec),
        grid=(n_p + 1,),
        in_specs=[pl.BlockSpec((tm, D_MODEL), lambda i: (i, 0)),
                  pl.BlockSpec((tm, D_MODEL), lambda i: (i, 0)),
                  pl.BlockSpec((1, D_MODEL), lambda i: (0, 0))],
        out_specs=[pl.BlockSpec((tm, D_MODEL), lambda i: (jnp.minimum(i, n_p - 1), 0)),
                   pl.BlockSpec((dec, D_MODEL), lambda i: (0, 0))],
        out_shape=[jax.ShapeDtypeStruct((m_prompt, D_MODEL), f32),
                   jax.ShapeDtypeStruct((dec, D_MODEL), f32)],
        compiler_params=_cparams(("arbitrary",)),
        name="final_norm",
    )(x, y, g)


O_MI = 3 * ML_W
O_MO = O_MI + 2 * ML_HEADS
O_GLR = O_MO + ML_W + 2 * GLA_KW + GLA_VW
O_GO = O_GLR + GLA_RANK
D_IN = O_GO + GLA_VW + POOL_W + N_BRANCH * D_MODEL


def _regroup_kernel(w_ref, if_ref, glr_ref, main_ref, small_ref):
    main_ref[0] = w_ref[0].astype(bf16)
    n_if, n_glr = if_ref.shape[1], glr_ref.shape[1]
    small_ref[0, 0:n_if, :] = if_ref[0].astype(bf16)
    small_ref[0, n_if:n_if + n_glr, :] = glr_ref[0].astype(bf16)
    small_ref[0, n_if + n_glr:, :] = jnp.zeros((LANES - n_if - n_glr, small_ref.shape[2]), bf16)


def _regroup_w_in(wt, tr=512):
    depth, n, d = wt.shape
    assert n == D_IN and Z_MAIN == D_IN - 2 * ML_HEADS - GLA_RANK
    assert O_MI % tr == 0 and (O_GLR - O_MO) % tr == 0 and (D_IN - O_GO) % tr == 0
    c1, c2 = O_MI // tr, (O_MI + O_GLR - O_MO) // tr

    def src_row(c):
        skip = jnp.where(c < c1, 0, jnp.where(c < c2, O_MO - O_MI, O_MO - O_MI + O_GO - O_GLR))
        return pl.multiple_of(c * tr + skip, 8)

    return pl.pallas_call(
        _regroup_kernel,
        grid=(depth, Z_MAIN // tr),
        in_specs=[pl.BlockSpec((pl.Element(1), pl.Element(rows), pl.Element(d)), index_map)
                  for rows, index_map in ((tr, lambda l, c: (l, src_row(c), 0)),
                                          (O_MO - O_MI, lambda l, c: (l, O_MI, 0)),
                                          (O_GO - O_GLR, lambda l, c: (l, O_GLR, 0)))],
        out_specs=[pl.BlockSpec((1, tr, d), lambda l, c: (l, c, 0)),
                   pl.BlockSpec((1, LANES, d), lambda l, c: (l, 0, 0))],
        out_shape=[jax.ShapeDtypeStruct((depth, Z_MAIN, d), bf16),
                   jax.ShapeDtypeStruct((depth, LANES, d), bf16)],
        compiler_params=_cparams(("parallel", "arbitrary")),
        name="regroup_w_in",
    )(wt, wt, wt)


def _layer_weights(l, norm1_g, if_bias, ml_g, gla_gw, gla_gb, gla_g, pool_w, pool_scale,
                   w_branch, w_out):
    ifb = jnp.zeros((1, LANES), f32)
    ifb = ifb.at[0, S_MI:S_MI + ML_HEADS].set(if_bias[l, 0]).at[0, S_MF:S_MF + ML_HEADS].set(if_bias[l, 1])
    gw = jnp.zeros((LANES, GLA_KW), f32).at[S_GLR:S_GLR + GLA_RANK].set(gla_gw[l]).astype(bf16)
    return dict(
        norm1_g=norm1_g[l][None], ifb=ifb, ml_g=ml_g[l][None],
        gla_gw=gw, gla_gb=gla_gb[l][None], gla_g=gla_g[l][None], pool_w=pool_w[l].astype(bf16),
        pool_scale=pool_scale[l][None], w_branch=w_branch[l].astype(bf16), w_out=w_out[l].astype(bf16))


def _forward(x_prompt, x_sample, state_mlstm_C, state_mlstm_n, state_mlstm_m, state_gla_S,
             state_pool_buf, norm1_g, w_in, mlstm_if_bias, mlstm_norm_g, gla_gate_w, gla_gate_b,
             gla_norm_g, pool_w, pool_scale, w_branch, w_out, norm2_g, ffn_wg, ffn_wu, ffn_wd,
             router_w, moe_wg, moe_wu, moe_wd, final_norm_g, *, tm, tn, tt, chunk, sub, nb, sb, tf_dense, moe_sub, moe_cap):
    batch, seq, _ = x_prompt.shape
    dec = x_sample.shape[0]
    depth = w_in.shape[0]
    m_prompt = batch * seq
    m_all = m_prompt + dec
    mp = -(-m_all // tm) * tm
    assert m_prompt % tm == 0
    x_parts = (x_prompt.reshape(m_prompt, D_MODEL),
               jnp.concatenate([x_sample.reshape(dec, D_MODEL), jnp.zeros((mp - m_all, D_MODEL), f32)], axis=0))
    outs = [[] for _ in range(5)]
    moe_out = None
    st = (state_mlstm_C, state_mlstm_n, jnp.pad(state_mlstm_m, ((0, 0), (0, 0), (0, LANES - ML_HEADS))),
          state_gla_S, state_pool_buf)
    st_new = None
    w_main, w_small = _regroup_w_in(jnp.swapaxes(w_in, 1, 2))
    for l in range(depth):
        lw = _layer_weights(l, norm1_g, mlstm_if_bias, mlstm_norm_g, gla_gate_w, gla_gate_b,
                            gla_norm_g, pool_w, pool_scale, w_branch, w_out)
        z, zs = _norm_matmul(x_parts, lw["norm1_g"], w_main, w_small, l, tm, tn)
        br_p, c_p, n_p, m_p, s_p, buf_p = _prompt_mix(z, zs, lw, batch, seq, tt, chunk, sub, nb)
        br_t, *st_new = _sample_mix(z, zs, st, st_new, l, lw, m_prompt, dec, sb)
        j = l // 2
        if l % 2 == 0:
            x = _merge(x_parts, br_p, br_t, z, lw["w_branch"], lw["w_out"], tm, m_all,
                       ffn_w=(norm2_g[l][None], ffn_wg[j].astype(bf16), ffn_wu[j].astype(bf16),
                              ffn_wd[j].astype(bf16)), tf=tf_dense)
        else:
            wr_hi = router_w[j].astype(bf16)
            wr_lo = (router_w[j] - wr_hi.astype(f32)).astype(bf16)
            wrt = jnp.concatenate([wr_hi, wr_lo], axis=1).T
            before = (jnp.arange(tm)[:, None] < jnp.arange(tm)[None, :]).astype(bf16)
            x, hb, comb, rnk, rnkt, cnt = _merge(x_parts, br_p, br_t, z, lw["w_branch"], lw["w_out"], tm, m_all,
                                                 route_w=(norm2_g[l][None], wrt, before))
            n_sup = mp // (tm * moe_sub)
            cnt = cnt[:, 0, :N_EXPERTS].reshape(n_sup, moe_sub, N_EXPERTS).max(axis=1)
            rounds = jnp.ceil(cnt / moe_cap).astype(jnp.int32).reshape(-1)
            moe_out = _moe(hb, rnk, rnkt, comb, rounds, moe_wg[j].astype(bf16), moe_wu[j].astype(bf16),
                           moe_wd[j].astype(bf16), tm, moe_sub, moe_cap)
            if l + 1 < depth:
                x = x + moe_out
                moe_out = None
        x_parts = (x,)
        for lst, val in zip(outs, (c_p, n_p, m_p[:, :ML_HEADS, 0], s_p, buf_p)):
            lst.append(val)
    if moe_out is None:
        moe_out = jnp.zeros_like(x)
    y_prompt, y_sample = _final_norm(x, moe_out, final_norm_g[None], tm, m_prompt, dec)
    c_p, n_p, m_p, s_p, buf_p = (jnp.stack(o) for o in outs)
    c_s, n_s, m_s, s_s, buf_s = st_new
    return (y_prompt.reshape(batch, seq, D_MODEL), y_sample.reshape(dec, 1, D_MODEL),
            c_p, c_s, n_p, n_s, m_p, m_s[:, :, :ML_HEADS], s_p, s_s, buf_p, buf_s)


def kernel(x_prompt, x_sample, state_mlstm_C, state_mlstm_n, state_mlstm_m, state_gla_S, state_pool_buf, norm1_g, w_in, mlstm_if_bias, mlstm_norm_g, gla_gate_w, gla_gate_b, gla_norm_g, pool_w, pool_scale, w_branch, w_out, norm2_g, ffn_wg, ffn_wu, ffn_wd, router_w, moe_wg, moe_wu, moe_wd, final_norm_g):
    return _forward(x_prompt, x_sample, state_mlstm_C, state_mlstm_n, state_mlstm_m, state_gla_S,
                    state_pool_buf, norm1_g, w_in, mlstm_if_bias, mlstm_norm_g, gla_gate_w, gla_gate_b,
                    gla_norm_g, pool_w, pool_scale, w_branch, w_out, norm2_g, ffn_wg, ffn_wu, ffn_wd,
                    router_w, moe_wg, moe_wu, moe_wd, final_norm_g,
                    tm=512, tn=1024, tt=512, chunk=128, sub=64, nb=2, sb=16, tf_dense=MXU_COLS, moe_sub=3, moe_cap=160)
```

```python
import functools

import jax
import jax.numpy as jnp
from jax import lax
from jax.experimental import pallas as pl
from jax.experimental.pallas import tpu as pltpu

f32 = jnp.float32
bf16 = jnp.bfloat16

D_MODEL = 1024
ML_HEADS, ML_DH = 4, 128
ML_W = ML_HEADS * ML_DH
GLA_HEADS, GLA_DK, GLA_DV = 4, 64, 128
GLA_KW, GLA_VW = GLA_HEADS * GLA_DK, GLA_HEADS * GLA_DV
GLA_RANK = 16
GLA_TAU = 16.0
POOL_GDIM = 128
POOL_WINDOWS = (2, 4, 8, 16)
POOL_W = POOL_GDIM * len(POOL_WINDOWS)
POOL_BUF = 15
POOL_BASE = 24
N_BRANCH = 3
N_EXPERTS = 8
EPS = 1e-6
NEG = -1e30
LANES = 128
MXU_COLS = 256

C_MQ, C_MK, C_MV, C_MO = 0, 512, 1024, 1536
C_GQ, C_GK, C_GV, C_GO, C_PU, C_GATES = 2048, 2304, 2560, 3072, 3584, 4096
Z_MAIN = C_GATES + N_BRANCH * D_MODEL
S_MI, S_MF, S_GLR = 0, 4, 8

VMEM_LIMIT = 56 * 1024 * 1024

_NT = (((1,), (1,)), ((), ()))


def _cparams(sem):
    return pltpu.CompilerParams(dimension_semantics=sem, vmem_limit_bytes=VMEM_LIMIT)


def _log_sigmoid(x):
    return jnp.minimum(x, 0.0) - jnp.log(1.0 + jnp.exp(-jnp.abs(x)))


def _sigmoid(x):
    return 0.5 * jnp.tanh(0.5 * x) + 0.5


def _rms(x, g):
    ms = jnp.mean(x * x, axis=-1, keepdims=True)
    return x * lax.rsqrt(ms + EPS) * g


def _lower_tri(n):
    r = lax.broadcasted_iota(jnp.int32, (n, n), 0)
    c = lax.broadcasted_iota(jnp.int32, (n, n), 1)
    return c <= r


def _cumsum_rows(tri_bf16, a):
    a1 = a.astype(bf16)
    r = a - a1.astype(f32)
    a2 = r.astype(bf16)
    a3 = (r - a2.astype(f32)).astype(bf16)
    d = lambda y: jnp.dot(tri_bf16, y, preferred_element_type=f32)
    return d(a1) + d(a2) + d(a3)


def _cumsum_lanes(triu_bf16, a):
    a1 = a.astype(bf16)
    r = a - a1.astype(f32)
    a2 = r.astype(bf16)
    a3 = (r - a2.astype(f32)).astype(bf16)
    d = lambda y: jnp.dot(y, triu_bf16, preferred_element_type=f32)
    return d(a1) + d(a2) + d(a3)


def _resident(shape):
    nd = len(shape)
    return pl.BlockSpec(shape, lambda *_: (0,) * nd, pipeline_mode=pl.Buffered(1))


def _split_rows_specs(parts, tm, width):
    specs, start = [], 0
    for p in parts:
        n_t = p.shape[0] // tm
        assert n_t * tm == p.shape[0]
        specs.append(pl.BlockSpec(
            (tm, width), functools.partial(lambda i, s, n: (jnp.clip(i - s, 0, n - 1), 0), s=start, n=n_t)))
        start += n_t
    return specs


def _pick_rows(refs, starts):
    i = pl.program_id(0)
    x = refs[0][...]
    for r, s in zip(refs[1:], starts[1:]):
        x = jnp.where(i >= s, r[...], x)
    return x


def _norm_matmul_kernel(*refs, tn, starts):
    x_refs = refs[:len(starts)]
    g_ref, w_ref, ws_ref, z_ref, zs_ref = refs[len(starts):]
    h = _rms(_pick_rows(x_refs, starts), g_ref[...]).astype(bf16)
    zs_ref[...] = lax.dot_general(h, ws_ref[0], _NT, preferred_element_type=f32)
    for c in range(w_ref.shape[1] // tn):
        cols = slice(c * tn, (c + 1) * tn)
        z_ref[:, cols] = lax.dot_general(h, w_ref[0, cols, :], _NT,
                                         preferred_element_type=f32).astype(z_ref.dtype)


def _norm_matmul(x_parts, g, wt_main, wt_small, layer, tm, tn):
    mp = sum(p.shape[0] for p in x_parts)
    n = wt_main.shape[1]
    starts = tuple(sum(p.shape[0] for p in x_parts[:j]) // tm for j in range(len(x_parts)))
    layer_block = lambda width: pl.BlockSpec((1, width, D_MODEL), lambda i: (layer, 0, 0),
                                             pipeline_mode=pl.Buffered(1))
    return pl.pallas_call(
        functools.partial(_norm_matmul_kernel, tn=tn, starts=starts),
        grid=(mp // tm,),
        in_specs=[
            *_split_rows_specs(x_parts, tm, D_MODEL),
            _resident((1, D_MODEL)),
            layer_block(n),
            layer_block(LANES),
        ],
        out_specs=[
            pl.BlockSpec((tm, n), lambda i: (i, 0)),
            pl.BlockSpec((tm, LANES), lambda i: (i, 0)),
        ],
        out_shape=[jax.ShapeDtypeStruct((mp, n), bf16), jax.ShapeDtypeStruct((mp, LANES), f32)],
        compiler_params=_cparams(("parallel",)),
        name="norm_matmul",
    )(*x_parts, g, wt_main, wt_small)


def _head_norm(h, g):
    outs = []
    for j in range(h.shape[1] // LANES):
        hj = h[:, j * LANES:(j + 1) * LANES]
        outs.append(hj * lax.rsqrt(jnp.mean(hj * hj, axis=-1, keepdims=True) + EPS))
    return jnp.concatenate(outs, axis=1) * g


def _gla_log_decay(sm, gw_ref, gb_ref):
    xg = jnp.dot(sm.astype(bf16), gw_ref[...], preferred_element_type=f32) + gb_ref[...]
    return _log_sigmoid(xg) * (1.0 / GLA_TAU)


N_ZBLOCKS = 9


def _prompt_mix_kernel(*refs, nb, tt, chunk, sub, n_t):
    z_refs = [refs[s * N_ZBLOCKS:(s + 1) * N_ZBLOCKS] for s in range(nb)]
    ifb_ref, mlg_ref, gw_ref, gb_ref, glag_ref, pw_ref, ps_ref = refs[nb * N_ZBLOCKS:nb * N_ZBLOCKS + 7]
    br_all, c_out, n_out, m_out, s_out, buf_out = refs[nb * N_ZBLOCKS + 7:nb * N_ZBLOCKS + 13]
    scratch = refs[nb * N_ZBLOCKS + 13:]
    per = ML_HEADS + GLA_HEADS + 7
    c_refs = [scratch[s * per:s * per + ML_HEADS] for s in range(nb)]
    s_refs = [scratch[s * per + ML_HEADS:s * per + ML_HEADS + GLA_HEADS] for s in range(nb)]
    n_refs = [scratch[s * per + ML_HEADS + GLA_HEADS] for s in range(nb)]
    m_refs = [scratch[s * per + ML_HEADS + GLA_HEADS + 1] for s in range(nb)]
    ext_refs = [scratch[s * per + ML_HEADS + GLA_HEADS + 2] for s in range(nb)]
    prep_refs = [scratch[s * per + ML_HEADS + GLA_HEADS + 3:(s + 1) * per] for s in range(nb)]
    t_idx = pl.program_id(1)
    L = chunk

    @pl.when(t_idx == 0)
    def _():
        for s in range(nb):
            for r in (*c_refs[s], *s_refs[s], n_refs[s], m_refs[s]):
                r[...] = jnp.zeros_like(r)
            ext_refs[s][0:POOL_BASE, :] = jnp.zeros((POOL_BASE, POOL_W), f32)

    tri = _lower_tri(L)
    tri_b = tri.astype(bf16)
    causal_sub = _lower_tri(sub)
    assert L == LANES
    ones_b = jnp.ones((L, LANES), bf16)
    triu_b = (lax.broadcasted_iota(jnp.int32, (L, L), 0)
              <= lax.broadcasted_iota(jnp.int32, (L, L), 1)).astype(bf16)
    ifb = ifb_ref[...]
    k_scale = ML_DH ** -0.5
    q_scale = GLA_DK ** -0.5

    def gate_prep(c, seq):
        sm_ref = z_refs[seq][8]
        y0_p, bc_p, rows_p, b_p = prep_refs[seq]
        sm = sm_ref[c * L:(c + 1) * L, :]
        y0 = sm + ifb
        y0_p[...] = y0
        y0t = y0.T[0:8, :]
        bct = _cumsum_lanes(triu_b, _log_sigmoid(y0t))
        rows_p[0:8, :] = y0t
        rows_p[8:16, :] = bct
        bc_p[...] = jnp.concatenate([bct, jnp.zeros((LANES - 8, L), f32)], axis=0).T
        b_p[...] = _cumsum_rows(tri_b, _gla_log_decay(sm, gw_ref, gb_ref))

    def one_seq(c, seq):
        q_ref, k_ref, v_ref, mo_ref, gqk_ref, gv_ref, go_ref, _, sm_ref = z_refs[seq]
        br_ref, c_s, s_s = br_all.at[seq], c_refs[seq], s_refs[seq]
        n_old, m_old = n_refs[seq][...], m_refs[seq][...]
        n_rows, m_rows = [], []
        rows = slice(c * L, (c + 1) * L)
        y0_p, bc_p, rows_p, b_p = prep_refs[seq]

        y0, bc = y0_p[...], bc_p[...]
        y0t, bct = rows_p[0:8, :], rows_p[8:16, :]
        b = b_p[...]
        hm = []
        for h in range(ML_HEADS):
            hs = slice(h * ML_DH, (h + 1) * ML_DH)
            v = v_ref[rows, hs].astype(f32)
            kb = k_ref[rows, hs].astype(bf16)
            qb, vb = q_ref[rows, hs].astype(bf16), v.astype(bf16)
            bcol = bc[:, S_MF + h:S_MF + h + 1]
            icol = y0[:, S_MI + h:S_MI + h + 1]
            brow = bct[S_MF + h:S_MF + h + 1, :]
            irow = y0t[S_MI + h:S_MI + h + 1, :]
            m_row = m_old[h:h + 1, :]
            cmat = c_s[h][...]
            nrow = n_old[h:h + 1, :]
            bcol_r = jnp.broadcast_to(bcol, (L, LANES))
            dm = jnp.where(tri, bcol_r - brow + irow, NEG)
            inter_r = bcol_r + m_row
            m_t_r = jnp.maximum(inter_r, jnp.broadcast_to(jnp.max(dm, axis=1, keepdims=True), (L, LANES)))
            w_intra = jnp.exp(dm - m_t_r) * k_scale
            w_inter_r = jnp.exp(inter_r - m_t_r)
            s = lax.dot_general(qb, kb, _NT, preferred_element_type=f32) * w_intra
            s_hi = s.astype(bf16)
            s_lo = (s - s_hi.astype(f32)).astype(bf16)
            r_intra = jnp.dot(s_hi, jnp.concatenate([vb, ones_b], axis=1), preferred_element_type=f32)
            c_aug = jnp.concatenate([cmat, jnp.broadcast_to(nrow, (LANES, LANES))], axis=0).astype(bf16)
            r_inter = lax.dot_general(qb, c_aug, _NT, preferred_element_type=f32)
            rs_lo = jnp.dot(s_lo, ones_b, preferred_element_type=f32)
            num = r_intra[:, 0:LANES] + w_inter_r * r_inter[:, 0:LANES]
            den_r = r_intra[:, LANES:] + rs_lo + w_inter_r * r_inter[:, LANES:]
            hm.append(num / jnp.maximum(jnp.abs(den_r), jnp.exp(-m_t_r)))
            b_last = bcol[L - 1:L, :]
            m = m_row[:, 0:1]
            g = b_last - bcol + icol
            m_new = jnp.maximum(b_last + m, jnp.max(g, axis=0, keepdims=True))
            w_s = jnp.broadcast_to(jnp.exp(g - m_new) * k_scale, (L, LANES))
            w_c = jnp.exp(b_last + m - m_new)
            vwt = (v * w_s).T.astype(bf16)
            c_s[h][...] = w_c * cmat + jnp.dot(vwt, kb, preferred_element_type=f32)
            n_rows.append(w_c * nrow + jnp.sum(w_s * k_ref[rows, hs].astype(f32), axis=0, keepdims=True))
            m_rows.append(jnp.broadcast_to(m_new, (1, LANES)))
        pad_rows = [jnp.zeros((8 - ML_HEADS, LANES), f32)]
        n_refs[seq][...] = jnp.concatenate(n_rows + pad_rows, axis=0)
        m_refs[seq][...] = jnp.concatenate(m_rows + pad_rows, axis=0)
        y_ml = (_head_norm(jnp.concatenate(hm, axis=1), mlg_ref[...])
                * _sigmoid(mo_ref[rows, :].astype(f32)))
        br_ref[rows, 0:ML_W] = y_ml.astype(br_ref.dtype)

        q2 = gqk_ref[rows, 0:GLA_KW].astype(f32) * q_scale
        k2 = gqk_ref[rows, GLA_KW:2 * GLA_KW].astype(f32)
        gv = gv_ref[rows, :].astype(bf16)
        s_old = [s_s[h][...] for h in range(GLA_HEADS)]
        qe_chunk = (q2 * jnp.exp(b)).astype(bf16)
        b_last = b[L - 1:L, :]
        kdt = (k2 * jnp.exp(b_last - b)).T.astype(bf16)
        decay_col = jnp.exp(b.T[:, L - 1:L])
        o_blocks = []
        for blk in range(L // sub):
            s0, s1 = blk * sub, (blk + 1) * sub
            mid = s0 + sub // 2
            b_blk = b[s0:s1]
            b_mid = b[mid - 1:mid, :]
            qe_d = (q2[s0:s1] * jnp.exp(b_blk - b_mid)).astype(bf16)
            ke_d = (k2[s0:s1] * jnp.exp(b_mid - b_blk)).astype(bf16)
            if blk > 0:
                b_start = b[s0 - 1:s0, :]
                qe_o = (q2[s0:s1] * jnp.exp(b_blk - b_start)).astype(bf16)
                ke_o = (k2[0:s0] * jnp.exp(b_start - b[0:s0])).astype(bf16)
            o_heads = []
            for h in range(GLA_HEADS):
                ks = slice(h * GLA_DK, (h + 1) * GLA_DK)
                vs = slice(h * GLA_DV, (h + 1) * GLA_DV)
                a = lax.dot_general(qe_d[:, ks], ke_d[:, ks], _NT, preferred_element_type=f32)
                a = jnp.where(causal_sub, a, 0.0)
                o = (jnp.dot(a.astype(bf16), gv[s0:s1, vs], preferred_element_type=f32)
                     + jnp.dot(qe_chunk[s0:s1, ks], s_old[h].astype(bf16), preferred_element_type=f32))
                if blk > 0:
                    a = lax.dot_general(qe_o[:, ks], ke_o[:, ks], _NT, preferred_element_type=f32)
                    o = o + jnp.dot(a.astype(bf16), gv[0:s0, vs], preferred_element_type=f32)
                o_heads.append(o)
            o_blocks.append(jnp.concatenate(o_heads, axis=1))
        og = jnp.concatenate(o_blocks, axis=0)
        if c + 1 < tt // L:
            gate_prep(c + 1, seq)
        for h in range(GLA_HEADS):
            ks = slice(h * GLA_DK, (h + 1) * GLA_DK)
            vs = slice(h * GLA_DV, (h + 1) * GLA_DV)
            s_s[h][...] = (decay_col[ks, :] * s_old[h]
                           + jnp.dot(kdt[ks, :], gv[:, vs], preferred_element_type=f32))
        go = go_ref[rows, :].astype(f32)
        y_gla = _head_norm(og, glag_ref[...]) * (go * _sigmoid(go))
        br_ref[rows, ML_W:ML_W + GLA_VW] = y_gla.astype(br_ref.dtype)

    for s in range(nb):
        gate_prep(0, s)
    for c in range(tt // L):
        for s in range(nb):
            one_seq(c, s)

    pos = t_idx * tt + lax.broadcasted_iota(jnp.int32, (tt, 1), 0)
    for s in range(nb):
        pu_ref, br_ref, ext_s = z_refs[s][7], br_all.at[s], ext_refs[s]
        n_ext = tt + POOL_BASE
        ext_s[POOL_BASE:n_ext, :] = pu_ref[...].astype(f32)
        for g, w in enumerate(POOL_WINDOWS):
            gs = slice(g * POOL_GDIM, (g + 1) * POOL_GDIM)
            u = ext_s[POOL_BASE:n_ext, gs]
            p, k = ext_s[:, gs], 1
            while k < w:
                p = p + pltpu.roll(p, k, 0)
                k *= 2
            acc = p[POOL_BASE:n_ext, :]
            cnt = jnp.minimum(pos + 1, w).astype(f32)
            d = acc / cnt - u
            y = jnp.dot(d.astype(bf16), pw_ref[g], preferred_element_type=f32) * ps_ref[:, gs]
            br_ref[:, ML_W + GLA_VW + g * POOL_GDIM:ML_W + GLA_VW + (g + 1) * POOL_GDIM] = y.astype(br_ref.dtype)
        ext_s[8:POOL_BASE, :] = ext_s[tt + 8:n_ext, :]

    @pl.when(t_idx == n_t - 1)
    def _():
        for s in range(nb):
            for h in range(ML_HEADS):
                c_out[s, h] = c_refs[s][h][...]
            for h in range(GLA_HEADS):
                s_out[s, h] = s_refs[s][h][...]
            n_out[s] = n_refs[s][0:ML_HEADS, :]
            m_out[s] = m_refs[s][...]
            buf_out[s] = ext_refs[s][POOL_BASE - POOL_BUF:POOL_BASE, :]


def _prompt_mix(z, zs, lw, batch, seq, tt, chunk, sub, nb):
    n_t = seq // tt
    assert batch % nb == 0

    def zspecs(s):
        row = lambda b, t: (b * nb + s) * n_t + t
        spec = lambda col, width: pl.BlockSpec((tt, width), lambda b, t: (row(b, t), col // width))
        return [spec(C_MQ, 512), spec(C_MK, 512), spec(C_MV, 512), spec(C_MO, 512),
                spec(C_GQ, 512), spec(C_GV, 512), spec(C_GO, 512), spec(C_PU, 512),
                pl.BlockSpec((tt, LANES), lambda b, t: (row(b, t), 0))]

    def full(shape):
        nd = len(shape)
        return pl.BlockSpec(shape, lambda b, t: (0,) * nd)

    def per_seq(shape):
        nd = len(shape)
        return pl.BlockSpec((nb,) + shape, lambda b, t: (b,) + (0,) * nd)

    kern = functools.partial(_prompt_mix_kernel, nb=nb, tt=tt, chunk=chunk, sub=sub, n_t=n_t)
    outs = pl.pallas_call(
        kern,
        grid=(batch // nb, n_t),
        in_specs=[sp for s in range(nb) for sp in zspecs(s)]
                 + [full((1, LANES)), full((1, ML_W)), full((LANES, GLA_KW)), full((1, GLA_KW)),
                    full((1, GLA_VW)), full((len(POOL_WINDOWS), POOL_GDIM, POOL_GDIM)), full((1, POOL_W))],
        out_specs=[
            pl.BlockSpec((nb, tt, N_BRANCH * ML_W), lambda b, t: (b, t, 0)),
            per_seq((ML_HEADS, ML_DH, ML_DH)), per_seq((ML_HEADS, ML_DH)), per_seq((8, LANES)),
            per_seq((GLA_HEADS, GLA_DK, GLA_DV)), per_seq((POOL_BUF, POOL_W)),
        ],
        out_shape=[
            jax.ShapeDtypeStruct((batch, seq, N_BRANCH * ML_W), bf16),
            jax.ShapeDtypeStruct((batch, ML_HEADS, ML_DH, ML_DH), f32),
            jax.ShapeDtypeStruct((batch, ML_HEADS, ML_DH), f32),
            jax.ShapeDtypeStruct((batch, 8, LANES), f32),
            jax.ShapeDtypeStruct((batch, GLA_HEADS, GLA_DK, GLA_DV), f32),
            jax.ShapeDtypeStruct((batch, POOL_BUF, POOL_W), f32),
        ],
        scratch_shapes=([pltpu.VMEM((ML_DH, ML_DH), f32)] * ML_HEADS
                        + [pltpu.VMEM((GLA_DK, GLA_DV), f32)] * GLA_HEADS
                        + [pltpu.VMEM((8, LANES), f32), pltpu.VMEM((8, LANES), f32),
                           pltpu.VMEM((tt + POOL_BASE, POOL_W), f32),
                           pltpu.VMEM((chunk, LANES), f32), pltpu.VMEM((chunk, LANES), f32),
                           pltpu.VMEM((16, chunk), f32), pltpu.VMEM((chunk, GLA_KW), f32)]) * nb,
        compiler_params=_cparams(("parallel", "arbitrary")),
        name="prompt_mix",
    )(*([z] * 8 + [zs]) * nb,
      lw["ifb"], lw["ml_g"], lw["gla_gw"], lw["gla_gb"], lw["gla_g"], lw["pool_w"], lw["pool_scale"])
    return (outs[0].reshape(batch * seq, N_BRANCH * ML_W),) + tuple(outs[1:])


def _sample_mix_kernel(q_ref, k_ref, v_ref, mo_ref, gqk_ref, gv_ref, go_ref, pu_ref, sm_ref,
                       c_in, n_in, m_in, s_in, buf_in,
                       ifb_ref, mlg_ref, gw_ref, gb_ref, glag_ref, pw_ref, ps_ref, *tail, sb, dec, slot):
    br_all = tail[-6]
    for r in tail[-5:]:
        for other in range(r.shape[0]):
            if other != slot:
                r[other] = jnp.zeros(r.shape[1:], r.dtype)
    c_out, n_out, m_out, s_out, buf_out = (r.at[slot] for r in tail[-5:])
    step = pl.program_id(0)
    c_in, n_in, m_in, s_in, buf_in = (r.at[0] for r in (c_in, n_in, m_in, s_in, buf_in))
    br_ref = br_all.at[pl.ds(pl.multiple_of(step * sb, sb), sb), :]

    @pl.when(step == 0)
    def _():
        br_all[dec:, :] = jnp.zeros((br_all.shape[0] - dec, br_all.shape[1]), br_all.dtype)

    sm = sm_ref[...]
    def to_cols(x):
        parts = [x[:, j * LANES:(j + 1) * LANES].T for j in range(x.shape[1] // LANES)]
        return jnp.concatenate(parts, axis=0)

    y0 = sm + ifb_ref[...]
    logf_all = _log_sigmoid(y0)
    k_scale = ML_DH ** -0.5
    m_all = m_in[...]
    hm, m_new_cols = [], []
    for h in range(ML_HEADS):
        hs = slice(h * ML_DH, (h + 1) * ML_DH)
        q = q_ref[:, hs].astype(f32)
        k = k_ref[:, hs].astype(f32) * k_scale
        v = v_ref[:, hs].astype(f32)
        i_pre = y0[:, S_MI + h:S_MI + h + 1]
        logf = logf_all[:, S_MF + h:S_MF + h + 1]
        m = m_all[:, h:h + 1]
        inter = logf + m
        m_t = jnp.maximum(inter, i_pre)
        w_intra = jnp.exp(i_pre - m_t)
        w_inter = jnp.exp(inter - m_t)
        s = jnp.sum(q * k, axis=1, keepdims=True) * w_intra
        qb = q.astype(bf16)
        n_h = n_in[:, h, :]
        cq = jnp.concatenate(
            [lax.dot_general(qb, c_in[j, h].astype(bf16), _NT, preferred_element_type=f32)[j:j + 1, :]
             for j in range(sb)], axis=0)
        num = s * v + w_inter * cq
        den = s + w_inter * jnp.sum(n_h * q, axis=1, keepdims=True)
        hm.append(num / jnp.maximum(jnp.abs(den), jnp.exp(-m_t)))
        m_new = m_t
        w_s = w_intra
        w_c = w_inter
        n_out[:, h, :] = w_c * n_h + w_s * k
        m_new_cols.append(m_new)
        vw_cols = to_cols(v * w_s)
        for j in range(sb):
            c_out[j, h] = w_c[j:j + 1, :] * c_in[j, h] + vw_cols[:, j:j + 1] * k[j:j + 1, :]
    lane = lax.broadcasted_iota(jnp.int32, (sb, LANES), 1)
    m_pack = jnp.zeros((sb, LANES), f32)
    for h in range(ML_HEADS):
        m_pack = jnp.where(lane == h, m_new_cols[h], m_pack)
    m_out[...] = m_pack
    y_ml = _head_norm(jnp.concatenate(hm, axis=1), mlg_ref[...]) * _sigmoid(mo_ref[...].astype(f32))
    br_ref[:, 0:ML_W] = y_ml.astype(br_ref.dtype)

    log_a = _gla_log_decay(sm, gw_ref, gb_ref)
    decay = jnp.exp(log_a)
    q2 = gqk_ref[:, 0:GLA_KW].astype(f32) * (GLA_DK ** -0.5)
    k2 = gqk_ref[:, GLA_KW:2 * GLA_KW].astype(f32)
    gv = gv_ref[...].astype(f32)
    qe = (q2 * decay).astype(bf16)
    qk = q2 * k2
    k_cols = to_cols(k2)
    decay_cols = to_cols(decay)
    og = []
    for h in range(GLA_HEADS):
        ks = slice(h * GLA_DK, (h + 1) * GLA_DK)
        vs = slice(h * GLA_DV, (h + 1) * GLA_DV)
        a = jnp.sum(qk[:, ks], axis=1, keepdims=True)
        inter = jnp.concatenate(
            [jnp.dot(qe[:, ks], s_in[j, h].astype(bf16), preferred_element_type=f32)[j:j + 1, :]
             for j in range(sb)], axis=0)
        og.append(a * gv[:, vs] + inter)
        for j in range(sb):
            s_out[j, h] = (decay_cols[ks, j:j + 1] * s_in[j, h]
                           + k_cols[ks, j:j + 1] * gv[j:j + 1, vs])
    go = go_ref[...].astype(f32)
    y_gla = _head_norm(jnp.concatenate(og, axis=1), glag_ref[...]) * (go * _sigmoid(go))
    br_ref[:, ML_W:ML_W + GLA_VW] = y_gla.astype(br_ref.dtype)

    u = pu_ref[...].astype(f32)
    rowi = lax.broadcasted_iota(jnp.int32, (POOL_BUF + 1, POOL_GDIM), 0)
    d_rows = []
    for j in range(sb):
        ext = jnp.concatenate([buf_in[j], u[j:j + 1, :]], axis=0)
        buf_out[j] = ext[1:POOL_BUF + 1, :]
        parts = []
        for g, w in enumerate(POOL_WINDOWS):
            gs = slice(g * POOL_GDIM, (g + 1) * POOL_GDIM)
            win = jnp.sum(jnp.where(rowi >= POOL_BUF + 1 - w, ext[:, gs], 0.0), axis=0, keepdims=True)
            parts.append(win / float(w) - u[j:j + 1, gs])
        d_rows.append(jnp.concatenate(parts, axis=1))
    d = jnp.concatenate(d_rows, axis=0)
    for g in range(len(POOL_WINDOWS)):
        gs = slice(g * POOL_GDIM, (g + 1) * POOL_GDIM)
        y = jnp.dot(d[:, gs].astype(bf16), pw_ref[g], preferred_element_type=f32) * ps_ref[:, gs]
        br_ref[:, ML_W + GLA_VW + g * POOL_GDIM:ML_W + GLA_VW + (g + 1) * POOL_GDIM] = y.astype(br_ref.dtype)


def _sample_mix(z, zs, st, prev, layer, lw, row0, dec, sb):
    depth = st[0].shape[0]
    base = row0 // sb
    tail_rows = z.shape[0] - row0

    def zspec(col, width):
        blk = col // width
        return pl.BlockSpec((sb, width), lambda i: (base + i, blk))

    def full(shape):
        nd = len(shape)
        return pl.BlockSpec(shape, lambda i: (0,) * nd)

    st_shapes = [(ML_HEADS, ML_DH, ML_DH), (ML_HEADS, ML_DH), (LANES,),
                 (GLA_HEADS, GLA_DK, GLA_DV), (POOL_BUF, POOL_W)]
    st_specs = [pl.BlockSpec((1, sb) + s, functools.partial(lambda i, nd: (layer, i) + (0,) * nd, nd=len(s)))
                for s in st_shapes]
    n_in = N_ZBLOCKS + len(st_shapes) + 7
    prev = () if prev is None else tuple(prev)
    if prev:
        out_st, slot = st_specs, 0
    else:
        out_st = [pl.BlockSpec((depth, sb) + s, functools.partial(lambda i, nd: (0, i) + (0,) * nd, nd=len(s)))
                  for s in st_shapes]
        slot = layer
    return pl.pallas_call(
        functools.partial(_sample_mix_kernel, sb=sb, dec=dec, slot=slot),
        grid=(dec // sb,),
        in_specs=[zspec(C_MQ, 512), zspec(C_MK, 512), zspec(C_MV, 512), zspec(C_MO, 512),
                  zspec(C_GQ, 512), zspec(C_GV, 512), zspec(C_GO, 512), zspec(C_PU, 512),
                  pl.BlockSpec((sb, LANES), lambda i: (base + i, 0))]
                 + st_specs
                 + [full((1, LANES)), full((1, ML_W)), full((LANES, GLA_KW)), full((1, GLA_KW)),
                    full((1, GLA_VW)), full((len(POOL_WINDOWS), POOL_GDIM, POOL_GDIM)), full((1, POOL_W))]
                 + [pl.BlockSpec(memory_space=pl.ANY)] * len(prev),
        out_specs=[pl.BlockSpec((tail_rows, N_BRANCH * ML_W), lambda i: (0, 0))] + out_st,
        out_shape=[jax.ShapeDtypeStruct((tail_rows, N_BRANCH * ML_W), bf16)]
                  + [jax.ShapeDtypeStruct((depth, dec) + s, f32) for s in st_shapes],
        input_output_aliases={n_in + j: 1 + j for j in range(len(prev))},
        compiler_params=_cparams(("arbitrary",)),
        name="sample_mix",
    )(z, z, z, z, z, z, z, z, zs, *st,
      lw["ifb"], lw["ml_g"], lw["gla_gw"], lw["gla_gb"], lw["gla_g"], lw["pool_w"], lw["pool_scale"], *prev)


def _merge_kernel(*refs, tm, n_valid, n_prompt_tiles, x_starts, n_extra, tf):
    x_refs = refs[:len(x_starts)]
    brp_ref, brt_ref, g0_ref, g1_ref, g2_ref, wb_ref, wo_ref = refs[len(x_starts):len(x_starts) + 7]
    extra_in = refs[len(x_starts) + 7:len(x_starts) + 7 + n_extra]
    o_ref = refs[len(x_starts) + 7 + n_extra]
    route_out = refs[len(x_starts) + 8 + n_extra:]
    in_tail = pl.program_id(0) >= n_prompt_tiles
    mixed = None
    for n, g_ref in enumerate((g0_ref, g1_ref, g2_ref)):
        cols = slice(n * ML_W, (n + 1) * ML_W)
        br = jnp.where(in_tail, brt_ref[:, cols], brp_ref[:, cols])
        proj = jnp.dot(br, wb_ref[n], preferred_element_type=f32)
        term = _sigmoid(g_ref[...].astype(f32)) * proj
        mixed = term if mixed is None else mixed + term
    out = (_pick_rows(x_refs, x_starts)
           + jnp.dot(mixed.astype(bf16), wo_ref[...], preferred_element_type=f32))
    row = pl.program_id(0) * tm + lax.broadcasted_iota(jnp.int32, (tm, 1), 0)
    out = jnp.where(row < n_valid, out, 0.0)
    if n_extra == 4:
        out = _swiglu_residual(out, *extra_in, tf=tf)
    o_ref[...] = out
    if n_extra == 3:
        _route(out, *extra_in, *route_out, tm=tm, n_valid=n_valid)


def _merge(x_parts, br_prompt, br_tail, z, w_branch, w_out, tm, n_valid, route_w=None, ffn_w=None, tf=None):
    mp = sum(p.shape[0] for p in x_parts)
    nt = mp // tm
    route = route_w is not None
    assert not (route and ffn_w is not None)
    extra = tuple(route_w or ffn_w or ())
    route_specs = [_resident(a.shape) for a in extra]
    out_specs = [pl.BlockSpec((tm, D_MODEL), lambda i: (i, 0))]
    out_shape = [jax.ShapeDtypeStruct((mp, D_MODEL), f32)]
    if route:
        out_specs += [pl.BlockSpec((tm, D_MODEL), lambda i: (i, 0)),
                      pl.BlockSpec((tm, LANES), lambda i: (i, 0)),
                      pl.BlockSpec((tm, LANES), lambda i: (i, 0)),
                      pl.BlockSpec((1, N_EXPERTS, tm), lambda i: (i, 0, 0)),
                      pl.BlockSpec((1, 1, LANES), lambda i: (i, 0, 0))]
        out_shape += [jax.ShapeDtypeStruct((mp, D_MODEL), bf16),
                      jax.ShapeDtypeStruct((mp, LANES), f32),
                      jax.ShapeDtypeStruct((mp, LANES), f32),
                      jax.ShapeDtypeStruct((nt, N_EXPERTS, tm), f32),
                      jax.ShapeDtypeStruct((nt, 1, LANES), f32)]
    x_starts = tuple(sum(p.shape[0] for p in x_parts[:j]) // tm for j in range(len(x_parts)))
    n_p = br_prompt.shape[0] // tm
    gate_specs = [pl.BlockSpec((tm, D_MODEL), functools.partial(lambda i, n: (i, C_GATES // D_MODEL + n), n=n))
                  for n in range(N_BRANCH)]
    outs = pl.pallas_call(
        functools.partial(_merge_kernel, tm=tm, n_valid=n_valid, n_prompt_tiles=n_p, x_starts=x_starts,
                          n_extra=len(extra), tf=tf),
        grid=(nt,),
        in_specs=[
            *_split_rows_specs(x_parts, tm, D_MODEL),
            pl.BlockSpec((tm, N_BRANCH * ML_W), lambda i: (jnp.minimum(i, n_p - 1), 0)),
            pl.BlockSpec((tm, N_BRANCH * ML_W), lambda i: (jnp.maximum(i - n_p, 0), 0)),
            *gate_specs,
            _resident((N_BRANCH, ML_W, D_MODEL)),
            _resident((D_MODEL, D_MODEL)),
            *route_specs,
        ],
        out_specs=out_specs,
        out_shape=out_shape,
        compiler_params=_cparams(("parallel",)),
        name="merge_route" if route else ("merge_ffn" if extra else "merge"),
    )(*x_parts, br_prompt, br_tail, z, z, z, w_branch, w_out, *extra)
    return outs if route else outs[0]


def _route(x, g_ref, wrt_ref, before_ref, hb_ref, comb_ref, rnk_ref, rnkt_ref, cnt_ref, *, tm, n_valid):
    h = _rms(x, g_ref[...])
    h_hi = h.astype(bf16)
    hb_ref[...] = h_hi
    h_lo = (h - h_hi.astype(f32)).astype(bf16)
    p_hi = lax.dot_general(wrt_ref[...], h_hi, _NT, preferred_element_type=f32)
    p_lo = lax.dot_general(wrt_ref[...], h_lo, _NT, preferred_element_type=f32)
    logits = p_hi[0:N_EXPERTS] + p_hi[N_EXPERTS:] + p_lo[0:N_EXPERTS]
    eidx = lax.broadcasted_iota(jnp.int32, logits.shape, 0)
    mx = jnp.max(logits, axis=0, keepdims=True)
    e = jnp.exp(logits - mx)
    probs = e / jnp.sum(e, axis=0, keepdims=True)
    p1 = jnp.max(probs, axis=0, keepdims=True)
    i1 = jnp.min(jnp.where(probs == p1, eidx, N_EXPERTS), axis=0, keepdims=True)
    rest = jnp.where(eidx == i1, -1.0, probs)
    p2 = jnp.max(rest, axis=0, keepdims=True)
    i2 = jnp.min(jnp.where(rest == p2, eidx, N_EXPERTS), axis=0, keepdims=True)
    tot = p1 + p2
    comb_t = jnp.where(eidx == i1, p1 / tot, 0.0) + jnp.where(eidx == i2, p2 / tot, 0.0)
    tok = pl.program_id(0) * tm + lax.broadcasted_iota(jnp.int32, logits.shape, 1)
    sel = ((eidx == i1) | (eidx == i2)) & (tok < n_valid)
    rank = jnp.dot(sel.astype(bf16), before_ref[...], preferred_element_type=f32)
    rnk_t = jnp.where(sel, rank, -1.0)
    rnkt_ref[0] = rnk_t
    both = jnp.concatenate([comb_t, rnk_t, jnp.zeros((LANES - 2 * N_EXPERTS, tm), f32)], axis=0).T
    lane = lax.broadcasted_iota(jnp.int32, (tm, LANES), 1)
    comb_ref[...] = jnp.where(lane < N_EXPERTS, both, 0.0)
    rnk = jnp.where(lane < N_EXPERTS, pltpu.roll(both, LANES - N_EXPERTS, 1), -1.0)
    rnk_ref[...] = rnk
    cnt_ref[0] = jnp.sum((rnk >= 0.0).astype(f32), axis=0, keepdims=True)


def _moe_kernel(rounds_ref, hb_ref, rnk_ref, rnkt_ref, comb_ref, wg_ref, wu_ref, wd_ref, o_ref,
                *, cap, n_sub, ts):
    i, e = pl.program_id(0), pl.program_id(1)

    @pl.when(e == 0)
    def _():
        o_ref[...] = jnp.zeros_like(o_ref)

    lane = lax.broadcasted_iota(jnp.int32, (ts, LANES), 1)
    slot_rows = lax.broadcasted_iota(jnp.int32, (cap, ts), 0).astype(f32)
    slot_cols = lax.broadcasted_iota(jnp.int32, (ts, cap), 1).astype(f32)

    def round_body(r, carry):
        base = (r * cap).astype(f32)
        xs = []
        for j in range(n_sub):
            rt = rnkt_ref[j, pl.ds(e, 1), :]
            p = (rt - base == slot_rows).astype(bf16)
            xs.append(jnp.dot(p, hb_ref[j * ts:(j + 1) * ts, :],
                              preferred_element_type=f32).astype(bf16))
        xs = jnp.concatenate(xs, axis=0)
        a = jnp.dot(xs, wg_ref[0], preferred_element_type=f32)
        a = (a * _sigmoid(a)) * jnp.dot(xs, wu_ref[0], preferred_element_type=f32)
        y = jnp.dot(a.astype(bf16), wd_ref[0], preferred_element_type=f32).astype(bf16)
        for j in range(n_sub):
            rows = slice(j * ts, (j + 1) * ts)
            col = jnp.sum(jnp.where(lane == e, rnk_ref[rows, :], 0.0), axis=1, keepdims=True)
            w = jnp.sum(jnp.where(lane == e, comb_ref[rows, :], 0.0), axis=1, keepdims=True)
            pt = (col - base == slot_cols).astype(bf16)
            o_ref[rows, :] += w * jnp.dot(pt, y[j * cap:(j + 1) * cap, :], preferred_element_type=f32)
        return carry

    lax.fori_loop(0, rounds_ref[i * N_EXPERTS + e], round_body, 0)


def _moe(hb, rnk, rnkt, comb, rounds, wg, wu, wd, ts, n_sub, cap):
    mp = hb.shape[0]
    n_e, _, d_ff = wg.shape
    tsup = ts * n_sub
    grid_spec = pltpu.PrefetchScalarGridSpec(
        num_scalar_prefetch=1,
        grid=(mp // tsup, n_e),
        in_specs=[pl.BlockSpec((tsup, D_MODEL), lambda i, e, r: (i, 0)),
                  pl.BlockSpec((tsup, LANES), lambda i, e, r: (i, 0)),
                  pl.BlockSpec((n_sub, N_EXPERTS, ts), lambda i, e, r: (i, 0, 0)),
                  pl.BlockSpec((tsup, LANES), lambda i, e, r: (i, 0)),
                  pl.BlockSpec((1, D_MODEL, d_ff), lambda i, e, r: (e, 0, 0)),
                  pl.BlockSpec((1, D_MODEL, d_ff), lambda i, e, r: (e, 0, 0)),
                  pl.BlockSpec((1, d_ff, D_MODEL), lambda i, e, r: (e, 0, 0))],
        out_specs=pl.BlockSpec((tsup, D_MODEL), lambda i, e, r: (i, 0)),
    )
    return pl.pallas_call(
        functools.partial(_moe_kernel, cap=cap, n_sub=n_sub, ts=ts),
        grid_spec=grid_spec,
        out_shape=jax.ShapeDtypeStruct((mp, D_MODEL), f32),
        compiler_params=_cparams(("parallel", "arbitrary")),
        name="moe",
    )(rounds, hb, rnk, rnkt, comb, wg, wu, wd)


def _swiglu_residual(x, g_ref, wg_ref, wu_ref, wd_ref, *, tf):
    h = _rms(x, g_ref[...]).astype(bf16)
    acc = x
    assert wg_ref.shape[1] % tf == 0
    for c in range(wg_ref.shape[1] // tf):
        cols = slice(c * tf, (c + 1) * tf)
        a = jnp.dot(h, wg_ref[:, cols], preferred_element_type=f32)
        a = (a * _sigmoid(a)) * jnp.dot(h, wu_ref[:, cols], preferred_element_type=f32)
        acc = acc + jnp.dot(a.astype(bf16), wd_ref[cols, :], preferred_element_type=f32)
    return acc


def _final_norm_kernel(x_ref, y_ref, g_ref, op_ref, os_ref, *, n_prompt_tiles, dec):
    i = pl.program_id(0)
    out = _rms(x_ref[...] + y_ref[...], g_ref[...])

    @pl.when(i < n_prompt_tiles)
    def _():
        op_ref[...] = out

    @pl.when(i == n_prompt_tiles)
    def _():
        os_ref[...] = out[0:dec]


def _final_norm(x, y, g, tm, m_prompt, dec):
    n_p = m_prompt // tm
    assert n_p * tm == m_prompt and dec <= tm
    return pl.pallas_call(
        functools.partial(_final_norm_kernel, n_prompt_tiles=n_p, dec=dec),
        grid=(n_p + 1,),
        in_specs=[pl.BlockSpec((tm, D_MODEL), lambda i: (i, 0)),
                  pl.BlockSpec((tm, D_MODEL), lambda i: (i, 0)),
                  pl.BlockSpec((1, D_MODEL), lambda i: (0, 0))],
        out_specs=[pl.BlockSpec((tm, D_MODEL), lambda i: (jnp.minimum(i, n_p - 1), 0)),
                   pl.BlockSpec((dec, D_MODEL), lambda i: (0, 0))],
        out_shape=[jax.ShapeDtypeStruct((m_prompt, D_MODEL), f32),
                   jax.ShapeDtypeStruct((dec, D_MODEL), f32)],
        compiler_params=_cparams(("arbitrary",)),
        name="final_norm",
    )(x, y, g)


O_MI = 3 * ML_W
O_MO = O_MI + 2 * ML_HEADS
O_GLR = O_MO + ML_W + 2 * GLA_KW + GLA_VW
O_GO = O_GLR + GLA_RANK
D_IN = O_GO + GLA_VW + POOL_W + N_BRANCH * D_MODEL


def _regroup_kernel(w_ref, if_ref, glr_ref, main_ref, small_ref):
    main_ref[0] = w_ref[0].astype(bf16)
    n_if, n_glr = if_ref.shape[1], glr_ref.shape[1]
    small_ref[0, 0:n_if, :] = if_ref[0].astype(bf16)
    small_ref[0, n_if:n_if + n_glr, :] = glr_ref[0].astype(bf16)
    small_ref[0, n_if + n_glr:, :] = jnp.zeros((LANES - n_if - n_glr, small_ref.shape[2]), bf16)


def _regroup_w_in(wt, tr=512):
    depth, n, d = wt.shape
    assert n == D_IN and Z_MAIN == D_IN - 2 * ML_HEADS - GLA_RANK
    assert O_MI % tr == 0 and (O_GLR - O_MO) % tr == 0 and (D_IN - O_GO) % tr == 0
    c1, c2 = O_MI // tr, (O_MI + O_GLR - O_MO) // tr

    def src_row(c):
        skip = jnp.where(c < c1, 0, jnp.where(c < c2, O_MO - O_MI, O_MO - O_MI + O_GO - O_GLR))
        return pl.multiple_of(c * tr + skip, 8)

    return pl.pallas_call(
        _regroup_kernel,
        grid=(depth, Z_MAIN // tr),
        in_specs=[pl.BlockSpec((pl.Element(1), pl.Element(rows), pl.Element(d)), index_map)
                  for rows, index_map in ((tr, lambda l, c: (l, src_row(c), 0)),
                                          (O_MO - O_MI, lambda l, c: (l, O_MI, 0)),
                                          (O_GO - O_GLR, lambda l, c: (l, O_GLR, 0)))],
        out_specs=[pl.BlockSpec((1, tr, d), lambda l, c: (l, c, 0)),
                   pl.BlockSpec((1, LANES, d), lambda l, c: (l, 0, 0))],
        out_shape=[jax.ShapeDtypeStruct((depth, Z_MAIN, d), bf16),
                   jax.ShapeDtypeStruct((depth, LANES, d), bf16)],
        compiler_params=_cparams(("parallel", "arbitrary")),
        name="regroup_w_in",
    )(wt, wt, wt)


def _layer_weights(l, norm1_g, if_bias, ml_g, gla_gw, gla_gb, gla_g, pool_w, pool_scale,
                   w_branch, w_out):
    ifb = jnp.zeros((1, LANES), f32)
    ifb = ifb.at[0, S_MI:S_MI + ML_HEADS].set(if_bias[l, 0]).at[0, S_MF:S_MF + ML_HEADS].set(if_bias[l, 1])
    gw = jnp.zeros((LANES, GLA_KW), f32).at[S_GLR:S_GLR + GLA_RANK].set(gla_gw[l]).astype(bf16)
    return dict(
        norm1_g=norm1_g[l][None], ifb=ifb, ml_g=ml_g[l][None],
        gla_gw=gw, gla_gb=gla_gb[l][None], gla_g=gla_g[l][None], pool_w=pool_w[l].astype(bf16),
        pool_scale=pool_scale[l][None], w_branch=w_branch[l].astype(bf16), w_out=w_out[l].astype(bf16))


def _forward(x_prompt, x_sample, state_mlstm_C, state_mlstm_n, state_mlstm_m, state_gla_S,
             state_pool_buf, norm1_g, w_in, mlstm_if_bias, mlstm_norm_g, gla_gate_w, gla_gate_b,
             gla_norm_g, pool_w, pool_scale, w_branch, w_out, norm2_g, ffn_wg, ffn_wu, ffn_wd,
             router_w, moe_wg, moe_wu, moe_wd, final_norm_g, *, tm, tn, tt, chunk, sub, nb, sb, tf_dense, moe_sub, moe_cap):
    batch, seq, _ = x_prompt.shape
    dec = x_sample.shape[0]
    depth = w_in.shape[0]
    m_prompt = batch * seq
    m_all = m_prompt + dec
    mp = -(-m_all // tm) * tm
    assert m_prompt % tm == 0
    x_parts = (x_prompt.reshape(m_prompt, D_MODEL),
               jnp.concatenate([x_sample.reshape(dec, D_MODEL), jnp.zeros((mp - m_all, D_MODEL), f32)], axis=0))
    outs = [[] for _ in range(5)]
    moe_out = None
    st = (state_mlstm_C, state_mlstm_n, jnp.pad(state_mlstm_m, ((0, 0), (0, 0), (0, LANES - ML_HEADS))),
          state_gla_S, state_pool_buf)
    st_new = None
    w_main, w_small = _regroup_w_in(jnp.swapaxes(w_in, 1, 2))
    for l in range(depth):
        lw = _layer_weights(l, norm1_g, mlstm_if_bias, mlstm_norm_g, gla_gate_w, gla_gate_b,
                            gla_norm_g, pool_w, pool_scale, w_branch, w_out)
        z, zs = _norm_matmul(x_parts, lw["norm1_g"], w_main, w_small, l, tm, tn)
        br_p, c_p, n_p, m_p, s_p, buf_p = _prompt_mix(z, zs, lw, batch, seq, tt, chunk, sub, nb)
        br_t, *st_new = _sample_mix(z, zs, st, st_new, l, lw, m_prompt, dec, sb)
        j = l // 2
        if l % 2 == 0:
            x = _merge(x_parts, br_p, br_t, z, lw["w_branch"], lw["w_out"], tm, m_all,
                       ffn_w=(norm2_g[l][None], ffn_wg[j].astype(bf16), ffn_wu[j].astype(bf16),
                              ffn_wd[j].astype(bf16)), tf=tf_dense)
        else:
            wr_hi = router_w[j].astype(bf16)
            wr_lo = (router_w[j] - wr_hi.astype(f32)).astype(bf16)
            wrt = jnp.concatenate([wr_hi, wr_lo], axis=1).T
            before = (jnp.arange(tm)[:, None] < jnp.arange(tm)[None, :]).astype(bf16)
            x, hb, comb, rnk, rnkt, cnt = _merge(x_parts, br_p, br_t, z, lw["w_branch"], lw["w_out"], tm, m_all,
                                                 route_w=(norm2_g[l][None], wrt, before))
            n_sup = mp // (tm * moe_sub)
            cnt = cnt[:, 0, :N_EXPERTS].reshape(n_sup, moe_sub, N_EXPERTS).max(axis=1)
            rounds = jnp.ceil(cnt / moe_cap).astype(jnp.int32).reshape(-1)
            moe_out = _moe(hb, rnk, rnkt, comb, rounds, moe_wg[j].astype(bf16), moe_wu[j].astype(bf16),
                           moe_wd[j].astype(bf16), tm, moe_sub, moe_cap)
            if l + 1 < depth:
                x = x + moe_out
                moe_out = None
        x_parts = (x,)
        for lst, val in zip(outs, (c_p, n_p, m_p[:, :ML_HEADS, 0], s_p, buf_p)):
            lst.append(val)
    if moe_out is None:
        moe_out = jnp.zeros_like(x)
    y_prompt, y_sample = _final_norm(x, moe_out, final_norm_g[None], tm, m_prompt, dec)
    c_p, n_p, m_p, s_p, buf_p = (jnp.stack(o) for o in outs)
    c_s, n_s, m_s, s_s, buf_s = st_new
    return (y_prompt.reshape(batch, seq, D_MODEL), y_sample.reshape(dec, 1, D_MODEL),
            c_p, c_s, n_p, n_s, m_p, m_s[:, :, :ML_HEADS], s_p, s_s, buf_p, buf_s)


def kernel(x_prompt, x_sample, state_mlstm_C, state_mlstm_n, state_mlstm_m, state_gla_S, state_pool_buf, norm1_g, w_in, mlstm_if_bias, mlstm_norm_g, gla_gate_w, gla_gate_b, gla_norm_g, pool_w, pool_scale, w_branch, w_out, norm2_g, ffn_wg, ffn_wu, ffn_wd, router_w, moe_wg, moe_wu, moe_wd, final_norm_g):
    return _forward(x_prompt, x_sample, state_mlstm_C, state_mlstm_n, state_mlstm_m, state_gla_S,
                    state_pool_buf, norm1_g, w_in, mlstm_if_bias, mlstm_norm_g, gla_gate_w, gla_gate_b,
                    gla_norm_g, pool_w, pool_scale, w_branch, w_out, norm2_g, ffn_wg, ffn_wu, ffn_wd,
                    router_w, moe_wg, moe_wu, moe_wd, final_norm_g,
                    tm=512, tn=1024, tt=512, chunk=128, sub=64, nb=2, sb=16, tf_dense=MXU_COLS, moe_sub=3, moe_cap=160)
```
